```python
import math
import jax
import jax.numpy as jnp
from jax import lax
import numpy as np

D_MODEL = 1024
BATCH = 4
SEQ = 4096
DEPTH = 1

DN_HEADS = 4
DN_HEAD_DIM = 128
DN_WIDTH = DN_HEADS * DN_HEAD_DIM
CONV_WIDTH = 4
CHUNK = 64
SWA_HEADS = 8
SWA_KV_HEADS = 2
SWA_GROUP = SWA_HEADS // SWA_KV_HEADS
SWA_HEAD_DIM = 64
SWA_WIDTH = SWA_HEADS * SWA_HEAD_DIM
SWA_KV_WIDTH = SWA_KV_HEADS * SWA_HEAD_DIM
WINDOW = 128
MIX_WIDTH = DN_WIDTH + SWA_WIDTH
IN_SIZES = (3 * DN_WIDTH, DN_WIDTH, DN_HEADS, DN_HEADS, SWA_WIDTH, SWA_KV_WIDTH, SWA_KV_WIDTH)
IN_COLS = sum(IN_SIZES)
N_EXPERTS = 32
TOP_K = 4
D_FF = D_MODEL
SWIGLU_ALPHA = 1.702
SWIGLU_LIMIT = 7.0
MOE_BLOCK = 128
EPS = 1e-6
NEG = -1e30

kernel_name = 'hybrid_deltanet_swa_moe_adaln'


def _rmsnorm(x, w):
    xf = x.astype(jnp.float32)
    y = xf * lax.rsqrt(jnp.mean(xf * xf, axis=-1, keepdims=True) + EPS)
    return (y * w.astype(jnp.float32)).astype(x.dtype)


def _l2norm(x):
    xf = x.astype(jnp.float32)
    return xf * lax.rsqrt(jnp.sum(xf * xf, axis=-1, keepdims=True) + EPS)


def _causal_conv_silu(u, w):
    n_ch = u.shape[-1]
    y = lax.conv_general_dilated(u, w[:, None, :].astype(u.dtype), window_strides=(1,),
                                 padding=[(CONV_WIDTH - 1, 0)],
                                 dimension_numbers=('NWC', 'WIO', 'NWC'),
                                 feature_group_count=n_ch)
    return jax.nn.silu(y)


def _chunk_gated_delta_rule(q, k, v, g, beta):
    Bn, Sn, H, Dk = q.shape
    Dv = v.shape[-1]
    N = Sn // CHUNK

    def to_chunks(t):
        t = t.astype(jnp.float32).reshape(Bn, N, CHUNK, *t.shape[2:])
        return jnp.moveaxis(t, 3, 2)

    qc = to_chunks(q) * (Dk ** -0.5)
    kc, vc, gc, bc = to_chunks(k), to_chunks(v), to_chunks(g), to_chunks(beta)
    gcum = jnp.cumsum(gc, axis=-1)
    incl = jnp.tril(jnp.ones((CHUNK, CHUNK), bool))
    strict = jnp.tril(jnp.ones((CHUNK, CHUNK), bool), -1)
    diff = gcum[..., :, None] - gcum[..., None, :]
    decay = jnp.where(incl, jnp.exp(jnp.where(incl, diff, 0.0)), 0.0)
    k_beta = kc * bc[..., None]
    a_kk = jnp.where(strict, jnp.einsum('bnhcd,bnhsd->bnhcs', k_beta, kc) * decay, 0.0)
    t_mat = a_kk + jnp.eye(CHUNK, dtype=jnp.float32)
    rhs = jnp.concatenate([vc * bc[..., None], k_beta * jnp.exp(gcum)[..., None]], axis=-1)
    sol = jax.lax.linalg.triangular_solve(t_mat, rhs, left_side=True, lower=True,
                                          unit_diagonal=True)
    u, w = sol[..., :Dv], sol[..., Dv:]
    a_qk = jnp.where(incl, jnp.einsum('bnhcd,bnhsd->bnhcs', qc, kc) * decay, 0.0)
    q_dec = qc * jnp.exp(gcum)[..., None]
    g_last = gcum[..., -1]
    k_dec = kc * jnp.exp(g_last[..., None] - gcum)[..., None]
    chunk_decay = jnp.exp(g_last)
    xs = tuple(jnp.moveaxis(t, 1, 0) for t in (q_dec, k_dec, u, w, a_qk, chunk_decay))

    def step(state, inp):
        q_i, k_i, u_i, w_i, a_i, d_i = inp
        v_new = u_i - jnp.einsum('bhcd,bhde->bhce', w_i, state)
        o_i = (jnp.einsum('bhcd,bhde->bhce', q_i, state)
               + jnp.einsum('bhcs,bhse->bhce', a_i, v_new))
        state = state * d_i[..., None, None] + jnp.einsum('bhcd,bhce->bhde', k_i, v_new)
        return state, o_i

    state0 = jnp.zeros((Bn, H, Dk, Dv), jnp.float32)
    _, o = lax.scan(step, state0, xs)
    return jnp.transpose(o, (1, 0, 3, 2, 4)).reshape(Bn, Sn, H, Dv)


def _sliding_window_attention(q, k, v, q_norm_w, k_norm_w, sinks):
    Bn, Sn, _ = q.shape
    NB = Sn // WINDOW
    q = _rmsnorm(q.reshape(Bn, Sn, SWA_HEADS, SWA_HEAD_DIM), q_norm_w).astype(jnp.float32)
    k = _rmsnorm(k.reshape(Bn, Sn, SWA_KV_HEADS, SWA_HEAD_DIM), k_norm_w).astype(jnp.float32)
    v = v.reshape(Bn, Sn, SWA_KV_HEADS, SWA_HEAD_DIM).astype(jnp.float32)
    pad = jnp.zeros((Bn, WINDOW, SWA_KV_HEADS, SWA_HEAD_DIM), jnp.float32)

    def band(t):
        tp = jnp.concatenate([pad, t], axis=1).reshape(Bn, NB + 1, WINDOW, SWA_KV_HEADS, SWA_HEAD_DIM)
        return jnp.concatenate([tp[:, :-1], tp[:, 1:]], axis=2)

    kw, vw = band(k), band(v)
    qb = q.reshape(Bn, NB, WINDOW, SWA_KV_HEADS, SWA_GROUP, SWA_HEAD_DIM)
    s = jnp.einsum('bnqkgd,bnskd->bnkgqs', qb, kw) * (SWA_HEAD_DIM ** -0.5)
    qi = jnp.arange(WINDOW)[:, None]
    kj = jnp.arange(2 * WINDOW)[None, :]
    rel = (qi + WINDOW - kj).astype(jnp.float32)
    key_pos = jnp.arange(NB)[:, None, None] * WINDOW - WINDOW + kj[None]
    valid = (rel[None] >= 0) & (rel[None] < WINDOW) & (key_pos >= 0)
    slopes = (2.0 ** (-8.0 * jnp.arange(1, SWA_HEADS + 1, dtype=jnp.float32) / SWA_HEADS))
    slopes = slopes.reshape(SWA_KV_HEADS, SWA_GROUP)
    logits = s - slopes[None, None, :, :, None, None] * rel
    logits = jnp.where(valid[None, :, None, None], logits, NEG)
    sink = sinks.astype(jnp.float32).reshape(SWA_KV_HEADS, SWA_GROUP)[None, None, :, :, None]
    m = jnp.maximum(jnp.max(logits, axis=-1), sink)
    p = jnp.exp(logits - m[..., None])
    denom = jnp.sum(p, axis=-1) + jnp.exp(sink - m)
    o = jnp.einsum('bnkgqs,bnskd->bnqkgd', p / denom[..., None], vw)
    return o.reshape(Bn, Sn, SWA_WIDTH)


def _mixer(h, w_in, conv_w, a_log, dt_bias, dn_norm_w, q_norm_w, k_norm_w, sinks, w_out):
    Bn, Sn, _ = h.shape
    proj = h @ w_in
    split_idx = np.cumsum(IN_SIZES)[:-1].tolist()
    dn_qkv, dn_z, dn_a, dn_b, sw_q, sw_k, sw_v = jnp.split(proj, split_idx, axis=-1)
    qkv = _causal_conv_silu(dn_qkv, conv_w)
    q, k, v = jnp.split(qkv, 3, axis=-1)
    q = _l2norm(q.reshape(Bn, Sn, DN_HEADS, DN_HEAD_DIM))
    k = _l2norm(k.reshape(Bn, Sn, DN_HEADS, DN_HEAD_DIM))
    v = v.reshape(Bn, Sn, DN_HEADS, DN_HEAD_DIM)
    beta = jax.nn.sigmoid(dn_b.astype(jnp.float32))
    g = -jnp.exp(a_log.astype(jnp.float32)) * jax.nn.softplus(
        dn_a.astype(jnp.float32) + dt_bias.astype(jnp.float32))
    o_dn = _chunk_gated_delta_rule(q, k, v, g, beta)
    o_dn = _rmsnorm(o_dn, dn_norm_w) * jax.nn.silu(
        dn_z.reshape(Bn, Sn, DN_HEADS, DN_HEAD_DIM).astype(jnp.float32))
    o_dn = o_dn.reshape(Bn, Sn, DN_WIDTH)
    o_sw = _sliding_window_attention(sw_q, sw_k, sw_v, q_norm_w, k_norm_w, sinks)
    mixed = jnp.concatenate([o_dn.astype(h.dtype), o_sw.astype(h.dtype)], axis=-1)
    return mixed @ w_out


def _moe(h, w_router, b_router, w_up, b_up, w_down, b_down):
    Bn, Sn, D = h.shape
    T = Bn * Sn
    xf = h.reshape(T, D)
    logits = (xf @ w_router).astype(jnp.float32) + b_router.astype(jnp.float32)
    top_val, top_idx = lax.top_k(logits, TOP_K)
    gates = jax.nn.softmax(top_val, axis=-1)
    n_assign = T * TOP_K
    flat_e = top_idx.reshape(-1)
    flat_tok = jnp.arange(n_assign, dtype=jnp.int32) // TOP_K
    flat_gate = gates.reshape(-1)
    order = jnp.argsort(flat_e)
    se, stok, sgate = flat_e[order], flat_tok[order], flat_gate[order]
    counts = jnp.bincount(flat_e, length=N_EXPERTS)
    padded = (counts + MOE_BLOCK - 1) // MOE_BLOCK * MOE_BLOCK
    start = jnp.cumsum(counts) - counts
    pend = jnp.cumsum(padded)
    pstart = pend - padded
    dest = pstart[se] + (jnp.arange(n_assign) - start[se])
    n_pad = n_assign + N_EXPERTS * MOE_BLOCK
    n_blocks = n_pad // MOE_BLOCK
    row_tok = jnp.zeros((n_pad,), jnp.int32).at[dest].set(stok)
    row_gate = jnp.zeros((n_pad,), jnp.float32).at[dest].set(sgate)
    block_e = jnp.minimum(jnp.searchsorted(pend, jnp.arange(n_blocks) * MOE_BLOCK, side='right'),
                          N_EXPERTS - 1)
    xs = xf[row_tok].reshape(n_blocks, MOE_BLOCK, D)

    def expert_block(args):
        xb, e = args
        hu = xb @ w_up[e] + b_up[e]
        x_glu = jnp.minimum(hu[:, :D_FF], SWIGLU_LIMIT)
        x_lin = jnp.clip(hu[:, D_FF:], -SWIGLU_LIMIT, SWIGLU_LIMIT)
        act = x_glu * jax.nn.sigmoid(SWIGLU_ALPHA * x_glu) * (x_lin + 1.0)
        return act @ w_down[e] + b_down[e]

    ys = lax.map(expert_block, (xs, block_e)).reshape(n_pad, D)
    ys = ys.astype(jnp.float32) * row_gate[:, None]
    out = jax.ops.segment_sum(ys, row_tok, num_segments=T)
    return out.reshape(Bn, Sn, D).astype(h.dtype)


def setup_inputs(seed: int = 0) -> dict:
    key = jax.random.key(seed)
    ks = jax.random.split(key, 24)
    L = DEPTH

    def nrm(k, shape, scale):
        return jax.random.normal(k, shape, jnp.float32) * scale

    dt = jnp.exp(jax.random.uniform(ks[8], (L, DN_HEADS), jnp.float32,
                                    math.log(1e-3), math.log(1e-1)))
    return {
        'x': nrm(ks[0], (BATCH, SEQ, D_MODEL), 1.0),
        'c': nrm(ks[1], (BATCH, D_MODEL), 1.0),
        'w_ada': nrm(ks[2], (L, D_MODEL, 6 * D_MODEL), 0.5 * D_MODEL ** -0.5),
        'b_ada': nrm(ks[3], (L, 6 * D_MODEL), 0.02),
        'norm1_w': 1.0 + nrm(ks[4], (L, D_MODEL), 0.02),
        'w_in': nrm(ks[5], (L, D_MODEL, IN_COLS), D_MODEL ** -0.5),
        'conv_w': nrm(ks[6], (L, CONV_WIDTH, 3 * DN_WIDTH), CONV_WIDTH ** -0.5),
        'a_log': jnp.log(jax.random.uniform(ks[7], (L, DN_HEADS), jnp.float32, 1.0, 16.0)),
        'dt_bias': dt + jnp.log(-jnp.expm1(-dt)),
        'dn_norm_w': 1.0 + nrm(ks[9], (L, DN_HEAD_DIM), 0.02),
        'q_norm_w': 1.0 + nrm(ks[10], (L, SWA_HEAD_DIM), 0.02),
        'k_norm_w': 1.0 + nrm(ks[11], (L, SWA_HEAD_DIM), 0.02),
        'sinks': nrm(ks[12], (L, SWA_HEADS), 1.0),
        'w_out': nrm(ks[13], (L, MIX_WIDTH, D_MODEL), MIX_WIDTH ** -0.5),
        'norm2_w': 1.0 + nrm(ks[14], (L, D_MODEL), 0.02),
        'w_router': nrm(ks[15], (L, D_MODEL, N_EXPERTS), D_MODEL ** -0.5),
        'b_router': nrm(ks[16], (L, N_EXPERTS), 0.01),
        'w_up': nrm(ks[17], (L, N_EXPERTS, D_MODEL, 2 * D_FF), D_MODEL ** -0.5),
        'b_up': nrm(ks[18], (L, N_EXPERTS, 2 * D_FF), 0.01),
        'w_down': nrm(ks[19], (L, N_EXPERTS, D_FF, D_MODEL), D_FF ** -0.5),
        'b_down': nrm(ks[20], (L, N_EXPERTS, D_MODEL), 0.01),
    }


def reference(x, c, w_ada, b_ada, norm1_w, w_in, conv_w, a_log, dt_bias, dn_norm_w,
              q_norm_w, k_norm_w, sinks, w_out, norm2_w, w_router, b_router,
              w_up, b_up, w_down, b_down):
    out_dtype = x.dtype
    c_act = jax.nn.silu(c)
    for l in range(DEPTH):
        mod = (c_act @ w_ada[l] + b_ada[l])[:, None, :]
        shift1, scale1, gate1, shift2, scale2, gate2 = jnp.split(mod, 6, axis=-1)
        h = _rmsnorm(x, norm1_w[l]) * (1.0 + scale1) + shift1
        y = _mixer(h, w_in[l], conv_w[l], a_log[l], dt_bias[l], dn_norm_w[l],
                   q_norm_w[l], k_norm_w[l], sinks[l], w_out[l])
        x = x + gate1 * y
        h = _rmsnorm(x, norm2_w[l]) * (1.0 + scale2) + shift2
        x = x + gate2 * _moe(h, w_router[l], b_router[l], w_up[l], b_up[l], w_down[l], b_down[l])
    return x.astype(out_dtype)
```

```python
import functools

import jax
import jax.numpy as jnp
from jax import lax
from jax.experimental import pallas as pl
from jax.experimental.pallas import tpu as pltpu

F32 = jnp.float32
BF16 = jnp.bfloat16
I32 = jnp.int32

D_MODEL = 1024
DN_HEADS = 4
DN_HEAD_DIM = 128
DN_WIDTH = DN_HEADS * DN_HEAD_DIM
CONV_WIDTH = 4
CHUNK = 64
SWA_HEADS = 8
SWA_KV_HEADS = 2
SWA_GROUP = SWA_HEADS // SWA_KV_HEADS
SWA_HEAD_DIM = 64
SWA_WIDTH = SWA_HEADS * SWA_HEAD_DIM
SWA_KV_WIDTH = SWA_KV_HEADS * SWA_HEAD_DIM
WINDOW = 128
N_EXPERTS = 32
TOP_K = 4
D_FF = D_MODEL
SWIGLU_ALPHA = 1.702
SWIGLU_LIMIT = 7.0
EPS = 1e-6
NEG = -1e30

LANES = 128
SUBLANES = 8
VMEM_LIMIT = 56 * 1024 * 1024

C_QKV = 0
C_Z = C_QKV + 3 * DN_WIDTH
C_SWQ = C_Z + DN_WIDTH
C_SWK = C_SWQ + SWA_WIDTH
C_SWV = C_SWK + SWA_KV_WIDTH
C_AB = C_SWV + SWA_KV_WIDTH
IN_PAD = C_AB + LANES

TM_IN = 256
DN_ROWS = 256
TM_POST = 512
MOE_ROWS = 256
TM_COMB = 128
TM_DISP = 512


def _silu(x):
    return x * jax.nn.sigmoid(x)


def _mm(a, b):
    return jnp.dot(a.astype(BF16), b.astype(BF16), preferred_element_type=F32)


def _mm_nt(a, b):
    return lax.dot_general(a.astype(BF16), b.astype(BF16), (((1,), (1,)), ((), ())),
                           preferred_element_type=F32)


def _mm_tn(a, b):
    return lax.dot_general(a.astype(BF16), b.astype(BF16), (((0,), (0,)), ((), ())),
                           preferred_element_type=F32)


def _ada_kernel(c_ref, w_ref, b_ref, o_ref):
    c = c_ref[...]
    o_ref[...] = jnp.dot(_silu(c), w_ref[...], precision=lax.Precision.HIGHEST,
                         preferred_element_type=F32) + b_ref[...]


def _ada(c_pad, w_ada, b_ada):
    n = w_ada.shape[1]
    return pl.pallas_call(
        _ada_kernel,
        grid=(n // D_MODEL,),
        in_specs=[pl.BlockSpec((SUBLANES, D_MODEL), lambda j: (0, 0)),
                  pl.BlockSpec((D_MODEL, D_MODEL), lambda j: (0, j)),
                  pl.BlockSpec((1, D_MODEL), lambda j: (0, j))],
        out_specs=pl.BlockSpec((SUBLANES, D_MODEL), lambda j: (0, j)),
        out_shape=jax.ShapeDtypeStruct((SUBLANES, n), F32),
        name="ada",
    )(c_pad, w_ada, b_ada)


def _rms_heads64(a, w128):
    lane = lax.broadcasted_iota(I32, a.shape, 1)
    lo = lane < SWA_HEAD_DIM
    sq = a * a
    s_lo = jnp.sum(jnp.where(lo, sq, 0.0), axis=-1, keepdims=True)
    s_hi = jnp.sum(jnp.where(lo, 0.0, sq), axis=-1, keepdims=True)
    ms = jnp.where(lo, s_lo, s_hi) * (1.0 / SWA_HEAD_DIM)
    return a * lax.rsqrt(ms + EPS) * w128


def _inproj_kernel(tiles_per_seq, x_ref, mod_ref, n1w_ref, w_ref, convw_ref, gp_ref, qnw_ref,
                   knw_ref, q_ref, k_ref, v_ref, z_ref, gb_ref, swq_ref, swk_ref, swv_ref,
                   conv_scr):
    i = pl.program_id(0)
    tm = x_ref.shape[0]
    x = x_ref[...]
    y = x * lax.rsqrt(jnp.mean(x * x, axis=-1, keepdims=True) + EPS) * n1w_ref[...]
    h = (y * (1.0 + mod_ref[0, 1:2, :]) + mod_ref[0, 0:1, :]).astype(BF16)

    @pl.when(i % tiles_per_seq == 0)
    def _():
        conv_scr[0:SUBLANES, :] = jnp.zeros((SUBLANES, 3 * DN_WIDTH), F32)

    u = jnp.dot(h, w_ref[:, C_QKV:C_Z], preferred_element_type=F32)
    conv_scr[SUBLANES:SUBLANES + tm, :] = u
    cw = convw_ref[...]
    acc = u * cw[3:4, :]
    for j in range(CONV_WIDTH - 1):
        off = SUBLANES - (CONV_WIDTH - 1) + j
        acc = acc + conv_scr[off:off + tm, :] * cw[j:j + 1, :]
    conv_scr[0:SUBLANES, :] = conv_scr[tm:tm + SUBLANES, :]
    qkv = _silu(acc)

    for hh in range(DN_HEADS):
        sl = slice(hh * DN_HEAD_DIM, (hh + 1) * DN_HEAD_DIM)
        qh = qkv[:, hh * DN_HEAD_DIM:(hh + 1) * DN_HEAD_DIM]
        kh = qkv[:, DN_WIDTH + hh * DN_HEAD_DIM:DN_WIDTH + (hh + 1) * DN_HEAD_DIM]
        q_ref[:, sl] = qh * lax.rsqrt(jnp.sum(qh * qh, axis=-1, keepdims=True) + EPS) * (
            DN_HEAD_DIM ** -0.5)
        k_ref[:, sl] = kh * lax.rsqrt(jnp.sum(kh * kh, axis=-1, keepdims=True) + EPS)
    v_ref[...] = qkv[:, 2 * DN_WIDTH:3 * DN_WIDTH]

    z_ref[...] = jnp.dot(h, w_ref[:, C_Z:C_SWQ], preferred_element_type=F32)

    ab = jnp.dot(h, w_ref[:, C_AB:IN_PAD], preferred_element_type=F32)
    lane = lax.broadcasted_iota(I32, ab.shape, 1)
    sp_in = ab + gp_ref[1:2, :]
    softplus = jnp.maximum(sp_in, 0.0) + jnp.log1p(jnp.exp(-jnp.abs(sp_in)))
    gb_ref[...] = jnp.where(lane < DN_HEADS, gp_ref[0:1, :] * softplus, jax.nn.sigmoid(ab))

    sq = jnp.dot(h, w_ref[:, C_SWQ:C_SWK], preferred_element_type=F32)
    for t in range(SWA_WIDTH // LANES):
        sl = slice(t * LANES, (t + 1) * LANES)
        swq_ref[:, sl] = (_rms_heads64(sq[:, sl], qnw_ref[...]) * (SWA_HEAD_DIM ** -0.5)).astype(BF16)
    sk = jnp.dot(h, w_ref[:, C_SWK:C_SWV], preferred_element_type=F32)
    swk_ref[...] = _rms_heads64(sk, knw_ref[...]).astype(BF16)
    swv_ref[...] = jnp.dot(h, w_ref[:, C_SWV:C_AB], preferred_element_type=F32).astype(BF16)


def _inproj(xf, mod, n1w, w_cat, conv_w, gparams, qnw128, knw128, seq):
    t = xf.shape[0]
    tiles_per_seq = seq // TM_IN
    row = lambda i: (i, 0)
    const = lambda i: (0, 0)
    outs = [jax.ShapeDtypeStruct((t, DN_WIDTH), F32)] * 4 + [
        jax.ShapeDtypeStruct((t, LANES), F32),
        jax.ShapeDtypeStruct((t, SWA_WIDTH), BF16),
        jax.ShapeDtypeStruct((t, SWA_KV_WIDTH), BF16),
        jax.ShapeDtypeStruct((t, SWA_KV_WIDTH), BF16)]
    out_specs = [pl.BlockSpec((TM_IN, DN_WIDTH), row)] * 4 + [
        pl.BlockSpec((TM_IN, LANES), row),
        pl.BlockSpec((TM_IN, SWA_WIDTH), row),
        pl.BlockSpec((TM_IN, SWA_KV_WIDTH), row),
        pl.BlockSpec((TM_IN, SWA_KV_WIDTH), row)]
    return pl.pallas_call(
        functools.partial(_inproj_kernel, tiles_per_seq),
        grid=(t // TM_IN,),
        in_specs=[pl.BlockSpec((TM_IN, D_MODEL), row),
                  pl.BlockSpec((1, 6, D_MODEL), lambda i: (i // tiles_per_seq, 0, 0)),
                  pl.BlockSpec((1, D_MODEL), const),
                  pl.BlockSpec((D_MODEL, IN_PAD), const),
                  pl.BlockSpec((CONV_WIDTH, 3 * DN_WIDTH), const),
                  pl.BlockSpec((2, LANES), const),
                  pl.BlockSpec((1, LANES), const),
                  pl.BlockSpec((1, LANES), const)],
        out_specs=out_specs,
        out_shape=outs,
        scratch_shapes=[pltpu.VMEM((TM_IN + SUBLANES, 3 * DN_WIDTH), F32)],
        compiler_params=pltpu.CompilerParams(dimension_semantics=("arbitrary",),
                                             vmem_limit_bytes=VMEM_LIMIT),
        name="inproj",
    )(xf, mod, n1w, w_cat, conv_w, gparams, qnw128, knw128)


def _dn_kernel(q_ref, k_ref, v_ref, z_ref, gb_ref, nw_ref, o_ref,
               s_scr, gc_scr, u_scr, w_scr, qd_scr, kd_scr, a_scr, d_scr):
    j = pl.program_id(1)
    rows = q_ref.shape[0]
    n_chunks = rows // CHUNK

    @pl.when(j == 0)
    def _():
        s_scr[...] = jnp.zeros(s_scr.shape, F32)

    r = lax.broadcasted_iota(I32, (rows, rows), 0)
    c = lax.broadcasted_iota(I32, (rows, rows), 1)
    tril_blk = jnp.where(((r // CHUNK) == (c // CHUNK)) & (c <= r), 1.0, 0.0).astype(F32)
    gc_scr[...] = jnp.dot(tril_blk, gb_ref[...], precision=lax.Precision.HIGHEST,
                          preferred_element_type=F32)

    ii = lax.broadcasted_iota(I32, (CHUNK, CHUNK), 0)
    jj = lax.broadcasted_iota(I32, (CHUNK, CHUNK), 1)
    incl = ii >= jj
    strict = ii > jj
    eye = ii == jj

    def prep(ci, carry):
        r0 = pl.multiple_of(ci * CHUNK, CHUNK)
        rs = pl.ds(r0, CHUNK)
        gcum = gc_scr[rs, :]
        gbc = gb_ref[rs, :]
        dvals = jnp.exp(gc_scr[pl.ds(r0 + CHUNK - 1, 1), :])
        d_scr[pl.ds(ci, 1), :] = dvals
        for h in range(DN_HEADS):
            ls = slice(h * DN_HEAD_DIM, (h + 1) * DN_HEAD_DIM)
            g_col = gcum[:, h:h + 1]
            g_row = jnp.sum(jnp.where(eye, g_col, 0.0), axis=0, keepdims=True)
            g_last = gcum[CHUNK - 1:CHUNK, h:h + 1]
            beta = gbc[:, DN_HEADS + h:DN_HEADS + h + 1]
            qq = q_ref[rs, ls]
            kk = k_ref[rs, ls]
            vv = v_ref[rs, ls]
            decay = jnp.where(incl, jnp.exp(jnp.where(incl, g_col - g_row, 0.0)), 0.0)
            kb = kk * beta
            rmat = -jnp.where(strict, _mm_nt(kb, kk) * decay, 0.0)
            xm = rmat
            pm = rmat
            for it in range(6):
                xm = xm + _mm(pm, xm)
                if it < 5:
                    pm = _mm(pm, pm)
            eg = jnp.exp(g_col)
            vb = vv * beta
            kg = kb * eg
            u_scr[rs, ls] = vb + _mm(xm, vb)
            w_scr[rs, ls] = kg + _mm(xm, kg)
            qd_scr[rs, ls] = qq * eg
            kd_scr[rs, ls] = kk * jnp.exp(g_last - g_col)
            a_scr[rs, h * LANES:h * LANES + CHUNK] = jnp.where(incl, _mm_nt(qq, kk) * decay, 0.0)
        return carry

    lax.fori_loop(0, n_chunks, prep, 0)

    nw = nw_ref[...]

    def scan(ci, carry):
        r0 = pl.multiple_of(ci * CHUNK, CHUNK)
        rs = pl.ds(r0, CHUNK)
        dvals = d_scr[pl.ds(ci, 1), :]
        for h in range(DN_HEADS):
            ls = slice(h * DN_HEAD_DIM, (h + 1) * DN_HEAD_DIM)
            st = s_scr[h]
            v_new = u_scr[rs, ls] - _mm(w_scr[rs, ls], st)
            o = _mm(qd_scr[rs, ls], st) + _mm(a_scr[rs, h * LANES:h * LANES + CHUNK], v_new)
            s_scr[h] = st * dvals[:, h:h + 1] + _mm_tn(kd_scr[rs, ls], v_new)
            on = o * lax.rsqrt(jnp.mean(o * o, axis=-1, keepdims=True) + EPS) * nw
            o_ref[rs, ls] = (on * _silu(z_ref[rs, ls])).astype(o_ref.dtype)
        return carry

    lax.fori_loop(0, n_chunks, scan, 0)


def _deltanet(q, k, v, z, gb, nw, batch, seq):
    t = q.shape[0]
    steps = seq // DN_ROWS
    row = lambda b, j: (b * steps + j, 0)
    blk = pl.BlockSpec((DN_ROWS, DN_WIDTH), row)
    return pl.pallas_call(
        _dn_kernel,
        grid=(batch, steps),
        in_specs=[blk, blk, blk, blk,
                  pl.BlockSpec((DN_ROWS, LANES), row),
                  pl.BlockSpec((1, DN_HEAD_DIM), lambda b, j: (0, 0))],
        out_specs=blk,
        out_shape=jax.ShapeDtypeStruct((t, DN_WIDTH), BF16),
        scratch_shapes=[pltpu.VMEM((DN_HEADS, DN_HEAD_DIM, DN_HEAD_DIM), F32),
                        pltpu.VMEM((DN_ROWS, LANES), F32),
                        pltpu.VMEM((DN_ROWS, DN_WIDTH), F32),
                        pltpu.VMEM((DN_ROWS, DN_WIDTH), F32),
                        pltpu.VMEM((DN_ROWS, DN_WIDTH), F32),
                        pltpu.VMEM((DN_ROWS, DN_WIDTH), F32),
                        pltpu.VMEM((DN_ROWS, DN_HEADS * LANES), F32),
                        pltpu.VMEM((SUBLANES, LANES), F32)],
        compiler_params=pltpu.CompilerParams(dimension_semantics=("arbitrary", "arbitrary"),
                                             vmem_limit_bytes=VMEM_LIMIT),
        name="deltanet",
    )(q, k, v, z, gb, nw)


def _swa_kernel(sink_ref, q_ref, kp_ref, kc_ref, vp_ref, vc_ref, o_ref):
    n = pl.program_id(1)
    w = WINDOW
    kcat = jnp.concatenate([kp_ref[...], kc_ref[...]], axis=0)
    vcat = jnp.concatenate([vp_ref[...], vc_ref[...]], axis=0)
    rows = SWA_GROUP * w
    ri = lax.broadcasted_iota(I32, (rows, 2 * w), 0)
    kj = lax.broadcasted_iota(I32, (rows, 2 * w), 1)
    grp = ri // w
    rel = (ri - grp * w) + w - kj
    valid = (rel >= 0) & (rel < w) & ((kj >= w) | (n > 0))
    relf = rel.astype(F32)
    grp_col = lax.broadcasted_iota(I32, (rows, 1), 0) // w
    q = q_ref[...]
    outs = []
    for kh in range(SWA_KV_HEADS):
        slope_col = jnp.zeros((rows, 1), F32)
        sink_col = jnp.zeros((rows, 1), F32)
        for g in range(SWA_GROUP):
            hd = kh * SWA_GROUP + g
            slope_col = jnp.where(grp_col == g, 2.0 ** (-8.0 * (hd + 1) / SWA_HEADS), slope_col)
            sink_col = jnp.where(grp_col == g, sink_ref[hd], sink_col)
        qs = jnp.concatenate(
            [q[:, (kh * SWA_GROUP + g) * SWA_HEAD_DIM:(kh * SWA_GROUP + g + 1) * SWA_HEAD_DIM]
             for g in range(SWA_GROUP)], axis=0)
        kk = kcat[:, kh * SWA_HEAD_DIM:(kh + 1) * SWA_HEAD_DIM]
        vv = vcat[:, kh * SWA_HEAD_DIM:(kh + 1) * SWA_HEAD_DIM]
        s = lax.dot_general(qs, kk, (((1,), (1,)), ((), ())), preferred_element_type=F32)
        logits = jnp.where(valid, s - slope_col * relf, NEG)
        m = jnp.maximum(jnp.max(logits, axis=-1, keepdims=True), sink_col)
        p = jnp.exp(logits - m)
        denom = jnp.sum(p, axis=-1, keepdims=True) + jnp.exp(sink_col - m)
        o = jnp.dot(p.astype(BF16), vv, preferred_element_type=F32) / denom
        for g in range(SWA_GROUP):
            outs.append(o[g * w:(g + 1) * w, :])
    for t in range(SWA_WIDTH // LANES):
        o_ref[:, t * LANES:(t + 1) * LANES] = jnp.concatenate(
            [outs[2 * t], outs[2 * t + 1]], axis=-1).astype(o_ref.dtype)


def _swa(swq, swk, swv, sinks, batch, seq):
    t = swq.shape[0]
    nb = seq // WINDOW
    cur = lambda b, n, s: (b * nb + n, 0)
    prev = lambda b, n, s: (b * nb + jnp.maximum(n - 1, 0), 0)
    return pl.pallas_call(
        _swa_kernel,
        grid_spec=pltpu.PrefetchScalarGridSpec(
            num_scalar_prefetch=1,
            grid=(batch, nb),
            in_specs=[pl.BlockSpec((WINDOW, SWA_WIDTH), cur),
                      pl.BlockSpec((WINDOW, SWA_KV_WIDTH), prev),
                      pl.BlockSpec((WINDOW, SWA_KV_WIDTH), cur),
                      pl.BlockSpec((WINDOW, SWA_KV_WIDTH), prev),
                      pl.BlockSpec((WINDOW, SWA_KV_WIDTH), cur)],
            out_specs=pl.BlockSpec((WINDOW, SWA_WIDTH), cur)),
        out_shape=jax.ShapeDtypeStruct((t, SWA_WIDTH), BF16),
        compiler_params=pltpu.CompilerParams(dimension_semantics=("arbitrary", "arbitrary"),
                                             vmem_limit_bytes=VMEM_LIMIT),
        name="swa",
    )(sinks, swq, swk, swk, swv, swv)


def _post_kernel(x_ref, odn_ref, osw_ref, mod_ref, wo_ref, n2w_ref, wr_ref, br_ref,
                 x1_ref, h2_ref, idx_ref, gate_ref, rank_ref, cnt_ref, run_scr):
    i = pl.program_id(0)
    tm = x_ref.shape[0]

    @pl.when(i == 0)
    def _():
        run_scr[...] = jnp.zeros(run_scr.shape, F32)

    y = (jnp.dot(odn_ref[...], wo_ref[0:DN_WIDTH, :], preferred_element_type=F32)
         + jnp.dot(osw_ref[...], wo_ref[DN_WIDTH:, :], preferred_element_type=F32))
    x1 = x_ref[...] + mod_ref[0, 2:3, :] * y
    x1_ref[...] = x1
    hn = x1 * lax.rsqrt(jnp.mean(x1 * x1, axis=-1, keepdims=True) + EPS) * n2w_ref[...]
    h2 = hn * (1.0 + mod_ref[0, 4:5, :]) + mod_ref[0, 3:4, :]
    h2_ref[...] = h2

    lane = lax.broadcasted_iota(I32, (tm, LANES), 1)
    lane_f = lane.astype(F32)
    logits = jnp.dot(h2.astype(BF16), wr_ref[...], preferred_element_type=F32) + br_ref[...]
    work = jnp.where(lane < N_EXPERTS, logits, NEG)
    sel = jnp.zeros((tm, LANES), F32)
    idx_out = jnp.zeros((tm, LANES), I32)
    vals = []
    idxs = []
    for kk in range(TOP_K):
        m = jnp.max(work, axis=-1, keepdims=True)
        ik = jnp.min(jnp.where(work == m, lane_f, float(LANES)), axis=-1,
                     keepdims=True).astype(I32)
        hit = lane == ik
        work = jnp.where(hit, NEG, work)
        sel = jnp.where(hit, 1.0, sel)
        idx_out = jnp.where(lane == kk, ik, idx_out)
        vals.append(m)
        idxs.append(ik)
    es = [jnp.exp(v - vals[0]) for v in vals]
    tot = es[0] + es[1] + es[2] + es[3]
    gate_out = jnp.zeros((tm, LANES), F32)
    for kk in range(TOP_K):
        gate_out = jnp.where(lane == kk, es[kk] / tot, gate_out)
    idx_ref[...] = idx_out
    gate_ref[...] = gate_out

    r = lax.broadcasted_iota(I32, (tm, tm), 0)
    c = lax.broadcasted_iota(I32, (tm, tm), 1)
    before = jnp.where(c < r, 1.0, 0.0).astype(BF16)
    base = jnp.dot(before, sel.astype(BF16), preferred_element_type=F32) + run_scr[0:1, :]
    rank_out = jnp.zeros((tm, LANES), I32)
    for kk in range(TOP_K):
        rk = jnp.sum(jnp.where(lane == idxs[kk], base, 0.0), axis=-1, keepdims=True)
        rank_out = jnp.where(lane == kk, rk.astype(I32), rank_out)
    rank_ref[...] = rank_out
    run = run_scr[0:1, :] + jnp.sum(sel, axis=0, keepdims=True)
    run_scr[0:1, :] = run
    cnt_ref[...] = jnp.broadcast_to(run, cnt_ref.shape).astype(I32)


def _post(xf, odn, osw, mod, w_out, n2w, wr_pad, br_pad, seq):
    t = xf.shape[0]
    tiles_per_seq = seq // TM_POST
    row = lambda i: (i, 0)
    const = lambda i: (0, 0)
    return pl.pallas_call(
        _post_kernel,
        grid=(t // TM_POST,),
        in_specs=[pl.BlockSpec((TM_POST, D_MODEL), row),
                  pl.BlockSpec((TM_POST, DN_WIDTH), row),
                  pl.BlockSpec((TM_POST, SWA_WIDTH), row),
                  pl.BlockSpec((1, 6, D_MODEL), lambda i: (i // tiles_per_seq, 0, 0)),
                  pl.BlockSpec((D_MODEL, D_MODEL), const),
                  pl.BlockSpec((1, D_MODEL), const),
                  pl.BlockSpec((D_MODEL, LANES), const),
                  pl.BlockSpec((1, LANES), const)],
        out_specs=[pl.BlockSpec((TM_POST, D_MODEL), row),
                   pl.BlockSpec((TM_POST, D_MODEL), row),
                   pl.BlockSpec((TM_POST, LANES), row),
                   pl.BlockSpec((TM_POST, LANES), row),
                   pl.BlockSpec((TM_POST, LANES), row),
                   pl.BlockSpec((SUBLANES, LANES), const)],
        out_shape=[jax.ShapeDtypeStruct((t, D_MODEL), F32),
                   jax.ShapeDtypeStruct((t, D_MODEL), F32),
                   jax.ShapeDtypeStruct((t, LANES), I32),
                   jax.ShapeDtypeStruct((t, LANES), F32),
                   jax.ShapeDtypeStruct((t, LANES), I32),
                   jax.ShapeDtypeStruct((SUBLANES, LANES), I32)],
        scratch_shapes=[pltpu.VMEM((SUBLANES, LANES), F32)],
        compiler_params=pltpu.CompilerParams(dimension_semantics=("arbitrary",),
                                             vmem_limit_bytes=VMEM_LIMIT),
        name="post",
    )(xf, odn, osw, mod, w_out, n2w, wr_pad, br_pad)


def _row_copy(src_ref, s, dst_ref, d, sem):
    return pltpu.make_async_copy(src_ref.at[pl.ds(s, 1), :], dst_ref.at[pl.ds(d, 1), :], sem)


def _dispatch_kernel(dest_ref, fill_ref, h_ref, zero_ref, xs_ref, sem):
    i = pl.program_id(0)
    n_tok = dest_ref.shape[2] // TOP_K

    def issue(tk, carry):
        tok = i * n_tok + tk
        for kk in range(TOP_K):
            _row_copy(h_ref, tok, xs_ref, dest_ref[0, 0, tk * TOP_K + kk], sem).start()
        return carry

    lax.fori_loop(0, n_tok, issue, 0)

    def drain(tk, carry):
        for kk in range(TOP_K):
            _row_copy(h_ref, 0, xs_ref, 0, sem).wait()
        return carry

    lax.fori_loop(0, n_tok, drain, 0)

    @pl.when(i == 0)
    def _():
        def per_expert(e, carry):
            lo = fill_ref[0, e]
            hi = fill_ref[1, e]

            def z_issue(rw, c2):
                _row_copy(zero_ref, 0, xs_ref, rw, sem).start()
                return c2

            lax.fori_loop(lo, hi, z_issue, 0)

            def z_wait(rw, c2):
                _row_copy(zero_ref, 0, xs_ref, 0, sem).wait()
                return c2

            lax.fori_loop(lo, hi, z_wait, 0)
            return carry

        lax.fori_loop(0, N_EXPERTS, per_expert, 0)

        def tail(b, carry):
            cp = pltpu.make_async_copy(
                zero_ref, xs_ref.at[pl.ds(pl.multiple_of(b * MOE_ROWS, MOE_ROWS), MOE_ROWS), :], sem)
            cp.start()
            cp.wait()
            return carry

        lax.fori_loop(fill_ref[2, 0], xs_ref.shape[0] // MOE_ROWS, tail, 0)


def _dispatch(dest3, fill, h2, zero_row, n_pad):
    t = h2.shape[0]
    return pl.pallas_call(
        _dispatch_kernel,
        grid=(t // TM_DISP,),
        in_specs=[pl.BlockSpec((1, 1, TM_DISP * TOP_K), lambda i: (i, 0, 0),
                               memory_space=pltpu.SMEM),
                  pl.BlockSpec(memory_space=pltpu.SMEM),
                  pl.BlockSpec(memory_space=pl.ANY),
                  pl.BlockSpec(memory_space=pl.ANY)],
        out_specs=pl.BlockSpec(memory_space=pl.ANY),
        out_shape=jax.ShapeDtypeStruct((n_pad, D_MODEL), F32),
        scratch_shapes=[pltpu.SemaphoreType.DMA(())],
        compiler_params=pltpu.CompilerParams(dimension_semantics=("arbitrary",),
                                             has_side_effects=True),
        name="dispatch",
    )(dest3, fill, h2, zero_row)


def _moe_kernel(be_ref, nv_ref, xs_ref, wu_ref, bu_ref, wd_ref, bd_ref, ys_ref, wu_bf, wd_bf):
    i = pl.program_id(0)

    @pl.when(i < nv_ref[0])
    def _():
        prev = be_ref[jnp.maximum(i - 1, 0)]

        @pl.when((i == 0) | (be_ref[i] != prev))
        def _():
            wu_bf[...] = wu_ref[0].astype(BF16)
            wd_bf[...] = wd_ref[0].astype(BF16)

        hu = jnp.dot(xs_ref[...].astype(BF16), wu_bf[...], preferred_element_type=F32) + bu_ref[0]
        x_glu = jnp.minimum(hu[:, :D_FF], SWIGLU_LIMIT)
        x_lin = jnp.clip(hu[:, D_FF:], -SWIGLU_LIMIT, SWIGLU_LIMIT)
        act = x_glu * jax.nn.sigmoid(SWIGLU_ALPHA * x_glu) * (x_lin + 1.0)
        ys_ref[...] = jnp.dot(act.astype(BF16), wd_bf[...], preferred_element_type=F32) + bd_ref[0]

    @pl.when(i >= nv_ref[0])
    def _():
        ys_ref[...] = jnp.zeros(ys_ref.shape, F32)


def _moe(block_e, n_valid, xs, w_up, b_up, w_down, b_down):
    n_pad = xs.shape[0]
    n_blocks = n_pad // MOE_ROWS
    rows = lambda i, be, nv: (jnp.minimum(i, nv[0] - 1), 0)
    out_rows = lambda i, be, nv: (i, 0)
    expert = lambda i, be, nv: (be[i], 0, 0)
    return pl.pallas_call(
        _moe_kernel,
        grid_spec=pltpu.PrefetchScalarGridSpec(
            num_scalar_prefetch=2,
            grid=(n_blocks,),
            in_specs=[pl.BlockSpec((MOE_ROWS, D_MODEL), rows),
                      pl.BlockSpec((1, D_MODEL, 2 * D_FF), expert),
                      pl.BlockSpec((1, 1, 2 * D_FF), expert),
                      pl.BlockSpec((1, D_FF, D_MODEL), expert),
                      pl.BlockSpec((1, 1, D_MODEL), expert)],
            out_specs=pl.BlockSpec((MOE_ROWS, D_MODEL), out_rows),
            scratch_shapes=[pltpu.VMEM((D_MODEL, 2 * D_FF), BF16),
                            pltpu.VMEM((D_FF, D_MODEL), BF16)]),
        out_shape=jax.ShapeDtypeStruct((n_pad, D_MODEL), F32),
        compiler_params=pltpu.CompilerParams(dimension_semantics=("arbitrary",),
                                             vmem_limit_bytes=VMEM_LIMIT),
        name="moe",
    )(block_e, n_valid, xs, w_up, b_up, w_down, b_down)


def _combine_kernel(dest_ref, ys_ref, x1_ref, gate_ref, mod_ref, o_ref, buf, sem):
    tm = x1_ref.shape[0]

    def issue(tk, carry):
        for kk in range(TOP_K):
            _row_copy(ys_ref, dest_ref[0, 0, tk * TOP_K + kk], buf, kk * tm + tk, sem).start()
        return carry

    lax.fori_loop(0, tm, issue, 0)

    def drain(tk, carry):
        for kk in range(TOP_K):
            _row_copy(ys_ref, 0, buf, 0, sem).wait()
        return carry

    lax.fori_loop(0, tm, drain, 0)

    gates = gate_ref[...]
    acc = jnp.zeros((tm, D_MODEL), F32)
    for kk in range(TOP_K):
        acc = acc + gates[:, kk:kk + 1] * buf[kk * tm:(kk + 1) * tm, :]
    o_ref[...] = x1_ref[...] + mod_ref[0, 5:6, :] * acc


def _combine(dest3, ys, x1, gates, mod, seq):
    t = x1.shape[0]
    tiles_per_seq = seq // TM_COMB
    row = lambda i: (i, 0)
    return pl.pallas_call(
        _combine_kernel,
        grid=(t // TM_COMB,),
        in_specs=[pl.BlockSpec((1, 1, TM_COMB * TOP_K), lambda i: (i, 0, 0),
                               memory_space=pltpu.SMEM),
                  pl.BlockSpec(memory_space=pl.ANY),
                  pl.BlockSpec((TM_COMB, D_MODEL), row),
                  pl.BlockSpec((TM_COMB, LANES), row),
                  pl.BlockSpec((1, 6, D_MODEL), lambda i: (i // tiles_per_seq, 0, 0))],
        out_specs=pl.BlockSpec((TM_COMB, D_MODEL), row),
        out_shape=jax.ShapeDtypeStruct((t, D_MODEL), F32),
        scratch_shapes=[pltpu.VMEM((TOP_K * TM_COMB, D_MODEL), F32),
                        pltpu.SemaphoreType.DMA(())],
        compiler_params=pltpu.CompilerParams(dimension_semantics=("arbitrary",),
                                             vmem_limit_bytes=VMEM_LIMIT),
        name="combine",
    )(dest3, ys, x1, gates, mod)


def _layer(x, mod, norm1_w, w_in, conv_w, a_log, dt_bias, dn_norm_w, q_norm_w, k_norm_w, sinks,
           w_out, norm2_w, w_router, b_router, w_up, b_up, w_down, b_down):
    batch, seq, _ = x.shape
    t = batch * seq
    xf = x.reshape(t, D_MODEL)

    o_qkv, o_z, o_a, o_b, o_q, o_k, o_v = 0, 1536, 2048, 2052, 2056, 2568, 2696
    w_ab = jnp.zeros((D_MODEL, LANES), F32).at[:, 0:2 * DN_HEADS].set(w_in[:, o_a:o_q])
    w_cat = jnp.concatenate([w_in[:, o_qkv:o_z], w_in[:, o_z:o_a], w_in[:, o_q:o_k],
                             w_in[:, o_k:o_v], w_in[:, o_v:], w_ab], axis=1).astype(BF16)
    gparams = jnp.zeros((2, LANES), F32)
    gparams = gparams.at[0, 0:DN_HEADS].set(-jnp.exp(a_log.astype(F32)))
    gparams = gparams.at[1, 0:DN_HEADS].set(dt_bias.astype(F32))
    qnw128 = jnp.tile(q_norm_w.astype(F32), 2)[None, :]
    knw128 = jnp.tile(k_norm_w.astype(F32), 2)[None, :]

    q, k, v, z, gb, swq, swk, swv = _inproj(xf, mod, norm1_w[None, :], w_cat, conv_w, gparams,
                                            qnw128, knw128, seq)
    o_dn = _deltanet(q, k, v, z, gb, dn_norm_w[None, :], batch, seq)
    o_sw = _swa(swq, swk, swv, sinks.astype(F32), batch, seq)

    wr_pad = jnp.zeros((D_MODEL, LANES), F32).at[:, 0:N_EXPERTS].set(w_router).astype(BF16)
    br_pad = jnp.zeros((1, LANES), F32).at[0, 0:N_EXPERTS].set(b_router)
    x1, h2, top_idx, gates, rank, cnt = _post(xf, o_dn, o_sw, mod, w_out.astype(BF16),
                                              norm2_w[None, :], wr_pad, br_pad, seq)

    counts = cnt[0, 0:N_EXPERTS]
    padded = (counts + MOE_ROWS - 1) // MOE_ROWS * MOE_ROWS
    pend = jnp.cumsum(padded)
    pstart = pend - padded
    n_blocks = (t * TOP_K) // MOE_ROWS + N_EXPERTS
    n_pad = n_blocks * MOE_ROWS
    n_valid = (pend[-1] // MOE_ROWS).astype(I32)
    blk = jnp.minimum(jnp.arange(n_blocks, dtype=I32), n_valid - 1)
    block_e = jnp.minimum(jnp.searchsorted(pend, blk * MOE_ROWS, side='right'),
                          N_EXPERTS - 1).astype(I32)
    dest = (pstart[top_idx[:, 0:TOP_K]] + rank[:, 0:TOP_K]).astype(I32)
    fill = jnp.stack([pstart + counts, pend, jnp.full((N_EXPERTS,), n_valid)]).astype(I32)

    xs = _dispatch(dest.reshape(t // TM_DISP, 1, TM_DISP * TOP_K), fill, h2,
                   jnp.zeros((MOE_ROWS, D_MODEL), F32), n_pad)
    ys = _moe(block_e, n_valid.reshape(1), xs, w_up, b_up[:, None, :], w_down, b_down[:, None, :])
    out = _combine(dest.reshape(t // TM_COMB, 1, TM_COMB * TOP_K), ys, x1, gates, mod, seq)
    return out.reshape(batch, seq, D_MODEL)


def kernel(x, c, w_ada, b_ada, norm1_w, w_in, conv_w, a_log, dt_bias, dn_norm_w, q_norm_w,
           k_norm_w, sinks, w_out, norm2_w, w_router, b_router, w_up, b_up, w_down, b_down):
    out_dtype = x.dtype
    batch = x.shape[0]
    depth = w_ada.shape[0]
    c_pad = jnp.zeros((SUBLANES, D_MODEL), F32).at[0:batch].set(c.astype(F32))
    for l in range(depth):
        mod = _ada(c_pad, w_ada[l], b_ada[l][None, :])[0:batch].reshape(batch, 6, D_MODEL)
        x = _layer(x, mod, norm1_w[l], w_in[l], conv_w[l], a_log[l], dt_bias[l], dn_norm_w[l],
                   q_norm_w[l], k_norm_w[l], sinks[l], w_out[l], norm2_w[l], w_router[l],
                   b_router[l], w_up[l], b_up[l], w_down[l], b_down[l])
    return x.astype(out_dtype)
```

```python
import functools

import jax
import jax.numpy as jnp
from jax import lax
from jax.experimental import pallas as pl
from jax.experimental.pallas import tpu as pltpu

F32 = jnp.float32
BF16 = jnp.bfloat16
I32 = jnp.int32

D_MODEL = 1024
DN_HEADS = 4
DN_HEAD_DIM = 128
DN_WIDTH = DN_HEADS * DN_HEAD_DIM
CONV_WIDTH = 4
CHUNK = 64
SWA_HEADS = 8
SWA_KV_HEADS = 2
SWA_GROUP = SWA_HEADS // SWA_KV_HEADS
SWA_HEAD_DIM = 64
SWA_WIDTH = SWA_HEADS * SWA_HEAD_DIM
SWA_KV_WIDTH = SWA_KV_HEADS * SWA_HEAD_DIM
WINDOW = 128
N_EXPERTS = 32
TOP_K = 4
D_FF = D_MODEL
SWIGLU_ALPHA = 1.702
SWIGLU_LIMIT = 7.0
EPS = 1e-6
NEG = -1e30

LANES = 128
SUBLANES = 8
VMEM_LIMIT = 56 * 1024 * 1024

C_QKV = 0
C_Z = C_QKV + 3 * DN_WIDTH
C_SWQ = C_Z + DN_WIDTH
C_SWK = C_SWQ + SWA_WIDTH
C_SWV = C_SWK + SWA_KV_WIDTH
C_AB = C_SWV + SWA_KV_WIDTH
IN_PAD = C_AB + LANES

TM_IN = 256
DN_ROWS = 256
TM_POST = 512
MOE_ROWS = 256
TM_COMB = 128
TM_DISP = 1024


def _silu(x):
    return x * jax.nn.sigmoid(x)


def _mm(a, b):
    return jnp.dot(a.astype(BF16), b.astype(BF16), preferred_element_type=F32)


def _mm_nt(a, b):
    return lax.dot_general(a.astype(BF16), b.astype(BF16), (((1,), (1,)), ((), ())),
                           preferred_element_type=F32)


def _mm_tn(a, b):
    return lax.dot_general(a.astype(BF16), b.astype(BF16), (((0,), (0,)), ((), ())),
                           preferred_element_type=F32)


def _ada_kernel(c_ref, w_ref, b_ref, o_ref):
    c = c_ref[...]
    o_ref[...] = jnp.dot(_silu(c), w_ref[...], precision=lax.Precision.HIGHEST,
                         preferred_element_type=F32) + b_ref[...]


def _ada(c_pad, w_ada, b_ada):
    n = w_ada.shape[1]
    return pl.pallas_call(
        _ada_kernel,
        grid=(n // D_MODEL,),
        in_specs=[pl.BlockSpec((SUBLANES, D_MODEL), lambda j: (0, 0)),
                  pl.BlockSpec((D_MODEL, D_MODEL), lambda j: (0, j)),
                  pl.BlockSpec((1, D_MODEL), lambda j: (0, j))],
        out_specs=pl.BlockSpec((SUBLANES, D_MODEL), lambda j: (0, j)),
        out_shape=jax.ShapeDtypeStruct((SUBLANES, n), F32),
        name="ada",
    )(c_pad, w_ada, b_ada)


def _rms_heads64(a, w128):
    lane = lax.broadcasted_iota(I32, a.shape, 1)
    lo = lane < SWA_HEAD_DIM
    sq = a * a
    s_lo = jnp.sum(jnp.where(lo, sq, 0.0), axis=-1, keepdims=True)
    s_hi = jnp.sum(jnp.where(lo, 0.0, sq), axis=-1, keepdims=True)
    ms = jnp.where(lo, s_lo, s_hi) * (1.0 / SWA_HEAD_DIM)
    return a * lax.rsqrt(ms + EPS) * w128


def _inproj_kernel(tiles_per_seq, x_ref, mod_ref, n1w_ref, w_ref, convw_ref, gp_ref, qnw_ref,
                   knw_ref, q_ref, k_ref, v_ref, z_ref, gb_ref, swq_ref, swk_ref, swv_ref,
                   conv_scr):
    i = pl.program_id(0)
    tm = x_ref.shape[0]
    x = x_ref[...]
    y = x * lax.rsqrt(jnp.mean(x * x, axis=-1, keepdims=True) + EPS) * n1w_ref[...]
    h = (y * (1.0 + mod_ref[0, 1:2, :]) + mod_ref[0, 0:1, :]).astype(BF16)

    @pl.when(i % tiles_per_seq == 0)
    def _():
        conv_scr[0:SUBLANES, :] = jnp.zeros((SUBLANES, 3 * DN_WIDTH), F32)

    u = jnp.dot(h, w_ref[:, C_QKV:C_Z], preferred_element_type=F32)
    conv_scr[SUBLANES:SUBLANES + tm, :] = u
    cw = convw_ref[...]
    acc = u * cw[3:4, :]
    for j in range(CONV_WIDTH - 1):
        off = SUBLANES - (CONV_WIDTH - 1) + j
        acc = acc + conv_scr[off:off + tm, :] * cw[j:j + 1, :]
    conv_scr[0:SUBLANES, :] = conv_scr[tm:tm + SUBLANES, :]
    qkv = _silu(acc)

    for hh in range(DN_HEADS):
        sl = slice(hh * DN_HEAD_DIM, (hh + 1) * DN_HEAD_DIM)
        qh = qkv[:, hh * DN_HEAD_DIM:(hh + 1) * DN_HEAD_DIM]
        kh = qkv[:, DN_WIDTH + hh * DN_HEAD_DIM:DN_WIDTH + (hh + 1) * DN_HEAD_DIM]
        q_ref[:, sl] = qh * lax.rsqrt(jnp.sum(qh * qh, axis=-1, keepdims=True) + EPS) * (
            DN_HEAD_DIM ** -0.5)
        k_ref[:, sl] = kh * lax.rsqrt(jnp.sum(kh * kh, axis=-1, keepdims=True) + EPS)
    v_ref[...] = qkv[:, 2 * DN_WIDTH:3 * DN_WIDTH]

    z_ref[...] = jnp.dot(h, w_ref[:, C_Z:C_SWQ], preferred_element_type=F32)

    ab = jnp.dot(h, w_ref[:, C_AB:IN_PAD], preferred_element_type=F32)
    lane = lax.broadcasted_iota(I32, ab.shape, 1)
    sp_in = ab + gp_ref[1:2, :]
    softplus = jnp.maximum(sp_in, 0.0) + jnp.log1p(jnp.exp(-jnp.abs(sp_in)))
    gb_ref[...] = jnp.where(lane < DN_HEADS, gp_ref[0:1, :] * softplus, jax.nn.sigmoid(ab))

    sq = jnp.dot(h, w_ref[:, C_SWQ:C_SWK], preferred_element_type=F32)
    for t in range(SWA_WIDTH // LANES):
        sl = slice(t * LANES, (t + 1) * LANES)
        swq_ref[:, sl] = (_rms_heads64(sq[:, sl], qnw_ref[...]) * (SWA_HEAD_DIM ** -0.5)).astype(BF16)
    sk = jnp.dot(h, w_ref[:, C_SWK:C_SWV], preferred_element_type=F32)
    swk_ref[...] = _rms_heads64(sk, knw_ref[...]).astype(BF16)
    swv_ref[...] = jnp.dot(h, w_ref[:, C_SWV:C_AB], preferred_element_type=F32).astype(BF16)


def _inproj(xf, mod, n1w, w_cat, conv_w, gparams, qnw128, knw128, seq):
    t = xf.shape[0]
    tiles_per_seq = seq // TM_IN
    row = lambda i: (i, 0)
    const = lambda i: (0, 0)
    outs = [jax.ShapeDtypeStruct((t, DN_WIDTH), F32)] * 4 + [
        jax.ShapeDtypeStruct((t, LANES), F32),
        jax.ShapeDtypeStruct((t, SWA_WIDTH), BF16),
        jax.ShapeDtypeStruct((t, SWA_KV_WIDTH), BF16),
        jax.ShapeDtypeStruct((t, SWA_KV_WIDTH), BF16)]
    out_specs = [pl.BlockSpec((TM_IN, DN_WIDTH), row)] * 4 + [
        pl.BlockSpec((TM_IN, LANES), row),
        pl.BlockSpec((TM_IN, SWA_WIDTH), row),
        pl.BlockSpec((TM_IN, SWA_KV_WIDTH), row),
        pl.BlockSpec((TM_IN, SWA_KV_WIDTH), row)]
    return pl.pallas_call(
        functools.partial(_inproj_kernel, tiles_per_seq),
        grid=(t // TM_IN,),
        in_specs=[pl.BlockSpec((TM_IN, D_MODEL), row),
                  pl.BlockSpec((1, 6, D_MODEL), lambda i: (i // tiles_per_seq, 0, 0)),
                  pl.BlockSpec((1, D_MODEL), const),
                  pl.BlockSpec((D_MODEL, IN_PAD), const),
                  pl.BlockSpec((CONV_WIDTH, 3 * DN_WIDTH), const),
                  pl.BlockSpec((2, LANES), const),
                  pl.BlockSpec((1, LANES), const),
                  pl.BlockSpec((1, LANES), const)],
        out_specs=out_specs,
        out_shape=outs,
        scratch_shapes=[pltpu.VMEM((TM_IN + SUBLANES, 3 * DN_WIDTH), F32)],
        compiler_params=pltpu.CompilerParams(dimension_semantics=("arbitrary",),
                                             vmem_limit_bytes=VMEM_LIMIT),
        name="inproj",
    )(xf, mod, n1w, w_cat, conv_w, gparams, qnw128, knw128)


def _dn_kernel(q_ref, k_ref, v_ref, z_ref, gb_ref, nw_ref, o_ref,
               s_scr, u_scr, w_scr, qd_scr, kd_scr, a_scr):
    j = pl.program_id(1)
    rows = q_ref.shape[0]
    n_chunks = rows // CHUNK
    heads = range(DN_HEADS)

    @pl.when(j == 0)
    def _():
        s_scr[...] = jnp.zeros(s_scr.shape, F32)

    r = lax.broadcasted_iota(I32, (rows, rows), 0)
    c = lax.broadcasted_iota(I32, (rows, rows), 1)
    same = (r // CHUNK) == (c // CHUNK)
    incl = same & (c <= r)
    strict = same & (c < r)
    gb = gb_ref[...]
    gc = jnp.dot(jnp.where(incl, 1.0, 0.0), gb, precision=lax.Precision.HIGHEST,
                 preferred_element_type=F32)
    gl = jnp.dot(jnp.where(same, 1.0, 0.0), gb, precision=lax.Precision.HIGHEST,
                 preferred_element_type=F32)
    gct = gc.T

    rmat = []
    rhs = []
    for h in heads:
        ls = slice(h * DN_HEAD_DIM, (h + 1) * DN_HEAD_DIM)
        g_col = gc[:, h:h + 1]
        beta = gb[:, DN_HEADS + h:DN_HEADS + h + 1]
        decay = jnp.where(incl, jnp.exp(jnp.where(incl, g_col - gct[h:h + 1, :], 0.0)), 0.0)
        qq = q_ref[:, ls]
        kk = k_ref[:, ls]
        kb = kk * beta
        eg = jnp.exp(g_col)
        rmat.append(-jnp.where(strict, _mm_nt(kb, kk) * decay, 0.0))
        rhs.append(jnp.concatenate([v_ref[:, ls] * beta, kb * eg], axis=1))
        a_scr[h] = jnp.where(incl, _mm_nt(qq, kk) * decay, 0.0)
        qd_scr[:, ls] = qq * eg
        kd_scr[:, ls] = kk * jnp.exp(gl[:, h:h + 1] - g_col)

    xm = list(rmat)
    pm = list(rmat)
    for it in range(6):
        xm = [xm[h] + _mm(pm[h], xm[h]) for h in heads]
        if it < 5:
            pm = [_mm(pm[h], pm[h]) for h in heads]
    for h in heads:
        ls = slice(h * DN_HEAD_DIM, (h + 1) * DN_HEAD_DIM)
        sol = rhs[h] + _mm(xm[h], rhs[h])
        u_scr[:, ls] = sol[:, 0:DN_HEAD_DIM]
        w_scr[:, ls] = sol[:, DN_HEAD_DIM:2 * DN_HEAD_DIM]

    nw = nw_ref[...]
    lsl = [slice(h * DN_HEAD_DIM, (h + 1) * DN_HEAD_DIM) for h in heads]
    for ci in range(n_chunks):
        rs = slice(ci * CHUNK, (ci + 1) * CHUNK)
        dvals = jnp.exp(gl[ci * CHUNK:ci * CHUNK + 1, :])
        st = [s_scr[h] for h in heads]
        ws = [_mm(w_scr[rs, lsl[h]], st[h]) for h in heads]
        qs = [_mm(qd_scr[rs, lsl[h]], st[h]) for h in heads]
        vn = [u_scr[rs, lsl[h]] - ws[h] for h in heads]
        av = [_mm(a_scr[h, rs, rs], vn[h]) for h in heads]
        kv = [_mm_tn(kd_scr[rs, lsl[h]], vn[h]) for h in heads]
        for h in heads:
            s_scr[h] = st[h] * dvals[:, h:h + 1] + kv[h]
            o = qs[h] + av[h]
            on = o * lax.rsqrt(jnp.mean(o * o, axis=-1, keepdims=True) + EPS) * nw
            o_ref[rs, lsl[h]] = (on * _silu(z_ref[rs, lsl[h]])).astype(o_ref.dtype)


def _deltanet(q, k, v, z, gb, nw, batch, seq):
    t = q.shape[0]
    steps = seq // DN_ROWS
    row = lambda b, j: (b * steps + j, 0)
    blk = pl.BlockSpec((DN_ROWS, DN_WIDTH), row)
    return pl.pallas_call(
        _dn_kernel,
        grid=(batch, steps),
        in_specs=[blk, blk, blk, blk,
                  pl.BlockSpec((DN_ROWS, LANES), row),
                  pl.BlockSpec((1, DN_HEAD_DIM), lambda b, j: (0, 0))],
        out_specs=blk,
        out_shape=jax.ShapeDtypeStruct((t, DN_WIDTH), BF16),
        scratch_shapes=[pltpu.VMEM((DN_HEADS, DN_HEAD_DIM, DN_HEAD_DIM), F32),
                        pltpu.VMEM((DN_ROWS, DN_WIDTH), F32),
                        pltpu.VMEM((DN_ROWS, DN_WIDTH), F32),
                        pltpu.VMEM((DN_ROWS, DN_WIDTH), F32),
                        pltpu.VMEM((DN_ROWS, DN_WIDTH), F32),
                        pltpu.VMEM((DN_HEADS, DN_ROWS, DN_ROWS), F32)],
        compiler_params=pltpu.CompilerParams(dimension_semantics=("arbitrary", "arbitrary"),
                                             vmem_limit_bytes=VMEM_LIMIT),
        name="deltanet",
    )(q, k, v, z, gb, nw)


def _swa_kernel(sink_ref, q_ref, kp_ref, kc_ref, vp_ref, vc_ref, o_ref):
    n = pl.program_id(1)
    w = WINDOW
    kcat = jnp.concatenate([kp_ref[...], kc_ref[...]], axis=0)
    vcat = jnp.concatenate([vp_ref[...], vc_ref[...]], axis=0)
    rows = SWA_GROUP * w
    ri = lax.broadcasted_iota(I32, (rows, 2 * w), 0)
    kj = lax.broadcasted_iota(I32, (rows, 2 * w), 1)
    grp = ri // w
    rel = (ri - grp * w) + w - kj
    valid = (rel >= 0) & (rel < w) & ((kj >= w) | (n > 0))
    relf = rel.astype(F32)
    grp_col = lax.broadcasted_iota(I32, (rows, 1), 0) // w
    q = q_ref[...]
    outs = []
    for kh in range(SWA_KV_HEADS):
        slope_col = jnp.zeros((rows, 1), F32)
        sink_col = jnp.zeros((rows, 1), F32)
        for g in range(SWA_GROUP):
            hd = kh * SWA_GROUP + g
            slope_col = jnp.where(grp_col == g, 2.0 ** (-8.0 * (hd + 1) / SWA_HEADS), slope_col)
            sink_col = jnp.where(grp_col == g, sink_ref[hd], sink_col)
        qs = jnp.concatenate(
            [q[:, (kh * SWA_GROUP + g) * SWA_HEAD_DIM:(kh * SWA_GROUP + g + 1) * SWA_HEAD_DIM]
             for g in range(SWA_GROUP)], axis=0)
        kk = kcat[:, kh * SWA_HEAD_DIM:(kh + 1) * SWA_HEAD_DIM]
        vv = vcat[:, kh * SWA_HEAD_DIM:(kh + 1) * SWA_HEAD_DIM]
        s = lax.dot_general(qs, kk, (((1,), (1,)), ((), ())), preferred_element_type=F32)
        logits = jnp.where(valid, s - slope_col * relf, NEG)
        m = jnp.maximum(jnp.max(logits, axis=-1, keepdims=True), sink_col)
        p = jnp.exp(logits - m)
        denom = jnp.sum(p, axis=-1, keepdims=True) + jnp.exp(sink_col - m)
        o = jnp.dot(p.astype(BF16), vv, preferred_element_type=F32) / denom
        for g in range(SWA_GROUP):
            outs.append(o[g * w:(g + 1) * w, :])
    for t in range(SWA_WIDTH // LANES):
        o_ref[:, t * LANES:(t + 1) * LANES] = jnp.concatenate(
            [outs[2 * t], outs[2 * t + 1]], axis=-1).astype(o_ref.dtype)


def _swa(swq, swk, swv, sinks, batch, seq):
    t = swq.shape[0]
    nb = seq // WINDOW
    cur = lambda b, n, s: (b * nb + n, 0)
    prev = lambda b, n, s: (b * nb + jnp.maximum(n - 1, 0), 0)
    return pl.pallas_call(
        _swa_kernel,
        grid_spec=pltpu.PrefetchScalarGridSpec(
            num_scalar_prefetch=1,
            grid=(batch, nb),
            in_specs=[pl.BlockSpec((WINDOW, SWA_WIDTH), cur),
                      pl.BlockSpec((WINDOW, SWA_KV_WIDTH), prev),
                      pl.BlockSpec((WINDOW, SWA_KV_WIDTH), cur),
                      pl.BlockSpec((WINDOW, SWA_KV_WIDTH), prev),
                      pl.BlockSpec((WINDOW, SWA_KV_WIDTH), cur)],
            out_specs=pl.BlockSpec((WINDOW, SWA_WIDTH), cur)),
        out_shape=jax.ShapeDtypeStruct((t, SWA_WIDTH), BF16),
        compiler_params=pltpu.CompilerParams(dimension_semantics=("arbitrary", "arbitrary"),
                                             vmem_limit_bytes=VMEM_LIMIT),
        name="swa",
    )(sinks, swq, swk, swk, swv, swv)


def _post_kernel(x_ref, odn_ref, osw_ref, mod_ref, wo_ref, n2w_ref, wr_ref, br_ref,
                 x1_ref, h2_ref, idx_ref, gate_ref, rank_ref, cnt_ref, run_scr):
    i = pl.program_id(0)
    tm = x_ref.shape[0]

    @pl.when(i == 0)
    def _():
        run_scr[...] = jnp.zeros(run_scr.shape, F32)

    y = (jnp.dot(odn_ref[...], wo_ref[0:DN_WIDTH, :], preferred_element_type=F32)
         + jnp.dot(osw_ref[...], wo_ref[DN_WIDTH:, :], preferred_element_type=F32))
    x1 = x_ref[...] + mod_ref[0, 2:3, :] * y
    x1_ref[...] = x1
    hn = x1 * lax.rsqrt(jnp.mean(x1 * x1, axis=-1, keepdims=True) + EPS) * n2w_ref[...]
    h2 = hn * (1.0 + mod_ref[0, 4:5, :]) + mod_ref[0, 3:4, :]
    h2_ref[...] = h2

    lane = lax.broadcasted_iota(I32, (tm, LANES), 1)
    lane_f = lane.astype(F32)
    logits = jnp.dot(h2.astype(BF16), wr_ref[...], preferred_element_type=F32) + br_ref[...]
    work = jnp.where(lane < N_EXPERTS, logits, NEG)
    sel = jnp.zeros((tm, LANES), F32)
    idx_out = jnp.zeros((tm, LANES), I32)
    vals = []
    idxs = []
    for kk in range(TOP_K):
        m = jnp.max(work, axis=-1, keepdims=True)
        ik = jnp.min(jnp.where(work == m, lane_f, float(LANES)), axis=-1,
                     keepdims=True).astype(I32)
        hit = lane == ik
        work = jnp.where(hit, NEG, work)
        sel = jnp.where(hit, 1.0, sel)
        idx_out = jnp.where(lane == kk, ik, idx_out)
        vals.append(m)
        idxs.append(ik)
    es = [jnp.exp(v - vals[0]) for v in vals]
    tot = es[0] + es[1] + es[2] + es[3]
    gate_out = jnp.zeros((tm, LANES), F32)
    for kk in range(TOP_K):
        gate_out = jnp.where(lane == kk, es[kk] / tot, gate_out)
    idx_ref[...] = idx_out
    gate_ref[...] = gate_out

    r = lax.broadcasted_iota(I32, (tm, tm), 0)
    c = lax.broadcasted_iota(I32, (tm, tm), 1)
    before = jnp.where(c < r, 1.0, 0.0).astype(BF16)
    base = jnp.dot(before, sel.astype(BF16), preferred_element_type=F32) + run_scr[0:1, :]
    rank_out = jnp.zeros((tm, LANES), I32)
    for kk in range(TOP_K):
        rk = jnp.sum(jnp.where(lane == idxs[kk], base, 0.0), axis=-1, keepdims=True)
        rank_out = jnp.where(lane == kk, rk.astype(I32), rank_out)
    rank_ref[...] = rank_out
    run = run_scr[0:1, :] + jnp.sum(sel, axis=0, keepdims=True)
    run_scr[0:1, :] = run
    cnt_ref[...] = jnp.broadcast_to(run, cnt_ref.shape).astype(I32)


def _post(xf, odn, osw, mod, w_out, n2w, wr_pad, br_pad, seq):
    t = xf.shape[0]
    tiles_per_seq = seq // TM_POST
    row = lambda i: (i, 0)
    const = lambda i: (0, 0)
    return pl.pallas_call(
        _post_kernel,
        grid=(t // TM_POST,),
        in_specs=[pl.BlockSpec((TM_POST, D_MODEL), row),
                  pl.BlockSpec((TM_POST, DN_WIDTH), row),
                  pl.BlockSpec((TM_POST, SWA_WIDTH), row),
                  pl.BlockSpec((1, 6, D_MODEL), lambda i: (i // tiles_per_seq, 0, 0)),
                  pl.BlockSpec((D_MODEL, D_MODEL), const),
                  pl.BlockSpec((1, D_MODEL), const),
                  pl.BlockSpec((D_MODEL, LANES), const),
                  pl.BlockSpec((1, LANES), const)],
        out_specs=[pl.BlockSpec((TM_POST, D_MODEL), row),
                   pl.BlockSpec((TM_POST, D_MODEL), row),
                   pl.BlockSpec((TM_POST, LANES), row),
                   pl.BlockSpec((TM_POST, LANES), row),
                   pl.BlockSpec((TM_POST, LANES), row),
                   pl.BlockSpec((SUBLANES, LANES), const)],
        out_shape=[jax.ShapeDtypeStruct((t, D_MODEL), F32),
                   jax.ShapeDtypeStruct((t, D_MODEL), F32),
                   jax.ShapeDtypeStruct((t, LANES), I32),
                   jax.ShapeDtypeStruct((t, LANES), F32),
                   jax.ShapeDtypeStruct((t, LANES), I32),
                   jax.ShapeDtypeStruct((SUBLANES, LANES), I32)],
        scratch_shapes=[pltpu.VMEM((SUBLANES, LANES), F32)],
        compiler_params=pltpu.CompilerParams(dimension_semantics=("arbitrary",),
                                             vmem_limit_bytes=VMEM_LIMIT),
        name="post",
    )(xf, odn, osw, mod, w_out, n2w, wr_pad, br_pad)


def _row_copy(src_ref, s, dst_ref, d, sem):
    return pltpu.make_async_copy(src_ref.at[pl.ds(s, 1), :], dst_ref.at[pl.ds(d, 1), :], sem)


def _dispatch_kernel(dest_ref, fill_ref, h_ref, zero_ref, xs_ref, sem):
    i = pl.program_id(0)
    n_tok = h_ref.shape[0]

    def issue(tk, carry):
        for kk in range(TOP_K):
            _row_copy(h_ref, tk, xs_ref, dest_ref[0, 0, tk * TOP_K + kk], sem).start()
        return carry

    lax.fori_loop(0, n_tok, issue, 0, unroll=4)

    for kk in range(TOP_K):
        pltpu.make_async_copy(h_ref, xs_ref.at[pl.ds(0, n_tok), :], sem).wait()

    @pl.when(i == 0)
    def _():
        def per_expert(e, carry):
            lo = fill_ref[0, e]
            hi = fill_ref[1, e]

            def z_issue(rw, c2):
                _row_copy(zero_ref, 0, xs_ref, rw, sem).start()
                return c2

            lax.fori_loop(lo, hi, z_issue, 0)

            def z_wait(rw, c2):
                _row_copy(zero_ref, 0, xs_ref, 0, sem).wait()
                return c2

            lax.fori_loop(lo, hi, z_wait, 0)
            return carry

        lax.fori_loop(0, N_EXPERTS, per_expert, 0)

        def tail(b, carry):
            cp = pltpu.make_async_copy(
                zero_ref, xs_ref.at[pl.ds(pl.multiple_of(b * MOE_ROWS, MOE_ROWS), MOE_ROWS), :], sem)
            cp.start()
            cp.wait()
            return carry

        lax.fori_loop(fill_ref[2, 0], xs_ref.shape[0] // MOE_ROWS, tail, 0)


def _dispatch(dest3, fill, h2, zero_row, n_pad):
    t = h2.shape[0]
    return pl.pallas_call(
        _dispatch_kernel,
        grid=(t // TM_DISP,),
        in_specs=[pl.BlockSpec((1, 1, TM_DISP * TOP_K), lambda i: (i, 0, 0),
                               memory_space=pltpu.SMEM),
                  pl.BlockSpec(memory_space=pltpu.SMEM),
                  pl.BlockSpec((TM_DISP, D_MODEL), lambda i: (i, 0)),
                  pl.BlockSpec(memory_space=pl.ANY)],
        out_specs=pl.BlockSpec(memory_space=pl.ANY),
        out_shape=jax.ShapeDtypeStruct((n_pad, D_MODEL), F32),
        scratch_shapes=[pltpu.SemaphoreType.DMA(())],
        compiler_params=pltpu.CompilerParams(dimension_semantics=("arbitrary",),
                                             vmem_limit_bytes=VMEM_LIMIT),
        name="dispatch",
    )(dest3, fill, h2, zero_row)


def _moe_kernel(be_ref, nv_ref, xs_ref, wu_ref, bu_ref, wd_ref, bd_ref, ys_ref, wu_bf, wd_bf):
    i = pl.program_id(0)

    @pl.when(i < nv_ref[0])
    def _():
        prev = be_ref[jnp.maximum(i - 1, 0)]

        @pl.when((i == 0) | (be_ref[i] != prev))
        def _():
            wu_bf[...] = wu_ref[0].astype(BF16)
            wd_bf[...] = wd_ref[0].astype(BF16)

        hu = jnp.dot(xs_ref[...].astype(BF16), wu_bf[...], preferred_element_type=F32) + bu_ref[0]
        x_glu = jnp.minimum(hu[:, :D_FF], SWIGLU_LIMIT)
        x_lin = jnp.clip(hu[:, D_FF:], -SWIGLU_LIMIT, SWIGLU_LIMIT)
        act = x_glu * jax.nn.sigmoid(SWIGLU_ALPHA * x_glu) * (x_lin + 1.0)
        ys_ref[...] = jnp.dot(act.astype(BF16), wd_bf[...], preferred_element_type=F32) + bd_ref[0]

    @pl.when(i >= nv_ref[0])
    def _():
        ys_ref[...] = jnp.zeros(ys_ref.shape, F32)


def _moe(block_e, n_valid, xs, w_up, b_up, w_down, b_down):
    n_pad = xs.shape[0]
    n_blocks = n_pad // MOE_ROWS
    rows = lambda i, be, nv: (jnp.minimum(i, nv[0] - 1), 0)
    out_rows = lambda i, be, nv: (i, 0)
    expert = lambda i, be, nv: (be[i], 0, 0)
    return pl.pallas_call(
        _moe_kernel,
        grid_spec=pltpu.PrefetchScalarGridSpec(
            num_scalar_prefetch=2,
            grid=(n_blocks,),
            in_specs=[pl.BlockSpec((MOE_ROWS, D_MODEL), rows),
                      pl.BlockSpec((1, D_MODEL, 2 * D_FF), expert),
                      pl.BlockSpec((1, 1, 2 * D_FF), expert),
                      pl.BlockSpec((1, D_FF, D_MODEL), expert),
                      pl.BlockSpec((1, 1, D_MODEL), expert)],
            out_specs=pl.BlockSpec((MOE_ROWS, D_MODEL), out_rows),
            scratch_shapes=[pltpu.VMEM((D_MODEL, 2 * D_FF), BF16),
                            pltpu.VMEM((D_FF, D_MODEL), BF16)]),
        out_shape=jax.ShapeDtypeStruct((n_pad, D_MODEL), F32),
        compiler_params=pltpu.CompilerParams(dimension_semantics=("arbitrary",),
                                             vmem_limit_bytes=VMEM_LIMIT),
        name="moe",
    )(block_e, n_valid, xs, w_up, b_up, w_down, b_down)


def _combine_kernel(dcur_ref, dnext_ref, ys_ref, x1_ref, gate_ref, mod_ref, o_ref, buf, sem):
    i = pl.program_id(0)
    tm = x1_ref.shape[0]
    slot = i % 2

    def issue(dref, s):
        def body(tk, carry):
            for kk in range(TOP_K):
                _row_copy(ys_ref, dref[0, 0, tk * TOP_K + kk], buf.at[s], kk * tm + tk,
                          sem.at[s]).start()
            return carry

        lax.fori_loop(0, tm, body, 0, unroll=4)

    @pl.when(i == 0)
    def _():
        issue(dcur_ref, 0)

    @pl.when(i + 1 < pl.num_programs(0))
    def _():
        issue(dnext_ref, 1 - slot)

    pltpu.make_async_copy(ys_ref.at[pl.ds(0, TOP_K * tm), :], buf.at[slot], sem.at[slot]).wait()

    gates = gate_ref[...]
    acc = jnp.zeros((tm, D_MODEL), F32)
    for kk in range(TOP_K):
        acc = acc + gates[:, kk:kk + 1] * buf[slot, kk * tm:(kk + 1) * tm, :]
    o_ref[...] = x1_ref[...] + mod_ref[0, 5:6, :] * acc


def _combine(dest3, ys, x1, gates, mod, seq):
    t = x1.shape[0]
    tiles_per_seq = seq // TM_COMB
    n_tiles = t // TM_COMB
    row = lambda i: (i, 0)
    return pl.pallas_call(
        _combine_kernel,
        grid=(n_tiles,),
        in_specs=[pl.BlockSpec((1, 1, TM_COMB * TOP_K), lambda i: (i, 0, 0),
                               memory_space=pltpu.SMEM),
                  pl.BlockSpec((1, 1, TM_COMB * TOP_K),
                               lambda i: (jnp.minimum(i + 1, n_tiles - 1), 0, 0),
                               memory_space=pltpu.SMEM),
                  pl.BlockSpec(memory_space=pl.ANY),
                  pl.BlockSpec((TM_COMB, D_MODEL), row),
                  pl.BlockSpec((TM_COMB, LANES), row),
                  pl.BlockSpec((1, 6, D_MODEL), lambda i: (i // tiles_per_seq, 0, 0))],
        out_specs=pl.BlockSpec((TM_COMB, D_MODEL), row),
        out_shape=jax.ShapeDtypeStruct((t, D_MODEL), F32),
        scratch_shapes=[pltpu.VMEM((2, TOP_K * TM_COMB, D_MODEL), F32),
                        pltpu.SemaphoreType.DMA((2,))],
        compiler_params=pltpu.CompilerParams(dimension_semantics=("arbitrary",),
                                             vmem_limit_bytes=VMEM_LIMIT),
        name="combine",
    )(dest3, dest3, ys, x1, gates, mod)


def _layer(x, mod, norm1_w, w_in, conv_w, a_log, dt_bias, dn_norm_w, q_norm_w, k_norm_w, sinks,
           w_out, norm2_w, w_router, b_router, w_up, b_up, w_down, b_down):
    batch, seq, _ = x.shape
    t = batch * seq
    xf = x.reshape(t, D_MODEL)

    o_qkv, o_z, o_a, o_b, o_q, o_k, o_v = 0, 1536, 2048, 2052, 2056, 2568, 2696
    w_ab = jnp.zeros((D_MODEL, LANES), F32).at[:, 0:2 * DN_HEADS].set(w_in[:, o_a:o_q])
    w_cat = jnp.concatenate([w_in[:, o_qkv:o_z], w_in[:, o_z:o_a], w_in[:, o_q:o_k],
                             w_in[:, o_k:o_v], w_in[:, o_v:], w_ab], axis=1).astype(BF16)
    gparams = jnp.zeros((2, LANES), F32)
    gparams = gparams.at[0, 0:DN_HEADS].set(-jnp.exp(a_log.astype(F32)))
    gparams = gparams.at[1, 0:DN_HEADS].set(dt_bias.astype(F32))
    qnw128 = jnp.tile(q_norm_w.astype(F32), 2)[None, :]
    knw128 = jnp.tile(k_norm_w.astype(F32), 2)[None, :]

    q, k, v, z, gb, swq, swk, swv = _inproj(xf, mod, norm1_w[None, :], w_cat, conv_w, gparams,
                                            qnw128, knw128, seq)
    o_dn = _deltanet(q, k, v, z, gb, dn_norm_w[None, :], batch, seq)
    o_sw = _swa(swq, swk, swv, sinks.astype(F32), batch, seq)

    wr_pad = jnp.zeros((D_MODEL, LANES), F32).at[:, 0:N_EXPERTS].set(w_router).astype(BF16)
    br_pad = jnp.zeros((1, LANES), F32).at[0, 0:N_EXPERTS].set(b_router)
    x1, h2, top_idx, gates, rank, cnt = _post(xf, o_dn, o_sw, mod, w_out.astype(BF16),
                                              norm2_w[None, :], wr_pad, br_pad, seq)

    counts = cnt[0, 0:N_EXPERTS]
    padded = (counts + MOE_ROWS - 1) // MOE_ROWS * MOE_ROWS
    pend = jnp.cumsum(padded)
    pstart = pend - padded
    n_blocks = (t * TOP_K) // MOE_ROWS + N_EXPERTS
    n_pad = n_blocks * MOE_ROWS
    n_valid = (pend[-1] // MOE_ROWS).astype(I32)
    blk = jnp.minimum(jnp.arange(n_blocks, dtype=I32), n_valid - 1)
    block_e = jnp.minimum(jnp.sum(pend[None, :] <= (blk * MOE_ROWS)[:, None], axis=1),
                          N_EXPERTS - 1).astype(I32)
    dest = (pstart[top_idx[:, 0:TOP_K]] + rank[:, 0:TOP_K]).astype(I32)
    fill = jnp.stack([pstart + counts, pend, jnp.full((N_EXPERTS,), n_valid)]).astype(I32)

    xs = _dispatch(dest.reshape(t // TM_DISP, 1, TM_DISP * TOP_K), fill, h2,
                   jnp.zeros((MOE_ROWS, D_MODEL), F32), n_pad)
    ys = _moe(block_e, n_valid.reshape(1), xs, w_up, b_up[:, None, :], w_down, b_down[:, None, :])
    out = _combine(dest.reshape(t // TM_COMB, 1, TM_COMB * TOP_K), ys, x1, gates, mod, seq)
    return out.reshape(batch, seq, D_MODEL)


def kernel(x, c, w_ada, b_ada, norm1_w, w_in, conv_w, a_log, dt_bias, dn_norm_w, q_norm_w,
           k_norm_w, sinks, w_out, norm2_w, w_router, b_router, w_up, b_up, w_down, b_down):
    out_dtype = x.dtype
    batch = x.shape[0]
    depth = w_ada.shape[0]
    c_pad = jnp.zeros((SUBLANES, D_MODEL), F32).at[0:batch].set(c.astype(F32))
    for l in range(depth):
        mod = _ada(c_pad, w_ada[l], b_ada[l][None, :])[0:batch].reshape(batch, 6, D_MODEL)
        x = _layer(x, mod, norm1_w[l], w_in[l], conv_w[l], a_log[l], dt_bias[l], dn_norm_w[l],
                   q_norm_w[l], k_norm_w[l], sinks[l], w_out[l], norm2_w[l], w_router[l],
                   b_router[l], w_up[l], b_up[l], w_down[l], b_down[l])
    return x.astype(out_dtype)
```

```python
import functools

import jax
import jax.numpy as jnp
from jax import lax
from jax.experimental import pallas as pl
from jax.experimental.pallas import tpu as pltpu

F32 = jnp.float32
BF16 = jnp.bfloat16
I32 = jnp.int32

D_MODEL = 1024
DN_HEADS = 4
DN_HEAD_DIM = 128
DN_WIDTH = DN_HEADS * DN_HEAD_DIM
CONV_WIDTH = 4
CHUNK = 64
SWA_HEADS = 8
SWA_KV_HEADS = 2
SWA_GROUP = SWA_HEADS // SWA_KV_HEADS
SWA_HEAD_DIM = 64
SWA_WIDTH = SWA_HEADS * SWA_HEAD_DIM
SWA_KV_WIDTH = SWA_KV_HEADS * SWA_HEAD_DIM
WINDOW = 128
N_EXPERTS = 32
TOP_K = 4
D_FF = D_MODEL
SWIGLU_ALPHA = 1.702
SWIGLU_LIMIT = 7.0
EPS = 1e-6
NEG = -1e30

LANES = 128
SUBLANES = 8
VMEM_LIMIT = 56 * 1024 * 1024

C_QKV = 0
C_Z = C_QKV + 3 * DN_WIDTH
C_SWQ = C_Z + DN_WIDTH
C_SWK = C_SWQ + SWA_WIDTH
C_SWV = C_SWK + SWA_KV_WIDTH
C_AB = C_SWV + SWA_KV_WIDTH
IN_PAD = C_AB + LANES

TM_IN = 256
DN_ROWS = 256
TM_POST = 512
MOE_ROWS = 256
TM_COMB = 128
TM_DISP = 1024


def _silu(x):
    return x * jax.nn.sigmoid(x)


def _mm(a, b):
    return jnp.dot(a.astype(BF16), b.astype(BF16), preferred_element_type=F32)


def _mm_nt(a, b):
    return lax.dot_general(a.astype(BF16), b.astype(BF16), (((1,), (1,)), ((), ())),
                           preferred_element_type=F32)


def _mm_tn(a, b):
    return lax.dot_general(a.astype(BF16), b.astype(BF16), (((0,), (0,)), ((), ())),
                           preferred_element_type=F32)


def _load_token_tiles(ref, row0, rows):
    return jnp.concatenate(
        [ref[pl.ds(row0 * SUBLANES + s, rows, stride=SUBLANES), :] for s in range(D_MODEL // LANES)],
        axis=1)


def _store_token_tiles(ref, val):
    rows = val.shape[0]
    for s in range(D_MODEL // LANES):
        ref[pl.ds(s, rows, stride=SUBLANES), :] = val[:, s * LANES:(s + 1) * LANES]


def _token_copy(src_ref, s, dst_ref, d, sem):
    return pltpu.make_async_copy(
        src_ref.at[pl.ds(pl.multiple_of(s * SUBLANES, SUBLANES), SUBLANES), :],
        dst_ref.at[pl.ds(pl.multiple_of(d * SUBLANES, SUBLANES), SUBLANES), :], sem)


def _ada_kernel(c_ref, w_ref, b_ref, o_ref):
    c = c_ref[...]
    o_ref[...] = jnp.dot(_silu(c), w_ref[...], precision=lax.Precision.HIGHEST,
                         preferred_element_type=F32) + b_ref[...]


def _ada(c_pad, w_ada, b_ada):
    n = w_ada.shape[1]
    return pl.pallas_call(
        _ada_kernel,
        grid=(n // D_MODEL,),
        in_specs=[pl.BlockSpec((SUBLANES, D_MODEL), lambda j: (0, 0)),
                  pl.BlockSpec((D_MODEL, D_MODEL), lambda j: (0, j)),
                  pl.BlockSpec((1, D_MODEL), lambda j: (0, j))],
        out_specs=pl.BlockSpec((SUBLANES, D_MODEL), lambda j: (0, j)),
        out_shape=jax.ShapeDtypeStruct((SUBLANES, n), F32),
        name="ada",
    )(c_pad, w_ada, b_ada)


def _rms_heads64(a, w128):
    lane = lax.broadcasted_iota(I32, a.shape, 1)
    lo = lane < SWA_HEAD_DIM
    sq = a * a
    s_lo = jnp.sum(jnp.where(lo, sq, 0.0), axis=-1, keepdims=True)
    s_hi = jnp.sum(jnp.where(lo, 0.0, sq), axis=-1, keepdims=True)
    ms = jnp.where(lo, s_lo, s_hi) * (1.0 / SWA_HEAD_DIM)
    return a * lax.rsqrt(ms + EPS) * w128


def _inproj_kernel(tiles_per_seq, x_ref, mod_ref, n1w_ref, w_ref, convw_ref, gp_ref, qnw_ref,
                   knw_ref, q_ref, k_ref, v_ref, z_ref, gb_ref, swq_ref, swk_ref, swv_ref,
                   conv_scr):
    i = pl.program_id(0)
    tm = x_ref.shape[0]
    x = x_ref[...]
    y = x * lax.rsqrt(jnp.mean(x * x, axis=-1, keepdims=True) + EPS) * n1w_ref[...]
    h = (y * (1.0 + mod_ref[0, 1:2, :]) + mod_ref[0, 0:1, :]).astype(BF16)

    @pl.when(i % tiles_per_seq == 0)
    def _():
        conv_scr[0:SUBLANES, :] = jnp.zeros((SUBLANES, 3 * DN_WIDTH), F32)

    u = jnp.dot(h, w_ref[:, C_QKV:C_Z], preferred_element_type=F32)
    conv_scr[SUBLANES:SUBLANES + tm, :] = u
    cw = convw_ref[...]
    acc = u * cw[3:4, :]
    for j in range(CONV_WIDTH - 1):
        off = SUBLANES - (CONV_WIDTH - 1) + j
        acc = acc + conv_scr[off:off + tm, :] * cw[j:j + 1, :]
    conv_scr[0:SUBLANES, :] = conv_scr[tm:tm + SUBLANES, :]
    qkv = _silu(acc)

    for hh in range(DN_HEADS):
        sl = slice(hh * DN_HEAD_DIM, (hh + 1) * DN_HEAD_DIM)
        qh = qkv[:, hh * DN_HEAD_DIM:(hh + 1) * DN_HEAD_DIM]
        kh = qkv[:, DN_WIDTH + hh * DN_HEAD_DIM:DN_WIDTH + (hh + 1) * DN_HEAD_DIM]
        q_ref[:, sl] = qh * lax.rsqrt(jnp.sum(qh * qh, axis=-1, keepdims=True) + EPS) * (
            DN_HEAD_DIM ** -0.5)
        k_ref[:, sl] = kh * lax.rsqrt(jnp.sum(kh * kh, axis=-1, keepdims=True) + EPS)
    v_ref[...] = qkv[:, 2 * DN_WIDTH:3 * DN_WIDTH]

    z_ref[...] = jnp.dot(h, w_ref[:, C_Z:C_SWQ], preferred_element_type=F32)

    ab = jnp.dot(h, w_ref[:, C_AB:IN_PAD], preferred_element_type=F32)
    lane = lax.broadcasted_iota(I32, ab.shape, 1)
    sp_in = ab + gp_ref[1:2, :]
    softplus = jnp.maximum(sp_in, 0.0) + jnp.log1p(jnp.exp(-jnp.abs(sp_in)))
    gb_ref[...] = jnp.where(lane < DN_HEADS, gp_ref[0:1, :] * softplus, jax.nn.sigmoid(ab))

    sq = jnp.dot(h, w_ref[:, C_SWQ:C_SWK], preferred_element_type=F32)
    for t in range(SWA_WIDTH // LANES):
        sl = slice(t * LANES, (t + 1) * LANES)
        swq_ref[:, sl] = (_rms_heads64(sq[:, sl], qnw_ref[...]) * (SWA_HEAD_DIM ** -0.5)).astype(BF16)
    sk = jnp.dot(h, w_ref[:, C_SWK:C_SWV], preferred_element_type=F32)
    swk_ref[...] = _rms_heads64(sk, knw_ref[...]).astype(BF16)
    swv_ref[...] = jnp.dot(h, w_ref[:, C_SWV:C_AB], preferred_element_type=F32).astype(BF16)


def _inproj(xf, mod, n1w, w_cat, conv_w, gparams, qnw128, knw128, seq):
    t = xf.shape[0]
    tiles_per_seq = seq // TM_IN
    row = lambda i: (i, 0)
    const = lambda i: (0, 0)
    outs = [jax.ShapeDtypeStruct((t, DN_WIDTH), F32)] * 4 + [
        jax.ShapeDtypeStruct((t, LANES), F32),
        jax.ShapeDtypeStruct((t, SWA_WIDTH), BF16),
        jax.ShapeDtypeStruct((t, SWA_KV_WIDTH), BF16),
        jax.ShapeDtypeStruct((t, SWA_KV_WIDTH), BF16)]
    out_specs = [pl.BlockSpec((TM_IN, DN_WIDTH), row)] * 4 + [
        pl.BlockSpec((TM_IN, LANES), row),
        pl.BlockSpec((TM_IN, SWA_WIDTH), row),
        pl.BlockSpec((TM_IN, SWA_KV_WIDTH), row),
        pl.BlockSpec((TM_IN, SWA_KV_WIDTH), row)]
    return pl.pallas_call(
        functools.partial(_inproj_kernel, tiles_per_seq),
        grid=(t // TM_IN,),
        in_specs=[pl.BlockSpec((TM_IN, D_MODEL), row),
                  pl.BlockSpec((1, 6, D_MODEL), lambda i: (i // tiles_per_seq, 0, 0)),
                  pl.BlockSpec((1, D_MODEL), const),
                  pl.BlockSpec((D_MODEL, IN_PAD), const),
                  pl.BlockSpec((CONV_WIDTH, 3 * DN_WIDTH), const),
                  pl.BlockSpec((2, LANES), const),
                  pl.BlockSpec((1, LANES), const),
                  pl.BlockSpec((1, LANES), const)],
        out_specs=out_specs,
        out_shape=outs,
        scratch_shapes=[pltpu.VMEM((TM_IN + SUBLANES, 3 * DN_WIDTH), F32)],
        compiler_params=pltpu.CompilerParams(dimension_semantics=("arbitrary",),
                                             vmem_limit_bytes=VMEM_LIMIT),
        name="inproj",
    )(xf, mod, n1w, w_cat, conv_w, gparams, qnw128, knw128)


def _dn_kernel(q_ref, k_ref, v_ref, z_ref, gb_ref, nw_ref, o_ref,
               s_scr, u_scr, w_scr, qd_scr, kd_scr, a_scr):
    j = pl.program_id(1)
    rows = q_ref.shape[0]
    n_chunks = rows // CHUNK
    heads = range(DN_HEADS)

    @pl.when(j == 0)
    def _():
        s_scr[...] = jnp.zeros(s_scr.shape, F32)

    r = lax.broadcasted_iota(I32, (rows, rows), 0)
    c = lax.broadcasted_iota(I32, (rows, rows), 1)
    same = (r // CHUNK) == (c // CHUNK)
    incl = same & (c <= r)
    strict = same & (c < r)
    gb = gb_ref[...]
    gc = jnp.dot(jnp.where(incl, 1.0, 0.0), gb, precision=lax.Precision.HIGHEST,
                 preferred_element_type=F32)
    gl = jnp.dot(jnp.where(same, 1.0, 0.0), gb, precision=lax.Precision.HIGHEST,
                 preferred_element_type=F32)
    gct = gc.T

    rmat = []
    rhs = []
    for h in heads:
        ls = slice(h * DN_HEAD_DIM, (h + 1) * DN_HEAD_DIM)
        g_col = gc[:, h:h + 1]
        beta = gb[:, DN_HEADS + h:DN_HEADS + h + 1]
        decay = jnp.where(incl, jnp.exp(jnp.where(incl, g_col - gct[h:h + 1, :], 0.0)), 0.0)
        qq = q_ref[:, ls]
        kk = k_ref[:, ls]
        kb = kk * beta
        eg = jnp.exp(g_col)
        rmat.append(-jnp.where(strict, _mm_nt(kb, kk) * decay, 0.0))
        rhs.append(jnp.concatenate([v_ref[:, ls] * beta, kb * eg], axis=1))
        a_scr[h] = jnp.where(incl, _mm_nt(qq, kk) * decay, 0.0)
        qd_scr[:, ls] = qq * eg
        kd_scr[:, ls] = kk * jnp.exp(gl[:, h:h + 1] - g_col)

    xm = list(rmat)
    pm = list(rmat)
    for it in range(6):
        xm = [xm[h] + _mm(pm[h], xm[h]) for h in heads]
        if it < 5:
            pm = [_mm(pm[h], pm[h]) for h in heads]
    for h in heads:
        ls = slice(h * DN_HEAD_DIM, (h + 1) * DN_HEAD_DIM)
        sol = rhs[h] + _mm(xm[h], rhs[h])
        u_scr[:, ls] = sol[:, 0:DN_HEAD_DIM]
        w_scr[:, ls] = sol[:, DN_HEAD_DIM:2 * DN_HEAD_DIM]

    nw = nw_ref[...]
    lsl = [slice(h * DN_HEAD_DIM, (h + 1) * DN_HEAD_DIM) for h in heads]
    for ci in range(n_chunks):
        rs = slice(ci * CHUNK, (ci + 1) * CHUNK)
        dvals = jnp.exp(gl[ci * CHUNK:ci * CHUNK + 1, :])
        st = [s_scr[h] for h in heads]
        ws = [_mm(w_scr[rs, lsl[h]], st[h]) for h in heads]
        qs = [_mm(qd_scr[rs, lsl[h]], st[h]) for h in heads]
        vn = [u_scr[rs, lsl[h]] - ws[h] for h in heads]
        av = [_mm(a_scr[h, rs, rs], vn[h]) for h in heads]
        kv = [_mm_tn(kd_scr[rs, lsl[h]], vn[h]) for h in heads]
        for h in heads:
            s_scr[h] = st[h] * dvals[:, h:h + 1] + kv[h]
            o = qs[h] + av[h]
            on = o * lax.rsqrt(jnp.mean(o * o, axis=-1, keepdims=True) + EPS) * nw
            o_ref[rs, lsl[h]] = (on * _silu(z_ref[rs, lsl[h]])).astype(o_ref.dtype)


def _deltanet(q, k, v, z, gb, nw, batch, seq):
    t = q.shape[0]
    steps = seq // DN_ROWS
    row = lambda b, j: (b * steps + j, 0)
    blk = pl.BlockSpec((DN_ROWS, DN_WIDTH), row)
    return pl.pallas_call(
        _dn_kernel,
        grid=(batch, steps),
        in_specs=[blk, blk, blk, blk,
                  pl.BlockSpec((DN_ROWS, LANES), row),
                  pl.BlockSpec((1, DN_HEAD_DIM), lambda b, j: (0, 0))],
        out_specs=blk,
        out_shape=jax.ShapeDtypeStruct((t, DN_WIDTH), BF16),
        scratch_shapes=[pltpu.VMEM((DN_HEADS, DN_HEAD_DIM, DN_HEAD_DIM), F32),
                        pltpu.VMEM((DN_ROWS, DN_WIDTH), F32),
                        pltpu.VMEM((DN_ROWS, DN_WIDTH), F32),
                        pltpu.VMEM((DN_ROWS, DN_WIDTH), F32),
                        pltpu.VMEM((DN_ROWS, DN_WIDTH), F32),
                        pltpu.VMEM((DN_HEADS, DN_ROWS, DN_ROWS), F32)],
        compiler_params=pltpu.CompilerParams(dimension_semantics=("arbitrary", "arbitrary"),
                                             vmem_limit_bytes=VMEM_LIMIT),
        name="deltanet",
    )(q, k, v, z, gb, nw)


def _swa_kernel(sink_ref, q_ref, kp_ref, kc_ref, vp_ref, vc_ref, o_ref):
    n = pl.program_id(1)
    w = WINDOW
    kcat = jnp.concatenate([kp_ref[...], kc_ref[...]], axis=0)
    vcat = jnp.concatenate([vp_ref[...], vc_ref[...]], axis=0)
    rows = SWA_GROUP * w
    ri = lax.broadcasted_iota(I32, (rows, 2 * w), 0)
    kj = lax.broadcasted_iota(I32, (rows, 2 * w), 1)
    grp = ri // w
    rel = (ri - grp * w) + w - kj
    valid = (rel >= 0) & (rel < w) & ((kj >= w) | (n > 0))
    relf = rel.astype(F32)
    grp_col = lax.broadcasted_iota(I32, (rows, 1), 0) // w
    q = q_ref[...]
    outs = []
    for kh in range(SWA_KV_HEADS):
        slope_col = jnp.zeros((rows, 1), F32)
        sink_col = jnp.zeros((rows, 1), F32)
        for g in range(SWA_GROUP):
            hd = kh * SWA_GROUP + g
            slope_col = jnp.where(grp_col == g, 2.0 ** (-8.0 * (hd + 1) / SWA_HEADS), slope_col)
            sink_col = jnp.where(grp_col == g, sink_ref[hd], sink_col)
        qs = jnp.concatenate(
            [q[:, (kh * SWA_GROUP + g) * SWA_HEAD_DIM:(kh * SWA_GROUP + g + 1) * SWA_HEAD_DIM]
             for g in range(SWA_GROUP)], axis=0)
        kk = kcat[:, kh * SWA_HEAD_DIM:(kh + 1) * SWA_HEAD_DIM]
        vv = vcat[:, kh * SWA_HEAD_DIM:(kh + 1) * SWA_HEAD_DIM]
        s = lax.dot_general(qs, kk, (((1,), (1,)), ((), ())), preferred_element_type=F32)
        logits = jnp.where(valid, s - slope_col * relf, NEG)
        m = jnp.maximum(jnp.max(logits, axis=-1, keepdims=True), sink_col)
        p = jnp.exp(logits - m)
        denom = jnp.sum(p, axis=-1, keepdims=True) + jnp.exp(sink_col - m)
        o = jnp.dot(p.astype(BF16), vv, preferred_element_type=F32) / denom
        for g in range(SWA_GROUP):
            outs.append(o[g * w:(g + 1) * w, :])
    for t in range(SWA_WIDTH // LANES):
        o_ref[:, t * LANES:(t + 1) * LANES] = jnp.concatenate(
            [outs[2 * t], outs[2 * t + 1]], axis=-1).astype(o_ref.dtype)


def _swa(swq, swk, swv, sinks, batch, seq):
    t = swq.shape[0]
    nb = seq // WINDOW
    cur = lambda b, n, s: (b * nb + n, 0)
    prev = lambda b, n, s: (b * nb + jnp.maximum(n - 1, 0), 0)
    return pl.pallas_call(
        _swa_kernel,
        grid_spec=pltpu.PrefetchScalarGridSpec(
            num_scalar_prefetch=1,
            grid=(batch, nb),
            in_specs=[pl.BlockSpec((WINDOW, SWA_WIDTH), cur),
                      pl.BlockSpec((WINDOW, SWA_KV_WIDTH), prev),
                      pl.BlockSpec((WINDOW, SWA_KV_WIDTH), cur),
                      pl.BlockSpec((WINDOW, SWA_KV_WIDTH), prev),
                      pl.BlockSpec((WINDOW, SWA_KV_WIDTH), cur)],
            out_specs=pl.BlockSpec((WINDOW, SWA_WIDTH), cur)),
        out_shape=jax.ShapeDtypeStruct((t, SWA_WIDTH), BF16),
        compiler_params=pltpu.CompilerParams(dimension_semantics=("arbitrary", "arbitrary"),
                                             vmem_limit_bytes=VMEM_LIMIT),
        name="swa",
    )(sinks, swq, swk, swk, swv, swv)


def _post_kernel(x_ref, odn_ref, osw_ref, mod_ref, wo_ref, n2w_ref, wr_ref, br_ref,
                 x1_ref, h2_ref, idx_ref, gate_ref, rank_ref, cnt_ref, run_scr):
    i = pl.program_id(0)
    tm = x_ref.shape[0]

    @pl.when(i == 0)
    def _():
        run_scr[...] = jnp.zeros(run_scr.shape, F32)

    y = (jnp.dot(odn_ref[...], wo_ref[0:DN_WIDTH, :], preferred_element_type=F32)
         + jnp.dot(osw_ref[...], wo_ref[DN_WIDTH:, :], preferred_element_type=F32))
    x1 = x_ref[...] + mod_ref[0, 2:3, :] * y
    x1_ref[...] = x1
    hn = x1 * lax.rsqrt(jnp.mean(x1 * x1, axis=-1, keepdims=True) + EPS) * n2w_ref[...]
    h2 = hn * (1.0 + mod_ref[0, 4:5, :]) + mod_ref[0, 3:4, :]
    _store_token_tiles(h2_ref, h2)

    lane = lax.broadcasted_iota(I32, (tm, LANES), 1)
    lane_f = lane.astype(F32)
    logits = jnp.dot(h2.astype(BF16), wr_ref[...], preferred_element_type=F32) + br_ref[...]
    work = jnp.where(lane < N_EXPERTS, logits, NEG)
    sel = jnp.zeros((tm, LANES), F32)
    idx_out = jnp.zeros((tm, LANES), I32)
    vals = []
    idxs = []
    for kk in range(TOP_K):
        m = jnp.max(work, axis=-1, keepdims=True)
        ik = jnp.min(jnp.where(work == m, lane_f, float(LANES)), axis=-1,
                     keepdims=True).astype(I32)
        hit = lane == ik
        work = jnp.where(hit, NEG, work)
        sel = jnp.where(hit, 1.0, sel)
        idx_out = jnp.where(lane == kk, ik, idx_out)
        vals.append(m)
        idxs.append(ik)
    es = [jnp.exp(v - vals[0]) for v in vals]
    tot = es[0] + es[1] + es[2] + es[3]
    gate_out = jnp.zeros((tm, LANES), F32)
    for kk in range(TOP_K):
        gate_out = jnp.where(lane == kk, es[kk] / tot, gate_out)
    idx_ref[...] = idx_out
    gate_ref[...] = gate_out

    r = lax.broadcasted_iota(I32, (tm, tm), 0)
    c = lax.broadcasted_iota(I32, (tm, tm), 1)
    before = jnp.where(c < r, 1.0, 0.0).astype(BF16)
    base = jnp.dot(before, sel.astype(BF16), preferred_element_type=F32) + run_scr[0:1, :]
    rank_out = jnp.zeros((tm, LANES), I32)
    for kk in range(TOP_K):
        rk = jnp.sum(jnp.where(lane == idxs[kk], base, 0.0), axis=-1, keepdims=True)
        rank_out = jnp.where(lane == kk, rk.astype(I32), rank_out)
    rank_ref[...] = rank_out
    run = run_scr[0:1, :] + jnp.sum(sel, axis=0, keepdims=True)
    run_scr[0:1, :] = run
    cnt_ref[...] = jnp.broadcast_to(run, cnt_ref.shape).astype(I32)


def _post(xf, odn, osw, mod, w_out, n2w, wr_pad, br_pad, seq):
    t = xf.shape[0]
    tiles_per_seq = seq // TM_POST
    row = lambda i: (i, 0)
    const = lambda i: (0, 0)
    return pl.pallas_call(
        _post_kernel,
        grid=(t // TM_POST,),
        in_specs=[pl.BlockSpec((TM_POST, D_MODEL), row),
                  pl.BlockSpec((TM_POST, DN_WIDTH), row),
                  pl.BlockSpec((TM_POST, SWA_WIDTH), row),
                  pl.BlockSpec((1, 6, D_MODEL), lambda i: (i // tiles_per_seq, 0, 0)),
                  pl.BlockSpec((D_MODEL, D_MODEL), const),
                  pl.BlockSpec((1, D_MODEL), const),
                  pl.BlockSpec((D_MODEL, LANES), const),
                  pl.BlockSpec((1, LANES), const)],
        out_specs=[pl.BlockSpec((TM_POST, D_MODEL), row),
                   pl.BlockSpec((TM_POST * SUBLANES, LANES), row),
                   pl.BlockSpec((TM_POST, LANES), row),
                   pl.BlockSpec((TM_POST, LANES), row),
                   pl.BlockSpec((TM_POST, LANES), row),
                   pl.BlockSpec((SUBLANES, LANES), const)],
        out_shape=[jax.ShapeDtypeStruct((t, D_MODEL), F32),
                   jax.ShapeDtypeStruct((t * SUBLANES, LANES), F32),
                   jax.ShapeDtypeStruct((t, LANES), I32),
                   jax.ShapeDtypeStruct((t, LANES), F32),
                   jax.ShapeDtypeStruct((t, LANES), I32),
                   jax.ShapeDtypeStruct((SUBLANES, LANES), I32)],
        scratch_shapes=[pltpu.VMEM((SUBLANES, LANES), F32)],
        compiler_params=pltpu.CompilerParams(dimension_semantics=("arbitrary",),
                                             vmem_limit_bytes=VMEM_LIMIT),
        name="post",
    )(xf, odn, osw, mod, w_out, n2w, wr_pad, br_pad)


def _dispatch_kernel(dest_ref, fill_ref, h_ref, zero_ref, xs_ref, sem):
    i = pl.program_id(0)
    n_tok = h_ref.shape[0] // SUBLANES

    def issue(tk, carry):
        for kk in range(TOP_K):
            _token_copy(h_ref, tk, xs_ref, dest_ref[0, 0, tk * TOP_K + kk], sem).start()
        return carry

    lax.fori_loop(0, n_tok, issue, 0, unroll=4)

    for kk in range(TOP_K):
        pltpu.make_async_copy(h_ref, xs_ref.at[pl.ds(0, n_tok * SUBLANES), :], sem).wait()

    @pl.when(i == 0)
    def _():
        def per_expert(e, carry):
            lo = fill_ref[0, e]
            hi = fill_ref[1, e]

            def z_issue(rw, c2):
                _token_copy(zero_ref, 0, xs_ref, rw, sem).start()
                return c2

            lax.fori_loop(lo, hi, z_issue, 0)

            def z_wait(rw, c2):
                _token_copy(zero_ref, 0, xs_ref, 0, sem).wait()
                return c2

            lax.fori_loop(lo, hi, z_wait, 0)
            return carry

        lax.fori_loop(0, N_EXPERTS, per_expert, 0)

        blk_rows = MOE_ROWS * SUBLANES

        def tail(b, carry):
            cp = pltpu.make_async_copy(
                zero_ref, xs_ref.at[pl.ds(pl.multiple_of(b * blk_rows, blk_rows), blk_rows), :], sem)
            cp.start()
            cp.wait()
            return carry

        lax.fori_loop(fill_ref[2, 0], xs_ref.shape[0] // blk_rows, tail, 0)


def _dispatch(dest3, fill, h2, zero_rows, n_pad):
    t = h2.shape[0] // SUBLANES
    return pl.pallas_call(
        _dispatch_kernel,
        grid=(t // TM_DISP,),
        in_specs=[pl.BlockSpec((1, 1, TM_DISP * TOP_K), lambda i: (i, 0, 0),
                               memory_space=pltpu.SMEM),
                  pl.BlockSpec(memory_space=pltpu.SMEM),
                  pl.BlockSpec((TM_DISP * SUBLANES, LANES), lambda i: (i, 0)),
                  pl.BlockSpec(memory_space=pl.ANY)],
        out_specs=pl.BlockSpec(memory_space=pl.ANY),
        out_shape=jax.ShapeDtypeStruct((n_pad * SUBLANES, LANES), F32),
        scratch_shapes=[pltpu.SemaphoreType.DMA(())],
        compiler_params=pltpu.CompilerParams(dimension_semantics=("arbitrary",),
                                             vmem_limit_bytes=VMEM_LIMIT),
        name="dispatch",
    )(dest3, fill, h2, zero_rows)


def _moe_kernel(be_ref, nv_ref, xs_ref, wu_ref, bu_ref, wd_ref, bd_ref, ys_ref, wu_bf, wd_bf):
    i = pl.program_id(0)

    @pl.when(i < nv_ref[0])
    def _():
        prev = be_ref[jnp.maximum(i - 1, 0)]

        @pl.when((i == 0) | (be_ref[i] != prev))
        def _():
            wu_bf[...] = wu_ref[0].astype(BF16)
            wd_bf[...] = wd_ref[0].astype(BF16)

        xb = _load_token_tiles(xs_ref, 0, MOE_ROWS).astype(BF16)
        hu = jnp.dot(xb, wu_bf[...], preferred_element_type=F32) + bu_ref[0]
        x_glu = jnp.minimum(hu[:, :D_FF], SWIGLU_LIMIT)
        x_lin = jnp.clip(hu[:, D_FF:], -SWIGLU_LIMIT, SWIGLU_LIMIT)
        act = x_glu * jax.nn.sigmoid(SWIGLU_ALPHA * x_glu) * (x_lin + 1.0)
        _store_token_tiles(
            ys_ref, jnp.dot(act.astype(BF16), wd_bf[...], preferred_element_type=F32) + bd_ref[0])

    @pl.when(i >= nv_ref[0])
    def _():
        ys_ref[...] = jnp.zeros(ys_ref.shape, F32)


def _moe(block_e, n_valid, xs, w_up, b_up, w_down, b_down):
    n_pad = xs.shape[0] // SUBLANES
    n_blocks = n_pad // MOE_ROWS
    rows = lambda i, be, nv: (jnp.minimum(i, nv[0] - 1), 0)
    out_rows = lambda i, be, nv: (i, 0)
    expert = lambda i, be, nv: (be[i], 0, 0)
    return pl.pallas_call(
        _moe_kernel,
        grid_spec=pltpu.PrefetchScalarGridSpec(
            num_scalar_prefetch=2,
            grid=(n_blocks,),
            in_specs=[pl.BlockSpec((MOE_ROWS * SUBLANES, LANES), rows),
                      pl.BlockSpec((1, D_MODEL, 2 * D_FF), expert),
                      pl.BlockSpec((1, 1, 2 * D_FF), expert),
                      pl.BlockSpec((1, D_FF, D_MODEL), expert),
                      pl.BlockSpec((1, 1, D_MODEL), expert)],
            out_specs=pl.BlockSpec((MOE_ROWS * SUBLANES, LANES), out_rows),
            scratch_shapes=[pltpu.VMEM((D_MODEL, 2 * D_FF), BF16),
                            pltpu.VMEM((D_FF, D_MODEL), BF16)]),
        out_shape=jax.ShapeDtypeStruct((n_pad * SUBLANES, LANES), F32),
        compiler_params=pltpu.CompilerParams(dimension_semantics=("arbitrary",),
                                             vmem_limit_bytes=VMEM_LIMIT),
        name="moe",
    )(block_e, n_valid, xs, w_up, b_up, w_down, b_down)


def _combine_kernel(dcur_ref, dnext_ref, ys_ref, x1_ref, gate_ref, mod_ref, o_ref, buf, sem):
    i = pl.program_id(0)
    tm = x1_ref.shape[0]
    slot = i % 2

    def issue(dref, s):
        def body(tk, carry):
            for kk in range(TOP_K):
                _token_copy(ys_ref, dref[0, 0, tk * TOP_K + kk], buf.at[s], kk * tm + tk,
                            sem.at[s]).start()
            return carry

        lax.fori_loop(0, tm, body, 0, unroll=4)

    @pl.when(i == 0)
    def _():
        issue(dcur_ref, 0)

    @pl.when(i + 1 < pl.num_programs(0))
    def _():
        issue(dnext_ref, 1 - slot)

    pltpu.make_async_copy(ys_ref.at[pl.ds(0, TOP_K * tm * SUBLANES), :], buf.at[slot],
                          sem.at[slot]).wait()

    gates = gate_ref[...]
    acc = jnp.zeros((tm, D_MODEL), F32)
    for kk in range(TOP_K):
        acc = acc + gates[:, kk:kk + 1] * _load_token_tiles(buf.at[slot], kk * tm, tm)
    o_ref[...] = x1_ref[...] + mod_ref[0, 5:6, :] * acc


def _combine(dest3, ys, x1, gates, mod, seq):
    t = x1.shape[0]
    tiles_per_seq = seq // TM_COMB
    n_tiles = t // TM_COMB
    row = lambda i: (i, 0)
    return pl.pallas_call(
        _combine_kernel,
        grid=(n_tiles,),
        in_specs=[pl.BlockSpec((1, 1, TM_COMB * TOP_K), lambda i: (i, 0, 0),
                               memory_space=pltpu.SMEM),
                  pl.BlockSpec((1, 1, TM_COMB * TOP_K),
                               lambda i: (jnp.minimum(i + 1, n_tiles - 1), 0, 0),
                               memory_space=pltpu.SMEM),
                  pl.BlockSpec(memory_space=pl.ANY),
                  pl.BlockSpec((TM_COMB, D_MODEL), row),
                  pl.BlockSpec((TM_COMB, LANES), row),
                  pl.BlockSpec((1, 6, D_MODEL), lambda i: (i // tiles_per_seq, 0, 0))],
        out_specs=pl.BlockSpec((TM_COMB, D_MODEL), row),
        out_shape=jax.ShapeDtypeStruct((t, D_MODEL), F32),
        scratch_shapes=[pltpu.VMEM((2, TOP_K * TM_COMB * SUBLANES, LANES), F32),
                        pltpu.SemaphoreType.DMA((2,))],
        compiler_params=pltpu.CompilerParams(dimension_semantics=("arbitrary",),
                                             vmem_limit_bytes=VMEM_LIMIT),
        name="combine",
    )(dest3, dest3, ys, x1, gates, mod)


def _layer(x, mod, norm1_w, w_in, conv_w, a_log, dt_bias, dn_norm_w, q_norm_w, k_norm_w, sinks,
           w_out, norm2_w, w_router, b_router, w_up, b_up, w_down, b_down):
    batch, seq, _ = x.shape
    t = batch * seq
    xf = x.reshape(t, D_MODEL)

    o_qkv, o_z, o_a, o_b, o_q, o_k, o_v = 0, 1536, 2048, 2052, 2056, 2568, 2696
    w_ab = jnp.zeros((D_MODEL, LANES), F32).at[:, 0:2 * DN_HEADS].set(w_in[:, o_a:o_q])
    w_cat = jnp.concatenate([w_in[:, o_qkv:o_z], w_in[:, o_z:o_a], w_in[:, o_q:o_k],
                             w_in[:, o_k:o_v], w_in[:, o_v:], w_ab], axis=1).astype(BF16)
    gparams = jnp.zeros((2, LANES), F32)
    gparams = gparams.at[0, 0:DN_HEADS].set(-jnp.exp(a_log.astype(F32)))
    gparams = gparams.at[1, 0:DN_HEADS].set(dt_bias.astype(F32))
    qnw128 = jnp.tile(q_norm_w.astype(F32), 2)[None, :]
    knw128 = jnp.tile(k_norm_w.astype(F32), 2)[None, :]

    q, k, v, z, gb, swq, swk, swv = _inproj(xf, mod, norm1_w[None, :], w_cat, conv_w, gparams,
                                            qnw128, knw128, seq)
    o_dn = _deltanet(q, k, v, z, gb, dn_norm_w[None, :], batch, seq)
    o_sw = _swa(swq, swk, swv, sinks.astype(F32), batch, seq)

    wr_pad = jnp.zeros((D_MODEL, LANES), F32).at[:, 0:N_EXPERTS].set(w_router).astype(BF16)
    br_pad = jnp.zeros((1, LANES), F32).at[0, 0:N_EXPERTS].set(b_router)
    x1, h2, top_idx, gates, rank, cnt = _post(xf, o_dn, o_sw, mod, w_out.astype(BF16),
                                              norm2_w[None, :], wr_pad, br_pad, seq)

    counts = cnt[0, 0:N_EXPERTS]
    padded = (counts + MOE_ROWS - 1) // MOE_ROWS * MOE_ROWS
    pend = jnp.cumsum(padded)
    pstart = pend - padded
    n_blocks = (t * TOP_K) // MOE_ROWS + N_EXPERTS
    n_pad = n_blocks * MOE_ROWS
    n_valid = (pend[-1] // MOE_ROWS).astype(I32)
    blk = jnp.minimum(jnp.arange(n_blocks, dtype=I32), n_valid - 1)
    block_e = jnp.minimum(jnp.sum(pend[None, :] <= (blk * MOE_ROWS)[:, None], axis=1),
                          N_EXPERTS - 1).astype(I32)
    dest = (pstart[top_idx[:, 0:TOP_K]] + rank[:, 0:TOP_K]).astype(I32)
    fill = jnp.stack([pstart + counts, pend, jnp.full((N_EXPERTS,), n_valid)]).astype(I32)

    xs = _dispatch(dest.reshape(t // TM_DISP, 1, TM_DISP * TOP_K), fill, h2,
                   jnp.zeros((MOE_ROWS * SUBLANES, LANES), F32), n_pad)
    ys = _moe(block_e, n_valid.reshape(1), xs, w_up, b_up[:, None, :], w_down, b_down[:, None, :])
    out = _combine(dest.reshape(t // TM_COMB, 1, TM_COMB * TOP_K), ys, x1, gates, mod, seq)
    return out.reshape(batch, seq, D_MODEL)


def kernel(x, c, w_ada, b_ada, norm1_w, w_in, conv_w, a_log, dt_bias, dn_norm_w, q_norm_w,
           k_norm_w, sinks, w_out, norm2_w, w_router, b_router, w_up, b_up, w_down, b_down):
    out_dtype = x.dtype
    batch = x.shape[0]
    depth = w_ada.shape[0]
    c_pad = jnp.zeros((SUBLANES, D_MODEL), F32).at[0:batch].set(c.astype(F32))
    for l in range(depth):
        mod = _ada(c_pad, w_ada[l], b_ada[l][None, :])[0:batch].reshape(batch, 6, D_MODEL)
        x = _layer(x, mod, norm1_w[l], w_in[l], conv_w[l], a_log[l], dt_bias[l], dn_norm_w[l],
                   q_norm_w[l], k_norm_w[l], sinks[l], w_out[l], norm2_w[l], w_router[l],
                   b_router[l], w_up[l], b_up[l], w_down[l], b_down[l])
    return x.astype(out_dtype)
```

```python
import functools

import jax
import jax.numpy as jnp
from jax import lax
from jax.experimental import pallas as pl
from jax.experimental.pallas import tpu as pltpu

F32 = jnp.float32
BF16 = jnp.bfloat16
I32 = jnp.int32

D_MODEL = 1024
DN_HEADS = 4
DN_HEAD_DIM = 128
DN_WIDTH = DN_HEADS * DN_HEAD_DIM
CONV_WIDTH = 4
CHUNK = 64
SWA_HEADS = 8
SWA_KV_HEADS = 2
SWA_GROUP = SWA_HEADS // SWA_KV_HEADS
SWA_HEAD_DIM = 64
SWA_WIDTH = SWA_HEADS * SWA_HEAD_DIM
SWA_KV_WIDTH = SWA_KV_HEADS * SWA_HEAD_DIM
WINDOW = 128
N_EXPERTS = 32
TOP_K = 4
D_FF = D_MODEL
SWIGLU_ALPHA = 1.702
SWIGLU_LIMIT = 7.0
EPS = 1e-6
NEG = -1e30

LANES = 128
SUBLANES = 8
VMEM_LIMIT = 56 * 1024 * 1024

C_QKV = 0
C_Z = C_QKV + 3 * DN_WIDTH
C_SWQ = C_Z + DN_WIDTH
C_SWK = C_SWQ + SWA_WIDTH
C_SWV = C_SWK + SWA_KV_WIDTH
C_AB = C_SWV + SWA_KV_WIDTH
IN_PAD = C_AB + LANES

TM_IN = 256
DN_ROWS = 256
TM_POST = 512
MOE_ROWS = 256
TM_COMB = 128


def _silu(x):
    return x * jax.nn.sigmoid(x)


def _mm(a, b):
    return jnp.dot(a.astype(BF16), b.astype(BF16), preferred_element_type=F32)


def _mm_nt(a, b):
    return lax.dot_general(a.astype(BF16), b.astype(BF16), (((1,), (1,)), ((), ())),
                           preferred_element_type=F32)


def _mm_tn(a, b):
    return lax.dot_general(a.astype(BF16), b.astype(BF16), (((0,), (0,)), ((), ())),
                           preferred_element_type=F32)


def _load_token_tiles(ref, row0, rows):
    return jnp.concatenate(
        [ref[pl.ds(row0 * SUBLANES + s, rows, stride=SUBLANES), :] for s in range(D_MODEL // LANES)],
        axis=1)


def _store_token_tiles(ref, val):
    rows = val.shape[0]
    for s in range(D_MODEL // LANES):
        ref[pl.ds(s, rows, stride=SUBLANES), :] = val[:, s * LANES:(s + 1) * LANES]


def _token_copy(src_ref, s, dst_ref, d, sem):
    return pltpu.make_async_copy(
        src_ref.at[pl.ds(pl.multiple_of(s * SUBLANES, SUBLANES), SUBLANES), :],
        dst_ref.at[pl.ds(pl.multiple_of(d * SUBLANES, SUBLANES), SUBLANES), :], sem)


def _ada_kernel(c_ref, w_ref, b_ref, o_ref):
    c = c_ref[...]
    o_ref[...] = jnp.dot(_silu(c), w_ref[...], precision=lax.Precision.HIGHEST,
                         preferred_element_type=F32) + b_ref[...]


def _ada(c_pad, w_ada, b_ada):
    n = w_ada.shape[1]
    return pl.pallas_call(
        _ada_kernel,
        grid=(n // D_MODEL,),
        in_specs=[pl.BlockSpec((SUBLANES, D_MODEL), lambda j: (0, 0)),
                  pl.BlockSpec((D_MODEL, D_MODEL), lambda j: (0, j)),
                  pl.BlockSpec((1, D_MODEL), lambda j: (0, j))],
        out_specs=pl.BlockSpec((SUBLANES, D_MODEL), lambda j: (0, j)),
        out_shape=jax.ShapeDtypeStruct((SUBLANES, n), F32),
        name="ada",
    )(c_pad, w_ada, b_ada)


def _rms_heads64(a, w128):
    lane = lax.broadcasted_iota(I32, a.shape, 1)
    lo = lane < SWA_HEAD_DIM
    sq = a * a
    s_lo = jnp.sum(jnp.where(lo, sq, 0.0), axis=-1, keepdims=True)
    s_hi = jnp.sum(jnp.where(lo, 0.0, sq), axis=-1, keepdims=True)
    ms = jnp.where(lo, s_lo, s_hi) * (1.0 / SWA_HEAD_DIM)
    return a * lax.rsqrt(ms + EPS) * w128


def _inproj_kernel(tiles_per_seq, x_ref, mod_ref, n1w_ref, w_ref, convw_ref, gp_ref, qnw_ref,
                   knw_ref, q_ref, k_ref, v_ref, z_ref, gb_ref, swq_ref, swk_ref, swv_ref,
                   conv_scr):
    i = pl.program_id(0)
    tm = x_ref.shape[0]
    x = x_ref[...]
    y = x * lax.rsqrt(jnp.mean(x * x, axis=-1, keepdims=True) + EPS) * n1w_ref[...]
    h = (y * (1.0 + mod_ref[0, 1:2, :]) + mod_ref[0, 0:1, :]).astype(BF16)

    @pl.when(i % tiles_per_seq == 0)
    def _():
        conv_scr[0:SUBLANES, :] = jnp.zeros((SUBLANES, 3 * DN_WIDTH), F32)

    u = jnp.dot(h, w_ref[:, C_QKV:C_Z], preferred_element_type=F32)
    conv_scr[SUBLANES:SUBLANES + tm, :] = u
    cw = convw_ref[...]
    acc = u * cw[3:4, :]
    for j in range(CONV_WIDTH - 1):
        off = SUBLANES - (CONV_WIDTH - 1) + j
        acc = acc + conv_scr[off:off + tm, :] * cw[j:j + 1, :]
    conv_scr[0:SUBLANES, :] = conv_scr[tm:tm + SUBLANES, :]
    qkv = _silu(acc)

    for hh in range(DN_HEADS):
        sl = slice(hh * DN_HEAD_DIM, (hh + 1) * DN_HEAD_DIM)
        qh = qkv[:, hh * DN_HEAD_DIM:(hh + 1) * DN_HEAD_DIM]
        kh = qkv[:, DN_WIDTH + hh * DN_HEAD_DIM:DN_WIDTH + (hh + 1) * DN_HEAD_DIM]
        q_ref[:, sl] = qh * lax.rsqrt(jnp.sum(qh * qh, axis=-1, keepdims=True) + EPS) * (
            DN_HEAD_DIM ** -0.5)
        k_ref[:, sl] = kh * lax.rsqrt(jnp.sum(kh * kh, axis=-1, keepdims=True) + EPS)
    v_ref[...] = qkv[:, 2 * DN_WIDTH:3 * DN_WIDTH]

    z_ref[...] = jnp.dot(h, w_ref[:, C_Z:C_SWQ], preferred_element_type=F32)

    ab = jnp.dot(h, w_ref[:, C_AB:IN_PAD], preferred_element_type=F32)
    lane = lax.broadcasted_iota(I32, ab.shape, 1)
    sp_in = ab + gp_ref[1:2, :]
    softplus = jnp.maximum(sp_in, 0.0) + jnp.log1p(jnp.exp(-jnp.abs(sp_in)))
    gb_ref[...] = jnp.where(lane < DN_HEADS, gp_ref[0:1, :] * softplus, jax.nn.sigmoid(ab))

    sq = jnp.dot(h, w_ref[:, C_SWQ:C_SWK], preferred_element_type=F32)
    for t in range(SWA_WIDTH // LANES):
        sl = slice(t * LANES, (t + 1) * LANES)
        swq_ref[:, sl] = (_rms_heads64(sq[:, sl], qnw_ref[...]) * (SWA_HEAD_DIM ** -0.5)).astype(BF16)
    sk = jnp.dot(h, w_ref[:, C_SWK:C_SWV], preferred_element_type=F32)
    swk_ref[...] = _rms_heads64(sk, knw_ref[...]).astype(BF16)
    swv_ref[...] = jnp.dot(h, w_ref[:, C_SWV:C_AB], preferred_element_type=F32).astype(BF16)


def _inproj(xf, mod, n1w, w_cat, conv_w, gparams, qnw128, knw128, seq):
    t = xf.shape[0]
    tiles_per_seq = seq // TM_IN
    row = lambda i: (i, 0)
    const = lambda i: (0, 0)
    outs = [jax.ShapeDtypeStruct((t, DN_WIDTH), F32)] * 4 + [
        jax.ShapeDtypeStruct((t, LANES), F32),
        jax.ShapeDtypeStruct((t, SWA_WIDTH), BF16),
        jax.ShapeDtypeStruct((t, SWA_KV_WIDTH), BF16),
        jax.ShapeDtypeStruct((t, SWA_KV_WIDTH), BF16)]
    out_specs = [pl.BlockSpec((TM_IN, DN_WIDTH), row)] * 4 + [
        pl.BlockSpec((TM_IN, LANES), row),
        pl.BlockSpec((TM_IN, SWA_WIDTH), row),
        pl.BlockSpec((TM_IN, SWA_KV_WIDTH), row),
        pl.BlockSpec((TM_IN, SWA_KV_WIDTH), row)]
    return pl.pallas_call(
        functools.partial(_inproj_kernel, tiles_per_seq),
        grid=(t // TM_IN,),
        in_specs=[pl.BlockSpec((TM_IN, D_MODEL), row),
                  pl.BlockSpec((1, 6, D_MODEL), lambda i: (i // tiles_per_seq, 0, 0)),
                  pl.BlockSpec((1, D_MODEL), const),
                  pl.BlockSpec((D_MODEL, IN_PAD), const),
                  pl.BlockSpec((CONV_WIDTH, 3 * DN_WIDTH), const),
                  pl.BlockSpec((2, LANES), const),
                  pl.BlockSpec((1, LANES), const),
                  pl.BlockSpec((1, LANES), const)],
        out_specs=out_specs,
        out_shape=outs,
        scratch_shapes=[pltpu.VMEM((TM_IN + SUBLANES, 3 * DN_WIDTH), F32)],
        compiler_params=pltpu.CompilerParams(dimension_semantics=("arbitrary",),
                                             vmem_limit_bytes=VMEM_LIMIT),
        name="inproj",
    )(xf, mod, n1w, w_cat, conv_w, gparams, qnw128, knw128)


def _dn_kernel(q_ref, k_ref, v_ref, z_ref, gb_ref, nw_ref, o_ref,
               s_scr, u_scr, w_scr, qd_scr, kd_scr, a_scr):
    j = pl.program_id(1)
    rows = q_ref.shape[0]
    n_chunks = rows // CHUNK
    heads = range(DN_HEADS)

    @pl.when(j == 0)
    def _():
        s_scr[...] = jnp.zeros(s_scr.shape, F32)

    r = lax.broadcasted_iota(I32, (rows, rows), 0)
    c = lax.broadcasted_iota(I32, (rows, rows), 1)
    same = (r // CHUNK) == (c // CHUNK)
    incl = same & (c <= r)
    strict = same & (c < r)
    gb = gb_ref[...]
    gc = jnp.dot(jnp.where(incl, 1.0, 0.0), gb, precision=lax.Precision.HIGHEST,
                 preferred_element_type=F32)
    gl = jnp.dot(jnp.where(same, 1.0, 0.0), gb, precision=lax.Precision.HIGHEST,
                 preferred_element_type=F32)
    gct = gc.T

    rmat = []
    rhs = []
    for h in heads:
        ls = slice(h * DN_HEAD_DIM, (h + 1) * DN_HEAD_DIM)
        g_col = gc[:, h:h + 1]
        beta = gb[:, DN_HEADS + h:DN_HEADS + h + 1]
        decay = jnp.where(incl, jnp.exp(jnp.where(incl, g_col - gct[h:h + 1, :], 0.0)), 0.0)
        qq = q_ref[:, ls]
        kk = k_ref[:, ls]
        kb = kk * beta
        eg = jnp.exp(g_col)
        rmat.append(-jnp.where(strict, _mm_nt(kb, kk) * decay, 0.0))
        rhs.append(jnp.concatenate([v_ref[:, ls] * beta, kb * eg], axis=1))
        a_scr[h] = jnp.where(incl, _mm_nt(qq, kk) * decay, 0.0)
        qd_scr[:, ls] = qq * eg
        kd_scr[:, ls] = kk * jnp.exp(gl[:, h:h + 1] - g_col)

    xm = list(rmat)
    pm = list(rmat)
    for it in range(6):
        xm = [xm[h] + _mm(pm[h], xm[h]) for h in heads]
        if it < 5:
            pm = [_mm(pm[h], pm[h]) for h in heads]
    for h in heads:
        ls = slice(h * DN_HEAD_DIM, (h + 1) * DN_HEAD_DIM)
        sol = rhs[h] + _mm(xm[h], rhs[h])
        u_scr[:, ls] = sol[:, 0:DN_HEAD_DIM]
        w_scr[:, ls] = sol[:, DN_HEAD_DIM:2 * DN_HEAD_DIM]

    nw = nw_ref[...]
    lsl = [slice(h * DN_HEAD_DIM, (h + 1) * DN_HEAD_DIM) for h in heads]
    for ci in range(n_chunks):
        rs = slice(ci * CHUNK, (ci + 1) * CHUNK)
        dvals = jnp.exp(gl[ci * CHUNK:ci * CHUNK + 1, :])
        st = [s_scr[h] for h in heads]
        ws = [_mm(w_scr[rs, lsl[h]], st[h]) for h in heads]
        qs = [_mm(qd_scr[rs, lsl[h]], st[h]) for h in heads]
        vn = [u_scr[rs, lsl[h]] - ws[h] for h in heads]
        av = [_mm(a_scr[h, rs, rs], vn[h]) for h in heads]
        kv = [_mm_tn(kd_scr[rs, lsl[h]], vn[h]) for h in heads]
        for h in heads:
            s_scr[h] = st[h] * dvals[:, h:h + 1] + kv[h]
            o = qs[h] + av[h]
            on = o * lax.rsqrt(jnp.mean(o * o, axis=-1, keepdims=True) + EPS) * nw
            o_ref[rs, lsl[h]] = (on * _silu(z_ref[rs, lsl[h]])).astype(o_ref.dtype)


def _deltanet(q, k, v, z, gb, nw, batch, seq):
    t = q.shape[0]
    steps = seq // DN_ROWS
    row = lambda b, j: (b * steps + j, 0)
    blk = pl.BlockSpec((DN_ROWS, DN_WIDTH), row)
    return pl.pallas_call(
        _dn_kernel,
        grid=(batch, steps),
        in_specs=[blk, blk, blk, blk,
                  pl.BlockSpec((DN_ROWS, LANES), row),
                  pl.BlockSpec((1, DN_HEAD_DIM), lambda b, j: (0, 0))],
        out_specs=blk,
        out_shape=jax.ShapeDtypeStruct((t, DN_WIDTH), BF16),
        scratch_shapes=[pltpu.VMEM((DN_HEADS, DN_HEAD_DIM, DN_HEAD_DIM), F32),
                        pltpu.VMEM((DN_ROWS, DN_WIDTH), F32),
                        pltpu.VMEM((DN_ROWS, DN_WIDTH), F32),
                        pltpu.VMEM((DN_ROWS, DN_WIDTH), F32),
                        pltpu.VMEM((DN_ROWS, DN_WIDTH), F32),
                        pltpu.VMEM((DN_HEADS, DN_ROWS, DN_ROWS), F32)],
        compiler_params=pltpu.CompilerParams(dimension_semantics=("arbitrary", "arbitrary"),
                                             vmem_limit_bytes=VMEM_LIMIT),
        name="deltanet",
    )(q, k, v, z, gb, nw)


def _swa_kernel(sink_ref, q_ref, kp_ref, kc_ref, vp_ref, vc_ref, o_ref):
    n = pl.program_id(1)
    w = WINDOW
    kcat = jnp.concatenate([kp_ref[...], kc_ref[...]], axis=0)
    vcat = jnp.concatenate([vp_ref[...], vc_ref[...]], axis=0)
    rows = SWA_GROUP * w
    ri = lax.broadcasted_iota(I32, (rows, 2 * w), 0)
    kj = lax.broadcasted_iota(I32, (rows, 2 * w), 1)
    grp = ri // w
    rel = (ri - grp * w) + w - kj
    valid = (rel >= 0) & (rel < w) & ((kj >= w) | (n > 0))
    relf = rel.astype(F32)
    grp_col = lax.broadcasted_iota(I32, (rows, 1), 0) // w
    q = q_ref[...]
    outs = []
    for kh in range(SWA_KV_HEADS):
        slope_col = jnp.zeros((rows, 1), F32)
        sink_col = jnp.zeros((rows, 1), F32)
        for g in range(SWA_GROUP):
            hd = kh * SWA_GROUP + g
            slope_col = jnp.where(grp_col == g, 2.0 ** (-8.0 * (hd + 1) / SWA_HEADS), slope_col)
            sink_col = jnp.where(grp_col == g, sink_ref[hd], sink_col)
        qs = jnp.concatenate(
            [q[:, (kh * SWA_GROUP + g) * SWA_HEAD_DIM:(kh * SWA_GROUP + g + 1) * SWA_HEAD_DIM]
             for g in range(SWA_GROUP)], axis=0)
        kk = kcat[:, kh * SWA_HEAD_DIM:(kh + 1) * SWA_HEAD_DIM]
        vv = vcat[:, kh * SWA_HEAD_DIM:(kh + 1) * SWA_HEAD_DIM]
        s = lax.dot_general(qs, kk, (((1,), (1,)), ((), ())), preferred_element_type=F32)
        logits = jnp.where(valid, s - slope_col * relf, NEG)
        m = jnp.maximum(jnp.max(logits, axis=-1, keepdims=True), sink_col)
        p = jnp.exp(logits - m)
        denom = jnp.sum(p, axis=-1, keepdims=True) + jnp.exp(sink_col - m)
        o = jnp.dot(p.astype(BF16), vv, preferred_element_type=F32) / denom
        for g in range(SWA_GROUP):
            outs.append(o[g * w:(g + 1) * w, :])
    for t in range(SWA_WIDTH // LANES):
        o_ref[:, t * LANES:(t + 1) * LANES] = jnp.concatenate(
            [outs[2 * t], outs[2 * t + 1]], axis=-1).astype(o_ref.dtype)


def _swa(swq, swk, swv, sinks, batch, seq):
    t = swq.shape[0]
    nb = seq // WINDOW
    cur = lambda b, n, s: (b * nb + n, 0)
    prev = lambda b, n, s: (b * nb + jnp.maximum(n - 1, 0), 0)
    return pl.pallas_call(
        _swa_kernel,
        grid_spec=pltpu.PrefetchScalarGridSpec(
            num_scalar_prefetch=1,
            grid=(batch, nb),
            in_specs=[pl.BlockSpec((WINDOW, SWA_WIDTH), cur),
                      pl.BlockSpec((WINDOW, SWA_KV_WIDTH), prev),
                      pl.BlockSpec((WINDOW, SWA_KV_WIDTH), cur),
                      pl.BlockSpec((WINDOW, SWA_KV_WIDTH), prev),
                      pl.BlockSpec((WINDOW, SWA_KV_WIDTH), cur)],
            out_specs=pl.BlockSpec((WINDOW, SWA_WIDTH), cur)),
        out_shape=jax.ShapeDtypeStruct((t, SWA_WIDTH), BF16),
        compiler_params=pltpu.CompilerParams(dimension_semantics=("arbitrary", "arbitrary"),
                                             vmem_limit_bytes=VMEM_LIMIT),
        name="swa",
    )(sinks, swq, swk, swk, swv, swv)


def _post_kernel(x_ref, odn_ref, osw_ref, mod_ref, wo_ref, n2w_ref, wr_ref, br_ref,
                 x1_ref, h2_ref, idx_ref, gate_ref, rank_ref, cnt_ref, run_scr):
    i = pl.program_id(0)
    tm = x_ref.shape[0]

    @pl.when(i == 0)
    def _():
        run_scr[...] = jnp.zeros(run_scr.shape, F32)

    y = (jnp.dot(odn_ref[...], wo_ref[0:DN_WIDTH, :], preferred_element_type=F32)
         + jnp.dot(osw_ref[...], wo_ref[DN_WIDTH:, :], preferred_element_type=F32))
    x1 = x_ref[...] + mod_ref[0, 2:3, :] * y
    x1_ref[...] = x1
    hn = x1 * lax.rsqrt(jnp.mean(x1 * x1, axis=-1, keepdims=True) + EPS) * n2w_ref[...]
    h2 = hn * (1.0 + mod_ref[0, 4:5, :]) + mod_ref[0, 3:4, :]
    _store_token_tiles(h2_ref, h2)

    lane = lax.broadcasted_iota(I32, (tm, LANES), 1)
    lane_f = lane.astype(F32)
    logits = jnp.dot(h2.astype(BF16), wr_ref[...], preferred_element_type=F32) + br_ref[...]
    work = jnp.where(lane < N_EXPERTS, logits, NEG)
    sel = jnp.zeros((tm, LANES), F32)
    idx_out = jnp.zeros((tm, LANES), I32)
    vals = []
    idxs = []
    for kk in range(TOP_K):
        m = jnp.max(work, axis=-1, keepdims=True)
        ik = jnp.min(jnp.where(work == m, lane_f, float(LANES)), axis=-1,
                     keepdims=True).astype(I32)
        hit = lane == ik
        work = jnp.where(hit, NEG, work)
        sel = jnp.where(hit, 1.0, sel)
        idx_out = jnp.where(lane == kk, ik, idx_out)
        vals.append(m)
        idxs.append(ik)
    es = [jnp.exp(v - vals[0]) for v in vals]
    tot = es[0] + es[1] + es[2] + es[3]
    gate_out = jnp.zeros((tm, LANES), F32)
    for kk in range(TOP_K):
        gate_out = jnp.where(lane == kk, es[kk] / tot, gate_out)
    idx_ref[...] = idx_out
    gate_ref[...] = gate_out

    r = lax.broadcasted_iota(I32, (tm, tm), 0)
    c = lax.broadcasted_iota(I32, (tm, tm), 1)
    before = jnp.where(c < r, 1.0, 0.0).astype(BF16)
    base = jnp.dot(before, sel.astype(BF16), preferred_element_type=F32) + run_scr[0:1, :]
    rank_out = jnp.zeros((tm, LANES), I32)
    for kk in range(TOP_K):
        rk = jnp.sum(jnp.where(lane == idxs[kk], base, 0.0), axis=-1, keepdims=True)
        rank_out = jnp.where(lane == kk, rk.astype(I32), rank_out)
    rank_ref[...] = rank_out
    run = run_scr[0:1, :] + jnp.sum(sel, axis=0, keepdims=True)
    run_scr[0:1, :] = run
    cnt_ref[...] = jnp.broadcast_to(run, cnt_ref.shape).astype(I32)


def _post(xf, odn, osw, mod, w_out, n2w, wr_pad, br_pad, seq):
    t = xf.shape[0]
    tiles_per_seq = seq // TM_POST
    row = lambda i: (i, 0)
    const = lambda i: (0, 0)
    return pl.pallas_call(
        _post_kernel,
        grid=(t // TM_POST,),
        in_specs=[pl.BlockSpec((TM_POST, D_MODEL), row),
                  pl.BlockSpec((TM_POST, DN_WIDTH), row),
                  pl.BlockSpec((TM_POST, SWA_WIDTH), row),
                  pl.BlockSpec((1, 6, D_MODEL), lambda i: (i // tiles_per_seq, 0, 0)),
                  pl.BlockSpec((D_MODEL, D_MODEL), const),
                  pl.BlockSpec((1, D_MODEL), const),
                  pl.BlockSpec((D_MODEL, LANES), const),
                  pl.BlockSpec((1, LANES), const)],
        out_specs=[pl.BlockSpec((TM_POST, D_MODEL), row),
                   pl.BlockSpec((TM_POST * SUBLANES, LANES), row),
                   pl.BlockSpec((TM_POST, LANES), row),
                   pl.BlockSpec((TM_POST, LANES), row),
                   pl.BlockSpec((TM_POST, LANES), row),
                   pl.BlockSpec((SUBLANES, LANES), const)],
        out_shape=[jax.ShapeDtypeStruct((t, D_MODEL), F32),
                   jax.ShapeDtypeStruct((t * SUBLANES, LANES), F32),
                   jax.ShapeDtypeStruct((t, LANES), I32),
                   jax.ShapeDtypeStruct((t, LANES), F32),
                   jax.ShapeDtypeStruct((t, LANES), I32),
                   jax.ShapeDtypeStruct((SUBLANES, LANES), I32)],
        scratch_shapes=[pltpu.VMEM((SUBLANES, LANES), F32)],
        compiler_params=pltpu.CompilerParams(dimension_semantics=("arbitrary",),
                                             vmem_limit_bytes=VMEM_LIMIT),
        name="post",
    )(xf, odn, osw, mod, w_out, n2w, wr_pad, br_pad)


def _invert_kernel(dest_ref, rowtok_ref):
    def clear(rw, carry):
        rowtok_ref[rw] = 0
        return carry

    lax.fori_loop(0, rowtok_ref.shape[0], clear, 0, unroll=8)

    def put(a, carry):
        rowtok_ref[dest_ref[a]] = a // TOP_K
        return carry

    lax.fori_loop(0, dest_ref.shape[0], put, 0, unroll=8)


def _invert(dest_flat, n_pad):
    return pl.pallas_call(
        _invert_kernel,
        in_specs=[pl.BlockSpec(memory_space=pltpu.SMEM)],
        out_specs=pl.BlockSpec(memory_space=pltpu.SMEM),
        out_shape=jax.ShapeDtypeStruct((n_pad,), I32),
        name="invert",
    )(dest_flat)


def _moe_kernel(be_ref, nv_ref, rowtok_ref, h_ref, wu_ref, bu_ref, wd_ref, bd_ref, ys_ref,
                wu_bf, wd_bf, xbuf, sem):
    i = pl.program_id(0)
    slot = i % 2

    def gather(blk, s):
        def body(rw, carry):
            for u in range(2):
                r = rw * 2 + u
                _token_copy(h_ref, rowtok_ref[blk * MOE_ROWS + r], xbuf.at[s], r,
                            sem.at[s]).start(priority=u)
            return carry

        lax.fori_loop(0, MOE_ROWS // 2, body, 0, unroll=4)

    @pl.when(i == 0)
    def _():
        gather(0, 0)

    @pl.when(i + 1 < nv_ref[0])
    def _():
        gather(i + 1, 1 - slot)

    @pl.when(i < nv_ref[0])
    def _():
        prev = be_ref[jnp.maximum(i - 1, 0)]

        @pl.when((i == 0) | (be_ref[i] != prev))
        def _():
            wu_bf[...] = wu_ref[0].astype(BF16)
            wd_bf[...] = wd_ref[0].astype(BF16)

        pltpu.make_async_copy(h_ref.at[pl.ds(0, MOE_ROWS * SUBLANES), :], xbuf.at[slot],
                              sem.at[slot]).wait()
        xb = _load_token_tiles(xbuf.at[slot], 0, MOE_ROWS).astype(BF16)
        hu = jnp.dot(xb, wu_bf[...], preferred_element_type=F32) + bu_ref[0]
        x_glu = jnp.minimum(hu[:, :D_FF], SWIGLU_LIMIT)
        x_lin = jnp.clip(hu[:, D_FF:], -SWIGLU_LIMIT, SWIGLU_LIMIT)
        act = x_glu * jax.nn.sigmoid(SWIGLU_ALPHA * x_glu) * (x_lin + 1.0)
        _store_token_tiles(
            ys_ref, jnp.dot(act.astype(BF16), wd_bf[...], preferred_element_type=F32) + bd_ref[0])

    @pl.when(i >= nv_ref[0])
    def _():
        ys_ref[...] = jnp.zeros(ys_ref.shape, F32)


def _moe(block_e, n_valid, row_tok, h2, w_up, b_up, w_down, b_down):
    n_pad = row_tok.shape[0]
    n_blocks = n_pad // MOE_ROWS
    out_rows = lambda i, be, nv, rt: (i, 0)
    expert = lambda i, be, nv, rt: (be[i], 0, 0)
    return pl.pallas_call(
        _moe_kernel,
        grid_spec=pltpu.PrefetchScalarGridSpec(
            num_scalar_prefetch=3,
            grid=(n_blocks,),
            in_specs=[pl.BlockSpec(memory_space=pl.ANY),
                      pl.BlockSpec((1, D_MODEL, 2 * D_FF), expert),
                      pl.BlockSpec((1, 1, 2 * D_FF), expert),
                      pl.BlockSpec((1, D_FF, D_MODEL), expert),
                      pl.BlockSpec((1, 1, D_MODEL), expert)],
            out_specs=pl.BlockSpec((MOE_ROWS * SUBLANES, LANES), out_rows),
            scratch_shapes=[pltpu.VMEM((D_MODEL, 2 * D_FF), BF16),
                            pltpu.VMEM((D_FF, D_MODEL), BF16),
                            pltpu.VMEM((2, MOE_ROWS * SUBLANES, LANES), F32),
                            pltpu.SemaphoreType.DMA((2,))]),
        out_shape=jax.ShapeDtypeStruct((n_pad * SUBLANES, LANES), F32),
        compiler_params=pltpu.CompilerParams(dimension_semantics=("arbitrary",),
                                             vmem_limit_bytes=VMEM_LIMIT),
        name="moe",
    )(block_e, n_valid, row_tok, h2, w_up, b_up, w_down, b_down)


def _combine_kernel(dcur_ref, dnext_ref, ys_ref, x1_ref, gate_ref, mod_ref, o_ref, buf, sem):
    i = pl.program_id(0)
    tm = x1_ref.shape[0]
    slot = i % 2

    def issue(dref, s):
        def body(tk, carry):
            for kk in range(TOP_K):
                _token_copy(ys_ref, dref[0, 0, tk * TOP_K + kk], buf.at[s], kk * tm + tk,
                            sem.at[s]).start(priority=kk % 2)
            return carry

        lax.fori_loop(0, tm, body, 0, unroll=4)

    @pl.when(i == 0)
    def _():
        issue(dcur_ref, 0)

    @pl.when(i + 1 < pl.num_programs(0))
    def _():
        issue(dnext_ref, 1 - slot)

    pltpu.make_async_copy(ys_ref.at[pl.ds(0, TOP_K * tm * SUBLANES), :], buf.at[slot],
                          sem.at[slot]).wait()

    gates = gate_ref[...]
    acc = jnp.zeros((tm, D_MODEL), F32)
    for kk in range(TOP_K):
        acc = acc + gates[:, kk:kk + 1] * _load_token_tiles(buf.at[slot], kk * tm, tm)
    o_ref[...] = x1_ref[...] + mod_ref[0, 5:6, :] * acc


def _combine(dest3, ys, x1, gates, mod, seq):
    t = x1.shape[0]
    tiles_per_seq = seq // TM_COMB
    n_tiles = t // TM_COMB
    row = lambda i: (i, 0)
    return pl.pallas_call(
        _combine_kernel,
        grid=(n_tiles,),
        in_specs=[pl.BlockSpec((1, 1, TM_COMB * TOP_K), lambda i: (i, 0, 0),
                               memory_space=pltpu.SMEM),
                  pl.BlockSpec((1, 1, TM_COMB * TOP_K),
                               lambda i: (jnp.minimum(i + 1, n_tiles - 1), 0, 0),
                               memory_space=pltpu.SMEM),
                  pl.BlockSpec(memory_space=pl.ANY),
                  pl.BlockSpec((TM_COMB, D_MODEL), row),
                  pl.BlockSpec((TM_COMB, LANES), row),
                  pl.BlockSpec((1, 6, D_MODEL), lambda i: (i // tiles_per_seq, 0, 0))],
        out_specs=pl.BlockSpec((TM_COMB, D_MODEL), row),
        out_shape=jax.ShapeDtypeStruct((t, D_MODEL), F32),
        scratch_shapes=[pltpu.VMEM((2, TOP_K * TM_COMB * SUBLANES, LANES), F32),
                        pltpu.SemaphoreType.DMA((2,))],
        compiler_params=pltpu.CompilerParams(dimension_semantics=("arbitrary",),
                                             vmem_limit_bytes=VMEM_LIMIT),
        name="combine",
    )(dest3, dest3, ys, x1, gates, mod)


def _layer(x, mod, norm1_w, w_in, conv_w, a_log, dt_bias, dn_norm_w, q_norm_w, k_norm_w, sinks,
           w_out, norm2_w, w_router, b_router, w_up, b_up, w_down, b_down):
    batch, seq, _ = x.shape
    t = batch * seq
    xf = x.reshape(t, D_MODEL)

    o_qkv, o_z, o_a, o_b, o_q, o_k, o_v = 0, 1536, 2048, 2052, 2056, 2568, 2696
    w_ab = jnp.zeros((D_MODEL, LANES), F32).at[:, 0:2 * DN_HEADS].set(w_in[:, o_a:o_q])
    w_cat = jnp.concatenate([w_in[:, o_qkv:o_z], w_in[:, o_z:o_a], w_in[:, o_q:o_k],
                             w_in[:, o_k:o_v], w_in[:, o_v:], w_ab], axis=1).astype(BF16)
    gparams = jnp.zeros((2, LANES), F32)
    gparams = gparams.at[0, 0:DN_HEADS].set(-jnp.exp(a_log.astype(F32)))
    gparams = gparams.at[1, 0:DN_HEADS].set(dt_bias.astype(F32))
    qnw128 = jnp.tile(q_norm_w.astype(F32), 2)[None, :]
    knw128 = jnp.tile(k_norm_w.astype(F32), 2)[None, :]

    q, k, v, z, gb, swq, swk, swv = _inproj(xf, mod, norm1_w[None, :], w_cat, conv_w, gparams,
                                            qnw128, knw128, seq)
    o_dn = _deltanet(q, k, v, z, gb, dn_norm_w[None, :], batch, seq)
    o_sw = _swa(swq, swk, swv, sinks.astype(F32), batch, seq)

    wr_pad = jnp.zeros((D_MODEL, LANES), F32).at[:, 0:N_EXPERTS].set(w_router).astype(BF16)
    br_pad = jnp.zeros((1, LANES), F32).at[0, 0:N_EXPERTS].set(b_router)
    x1, h2, top_idx, gates, rank, cnt = _post(xf, o_dn, o_sw, mod, w_out.astype(BF16),
                                              norm2_w[None, :], wr_pad, br_pad, seq)

    counts = cnt[0, 0:N_EXPERTS]
    padded = (counts + MOE_ROWS - 1) // MOE_ROWS * MOE_ROWS
    pend = jnp.cumsum(padded)
    pstart = pend - padded
    n_blocks = (t * TOP_K) // MOE_ROWS + N_EXPERTS
    n_pad = n_blocks * MOE_ROWS
    n_valid = (pend[-1] // MOE_ROWS).astype(I32)
    blk = jnp.minimum(jnp.arange(n_blocks, dtype=I32), n_valid - 1)
    block_e = jnp.minimum(jnp.sum(pend[None, :] <= (blk * MOE_ROWS)[:, None], axis=1),
                          N_EXPERTS - 1).astype(I32)
    dest = (pstart[top_idx[:, 0:TOP_K]] + rank[:, 0:TOP_K]).astype(I32)

    row_tok = _invert(dest.reshape(t * TOP_K), n_pad)
    ys = _moe(block_e, n_valid.reshape(1), row_tok, h2, w_up, b_up[:, None, :], w_down,
              b_down[:, None, :])
    out = _combine(dest.reshape(t // TM_COMB, 1, TM_COMB * TOP_K), ys, x1, gates, mod, seq)
    return out.reshape(batch, seq, D_MODEL)


def kernel(x, c, w_ada, b_ada, norm1_w, w_in, conv_w, a_log, dt_bias, dn_norm_w, q_norm_w,
           k_norm_w, sinks, w_out, norm2_w, w_router, b_router, w_up, b_up, w_down, b_down):
    out_dtype = x.dtype
    batch = x.shape[0]
    depth = w_ada.shape[0]
    c_pad = jnp.zeros((SUBLANES, D_MODEL), F32).at[0:batch].set(c.astype(F32))
    for l in range(depth):
        mod = _ada(c_pad, w_ada[l], b_ada[l][None, :])[0:batch].reshape(batch, 6, D_MODEL)
        x = _layer(x, mod, norm1_w[l], w_in[l], conv_w[l], a_log[l], dt_bias[l], dn_norm_w[l],
                   q_norm_w[l], k_norm_w[l], sinks[l], w_out[l], norm2_w[l], w_router[l],
                   b_router[l], w_up[l], b_up[l], w_down[l], b_down[l])
    return x.astype(out_dtype)
```

```python
import functools

import jax
import jax.numpy as jnp
from jax import lax
from jax.experimental import pallas as pl
from jax.experimental.pallas import tpu as pltpu

F32 = jnp.float32
BF16 = jnp.bfloat16
I32 = jnp.int32

D_MODEL = 1024
DN_HEADS = 4
DN_HEAD_DIM = 128
DN_WIDTH = DN_HEADS * DN_HEAD_DIM
CONV_WIDTH = 4
CHUNK = 64
SWA_HEADS = 8
SWA_KV_HEADS = 2
SWA_GROUP = SWA_HEADS // SWA_KV_HEADS
SWA_HEAD_DIM = 64
SWA_WIDTH = SWA_HEADS * SWA_HEAD_DIM
SWA_KV_WIDTH = SWA_KV_HEADS * SWA_HEAD_DIM
WINDOW = 128
N_EXPERTS = 32
TOP_K = 4
D_FF = D_MODEL
SWIGLU_ALPHA = 1.702
SWIGLU_LIMIT = 7.0
EPS = 1e-6
NEG = -1e30

LANES = 128
SUBLANES = 8
VMEM_LIMIT = 56 * 1024 * 1024

C_QKV = 0
C_Z = C_QKV + 3 * DN_WIDTH
C_SWQ = C_Z + DN_WIDTH
C_SWK = C_SWQ + SWA_WIDTH
C_SWV = C_SWK + SWA_KV_WIDTH
C_AB = C_SWV + SWA_KV_WIDTH
IN_PAD = C_AB + LANES

TM_IN = 256
DN_ROWS = 256
TM_POST = 512
MOE_ROWS = 256
TM_COMB = 128


def _silu(x):
    return x * jax.nn.sigmoid(x)


def _mm(a, b):
    return jnp.dot(a.astype(BF16), b.astype(BF16), preferred_element_type=F32)


def _mm_nt(a, b):
    return lax.dot_general(a.astype(BF16), b.astype(BF16), (((1,), (1,)), ((), ())),
                           preferred_element_type=F32)


def _mm_tn(a, b):
    return lax.dot_general(a.astype(BF16), b.astype(BF16), (((0,), (0,)), ((), ())),
                           preferred_element_type=F32)


def _load_token_tiles(ref, row0, rows):
    return jnp.concatenate(
        [ref[pl.ds(row0 * SUBLANES + s, rows, stride=SUBLANES), :] for s in range(D_MODEL // LANES)],
        axis=1)


def _store_token_tiles(ref, val):
    rows = val.shape[0]
    for s in range(D_MODEL // LANES):
        ref[pl.ds(s, rows, stride=SUBLANES), :] = val[:, s * LANES:(s + 1) * LANES]


def _token_copy(src_ref, s, dst_ref, d, sem):
    return pltpu.make_async_copy(
        src_ref.at[pl.ds(pl.multiple_of(s * SUBLANES, SUBLANES), SUBLANES), :],
        dst_ref.at[pl.ds(pl.multiple_of(d * SUBLANES, SUBLANES), SUBLANES), :], sem)


def _ada_kernel(c_ref, w_ref, b_ref, o_ref):
    c = c_ref[...]
    o_ref[...] = jnp.dot(_silu(c), w_ref[...], precision=lax.Precision.HIGHEST,
                         preferred_element_type=F32) + b_ref[...]


def _ada(c_pad, w_ada, b_ada):
    n = w_ada.shape[1]
    return pl.pallas_call(
        _ada_kernel,
        grid=(n // D_MODEL,),
        in_specs=[pl.BlockSpec((SUBLANES, D_MODEL), lambda j: (0, 0)),
                  pl.BlockSpec((D_MODEL, D_MODEL), lambda j: (0, j)),
                  pl.BlockSpec((1, D_MODEL), lambda j: (0, j))],
        out_specs=pl.BlockSpec((SUBLANES, D_MODEL), lambda j: (0, j)),
        out_shape=jax.ShapeDtypeStruct((SUBLANES, n), F32),
        name="ada",
    )(c_pad, w_ada, b_ada)


def _rms_heads64(a, w128):
    lane = lax.broadcasted_iota(I32, a.shape, 1)
    lo = lane < SWA_HEAD_DIM
    sq = a * a
    s_lo = jnp.sum(jnp.where(lo, sq, 0.0), axis=-1, keepdims=True)
    s_hi = jnp.sum(jnp.where(lo, 0.0, sq), axis=-1, keepdims=True)
    ms = jnp.where(lo, s_lo, s_hi) * (1.0 / SWA_HEAD_DIM)
    return a * lax.rsqrt(ms + EPS) * w128


def _inproj_kernel(tiles_per_seq, x_ref, mod_ref, n1w_ref, w_ref, convw_ref, gp_ref, qnw_ref,
                   knw_ref, q_ref, k_ref, v_ref, z_ref, gb_ref, swq_ref, swk_ref, swv_ref,
                   conv_scr):
    i = pl.program_id(0)
    tm = x_ref.shape[0]
    x = x_ref[...]
    y = x * lax.rsqrt(jnp.mean(x * x, axis=-1, keepdims=True) + EPS) * n1w_ref[...]
    h = (y * (1.0 + mod_ref[0, 1:2, :]) + mod_ref[0, 0:1, :]).astype(BF16)

    @pl.when(i % tiles_per_seq == 0)
    def _():
        conv_scr[0:SUBLANES, :] = jnp.zeros((SUBLANES, 3 * DN_WIDTH), F32)

    u = jnp.dot(h, w_ref[:, C_QKV:C_Z], preferred_element_type=F32)
    conv_scr[SUBLANES:SUBLANES + tm, :] = u
    cw = convw_ref[...]
    acc = u * cw[3:4, :]
    for j in range(CONV_WIDTH - 1):
        off = SUBLANES - (CONV_WIDTH - 1) + j
        acc = acc + conv_scr[off:off + tm, :] * cw[j:j + 1, :]
    conv_scr[0:SUBLANES, :] = conv_scr[tm:tm + SUBLANES, :]
    qkv = _silu(acc)

    for hh in range(DN_HEADS):
        sl = slice(hh * DN_HEAD_DIM, (hh + 1) * DN_HEAD_DIM)
        qh = qkv[:, hh * DN_HEAD_DIM:(hh + 1) * DN_HEAD_DIM]
        kh = qkv[:, DN_WIDTH + hh * DN_HEAD_DIM:DN_WIDTH + (hh + 1) * DN_HEAD_DIM]
        q_ref[:, sl] = qh * lax.rsqrt(jnp.sum(qh * qh, axis=-1, keepdims=True) + EPS) * (
            DN_HEAD_DIM ** -0.5)
        k_ref[:, sl] = kh * lax.rsqrt(jnp.sum(kh * kh, axis=-1, keepdims=True) + EPS)
    v_ref[...] = qkv[:, 2 * DN_WIDTH:3 * DN_WIDTH]

    z_ref[...] = jnp.dot(h, w_ref[:, C_Z:C_SWQ], preferred_element_type=F32)

    ab = jnp.dot(h, w_ref[:, C_AB:IN_PAD], preferred_element_type=F32)
    lane = lax.broadcasted_iota(I32, ab.shape, 1)
    sp_in = ab + gp_ref[1:2, :]
    softplus = jnp.maximum(sp_in, 0.0) + jnp.log1p(jnp.exp(-jnp.abs(sp_in)))
    gb_ref[...] = jnp.where(lane < DN_HEADS, gp_ref[0:1, :] * softplus, jax.nn.sigmoid(ab))

    sq = jnp.dot(h, w_ref[:, C_SWQ:C_SWK], preferred_element_type=F32)
    for t in range(SWA_WIDTH // LANES):
        sl = slice(t * LANES, (t + 1) * LANES)
        swq_ref[:, sl] = (_rms_heads64(sq[:, sl], qnw_ref[...]) * (SWA_HEAD_DIM ** -0.5)).astype(BF16)
    sk = jnp.dot(h, w_ref[:, C_SWK:C_SWV], preferred_element_type=F32)
    swk_ref[...] = _rms_heads64(sk, knw_ref[...]).astype(BF16)
    swv_ref[...] = jnp.dot(h, w_ref[:, C_SWV:C_AB], preferred_element_type=F32).astype(BF16)


def _inproj(xf, mod, n1w, w_cat, conv_w, gparams, qnw128, knw128, seq):
    t = xf.shape[0]
    tiles_per_seq = seq // TM_IN
    row = lambda i: (i, 0)
    const = lambda i: (0, 0)
    outs = [jax.ShapeDtypeStruct((t, DN_WIDTH), F32)] * 4 + [
        jax.ShapeDtypeStruct((t, LANES), F32),
        jax.ShapeDtypeStruct((t, SWA_WIDTH), BF16),
        jax.ShapeDtypeStruct((t, SWA_KV_WIDTH), BF16),
        jax.ShapeDtypeStruct((t, SWA_KV_WIDTH), BF16)]
    out_specs = [pl.BlockSpec((TM_IN, DN_WIDTH), row)] * 4 + [
        pl.BlockSpec((TM_IN, LANES), row),
        pl.BlockSpec((TM_IN, SWA_WIDTH), row),
        pl.BlockSpec((TM_IN, SWA_KV_WIDTH), row),
        pl.BlockSpec((TM_IN, SWA_KV_WIDTH), row)]
    return pl.pallas_call(
        functools.partial(_inproj_kernel, tiles_per_seq),
        grid=(t // TM_IN,),
        in_specs=[pl.BlockSpec((TM_IN, D_MODEL), row),
                  pl.BlockSpec((1, 6, D_MODEL), lambda i: (i // tiles_per_seq, 0, 0)),
                  pl.BlockSpec((1, D_MODEL), const),
                  pl.BlockSpec((D_MODEL, IN_PAD), const),
                  pl.BlockSpec((CONV_WIDTH, 3 * DN_WIDTH), const),
                  pl.BlockSpec((2, LANES), const),
                  pl.BlockSpec((1, LANES), const),
                  pl.BlockSpec((1, LANES), const)],
        out_specs=out_specs,
        out_shape=outs,
        scratch_shapes=[pltpu.VMEM((TM_IN + SUBLANES, 3 * DN_WIDTH), F32)],
        compiler_params=pltpu.CompilerParams(dimension_semantics=("arbitrary",),
                                             vmem_limit_bytes=VMEM_LIMIT),
        name="inproj",
    )(xf, mod, n1w, w_cat, conv_w, gparams, qnw128, knw128)


def _dn_kernel(q_ref, k_ref, v_ref, z_ref, gb_ref, nw_ref, o_ref,
               s_scr, u_scr, w_scr, qd_scr, kd_scr, a_scr):
    j = pl.program_id(1)
    rows = q_ref.shape[0]
    n_chunks = rows // CHUNK
    heads = range(DN_HEADS)

    @pl.when(j == 0)
    def _():
        s_scr[...] = jnp.zeros(s_scr.shape, F32)

    r = lax.broadcasted_iota(I32, (rows, rows), 0)
    c = lax.broadcasted_iota(I32, (rows, rows), 1)
    same = (r // CHUNK) == (c // CHUNK)
    incl = same & (c <= r)
    strict = same & (c < r)
    gb = gb_ref[...]
    gc = jnp.dot(jnp.where(incl, 1.0, 0.0), gb, precision=lax.Precision.HIGHEST,
                 preferred_element_type=F32)
    gl = jnp.dot(jnp.where(same, 1.0, 0.0), gb, precision=lax.Precision.HIGHEST,
                 preferred_element_type=F32)
    gct = gc.T

    rmat = []
    rhs = []
    for h in heads:
        ls = slice(h * DN_HEAD_DIM, (h + 1) * DN_HEAD_DIM)
        g_col = gc[:, h:h + 1]
        beta = gb[:, DN_HEADS + h:DN_HEADS + h + 1]
        decay = jnp.where(incl, jnp.exp(jnp.where(incl, g_col - gct[h:h + 1, :], 0.0)), 0.0)
        qq = q_ref[:, ls]
        kk = k_ref[:, ls]
        kb = kk * beta
        eg = jnp.exp(g_col)
        rmat.append(-jnp.where(strict, _mm_nt(kb, kk) * decay, 0.0))
        rhs.append(jnp.concatenate([v_ref[:, ls] * beta, kb * eg], axis=1))
        a_scr[h] = jnp.where(incl, _mm_nt(qq, kk) * decay, 0.0)
        qd_scr[:, ls] = qq * eg
        kd_scr[:, ls] = kk * jnp.exp(gl[:, h:h + 1] - g_col)

    xm = list(rmat)
    pm = list(rmat)
    for it in range(6):
        xm = [xm[h] + _mm(pm[h], xm[h]) for h in heads]
        if it < 5:
            pm = [_mm(pm[h], pm[h]) for h in heads]
    for h in heads:
        ls = slice(h * DN_HEAD_DIM, (h + 1) * DN_HEAD_DIM)
        sol = rhs[h] + _mm(xm[h], rhs[h])
        u_scr[:, ls] = sol[:, 0:DN_HEAD_DIM]
        w_scr[:, ls] = sol[:, DN_HEAD_DIM:2 * DN_HEAD_DIM]

    nw = nw_ref[...]
    lsl = [slice(h * DN_HEAD_DIM, (h + 1) * DN_HEAD_DIM) for h in heads]
    for ci in range(n_chunks):
        rs = slice(ci * CHUNK, (ci + 1) * CHUNK)
        dvals = jnp.exp(gl[ci * CHUNK:ci * CHUNK + 1, :])
        st = [s_scr[h] for h in heads]
        ws = [_mm(w_scr[rs, lsl[h]], st[h]) for h in heads]
        qs = [_mm(qd_scr[rs, lsl[h]], st[h]) for h in heads]
        vn = [u_scr[rs, lsl[h]] - ws[h] for h in heads]
        av = [_mm(a_scr[h, rs, rs], vn[h]) for h in heads]
        kv = [_mm_tn(kd_scr[rs, lsl[h]], vn[h]) for h in heads]
        for h in heads:
            s_scr[h] = st[h] * dvals[:, h:h + 1] + kv[h]
            o = qs[h] + av[h]
            on = o * lax.rsqrt(jnp.mean(o * o, axis=-1, keepdims=True) + EPS) * nw
            o_ref[rs, lsl[h]] = (on * _silu(z_ref[rs, lsl[h]])).astype(o_ref.dtype)


def _deltanet(q, k, v, z, gb, nw, batch, seq):
    t = q.shape[0]
    steps = seq // DN_ROWS
    row = lambda b, j: (b * steps + j, 0)
    blk = pl.BlockSpec((DN_ROWS, DN_WIDTH), row)
    return pl.pallas_call(
        _dn_kernel,
        grid=(batch, steps),
        in_specs=[blk, blk, blk, blk,
                  pl.BlockSpec((DN_ROWS, LANES), row),
                  pl.BlockSpec((1, DN_HEAD_DIM), lambda b, j: (0, 0))],
        out_specs=blk,
        out_shape=jax.ShapeDtypeStruct((t, DN_WIDTH), BF16),
        scratch_shapes=[pltpu.VMEM((DN_HEADS, DN_HEAD_DIM, DN_HEAD_DIM), F32),
                        pltpu.VMEM((DN_ROWS, DN_WIDTH), F32),
                        pltpu.VMEM((DN_ROWS, DN_WIDTH), F32),
                        pltpu.VMEM((DN_ROWS, DN_WIDTH), F32),
                        pltpu.VMEM((DN_ROWS, DN_WIDTH), F32),
                        pltpu.VMEM((DN_HEADS, DN_ROWS, DN_ROWS), F32)],
        compiler_params=pltpu.CompilerParams(dimension_semantics=("arbitrary", "arbitrary"),
                                             vmem_limit_bytes=VMEM_LIMIT),
        name="deltanet",
    )(q, k, v, z, gb, nw)


def _swa_kernel(sink_ref, q_ref, kp_ref, kc_ref, vp_ref, vc_ref, o_ref):
    n = pl.program_id(1)
    w = WINDOW
    kcat = jnp.concatenate([kp_ref[...], kc_ref[...]], axis=0)
    vcat = jnp.concatenate([vp_ref[...], vc_ref[...]], axis=0)
    rows = SWA_GROUP * w
    ri = lax.broadcasted_iota(I32, (rows, 2 * w), 0)
    kj = lax.broadcasted_iota(I32, (rows, 2 * w), 1)
    grp = ri // w
    rel = (ri - grp * w) + w - kj
    valid = (rel >= 0) & (rel < w) & ((kj >= w) | (n > 0))
    relf = rel.astype(F32)
    grp_col = lax.broadcasted_iota(I32, (rows, 1), 0) // w
    q = q_ref[...]
    outs = []
    for kh in range(SWA_KV_HEADS):
        slope_col = jnp.zeros((rows, 1), F32)
        sink_col = jnp.zeros((rows, 1), F32)
        for g in range(SWA_GROUP):
            hd = kh * SWA_GROUP + g
            slope_col = jnp.where(grp_col == g, 2.0 ** (-8.0 * (hd + 1) / SWA_HEADS), slope_col)
            sink_col = jnp.where(grp_col == g, sink_ref[hd], sink_col)
        qs = jnp.concatenate(
            [q[:, (kh * SWA_GROUP + g) * SWA_HEAD_DIM:(kh * SWA_GROUP + g + 1) * SWA_HEAD_DIM]
             for g in range(SWA_GROUP)], axis=0)
        kk = kcat[:, kh * SWA_HEAD_DIM:(kh + 1) * SWA_HEAD_DIM]
        vv = vcat[:, kh * SWA_HEAD_DIM:(kh + 1) * SWA_HEAD_DIM]
        s = lax.dot_general(qs, kk, (((1,), (1,)), ((), ())), preferred_element_type=F32)
        logits = jnp.where(valid, s - slope_col * relf, NEG)
        m = jnp.maximum(jnp.max(logits, axis=-1, keepdims=True), sink_col)
        p = jnp.exp(logits - m)
        denom = jnp.sum(p, axis=-1, keepdims=True) + jnp.exp(sink_col - m)
        o = jnp.dot(p.astype(BF16), vv, preferred_element_type=F32) / denom
        for g in range(SWA_GROUP):
            outs.append(o[g * w:(g + 1) * w, :])
    for t in range(SWA_WIDTH // LANES):
        o_ref[:, t * LANES:(t + 1) * LANES] = jnp.concatenate(
            [outs[2 * t], outs[2 * t + 1]], axis=-1).astype(o_ref.dtype)


def _swa(swq, swk, swv, sinks, batch, seq):
    t = swq.shape[0]
    nb = seq // WINDOW
    cur = lambda b, n, s: (b * nb + n, 0)
    prev = lambda b, n, s: (b * nb + jnp.maximum(n - 1, 0), 0)
    return pl.pallas_call(
        _swa_kernel,
        grid_spec=pltpu.PrefetchScalarGridSpec(
            num_scalar_prefetch=1,
            grid=(batch, nb),
            in_specs=[pl.BlockSpec((WINDOW, SWA_WIDTH), cur),
                      pl.BlockSpec((WINDOW, SWA_KV_WIDTH), prev),
                      pl.BlockSpec((WINDOW, SWA_KV_WIDTH), cur),
                      pl.BlockSpec((WINDOW, SWA_KV_WIDTH), prev),
                      pl.BlockSpec((WINDOW, SWA_KV_WIDTH), cur)],
            out_specs=pl.BlockSpec((WINDOW, SWA_WIDTH), cur)),
        out_shape=jax.ShapeDtypeStruct((t, SWA_WIDTH), BF16),
        compiler_params=pltpu.CompilerParams(dimension_semantics=("arbitrary", "arbitrary"),
                                             vmem_limit_bytes=VMEM_LIMIT),
        name="swa",
    )(sinks, swq, swk, swk, swv, swv)


def _post_kernel(x_ref, odn_ref, osw_ref, mod_ref, wo_ref, n2w_ref, wr_ref, br_ref,
                 x1_ref, h2_ref, idx_ref, gate_ref, rank_ref, cnt_ref, run_scr):
    i = pl.program_id(0)
    tm = x_ref.shape[0]

    @pl.when(i == 0)
    def _():
        run_scr[...] = jnp.zeros(run_scr.shape, F32)

    y = (jnp.dot(odn_ref[...], wo_ref[0:DN_WIDTH, :], preferred_element_type=F32)
         + jnp.dot(osw_ref[...], wo_ref[DN_WIDTH:, :], preferred_element_type=F32))
    x1 = x_ref[...] + mod_ref[0, 2:3, :] * y
    x1_ref[...] = x1
    hn = x1 * lax.rsqrt(jnp.mean(x1 * x1, axis=-1, keepdims=True) + EPS) * n2w_ref[...]
    h2 = hn * (1.0 + mod_ref[0, 4:5, :]) + mod_ref[0, 3:4, :]
    _store_token_tiles(h2_ref, h2)

    lane = lax.broadcasted_iota(I32, (tm, LANES), 1)
    lane_f = lane.astype(F32)
    logits = jnp.dot(h2.astype(BF16), wr_ref[...], preferred_element_type=F32) + br_ref[...]
    work = jnp.where(lane < N_EXPERTS, logits, NEG)
    sel = jnp.zeros((tm, LANES), F32)
    idx_out = jnp.zeros((tm, LANES), F32)
    vals = []
    idxs = []
    for kk in range(TOP_K):
        m = jnp.max(work, axis=-1, keepdims=True)
        ik = jnp.min(jnp.where(work == m, lane_f, float(LANES)), axis=-1, keepdims=True)
        hit = lane_f == ik
        work = jnp.where(hit, NEG, work)
        sel = jnp.where(hit, 1.0, sel)
        idx_out = jnp.where(lane == kk, ik, idx_out)
        vals.append(m)
        idxs.append(ik)
    es = [jnp.exp(v - vals[0]) for v in vals]
    tot = es[0] + es[1] + es[2] + es[3]
    gate_out = jnp.zeros((tm, LANES), F32)
    for kk in range(TOP_K):
        gate_out = jnp.where(lane == kk, es[kk] / tot, gate_out)
    gate_ref[...] = gate_out

    r = lax.broadcasted_iota(I32, (tm, tm), 0)
    c = lax.broadcasted_iota(I32, (tm, tm), 1)
    before = jnp.where(c < r, 1.0, 0.0).astype(BF16)
    base = jnp.dot(before, sel.astype(BF16), preferred_element_type=F32) + run_scr[0:1, :]
    rank_out = jnp.zeros((tm, LANES), F32)
    for kk in range(TOP_K):
        rk = jnp.sum(jnp.where(lane_f == idxs[kk], base, 0.0), axis=-1, keepdims=True)
        rank_out = jnp.where(lane == kk, rk, rank_out)
    idx_ref[...] = idx_out.T[0:SUBLANES, :].astype(I32)
    rank_ref[...] = rank_out.T[0:SUBLANES, :].astype(I32)
    run = run_scr[0:1, :] + jnp.sum(sel, axis=0, keepdims=True)
    run_scr[0:1, :] = run
    cnt_ref[...] = jnp.broadcast_to(run, cnt_ref.shape).astype(I32)


def _post(xf, odn, osw, mod, w_out, n2w, wr_pad, br_pad, seq):
    t = xf.shape[0]
    tiles_per_seq = seq // TM_POST
    row = lambda i: (i, 0)
    const = lambda i: (0, 0)
    return pl.pallas_call(
        _post_kernel,
        grid=(t // TM_POST,),
        in_specs=[pl.BlockSpec((TM_POST, D_MODEL), row),
                  pl.BlockSpec((TM_POST, DN_WIDTH), row),
                  pl.BlockSpec((TM_POST, SWA_WIDTH), row),
                  pl.BlockSpec((1, 6, D_MODEL), lambda i: (i // tiles_per_seq, 0, 0)),
                  pl.BlockSpec((D_MODEL, D_MODEL), const),
                  pl.BlockSpec((1, D_MODEL), const),
                  pl.BlockSpec((D_MODEL, LANES), const),
                  pl.BlockSpec((1, LANES), const)],
        out_specs=[pl.BlockSpec((TM_POST, D_MODEL), row),
                   pl.BlockSpec((TM_POST * SUBLANES, LANES), row),
                   pl.BlockSpec((SUBLANES, TM_POST), lambda i: (0, i)),
                   pl.BlockSpec((TM_POST, LANES), row),
                   pl.BlockSpec((SUBLANES, TM_POST), lambda i: (0, i)),
                   pl.BlockSpec((SUBLANES, LANES), const)],
        out_shape=[jax.ShapeDtypeStruct((t, D_MODEL), F32),
                   jax.ShapeDtypeStruct((t * SUBLANES, LANES), F32),
                   jax.ShapeDtypeStruct((SUBLANES, t), I32),
                   jax.ShapeDtypeStruct((t, LANES), F32),
                   jax.ShapeDtypeStruct((SUBLANES, t), I32),
                   jax.ShapeDtypeStruct((SUBLANES, LANES), I32)],
        scratch_shapes=[pltpu.VMEM((SUBLANES, LANES), F32)],
        compiler_params=pltpu.CompilerParams(dimension_semantics=("arbitrary",),
                                             vmem_limit_bytes=VMEM_LIMIT),
        name="post",
    )(xf, odn, osw, mod, w_out, n2w, wr_pad, br_pad)


INVERT_GROUP = 16


def _invert_kernel(fill_ref, dest_ref, rowtok_ref):
    def per_range(e, carry):
        def clear(rw, c2):
            rowtok_ref[rw] = 0
            return c2

        lax.fori_loop(fill_ref[0, e], fill_ref[1, e], clear, 0)
        return carry

    lax.fori_loop(0, fill_ref.shape[1], per_range, 0)

    n_tok = dest_ref.shape[0] // TOP_K
    for kk in range(TOP_K):
        def put(g, carry, kk=kk):
            tok0 = g * INVERT_GROUP
            rows = [dest_ref[kk * n_tok + tok0 + u] for u in range(INVERT_GROUP)]
            for u in range(INVERT_GROUP):
                rowtok_ref[rows[u]] = tok0 + u
            return carry

        lax.fori_loop(0, n_tok // INVERT_GROUP, put, 0)


def _invert(fill, dest_flat, n_pad):
    return pl.pallas_call(
        _invert_kernel,
        in_specs=[pl.BlockSpec(memory_space=pltpu.SMEM),
                  pl.BlockSpec(memory_space=pltpu.SMEM)],
        out_specs=pl.BlockSpec(memory_space=pltpu.SMEM),
        out_shape=jax.ShapeDtypeStruct((n_pad,), I32),
        name="invert",
    )(fill, dest_flat)


def _moe_kernel(be_ref, nv_ref, rowtok_ref, h_ref, wu_ref, bu_ref, wd_ref, bd_ref, ys_ref,
                wu_bf, wd_bf, xbuf, sem):
    i = pl.program_id(0)
    slot = i % 2
    n_valid = nv_ref[0]

    def gather(blk, s):
        for r in range(MOE_ROWS):
            _token_copy(h_ref, rowtok_ref[blk * MOE_ROWS + r], xbuf.at[s], r,
                        sem.at[s]).start(priority=r % 2)

    def gather_wait(s):
        pltpu.make_async_copy(h_ref.at[pl.ds(0, MOE_ROWS * SUBLANES), :], xbuf.at[s],
                              sem.at[s]).wait()

    @pl.when(i == 0)
    def _():
        gather(0, 0)

    @pl.when(i < n_valid)
    def _():
        prev = be_ref[jnp.maximum(i - 1, 0)]

        @pl.when((i == 0) | (be_ref[i] != prev))
        def _():
            wu_bf[...] = wu_ref[0].astype(BF16)
            wd_bf[...] = wd_ref[0].astype(BF16)

        gather_wait(slot)
        gather(jnp.minimum(i + 1, n_valid - 1), 1 - slot)
        xb = _load_token_tiles(xbuf.at[slot], 0, MOE_ROWS).astype(BF16)
        hu = jnp.dot(xb, wu_bf[...], preferred_element_type=F32) + bu_ref[0]
        x_glu = jnp.minimum(hu[:, :D_FF], SWIGLU_LIMIT)
        x_lin = jnp.clip(hu[:, D_FF:], -SWIGLU_LIMIT, SWIGLU_LIMIT)
        act = x_glu * jax.nn.sigmoid(SWIGLU_ALPHA * x_glu) * (x_lin + 1.0)
        _store_token_tiles(
            ys_ref, jnp.dot(act.astype(BF16), wd_bf[...], preferred_element_type=F32) + bd_ref[0])

    @pl.when(i == n_valid - 1)
    def _():
        gather_wait(1 - slot)

    @pl.when(i >= n_valid)
    def _():
        ys_ref[...] = jnp.zeros(ys_ref.shape, F32)


def _moe(block_e, n_valid, row_tok, h2, w_up, b_up, w_down, b_down):
    n_pad = row_tok.shape[0]
    n_blocks = n_pad // MOE_ROWS
    out_rows = lambda i, be, nv, rt: (i, 0)
    expert = lambda i, be, nv, rt: (be[i], 0, 0)
    return pl.pallas_call(
        _moe_kernel,
        grid_spec=pltpu.PrefetchScalarGridSpec(
            num_scalar_prefetch=3,
            grid=(n_blocks,),
            in_specs=[pl.BlockSpec(memory_space=pl.ANY),
                      pl.BlockSpec((1, D_MODEL, 2 * D_FF), expert),
                      pl.BlockSpec((1, 1, 2 * D_FF), expert),
                      pl.BlockSpec((1, D_FF, D_MODEL), expert),
                      pl.BlockSpec((1, 1, D_MODEL), expert)],
            out_specs=pl.BlockSpec((MOE_ROWS * SUBLANES, LANES), out_rows),
            scratch_shapes=[pltpu.VMEM((D_MODEL, 2 * D_FF), BF16),
                            pltpu.VMEM((D_FF, D_MODEL), BF16),
                            pltpu.VMEM((2, MOE_ROWS * SUBLANES, LANES), F32),
                            pltpu.SemaphoreType.DMA((2,))]),
        out_shape=jax.ShapeDtypeStruct((n_pad * SUBLANES, LANES), F32),
        compiler_params=pltpu.CompilerParams(dimension_semantics=("arbitrary",),
                                             vmem_limit_bytes=VMEM_LIMIT),
        name="moe",
    )(block_e, n_valid, row_tok, h2, w_up, b_up, w_down, b_down)


def _combine_kernel(dcur_ref, dnext_ref, ys_ref, x1_ref, gate_ref, mod_ref, o_ref, buf, sem):
    i = pl.program_id(0)
    tm = x1_ref.shape[0]
    slot = i % 2

    def issue(dref, s):
        def body(tk, carry):
            for kk in range(TOP_K):
                _token_copy(ys_ref, dref[0, 0, kk * tm + tk], buf.at[s], kk * tm + tk,
                            sem.at[s]).start(priority=kk % 2)
            return carry

        lax.fori_loop(0, tm, body, 0, unroll=4)

    @pl.when(i == 0)
    def _():
        issue(dcur_ref, 0)

    @pl.when(i + 1 < pl.num_programs(0))
    def _():
        issue(dnext_ref, 1 - slot)

    pltpu.make_async_copy(ys_ref.at[pl.ds(0, TOP_K * tm * SUBLANES), :], buf.at[slot],
                          sem.at[slot]).wait()

    gates = gate_ref[...]
    acc = jnp.zeros((tm, D_MODEL), F32)
    for kk in range(TOP_K):
        acc = acc + gates[:, kk:kk + 1] * _load_token_tiles(buf.at[slot], kk * tm, tm)
    o_ref[...] = x1_ref[...] + mod_ref[0, 5:6, :] * acc


def _combine(dest3, ys, x1, gates, mod, seq):
    t = x1.shape[0]
    tiles_per_seq = seq // TM_COMB
    n_tiles = t // TM_COMB
    row = lambda i: (i, 0)
    return pl.pallas_call(
        _combine_kernel,
        grid=(n_tiles,),
        in_specs=[pl.BlockSpec((1, 1, TM_COMB * TOP_K), lambda i: (i, 0, 0),
                               memory_space=pltpu.SMEM),
                  pl.BlockSpec((1, 1, TM_COMB * TOP_K),
                               lambda i: (jnp.minimum(i + 1, n_tiles - 1), 0, 0),
                               memory_space=pltpu.SMEM),
                  pl.BlockSpec(memory_space=pl.ANY),
                  pl.BlockSpec((TM_COMB, D_MODEL), row),
                  pl.BlockSpec((TM_COMB, LANES), row),
                  pl.BlockSpec((1, 6, D_MODEL), lambda i: (i // tiles_per_seq, 0, 0))],
        out_specs=pl.BlockSpec((TM_COMB, D_MODEL), row),
        out_shape=jax.ShapeDtypeStruct((t, D_MODEL), F32),
        scratch_shapes=[pltpu.VMEM((2, TOP_K * TM_COMB * SUBLANES, LANES), F32),
                        pltpu.SemaphoreType.DMA((2,))],
        compiler_params=pltpu.CompilerParams(dimension_semantics=("arbitrary",),
                                             vmem_limit_bytes=VMEM_LIMIT),
        name="combine",
    )(dest3, dest3, ys, x1, gates, mod)


def _layer(x, mod, norm1_w, w_in, conv_w, a_log, dt_bias, dn_norm_w, q_norm_w, k_norm_w, sinks,
           w_out, norm2_w, w_router, b_router, w_up, b_up, w_down, b_down):
    batch, seq, _ = x.shape
    t = batch * seq
    xf = x.reshape(t, D_MODEL)

    o_qkv, o_z, o_a, o_b, o_q, o_k, o_v = 0, 1536, 2048, 2052, 2056, 2568, 2696
    w_ab = jnp.zeros((D_MODEL, LANES), F32).at[:, 0:2 * DN_HEADS].set(w_in[:, o_a:o_q])
    w_cat = jnp.concatenate([w_in[:, o_qkv:o_z], w_in[:, o_z:o_a], w_in[:, o_q:o_k],
                             w_in[:, o_k:o_v], w_in[:, o_v:], w_ab], axis=1).astype(BF16)
    gparams = jnp.zeros((2, LANES), F32)
    gparams = gparams.at[0, 0:DN_HEADS].set(-jnp.exp(a_log.astype(F32)))
    gparams = gparams.at[1, 0:DN_HEADS].set(dt_bias.astype(F32))
    qnw128 = jnp.tile(q_norm_w.astype(F32), 2)[None, :]
    knw128 = jnp.tile(k_norm_w.astype(F32), 2)[None, :]

    q, k, v, z, gb, swq, swk, swv = _inproj(xf, mod, norm1_w[None, :], w_cat, conv_w, gparams,
                                            qnw128, knw128, seq)
    o_dn = _deltanet(q, k, v, z, gb, dn_norm_w[None, :], batch, seq)
    o_sw = _swa(swq, swk, swv, sinks.astype(F32), batch, seq)

    wr_pad = jnp.zeros((D_MODEL, LANES), F32).at[:, 0:N_EXPERTS].set(w_router).astype(BF16)
    br_pad = jnp.zeros((1, LANES), F32).at[0, 0:N_EXPERTS].set(b_router)
    x1, h2, top_idx, gates, rank, cnt = _post(xf, o_dn, o_sw, mod, w_out.astype(BF16),
                                              norm2_w[None, :], wr_pad, br_pad, seq)

    counts = cnt[0, 0:N_EXPERTS]
    padded = (counts + MOE_ROWS - 1) // MOE_ROWS * MOE_ROWS
    pend = jnp.cumsum(padded)
    pstart = pend - padded
    n_blocks = (t * TOP_K) // MOE_ROWS + N_EXPERTS
    n_pad = n_blocks * MOE_ROWS
    n_valid = (pend[-1] // MOE_ROWS).astype(I32)
    blk = jnp.minimum(jnp.arange(n_blocks, dtype=I32), n_valid - 1)
    block_e = jnp.minimum(jnp.sum(pend[None, :] <= (blk * MOE_ROWS)[:, None], axis=1),
                          N_EXPERTS - 1).astype(I32)
    dest = (pstart[top_idx[0:TOP_K]] + rank[0:TOP_K]).astype(I32)
    fill = jnp.stack([jnp.append(pstart + counts, pend[-1]),
                      jnp.append(pend, n_pad)]).astype(I32)

    row_tok = _invert(fill, dest.reshape(TOP_K * t), n_pad)
    ys = _moe(block_e, n_valid.reshape(1), row_tok, h2, w_up, b_up[:, None, :], w_down,
              b_down[:, None, :])
    n_tiles = t // TM_COMB
    dest_tiles = dest.reshape(TOP_K, n_tiles, TM_COMB).transpose(1, 0, 2).reshape(
        n_tiles, 1, TOP_K * TM_COMB)
    out = _combine(dest_tiles, ys, x1, gates, mod, seq)
    return out.reshape(batch, seq, D_MODEL)


def kernel(x, c, w_ada, b_ada, norm1_w, w_in, conv_w, a_log, dt_bias, dn_norm_w, q_norm_w,
           k_norm_w, sinks, w_out, norm2_w, w_router, b_router, w_up, b_up, w_down, b_down):
    out_dtype = x.dtype
    batch = x.shape[0]
    depth = w_ada.shape[0]
    c_pad = jnp.zeros((SUBLANES, D_MODEL), F32).at[0:batch].set(c.astype(F32))
    for l in range(depth):
        mod = _ada(c_pad, w_ada[l], b_ada[l][None, :])[0:batch].reshape(batch, 6, D_MODEL)
        x = _layer(x, mod, norm1_w[l], w_in[l], conv_w[l], a_log[l], dt_bias[l], dn_norm_w[l],
                   q_norm_w[l], k_norm_w[l], sinks[l], w_out[l], norm2_w[l], w_router[l],
                   b_router[l], w_up[l], b_up[l], w_down[l], b_down[l])
    return x.astype(out_dtype)
```

```python
import functools

import jax
import jax.numpy as jnp
from jax import lax
from jax.experimental import pallas as pl
from jax.experimental.pallas import tpu as pltpu

F32 = jnp.float32
BF16 = jnp.bfloat16
I32 = jnp.int32

D_MODEL = 1024
DN_HEADS = 4
DN_HEAD_DIM = 128
DN_WIDTH = DN_HEADS * DN_HEAD_DIM
CONV_WIDTH = 4
CHUNK = 64
SWA_HEADS = 8
SWA_KV_HEADS = 2
SWA_GROUP = SWA_HEADS // SWA_KV_HEADS
SWA_HEAD_DIM = 64
SWA_WIDTH = SWA_HEADS * SWA_HEAD_DIM
SWA_KV_WIDTH = SWA_KV_HEADS * SWA_HEAD_DIM
WINDOW = 128
N_EXPERTS = 32
TOP_K = 4
D_FF = D_MODEL
SWIGLU_ALPHA = 1.702
SWIGLU_LIMIT = 7.0
EPS = 1e-6
NEG = -1e30

LANES = 128
SUBLANES = 8
VMEM_LIMIT = 56 * 1024 * 1024

C_QKV = 0
C_Z = C_QKV + 3 * DN_WIDTH
C_SWQ = C_Z + DN_WIDTH
C_SWK = C_SWQ + SWA_WIDTH
C_SWV = C_SWK + SWA_KV_WIDTH
C_AB = C_SWV + SWA_KV_WIDTH
IN_PAD = C_AB + LANES

TM_IN = 256
DN_ROWS = 256
TM_POST = 512
MOE_ROWS = 256
TM_COMB = 128


def _silu(x):
    return x * jax.nn.sigmoid(x)


def _mm(a, b):
    return jnp.dot(a.astype(BF16), b.astype(BF16), preferred_element_type=F32)


def _mm_nt(a, b):
    return lax.dot_general(a.astype(BF16), b.astype(BF16), (((1,), (1,)), ((), ())),
                           preferred_element_type=F32)


def _mm_tn(a, b):
    return lax.dot_general(a.astype(BF16), b.astype(BF16), (((0,), (0,)), ((), ())),
                           preferred_element_type=F32)


def _load_token_tiles(ref, row0, rows):
    return jnp.concatenate(
        [ref[pl.ds(row0 * SUBLANES + s, rows, stride=SUBLANES), :] for s in range(D_MODEL // LANES)],
        axis=1)


def _store_token_tiles(ref, val):
    rows = val.shape[0]
    for s in range(D_MODEL // LANES):
        ref[pl.ds(s, rows, stride=SUBLANES), :] = val[:, s * LANES:(s + 1) * LANES]


def _token_copy(src_ref, s, dst_ref, d, sem):
    return pltpu.make_async_copy(
        src_ref.at[pl.ds(pl.multiple_of(s * SUBLANES, SUBLANES), SUBLANES), :],
        dst_ref.at[pl.ds(pl.multiple_of(d * SUBLANES, SUBLANES), SUBLANES), :], sem)


def _ada_kernel(c_ref, w_ref, b_ref, o_ref):
    c = c_ref[...]
    o_ref[...] = jnp.dot(_silu(c), w_ref[...], precision=lax.Precision.HIGHEST,
                         preferred_element_type=F32) + b_ref[...]


def _ada(c_pad, w_ada, b_ada):
    n = w_ada.shape[1]
    return pl.pallas_call(
        _ada_kernel,
        grid=(n // D_MODEL,),
        in_specs=[pl.BlockSpec((SUBLANES, D_MODEL), lambda j: (0, 0)),
                  pl.BlockSpec((D_MODEL, D_MODEL), lambda j: (0, j)),
                  pl.BlockSpec((1, D_MODEL), lambda j: (0, j))],
        out_specs=pl.BlockSpec((SUBLANES, D_MODEL), lambda j: (0, j)),
        out_shape=jax.ShapeDtypeStruct((SUBLANES, n), F32),
        name="ada",
    )(c_pad, w_ada, b_ada)


def _rms_heads64(a, w128):
    lane = lax.broadcasted_iota(I32, a.shape, 1)
    lo = lane < SWA_HEAD_DIM
    sq = a * a
    s_lo = jnp.sum(jnp.where(lo, sq, 0.0), axis=-1, keepdims=True)
    s_hi = jnp.sum(jnp.where(lo, 0.0, sq), axis=-1, keepdims=True)
    ms = jnp.where(lo, s_lo, s_hi) * (1.0 / SWA_HEAD_DIM)
    return a * lax.rsqrt(ms + EPS) * w128


def _inproj_kernel(tiles_per_seq, x_ref, mod_ref, n1w_ref, w_ref, convw_ref, gp_ref, qnw_ref,
                   knw_ref, q_ref, k_ref, v_ref, z_ref, gb_ref, swq_ref, swk_ref, swv_ref,
                   conv_scr):
    i = pl.program_id(0)
    tm = x_ref.shape[0]
    x = x_ref[...]
    y = x * lax.rsqrt(jnp.mean(x * x, axis=-1, keepdims=True) + EPS) * n1w_ref[...]
    h = (y * (1.0 + mod_ref[0, 1:2, :]) + mod_ref[0, 0:1, :]).astype(BF16)

    @pl.when(i % tiles_per_seq == 0)
    def _():
        conv_scr[0:SUBLANES, :] = jnp.zeros((SUBLANES, 3 * DN_WIDTH), F32)

    u = jnp.dot(h, w_ref[:, C_QKV:C_Z], preferred_element_type=F32)
    conv_scr[SUBLANES:SUBLANES + tm, :] = u
    cw = convw_ref[...]
    acc = u * cw[3:4, :]
    for j in range(CONV_WIDTH - 1):
        off = SUBLANES - (CONV_WIDTH - 1) + j
        acc = acc + conv_scr[off:off + tm, :] * cw[j:j + 1, :]
    conv_scr[0:SUBLANES, :] = conv_scr[tm:tm + SUBLANES, :]
    qkv = _silu(acc)

    for hh in range(DN_HEADS):
        sl = slice(hh * DN_HEAD_DIM, (hh + 1) * DN_HEAD_DIM)
        qh = qkv[:, hh * DN_HEAD_DIM:(hh + 1) * DN_HEAD_DIM]
        kh = qkv[:, DN_WIDTH + hh * DN_HEAD_DIM:DN_WIDTH + (hh + 1) * DN_HEAD_DIM]
        q_ref[:, sl] = qh * lax.rsqrt(jnp.sum(qh * qh, axis=-1, keepdims=True) + EPS) * (
            DN_HEAD_DIM ** -0.5)
        k_ref[:, sl] = kh * lax.rsqrt(jnp.sum(kh * kh, axis=-1, keepdims=True) + EPS)
    v_ref[...] = qkv[:, 2 * DN_WIDTH:3 * DN_WIDTH]

    z_ref[...] = jnp.dot(h, w_ref[:, C_Z:C_SWQ], preferred_element_type=F32)

    ab = jnp.dot(h, w_ref[:, C_AB:IN_PAD], preferred_element_type=F32)
    lane = lax.broadcasted_iota(I32, ab.shape, 1)
    sp_in = ab + gp_ref[1:2, :]
    softplus = jnp.maximum(sp_in, 0.0) + jnp.log1p(jnp.exp(-jnp.abs(sp_in)))
    gb_ref[...] = jnp.where(lane < DN_HEADS, gp_ref[0:1, :] * softplus, jax.nn.sigmoid(ab))

    sq = jnp.dot(h, w_ref[:, C_SWQ:C_SWK], preferred_element_type=F32)
    for t in range(SWA_WIDTH // LANES):
        sl = slice(t * LANES, (t + 1) * LANES)
        swq_ref[:, sl] = (_rms_heads64(sq[:, sl], qnw_ref[...]) * (SWA_HEAD_DIM ** -0.5)).astype(BF16)
    sk = jnp.dot(h, w_ref[:, C_SWK:C_SWV], preferred_element_type=F32)
    swk_ref[...] = _rms_heads64(sk, knw_ref[...]).astype(BF16)
    swv_ref[...] = jnp.dot(h, w_ref[:, C_SWV:C_AB], preferred_element_type=F32).astype(BF16)


def _inproj(xf, mod, n1w, w_cat, conv_w, gparams, qnw128, knw128, seq):
    t = xf.shape[0]
    tiles_per_seq = seq // TM_IN
    row = lambda i: (i, 0)
    const = lambda i: (0, 0)
    outs = [jax.ShapeDtypeStruct((t, DN_WIDTH), F32)] * 4 + [
        jax.ShapeDtypeStruct((t, LANES), F32),
        jax.ShapeDtypeStruct((t, SWA_WIDTH), BF16),
        jax.ShapeDtypeStruct((t, SWA_KV_WIDTH), BF16),
        jax.ShapeDtypeStruct((t, SWA_KV_WIDTH), BF16)]
    out_specs = [pl.BlockSpec((TM_IN, DN_WIDTH), row)] * 4 + [
        pl.BlockSpec((TM_IN, LANES), row),
        pl.BlockSpec((TM_IN, SWA_WIDTH), row),
        pl.BlockSpec((TM_IN, SWA_KV_WIDTH), row),
        pl.BlockSpec((TM_IN, SWA_KV_WIDTH), row)]
    return pl.pallas_call(
        functools.partial(_inproj_kernel, tiles_per_seq),
        grid=(t // TM_IN,),
        in_specs=[pl.BlockSpec((TM_IN, D_MODEL), row),
                  pl.BlockSpec((1, 6, D_MODEL), lambda i: (i // tiles_per_seq, 0, 0)),
                  pl.BlockSpec((1, D_MODEL), const),
                  pl.BlockSpec((D_MODEL, IN_PAD), const),
                  pl.BlockSpec((CONV_WIDTH, 3 * DN_WIDTH), const),
                  pl.BlockSpec((2, LANES), const),
                  pl.BlockSpec((1, LANES), const),
                  pl.BlockSpec((1, LANES), const)],
        out_specs=out_specs,
        out_shape=outs,
        scratch_shapes=[pltpu.VMEM((TM_IN + SUBLANES, 3 * DN_WIDTH), F32)],
        compiler_params=pltpu.CompilerParams(dimension_semantics=("arbitrary",),
                                             vmem_limit_bytes=VMEM_LIMIT),
        name="inproj",
    )(xf, mod, n1w, w_cat, conv_w, gparams, qnw128, knw128)


def _dn_kernel(q_ref, k_ref, v_ref, z_ref, gb_ref, nw_ref, o_ref,
               s_scr, u_scr, w_scr, qd_scr, kd_scr, a_scr):
    j = pl.program_id(1)
    rows = q_ref.shape[0]
    n_chunks = rows // CHUNK
    heads = range(DN_HEADS)

    @pl.when(j == 0)
    def _():
        s_scr[...] = jnp.zeros(s_scr.shape, F32)

    r = lax.broadcasted_iota(I32, (rows, rows), 0)
    c = lax.broadcasted_iota(I32, (rows, rows), 1)
    same = (r // CHUNK) == (c // CHUNK)
    incl = same & (c <= r)
    strict = same & (c < r)
    gb = gb_ref[...]
    gc = jnp.dot(jnp.where(incl, 1.0, 0.0), gb, precision=lax.Precision.HIGHEST,
                 preferred_element_type=F32)
    gl = jnp.dot(jnp.where(same, 1.0, 0.0), gb, precision=lax.Precision.HIGHEST,
                 preferred_element_type=F32)
    gct = gc.T

    rmat = []
    rhs = []
    for h in heads:
        ls = slice(h * DN_HEAD_DIM, (h + 1) * DN_HEAD_DIM)
        g_col = gc[:, h:h + 1]
        beta = gb[:, DN_HEADS + h:DN_HEADS + h + 1]
        decay = jnp.where(incl, jnp.exp(jnp.where(incl, g_col - gct[h:h + 1, :], 0.0)), 0.0)
        qq = q_ref[:, ls]
        kk = k_ref[:, ls]
        kb = kk * beta
        eg = jnp.exp(g_col)
        rmat.append(-jnp.where(strict, _mm_nt(kb, kk) * decay, 0.0))
        rhs.append(jnp.concatenate([v_ref[:, ls] * beta, kb * eg], axis=1))
        a_scr[h] = jnp.where(incl, _mm_nt(qq, kk) * decay, 0.0)
        qd_scr[:, ls] = qq * eg
        kd_scr[:, ls] = kk * jnp.exp(gl[:, h:h + 1] - g_col)

    xm = list(rmat)
    pm = list(rmat)
    for it in range(6):
        xm = [xm[h] + _mm(pm[h], xm[h]) for h in heads]
        if it < 5:
            pm = [_mm(pm[h], pm[h]) for h in heads]
    for h in heads:
        ls = slice(h * DN_HEAD_DIM, (h + 1) * DN_HEAD_DIM)
        sol = rhs[h] + _mm(xm[h], rhs[h])
        u_scr[:, ls] = sol[:, 0:DN_HEAD_DIM]
        w_scr[:, ls] = sol[:, DN_HEAD_DIM:2 * DN_HEAD_DIM]

    nw = nw_ref[...]
    lsl = [slice(h * DN_HEAD_DIM, (h + 1) * DN_HEAD_DIM) for h in heads]
    for ci in range(n_chunks):
        rs = slice(ci * CHUNK, (ci + 1) * CHUNK)
        dvals = jnp.exp(gl[ci * CHUNK:ci * CHUNK + 1, :])
        st = [s_scr[h] for h in heads]
        ws = [_mm(w_scr[rs, lsl[h]], st[h]) for h in heads]
        qs = [_mm(qd_scr[rs, lsl[h]], st[h]) for h in heads]
        vn = [u_scr[rs, lsl[h]] - ws[h] for h in heads]
        av = [_mm(a_scr[h, rs, rs], vn[h]) for h in heads]
        kv = [_mm_tn(kd_scr[rs, lsl[h]], vn[h]) for h in heads]
        for h in heads:
            s_scr[h] = st[h] * dvals[:, h:h + 1] + kv[h]
            o = qs[h] + av[h]
            on = o * lax.rsqrt(jnp.mean(o * o, axis=-1, keepdims=True) + EPS) * nw
            o_ref[rs, lsl[h]] = (on * _silu(z_ref[rs, lsl[h]])).astype(o_ref.dtype)


def _deltanet(q, k, v, z, gb, nw, batch, seq):
    t = q.shape[0]
    steps = seq // DN_ROWS
    row = lambda b, j: (b * steps + j, 0)
    blk = pl.BlockSpec((DN_ROWS, DN_WIDTH), row)
    return pl.pallas_call(
        _dn_kernel,
        grid=(batch, steps),
        in_specs=[blk, blk, blk, blk,
                  pl.BlockSpec((DN_ROWS, LANES), row),
                  pl.BlockSpec((1, DN_HEAD_DIM), lambda b, j: (0, 0))],
        out_specs=blk,
        out_shape=jax.ShapeDtypeStruct((t, DN_WIDTH), BF16),
        scratch_shapes=[pltpu.VMEM((DN_HEADS, DN_HEAD_DIM, DN_HEAD_DIM), F32),
                        pltpu.VMEM((DN_ROWS, DN_WIDTH), F32),
                        pltpu.VMEM((DN_ROWS, DN_WIDTH), F32),
                        pltpu.VMEM((DN_ROWS, DN_WIDTH), F32),
                        pltpu.VMEM((DN_ROWS, DN_WIDTH), F32),
                        pltpu.VMEM((DN_HEADS, DN_ROWS, DN_ROWS), F32)],
        compiler_params=pltpu.CompilerParams(dimension_semantics=("arbitrary", "arbitrary"),
                                             vmem_limit_bytes=VMEM_LIMIT),
        name="deltanet",
    )(q, k, v, z, gb, nw)


def _swa_kernel(sink_ref, q_ref, kp_ref, kc_ref, vp_ref, vc_ref, o_ref):
    n = pl.program_id(1)
    w = WINDOW
    kcat = jnp.concatenate([kp_ref[...], kc_ref[...]], axis=0)
    vcat = jnp.concatenate([vp_ref[...], vc_ref[...]], axis=0)
    rows = SWA_GROUP * w
    ri = lax.broadcasted_iota(I32, (rows, 2 * w), 0)
    kj = lax.broadcasted_iota(I32, (rows, 2 * w), 1)
    grp = ri // w
    rel = (ri - grp * w) + w - kj
    valid = (rel >= 0) & (rel < w) & ((kj >= w) | (n > 0))
    relf = rel.astype(F32)
    grp_col = lax.broadcasted_iota(I32, (rows, 1), 0) // w
    q = q_ref[...]
    outs = []
    for kh in range(SWA_KV_HEADS):
        slope_col = jnp.zeros((rows, 1), F32)
        sink_col = jnp.zeros((rows, 1), F32)
        for g in range(SWA_GROUP):
            hd = kh * SWA_GROUP + g
            slope_col = jnp.where(grp_col == g, 2.0 ** (-8.0 * (hd + 1) / SWA_HEADS), slope_col)
            sink_col = jnp.where(grp_col == g, sink_ref[hd], sink_col)
        qs = jnp.concatenate(
            [q[:, (kh * SWA_GROUP + g) * SWA_HEAD_DIM:(kh * SWA_GROUP + g + 1) * SWA_HEAD_DIM]
             for g in range(SWA_GROUP)], axis=0)
        kk = kcat[:, kh * SWA_HEAD_DIM:(kh + 1) * SWA_HEAD_DIM]
        vv = vcat[:, kh * SWA_HEAD_DIM:(kh + 1) * SWA_HEAD_DIM]
        s = lax.dot_general(qs, kk, (((1,), (1,)), ((), ())), preferred_element_type=F32)
        logits = jnp.where(valid, s - slope_col * relf, NEG)
        m = jnp.maximum(jnp.max(logits, axis=-1, keepdims=True), sink_col)
        p = jnp.exp(logits - m)
        denom = jnp.sum(p, axis=-1, keepdims=True) + jnp.exp(sink_col - m)
        o = jnp.dot(p.astype(BF16), vv, preferred_element_type=F32) / denom
        for g in range(SWA_GROUP):
            outs.append(o[g * w:(g + 1) * w, :])
    for t in range(SWA_WIDTH // LANES):
        o_ref[:, t * LANES:(t + 1) * LANES] = jnp.concatenate(
            [outs[2 * t], outs[2 * t + 1]], axis=-1).astype(o_ref.dtype)


def _swa(swq, swk, swv, sinks, batch, seq):
    t = swq.shape[0]
    nb = seq // WINDOW
    cur = lambda b, n, s: (b * nb + n, 0)
    prev = lambda b, n, s: (b * nb + jnp.maximum(n - 1, 0), 0)
    return pl.pallas_call(
        _swa_kernel,
        grid_spec=pltpu.PrefetchScalarGridSpec(
            num_scalar_prefetch=1,
            grid=(batch, nb),
            in_specs=[pl.BlockSpec((WINDOW, SWA_WIDTH), cur),
                      pl.BlockSpec((WINDOW, SWA_KV_WIDTH), prev),
                      pl.BlockSpec((WINDOW, SWA_KV_WIDTH), cur),
                      pl.BlockSpec((WINDOW, SWA_KV_WIDTH), prev),
                      pl.BlockSpec((WINDOW, SWA_KV_WIDTH), cur)],
            out_specs=pl.BlockSpec((WINDOW, SWA_WIDTH), cur)),
        out_shape=jax.ShapeDtypeStruct((t, SWA_WIDTH), BF16),
        compiler_params=pltpu.CompilerParams(dimension_semantics=("arbitrary", "arbitrary"),
                                             vmem_limit_bytes=VMEM_LIMIT),
        name="swa",
    )(sinks, swq, swk, swk, swv, swv)


def _post_kernel(x_ref, odn_ref, osw_ref, mod_ref, wo_ref, n2w_ref, wr_ref, br_ref,
                 x1_ref, h2_ref, idx_ref, gate_ref, rank_ref, cnt_ref, run_scr):
    i = pl.program_id(0)
    tm = x_ref.shape[0]

    @pl.when(i == 0)
    def _():
        run_scr[...] = jnp.zeros(run_scr.shape, F32)

    y = (jnp.dot(odn_ref[...], wo_ref[0:DN_WIDTH, :], preferred_element_type=F32)
         + jnp.dot(osw_ref[...], wo_ref[DN_WIDTH:, :], preferred_element_type=F32))
    x1 = x_ref[...] + mod_ref[0, 2:3, :] * y
    x1_ref[...] = x1
    hn = x1 * lax.rsqrt(jnp.mean(x1 * x1, axis=-1, keepdims=True) + EPS) * n2w_ref[...]
    h2 = hn * (1.0 + mod_ref[0, 4:5, :]) + mod_ref[0, 3:4, :]
    _store_token_tiles(h2_ref, h2)

    lane = lax.broadcasted_iota(I32, (tm, LANES), 1)
    lane_f = lane.astype(F32)
    logits = jnp.dot(h2.astype(BF16), wr_ref[...], preferred_element_type=F32) + br_ref[...]
    work = jnp.where(lane < N_EXPERTS, logits, NEG)
    sel = jnp.zeros((tm, LANES), F32)
    idx_out = jnp.zeros((tm, LANES), F32)
    vals = []
    idxs = []
    for kk in range(TOP_K):
        m = jnp.max(work, axis=-1, keepdims=True)
        ik = jnp.min(jnp.where(work == m, lane_f, float(LANES)), axis=-1, keepdims=True)
        hit = lane_f == ik
        work = jnp.where(hit, NEG, work)
        sel = jnp.where(hit, 1.0, sel)
        idx_out = jnp.where(lane == kk, ik, idx_out)
        vals.append(m)
        idxs.append(ik)
    es = [jnp.exp(v - vals[0]) for v in vals]
    tot = es[0] + es[1] + es[2] + es[3]
    gate_out = jnp.zeros((tm, LANES), F32)
    for kk in range(TOP_K):
        gate_out = jnp.where(lane == kk, es[kk] / tot, gate_out)
    gate_ref[...] = gate_out

    r = lax.broadcasted_iota(I32, (tm, tm), 0)
    c = lax.broadcasted_iota(I32, (tm, tm), 1)
    before = jnp.where(c < r, 1.0, 0.0).astype(BF16)
    base = jnp.dot(before, sel.astype(BF16), preferred_element_type=F32) + run_scr[0:1, :]
    rank_out = jnp.zeros((tm, LANES), F32)
    for kk in range(TOP_K):
        rk = jnp.sum(jnp.where(lane_f == idxs[kk], base, 0.0), axis=-1, keepdims=True)
        rank_out = jnp.where(lane == kk, rk, rank_out)
    idx_ref[...] = idx_out.T[0:SUBLANES, :].astype(I32)
    rank_ref[...] = rank_out.T[0:SUBLANES, :].astype(I32)
    run = run_scr[0:1, :] + jnp.sum(sel, axis=0, keepdims=True)
    run_scr[0:1, :] = run
    cnt_ref[...] = jnp.broadcast_to(run, cnt_ref.shape).astype(I32)


def _post(xf, odn, osw, mod, w_out, n2w, wr_pad, br_pad, seq):
    t = xf.shape[0]
    tiles_per_seq = seq // TM_POST
    row = lambda i: (i, 0)
    const = lambda i: (0, 0)
    return pl.pallas_call(
        _post_kernel,
        grid=(t // TM_POST,),
        in_specs=[pl.BlockSpec((TM_POST, D_MODEL), row),
                  pl.BlockSpec((TM_POST, DN_WIDTH), row),
                  pl.BlockSpec((TM_POST, SWA_WIDTH), row),
                  pl.BlockSpec((1, 6, D_MODEL), lambda i: (i // tiles_per_seq, 0, 0)),
                  pl.BlockSpec((D_MODEL, D_MODEL), const),
                  pl.BlockSpec((1, D_MODEL), const),
                  pl.BlockSpec((D_MODEL, LANES), const),
                  pl.BlockSpec((1, LANES), const)],
        out_specs=[pl.BlockSpec((TM_POST, D_MODEL), row),
                   pl.BlockSpec((TM_POST * SUBLANES, LANES), row),
                   pl.BlockSpec((SUBLANES, TM_POST), lambda i: (0, i)),
                   pl.BlockSpec((TM_POST, LANES), row),
                   pl.BlockSpec((SUBLANES, TM_POST), lambda i: (0, i)),
                   pl.BlockSpec((SUBLANES, LANES), const)],
        out_shape=[jax.ShapeDtypeStruct((t, D_MODEL), F32),
                   jax.ShapeDtypeStruct((t * SUBLANES, LANES), F32),
                   jax.ShapeDtypeStruct((SUBLANES, t), I32),
                   jax.ShapeDtypeStruct((t, LANES), F32),
                   jax.ShapeDtypeStruct((SUBLANES, t), I32),
                   jax.ShapeDtypeStruct((SUBLANES, LANES), I32)],
        scratch_shapes=[pltpu.VMEM((SUBLANES, LANES), F32)],
        compiler_params=pltpu.CompilerParams(dimension_semantics=("arbitrary",),
                                             vmem_limit_bytes=VMEM_LIMIT),
        name="post",
    )(xf, odn, osw, mod, w_out, n2w, wr_pad, br_pad)


INVERT_GROUP = 16


def _invert_kernel(fill_ref, dest_ref, rowtok_ref):
    def per_range(e, carry):
        def clear(rw, c2):
            rowtok_ref[rw] = 0
            return c2

        lax.fori_loop(fill_ref[0, e], fill_ref[1, e], clear, 0)
        return carry

    lax.fori_loop(0, fill_ref.shape[1], per_range, 0)

    n_tok = dest_ref.shape[0] // TOP_K
    for kk in range(TOP_K):
        def put(g, carry, kk=kk):
            tok0 = g * INVERT_GROUP
            rows = [dest_ref[kk * n_tok + tok0 + u] for u in range(INVERT_GROUP)]
            for u in range(INVERT_GROUP):
                rowtok_ref[rows[u]] = tok0 + u
            return carry

        lax.fori_loop(0, n_tok // INVERT_GROUP, put, 0)


def _invert(fill, dest_flat, n_pad):
    return pl.pallas_call(
        _invert_kernel,
        in_specs=[pl.BlockSpec(memory_space=pltpu.SMEM),
                  pl.BlockSpec(memory_space=pltpu.SMEM)],
        out_specs=pl.BlockSpec(memory_space=pltpu.SMEM),
        out_shape=jax.ShapeDtypeStruct((n_pad,), I32),
        name="invert",
    )(fill, dest_flat)


GATHER_SLOTS = 3


def _moe_kernel(be_ref, nv_ref, rowtok_ref, h_ref, wu_ref, bu_ref, wd_ref, bd_ref, ys_ref,
                wu_bf, wd_bf, xbuf, sem):
    i = pl.program_id(0)
    slot = i % GATHER_SLOTS
    n_valid = nv_ref[0]

    def gather(blk, s):
        for r in range(MOE_ROWS):
            _token_copy(h_ref, rowtok_ref[blk * MOE_ROWS + r], xbuf.at[s], r,
                        sem.at[s]).start(priority=r % 2)

    def gather_wait(s):
        pltpu.make_async_copy(h_ref.at[pl.ds(0, MOE_ROWS * SUBLANES), :], xbuf.at[s],
                              sem.at[s]).wait()

    last = n_valid - 1

    @pl.when(i == 0)
    def _():
        gather(0, 0)
        gather(jnp.minimum(1, last), 1)

    @pl.when(i < n_valid)
    def _():
        prev = be_ref[jnp.maximum(i - 1, 0)]

        @pl.when((i == 0) | (be_ref[i] != prev))
        def _():
            wu_bf[...] = wu_ref[0].astype(BF16)
            wd_bf[...] = wd_ref[0].astype(BF16)

        gather_wait(slot)
        xb = _load_token_tiles(xbuf.at[slot], 0, MOE_ROWS).astype(BF16)
        gather(jnp.minimum(i + 2, last), (i + 2) % GATHER_SLOTS)
        hu = jnp.dot(xb, wu_bf[...], preferred_element_type=F32) + bu_ref[0]
        x_glu = jnp.minimum(hu[:, :D_FF], SWIGLU_LIMIT)
        x_lin = jnp.clip(hu[:, D_FF:], -SWIGLU_LIMIT, SWIGLU_LIMIT)
        act = x_glu * jax.nn.sigmoid(SWIGLU_ALPHA * x_glu) * (x_lin + 1.0)
        _store_token_tiles(
            ys_ref, jnp.dot(act.astype(BF16), wd_bf[...], preferred_element_type=F32) + bd_ref[0])

    @pl.when(i == last)
    def _():
        gather_wait((i + 1) % GATHER_SLOTS)
        gather_wait((i + 2) % GATHER_SLOTS)

    @pl.when(i >= n_valid)
    def _():
        ys_ref[...] = jnp.zeros(ys_ref.shape, F32)


def _moe(block_e, n_valid, row_tok, h2, w_up, b_up, w_down, b_down):
    n_pad = row_tok.shape[0]
    n_blocks = n_pad // MOE_ROWS
    out_rows = lambda i, be, nv, rt: (i, 0)
    expert = lambda i, be, nv, rt: (be[i], 0, 0)
    return pl.pallas_call(
        _moe_kernel,
        grid_spec=pltpu.PrefetchScalarGridSpec(
            num_scalar_prefetch=3,
            grid=(n_blocks,),
            in_specs=[pl.BlockSpec(memory_space=pl.ANY),
                      pl.BlockSpec((1, D_MODEL, 2 * D_FF), expert),
                      pl.BlockSpec((1, 1, 2 * D_FF), expert),
                      pl.BlockSpec((1, D_FF, D_MODEL), expert),
                      pl.BlockSpec((1, 1, D_MODEL), expert)],
            out_specs=pl.BlockSpec((MOE_ROWS * SUBLANES, LANES), out_rows),
            scratch_shapes=[pltpu.VMEM((D_MODEL, 2 * D_FF), BF16),
                            pltpu.VMEM((D_FF, D_MODEL), BF16),
                            pltpu.VMEM((GATHER_SLOTS, MOE_ROWS * SUBLANES, LANES), F32),
                            pltpu.SemaphoreType.DMA((GATHER_SLOTS,))]),
        out_shape=jax.ShapeDtypeStruct((n_pad * SUBLANES, LANES), F32),
        compiler_params=pltpu.CompilerParams(dimension_semantics=("arbitrary",),
                                             vmem_limit_bytes=VMEM_LIMIT),
        name="moe",
    )(block_e, n_valid, row_tok, h2, w_up, b_up, w_down, b_down)


def _combine_kernel(dcur_ref, dnext_ref, ys_ref, x1_ref, gate_ref, mod_ref, o_ref, buf, sem):
    i = pl.program_id(0)
    tm = x1_ref.shape[0]
    slot = i % 2

    def issue(dref, s):
        def body(tk, carry):
            for kk in range(TOP_K):
                _token_copy(ys_ref, dref[0, 0, kk * tm + tk], buf.at[s], kk * tm + tk,
                            sem.at[s]).start(priority=kk % 2)
            return carry

        lax.fori_loop(0, tm, body, 0, unroll=4)

    @pl.when(i == 0)
    def _():
        issue(dcur_ref, 0)

    @pl.when(i + 1 < pl.num_programs(0))
    def _():
        issue(dnext_ref, 1 - slot)

    pltpu.make_async_copy(ys_ref.at[pl.ds(0, TOP_K * tm * SUBLANES), :], buf.at[slot],
                          sem.at[slot]).wait()

    gates = gate_ref[...]
    acc = jnp.zeros((tm, D_MODEL), F32)
    for kk in range(TOP_K):
        acc = acc + gates[:, kk:kk + 1] * _load_token_tiles(buf.at[slot], kk * tm, tm)
    o_ref[...] = x1_ref[...] + mod_ref[0, 5:6, :] * acc


def _combine(dest3, ys, x1, gates, mod, seq):
    t = x1.shape[0]
    tiles_per_seq = seq // TM_COMB
    n_tiles = t // TM_COMB
    row = lambda i: (i, 0)
    return pl.pallas_call(
        _combine_kernel,
        grid=(n_tiles,),
        in_specs=[pl.BlockSpec((1, 1, TM_COMB * TOP_K), lambda i: (i, 0, 0),
                               memory_space=pltpu.SMEM),
                  pl.BlockSpec((1, 1, TM_COMB * TOP_K),
                               lambda i: (jnp.minimum(i + 1, n_tiles - 1), 0, 0),
                               memory_space=pltpu.SMEM),
                  pl.BlockSpec(memory_space=pl.ANY),
                  pl.BlockSpec((TM_COMB, D_MODEL), row),
                  pl.BlockSpec((TM_COMB, LANES), row),
                  pl.BlockSpec((1, 6, D_MODEL), lambda i: (i // tiles_per_seq, 0, 0))],
        out_specs=pl.BlockSpec((TM_COMB, D_MODEL), row),
        out_shape=jax.ShapeDtypeStruct((t, D_MODEL), F32),
        scratch_shapes=[pltpu.VMEM((2, TOP_K * TM_COMB * SUBLANES, LANES), F32),
                        pltpu.SemaphoreType.DMA((2,))],
        compiler_params=pltpu.CompilerParams(dimension_semantics=("arbitrary",),
                                             vmem_limit_bytes=VMEM_LIMIT),
        name="combine",
    )(dest3, dest3, ys, x1, gates, mod)


def _layer(x, mod, norm1_w, w_in, conv_w, a_log, dt_bias, dn_norm_w, q_norm_w, k_norm_w, sinks,
           w_out, norm2_w, w_router, b_router, w_up, b_up, w_down, b_down):
    batch, seq, _ = x.shape
    t = batch * seq
    xf = x.reshape(t, D_MODEL)

    o_qkv, o_z, o_a, o_b, o_q, o_k, o_v = 0, 1536, 2048, 2052, 2056, 2568, 2696
    w_ab = jnp.zeros((D_MODEL, LANES), F32).at[:, 0:2 * DN_HEADS].set(w_in[:, o_a:o_q])
    w_cat = jnp.concatenate([w_in[:, o_qkv:o_z], w_in[:, o_z:o_a], w_in[:, o_q:o_k],
                             w_in[:, o_k:o_v], w_in[:, o_v:], w_ab], axis=1).astype(BF16)
    gparams = jnp.zeros((2, LANES), F32)
    gparams = gparams.at[0, 0:DN_HEADS].set(-jnp.exp(a_log.astype(F32)))
    gparams = gparams.at[1, 0:DN_HEADS].set(dt_bias.astype(F32))
    qnw128 = jnp.tile(q_norm_w.astype(F32), 2)[None, :]
    knw128 = jnp.tile(k_norm_w.astype(F32), 2)[None, :]

    q, k, v, z, gb, swq, swk, swv = _inproj(xf, mod, norm1_w[None, :], w_cat, conv_w, gparams,
                                            qnw128, knw128, seq)
    o_dn = _deltanet(q, k, v, z, gb, dn_norm_w[None, :], batch, seq)
    o_sw = _swa(swq, swk, swv, sinks.astype(F32), batch, seq)

    wr_pad = jnp.zeros((D_MODEL, LANES), F32).at[:, 0:N_EXPERTS].set(w_router).astype(BF16)
    br_pad = jnp.zeros((1, LANES), F32).at[0, 0:N_EXPERTS].set(b_router)
    x1, h2, top_idx, gates, rank, cnt = _post(xf, o_dn, o_sw, mod, w_out.astype(BF16),
                                              norm2_w[None, :], wr_pad, br_pad, seq)

    counts = cnt[0, 0:N_EXPERTS]
    padded = (counts + MOE_ROWS - 1) // MOE_ROWS * MOE_ROWS
    pend = jnp.cumsum(padded)
    pstart = pend - padded
    n_blocks = (t * TOP_K) // MOE_ROWS + N_EXPERTS
    n_pad = n_blocks * MOE_ROWS
    n_valid = (pend[-1] // MOE_ROWS).astype(I32)
    blk = jnp.minimum(jnp.arange(n_blocks, dtype=I32), n_valid - 1)
    block_e = jnp.minimum(jnp.sum(pend[None, :] <= (blk * MOE_ROWS)[:, None], axis=1),
                          N_EXPERTS - 1).astype(I32)
    experts = jnp.arange(N_EXPERTS, dtype=I32)[:, None, None]
    dest = (jnp.sum(jnp.where(top_idx[None, 0:TOP_K] == experts, pstart[:, None, None], 0), axis=0)
            + rank[0:TOP_K]).astype(I32)
    fill = jnp.stack([jnp.append(pstart + counts, pend[-1]),
                      jnp.append(pend, n_pad)]).astype(I32)

    row_tok = _invert(fill, dest.reshape(TOP_K * t), n_pad)
    ys = _moe(block_e, n_valid.reshape(1), row_tok, h2, w_up, b_up[:, None, :], w_down,
              b_down[:, None, :])
    n_tiles = t // TM_COMB
    dest_tiles = dest.reshape(TOP_K, n_tiles, TM_COMB).transpose(1, 0, 2).reshape(
        n_tiles, 1, TOP_K * TM_COMB)
    out = _combine(dest_tiles, ys, x1, gates, mod, seq)
    return out.reshape(batch, seq, D_MODEL)


def kernel(x, c, w_ada, b_ada, norm1_w, w_in, conv_w, a_log, dt_bias, dn_norm_w, q_norm_w,
           k_norm_w, sinks, w_out, norm2_w, w_router, b_router, w_up, b_up, w_down, b_down):
    out_dtype = x.dtype
    batch = x.shape[0]
    depth = w_ada.shape[0]
    c_pad = jnp.zeros((SUBLANES, D_MODEL), F32).at[0:batch].set(c.astype(F32))
    for l in range(depth):
        mod = _ada(c_pad, w_ada[l], b_ada[l][None, :])[0:batch].reshape(batch, 6, D_MODEL)
        x = _layer(x, mod, norm1_w[l], w_in[l], conv_w[l], a_log[l], dt_bias[l], dn_norm_w[l],
                   q_norm_w[l], k_norm_w[l], sinks[l], w_out[l], norm2_w[l], w_router[l],
                   b_router[l], w_up[l], b_up[l], w_down[l], b_down[l])
    return x.astype(out_dtype)
```

```python
import functools

import jax
import jax.numpy as jnp
from jax import lax
from jax.experimental import pallas as pl
from jax.experimental.pallas import tpu as pltpu

F32 = jnp.float32
BF16 = jnp.bfloat16
I32 = jnp.int32

D_MODEL = 1024
DN_HEADS = 4
DN_HEAD_DIM = 128
DN_WIDTH = DN_HEADS * DN_HEAD_DIM
CONV_WIDTH = 4
CHUNK = 64
SWA_HEADS = 8
SWA_KV_HEADS = 2
SWA_GROUP = SWA_HEADS // SWA_KV_HEADS
SWA_HEAD_DIM = 64
SWA_WIDTH = SWA_HEADS * SWA_HEAD_DIM
SWA_KV_WIDTH = SWA_KV_HEADS * SWA_HEAD_DIM
WINDOW = 128
N_EXPERTS = 32
TOP_K = 4
D_FF = D_MODEL
SWIGLU_ALPHA = 1.702
SWIGLU_LIMIT = 7.0
EPS = 1e-6
NEG = -1e30

LANES = 128
SUBLANES = 8
VMEM_LIMIT = 56 * 1024 * 1024

C_QKV = 0
C_Z = C_QKV + 3 * DN_WIDTH
C_SWQ = C_Z + DN_WIDTH
C_SWK = C_SWQ + SWA_WIDTH
C_SWV = C_SWK + SWA_KV_WIDTH
C_AB = C_SWV + SWA_KV_WIDTH
IN_PAD = C_AB + LANES

TM_IN = 256
DN_ROWS = 256
TM_POST = 512
MOE_ROWS = 256
TM_COMB = 128


def _silu(x):
    return x * jax.nn.sigmoid(x)


def _mm(a, b):
    return jnp.dot(a.astype(BF16), b.astype(BF16), preferred_element_type=F32)


def _mm_nt(a, b):
    return lax.dot_general(a.astype(BF16), b.astype(BF16), (((1,), (1,)), ((), ())),
                           preferred_element_type=F32)


def _mm_tn(a, b):
    return lax.dot_general(a.astype(BF16), b.astype(BF16), (((0,), (0,)), ((), ())),
                           preferred_element_type=F32)


def _load_token_tiles(ref, row0, rows):
    return jnp.concatenate(
        [ref[pl.ds(row0 * SUBLANES + s, rows, stride=SUBLANES), :] for s in range(D_MODEL // LANES)],
        axis=1)


def _store_token_tiles(ref, val):
    rows = val.shape[0]
    for s in range(D_MODEL // LANES):
        ref[pl.ds(s, rows, stride=SUBLANES), :] = val[:, s * LANES:(s + 1) * LANES]


def _token_copy(src_ref, s, dst_ref, d, sem):
    return pltpu.make_async_copy(
        src_ref.at[pl.ds(pl.multiple_of(s * SUBLANES, SUBLANES), SUBLANES), :],
        dst_ref.at[pl.ds(pl.multiple_of(d * SUBLANES, SUBLANES), SUBLANES), :], sem)


def _ada_kernel(c_ref, w_ref, b_ref, o_ref):
    c = c_ref[...]
    o_ref[...] = jnp.dot(_silu(c), w_ref[...], precision=lax.Precision.HIGHEST,
                         preferred_element_type=F32) + b_ref[...]


def _ada(c_pad, w_ada, b_ada):
    n = w_ada.shape[1]
    return pl.pallas_call(
        _ada_kernel,
        grid=(n // D_MODEL,),
        in_specs=[pl.BlockSpec((SUBLANES, D_MODEL), lambda j: (0, 0)),
                  pl.BlockSpec((D_MODEL, D_MODEL), lambda j: (0, j)),
                  pl.BlockSpec((1, D_MODEL), lambda j: (0, j))],
        out_specs=pl.BlockSpec((SUBLANES, D_MODEL), lambda j: (0, j)),
        out_shape=jax.ShapeDtypeStruct((SUBLANES, n), F32),
        name="ada",
    )(c_pad, w_ada, b_ada)


def _rms_heads64(a, w128):
    lane = lax.broadcasted_iota(I32, a.shape, 1)
    lo = lane < SWA_HEAD_DIM
    sq = a * a
    s_lo = jnp.sum(jnp.where(lo, sq, 0.0), axis=-1, keepdims=True)
    s_hi = jnp.sum(jnp.where(lo, 0.0, sq), axis=-1, keepdims=True)
    ms = jnp.where(lo, s_lo, s_hi) * (1.0 / SWA_HEAD_DIM)
    return a * lax.rsqrt(ms + EPS) * w128


def _inproj_kernel(tiles_per_seq, x_ref, mod_ref, n1w_ref, w_ref, convw_ref, gp_ref, qnw_ref,
                   knw_ref, q_ref, k_ref, v_ref, z_ref, gb_ref, swq_ref, swk_ref, swv_ref,
                   conv_scr):
    i = pl.program_id(0)
    tm = x_ref.shape[0]
    x = x_ref[...]
    y = x * lax.rsqrt(jnp.mean(x * x, axis=-1, keepdims=True) + EPS) * n1w_ref[...]
    h = (y * (1.0 + mod_ref[0, 1:2, :]) + mod_ref[0, 0:1, :]).astype(BF16)

    @pl.when(i % tiles_per_seq == 0)
    def _():
        conv_scr[0:SUBLANES, :] = jnp.zeros((SUBLANES, 3 * DN_WIDTH), F32)

    u = jnp.dot(h, w_ref[:, C_QKV:C_Z], preferred_element_type=F32)
    conv_scr[SUBLANES:SUBLANES + tm, :] = u
    cw = convw_ref[...]
    acc = u * cw[3:4, :]
    for j in range(CONV_WIDTH - 1):
        off = SUBLANES - (CONV_WIDTH - 1) + j
        acc = acc + conv_scr[off:off + tm, :] * cw[j:j + 1, :]
    conv_scr[0:SUBLANES, :] = conv_scr[tm:tm + SUBLANES, :]
    qkv = _silu(acc)

    for hh in range(DN_HEADS):
        sl = slice(hh * DN_HEAD_DIM, (hh + 1) * DN_HEAD_DIM)
        qh = qkv[:, hh * DN_HEAD_DIM:(hh + 1) * DN_HEAD_DIM]
        kh = qkv[:, DN_WIDTH + hh * DN_HEAD_DIM:DN_WIDTH + (hh + 1) * DN_HEAD_DIM]
        q_ref[:, sl] = qh * lax.rsqrt(jnp.sum(qh * qh, axis=-1, keepdims=True) + EPS) * (
            DN_HEAD_DIM ** -0.5)
        k_ref[:, sl] = kh * lax.rsqrt(jnp.sum(kh * kh, axis=-1, keepdims=True) + EPS)
    v_ref[...] = qkv[:, 2 * DN_WIDTH:3 * DN_WIDTH]

    z_ref[...] = jnp.dot(h, w_ref[:, C_Z:C_SWQ], preferred_element_type=F32)

    ab = jnp.dot(h, w_ref[:, C_AB:IN_PAD], preferred_element_type=F32)
    lane = lax.broadcasted_iota(I32, ab.shape, 1)
    sp_in = ab + gp_ref[1:2, :]
    softplus = jnp.maximum(sp_in, 0.0) + jnp.log1p(jnp.exp(-jnp.abs(sp_in)))
    gb_ref[...] = jnp.where(lane < DN_HEADS, gp_ref[0:1, :] * softplus, jax.nn.sigmoid(ab))

    sq = jnp.dot(h, w_ref[:, C_SWQ:C_SWK], preferred_element_type=F32)
    for t in range(SWA_WIDTH // LANES):
        sl = slice(t * LANES, (t + 1) * LANES)
        swq_ref[:, sl] = (_rms_heads64(sq[:, sl], qnw_ref[...]) * (SWA_HEAD_DIM ** -0.5)).astype(BF16)
    sk = jnp.dot(h, w_ref[:, C_SWK:C_SWV], preferred_element_type=F32)
    swk_ref[...] = _rms_heads64(sk, knw_ref[...]).astype(BF16)
    swv_ref[...] = jnp.dot(h, w_ref[:, C_SWV:C_AB], preferred_element_type=F32).astype(BF16)


def _inproj(xf, mod, n1w, w_cat, conv_w, gparams, qnw128, knw128, seq):
    t = xf.shape[0]
    tiles_per_seq = seq // TM_IN
    row = lambda i: (i, 0)
    const = lambda i: (0, 0)
    outs = [jax.ShapeDtypeStruct((t, DN_WIDTH), F32)] * 4 + [
        jax.ShapeDtypeStruct((t, LANES), F32),
        jax.ShapeDtypeStruct((t, SWA_WIDTH), BF16),
        jax.ShapeDtypeStruct((t, SWA_KV_WIDTH), BF16),
        jax.ShapeDtypeStruct((t, SWA_KV_WIDTH), BF16)]
    out_specs = [pl.BlockSpec((TM_IN, DN_WIDTH), row)] * 4 + [
        pl.BlockSpec((TM_IN, LANES), row),
        pl.BlockSpec((TM_IN, SWA_WIDTH), row),
        pl.BlockSpec((TM_IN, SWA_KV_WIDTH), row),
        pl.BlockSpec((TM_IN, SWA_KV_WIDTH), row)]
    return pl.pallas_call(
        functools.partial(_inproj_kernel, tiles_per_seq),
        grid=(t // TM_IN,),
        in_specs=[pl.BlockSpec((TM_IN, D_MODEL), row),
                  pl.BlockSpec((1, 6, D_MODEL), lambda i: (i // tiles_per_seq, 0, 0)),
                  pl.BlockSpec((1, D_MODEL), const),
                  pl.BlockSpec((D_MODEL, IN_PAD), const),
                  pl.BlockSpec((CONV_WIDTH, 3 * DN_WIDTH), const),
                  pl.BlockSpec((2, LANES), const),
                  pl.BlockSpec((1, LANES), const),
                  pl.BlockSpec((1, LANES), const)],
        out_specs=out_specs,
        out_shape=outs,
        scratch_shapes=[pltpu.VMEM((TM_IN + SUBLANES, 3 * DN_WIDTH), F32)],
        compiler_params=pltpu.CompilerParams(dimension_semantics=("arbitrary",),
                                             vmem_limit_bytes=VMEM_LIMIT),
        name="inproj",
    )(xf, mod, n1w, w_cat, conv_w, gparams, qnw128, knw128)


def _dn_kernel(q_ref, k_ref, v_ref, z_ref, gb_ref, nw_ref, o_ref,
               s_scr, u_scr, w_scr, qd_scr, kd_scr, a_scr):
    j = pl.program_id(1)
    rows = q_ref.shape[0]
    n_chunks = rows // CHUNK
    heads = range(DN_HEADS)

    @pl.when(j == 0)
    def _():
        s_scr[...] = jnp.zeros(s_scr.shape, F32)

    r = lax.broadcasted_iota(I32, (rows, rows), 0)
    c = lax.broadcasted_iota(I32, (rows, rows), 1)
    same = (r // CHUNK) == (c // CHUNK)
    incl = same & (c <= r)
    strict = same & (c < r)
    gb = gb_ref[...]
    gc = jnp.dot(jnp.where(incl, 1.0, 0.0), gb, precision=lax.Precision.HIGHEST,
                 preferred_element_type=F32)
    gl = jnp.dot(jnp.where(same, 1.0, 0.0), gb, precision=lax.Precision.HIGHEST,
                 preferred_element_type=F32)
    gct = gc.T

    rmat = []
    rhs = []
    for h in heads:
        ls = slice(h * DN_HEAD_DIM, (h + 1) * DN_HEAD_DIM)
        g_col = gc[:, h:h + 1]
        beta = gb[:, DN_HEADS + h:DN_HEADS + h + 1]
        decay = jnp.where(incl, jnp.exp(jnp.where(incl, g_col - gct[h:h + 1, :], 0.0)), 0.0)
        qq = q_ref[:, ls]
        kk = k_ref[:, ls]
        kb = kk * beta
        eg = jnp.exp(g_col)
        rmat.append(-jnp.where(strict, _mm_nt(kb, kk) * decay, 0.0))
        rhs.append(jnp.concatenate([v_ref[:, ls] * beta, kb * eg], axis=1))
        a_scr[h] = jnp.where(incl, _mm_nt(qq, kk) * decay, 0.0)
        qd_scr[:, ls] = qq * eg
        kd_scr[:, ls] = kk * jnp.exp(gl[:, h:h + 1] - g_col)

    xm = list(rmat)
    pm = list(rmat)
    for it in range(6):
        xm = [xm[h] + _mm(pm[h], xm[h]) for h in heads]
        if it < 5:
            pm = [_mm(pm[h], pm[h]) for h in heads]
    for h in heads:
        ls = slice(h * DN_HEAD_DIM, (h + 1) * DN_HEAD_DIM)
        sol = rhs[h] + _mm(xm[h], rhs[h])
        u_scr[:, ls] = sol[:, 0:DN_HEAD_DIM]
        w_scr[:, ls] = sol[:, DN_HEAD_DIM:2 * DN_HEAD_DIM]

    nw = nw_ref[...]
    lsl = [slice(h * DN_HEAD_DIM, (h + 1) * DN_HEAD_DIM) for h in heads]
    for ci in range(n_chunks):
        rs = slice(ci * CHUNK, (ci + 1) * CHUNK)
        dvals = jnp.exp(gl[ci * CHUNK:ci * CHUNK + 1, :])
        st = [s_scr[h] for h in heads]
        ws = [_mm(w_scr[rs, lsl[h]], st[h]) for h in heads]
        qs = [_mm(qd_scr[rs, lsl[h]], st[h]) for h in heads]
        vn = [u_scr[rs, lsl[h]] - ws[h] for h in heads]
        av = [_mm(a_scr[h, rs, rs], vn[h]) for h in heads]
        kv = [_mm_tn(kd_scr[rs, lsl[h]], vn[h]) for h in heads]
        for h in heads:
            s_scr[h] = st[h] * dvals[:, h:h + 1] + kv[h]
            o = qs[h] + av[h]
            on = o * lax.rsqrt(jnp.mean(o * o, axis=-1, keepdims=True) + EPS) * nw
            o_ref[rs, lsl[h]] = (on * _silu(z_ref[rs, lsl[h]])).astype(o_ref.dtype)


def _deltanet(q, k, v, z, gb, nw, batch, seq):
    t = q.shape[0]
    steps = seq // DN_ROWS
    row = lambda b, j: (b * steps + j, 0)
    blk = pl.BlockSpec((DN_ROWS, DN_WIDTH), row)
    return pl.pallas_call(
        _dn_kernel,
        grid=(batch, steps),
        in_specs=[blk, blk, blk, blk,
                  pl.BlockSpec((DN_ROWS, LANES), row),
                  pl.BlockSpec((1, DN_HEAD_DIM), lambda b, j: (0, 0))],
        out_specs=blk,
        out_shape=jax.ShapeDtypeStruct((t, DN_WIDTH), BF16),
        scratch_shapes=[pltpu.VMEM((DN_HEADS, DN_HEAD_DIM, DN_HEAD_DIM), F32),
                        pltpu.VMEM((DN_ROWS, DN_WIDTH), F32),
                        pltpu.VMEM((DN_ROWS, DN_WIDTH), F32),
                        pltpu.VMEM((DN_ROWS, DN_WIDTH), F32),
                        pltpu.VMEM((DN_ROWS, DN_WIDTH), F32),
                        pltpu.VMEM((DN_HEADS, DN_ROWS, DN_ROWS), F32)],
        compiler_params=pltpu.CompilerParams(dimension_semantics=("arbitrary", "arbitrary"),
                                             vmem_limit_bytes=VMEM_LIMIT),
        name="deltanet",
    )(q, k, v, z, gb, nw)


def _swa_kernel(sink_ref, q_ref, kp_ref, kc_ref, vp_ref, vc_ref, o_ref):
    n = pl.program_id(1)
    w = WINDOW
    kcat = jnp.concatenate([kp_ref[...], kc_ref[...]], axis=0)
    vcat = jnp.concatenate([vp_ref[...], vc_ref[...]], axis=0)
    rows = SWA_GROUP * w
    ri = lax.broadcasted_iota(I32, (rows, 2 * w), 0)
    kj = lax.broadcasted_iota(I32, (rows, 2 * w), 1)
    grp = ri // w
    rel = (ri - grp * w) + w - kj
    valid = (rel >= 0) & (rel < w) & ((kj >= w) | (n > 0))
    relf = rel.astype(F32)
    grp_col = lax.broadcasted_iota(I32, (rows, 1), 0) // w
    q = q_ref[...]
    outs = []
    for kh in range(SWA_KV_HEADS):
        slope_col = jnp.zeros((rows, 1), F32)
        sink_col = jnp.zeros((rows, 1), F32)
        for g in range(SWA_GROUP):
            hd = kh * SWA_GROUP + g
            slope_col = jnp.where(grp_col == g, 2.0 ** (-8.0 * (hd + 1) / SWA_HEADS), slope_col)
            sink_col = jnp.where(grp_col == g, sink_ref[hd], sink_col)
        qs = jnp.concatenate(
            [q[:, (kh * SWA_GROUP + g) * SWA_HEAD_DIM:(kh * SWA_GROUP + g + 1) * SWA_HEAD_DIM]
             for g in range(SWA_GROUP)], axis=0)
        kk = kcat[:, kh * SWA_HEAD_DIM:(kh + 1) * SWA_HEAD_DIM]
        vv = vcat[:, kh * SWA_HEAD_DIM:(kh + 1) * SWA_HEAD_DIM]
        s = lax.dot_general(qs, kk, (((1,), (1,)), ((), ())), preferred_element_type=F32)
        logits = jnp.where(valid, s - slope_col * relf, NEG)
        m = jnp.maximum(jnp.max(logits, axis=-1, keepdims=True), sink_col)
        p = jnp.exp(logits - m)
        denom = jnp.sum(p, axis=-1, keepdims=True) + jnp.exp(sink_col - m)
        o = jnp.dot(p.astype(BF16), vv, preferred_element_type=F32) / denom
        for g in range(SWA_GROUP):
            outs.append(o[g * w:(g + 1) * w, :])
    for t in range(SWA_WIDTH // LANES):
        o_ref[:, t * LANES:(t + 1) * LANES] = jnp.concatenate(
            [outs[2 * t], outs[2 * t + 1]], axis=-1).astype(o_ref.dtype)


def _swa(swq, swk, swv, sinks, batch, seq):
    t = swq.shape[0]
    nb = seq // WINDOW
    cur = lambda b, n, s: (b * nb + n, 0)
    prev = lambda b, n, s: (b * nb + jnp.maximum(n - 1, 0), 0)
    return pl.pallas_call(
        _swa_kernel,
        grid_spec=pltpu.PrefetchScalarGridSpec(
            num_scalar_prefetch=1,
            grid=(batch, nb),
            in_specs=[pl.BlockSpec((WINDOW, SWA_WIDTH), cur),
                      pl.BlockSpec((WINDOW, SWA_KV_WIDTH), prev),
                      pl.BlockSpec((WINDOW, SWA_KV_WIDTH), cur),
                      pl.BlockSpec((WINDOW, SWA_KV_WIDTH), prev),
                      pl.BlockSpec((WINDOW, SWA_KV_WIDTH), cur)],
            out_specs=pl.BlockSpec((WINDOW, SWA_WIDTH), cur)),
        out_shape=jax.ShapeDtypeStruct((t, SWA_WIDTH), BF16),
        compiler_params=pltpu.CompilerParams(dimension_semantics=("arbitrary", "arbitrary"),
                                             vmem_limit_bytes=VMEM_LIMIT),
        name="swa",
    )(sinks, swq, swk, swk, swv, swv)


def _post_kernel(x_ref, odn_ref, osw_ref, mod_ref, wo_ref, n2w_ref, wr_ref, br_ref,
                 x1_ref, h2_ref, idx_ref, gate_ref, rank_ref, cnt_ref, run_scr):
    i = pl.program_id(0)
    tm = x_ref.shape[0]

    @pl.when(i == 0)
    def _():
        run_scr[...] = jnp.zeros(run_scr.shape, F32)

    y = (jnp.dot(odn_ref[...], wo_ref[0:DN_WIDTH, :], preferred_element_type=F32)
         + jnp.dot(osw_ref[...], wo_ref[DN_WIDTH:, :], preferred_element_type=F32))
    x1 = x_ref[...] + mod_ref[0, 2:3, :] * y
    x1_ref[...] = x1
    hn = x1 * lax.rsqrt(jnp.mean(x1 * x1, axis=-1, keepdims=True) + EPS) * n2w_ref[...]
    h2 = hn * (1.0 + mod_ref[0, 4:5, :]) + mod_ref[0, 3:4, :]
    _store_token_tiles(h2_ref, h2)

    lane = lax.broadcasted_iota(I32, (tm, LANES), 1)
    lane_f = lane.astype(F32)
    logits = jnp.dot(h2.astype(BF16), wr_ref[...], preferred_element_type=F32) + br_ref[...]
    work = jnp.where(lane < N_EXPERTS, logits, NEG)
    sel = jnp.zeros((tm, LANES), F32)
    idx_out = jnp.zeros((tm, LANES), F32)
    vals = []
    idxs = []
    for kk in range(TOP_K):
        m = jnp.max(work, axis=-1, keepdims=True)
        ik = jnp.min(jnp.where(work == m, lane_f, float(LANES)), axis=-1, keepdims=True)
        hit = lane_f == ik
        work = jnp.where(hit, NEG, work)
        sel = jnp.where(hit, 1.0, sel)
        idx_out = jnp.where(lane == kk, ik, idx_out)
        vals.append(m)
        idxs.append(ik)
    es = [jnp.exp(v - vals[0]) for v in vals]
    tot = es[0] + es[1] + es[2] + es[3]
    gate_out = jnp.zeros((tm, LANES), F32)
    for kk in range(TOP_K):
        gate_out = jnp.where(lane == kk, es[kk] / tot, gate_out)
    gate_ref[...] = gate_out

    r = lax.broadcasted_iota(I32, (tm, tm), 0)
    c = lax.broadcasted_iota(I32, (tm, tm), 1)
    before = jnp.where(c < r, 1.0, 0.0).astype(BF16)
    base = jnp.dot(before, sel.astype(BF16), preferred_element_type=F32) + run_scr[0:1, :]
    rank_out = jnp.zeros((tm, LANES), F32)
    for kk in range(TOP_K):
        rk = jnp.sum(jnp.where(lane_f == idxs[kk], base, 0.0), axis=-1, keepdims=True)
        rank_out = jnp.where(lane == kk, rk, rank_out)
    idx_ref[...] = idx_out.T[0:SUBLANES, :].astype(I32)
    rank_ref[...] = rank_out.T[0:SUBLANES, :].astype(I32)
    run = run_scr[0:1, :] + jnp.sum(sel, axis=0, keepdims=True)
    run_scr[0:1, :] = run
    cnt_ref[...] = jnp.broadcast_to(run, cnt_ref.shape).astype(I32)


def _post(xf, odn, osw, mod, w_out, n2w, wr_pad, br_pad, seq):
    t = xf.shape[0]
    tiles_per_seq = seq // TM_POST
    row = lambda i: (i, 0)
    const = lambda i: (0, 0)
    return pl.pallas_call(
        _post_kernel,
        grid=(t // TM_POST,),
        in_specs=[pl.BlockSpec((TM_POST, D_MODEL), row),
                  pl.BlockSpec((TM_POST, DN_WIDTH), row),
                  pl.BlockSpec((TM_POST, SWA_WIDTH), row),
                  pl.BlockSpec((1, 6, D_MODEL), lambda i: (i // tiles_per_seq, 0, 0)),
                  pl.BlockSpec((D_MODEL, D_MODEL), const),
                  pl.BlockSpec((1, D_MODEL), const),
                  pl.BlockSpec((D_MODEL, LANES), const),
                  pl.BlockSpec((1, LANES), const)],
        out_specs=[pl.BlockSpec((TM_POST, D_MODEL), row),
                   pl.BlockSpec((TM_POST * SUBLANES, LANES), row),
                   pl.BlockSpec((SUBLANES, TM_POST), lambda i: (0, i)),
                   pl.BlockSpec((TM_POST, LANES), row),
                   pl.BlockSpec((SUBLANES, TM_POST), lambda i: (0, i)),
                   pl.BlockSpec((SUBLANES, LANES), const)],
        out_shape=[jax.ShapeDtypeStruct((t, D_MODEL), F32),
                   jax.ShapeDtypeStruct((t * SUBLANES, LANES), F32),
                   jax.ShapeDtypeStruct((SUBLANES, t), I32),
                   jax.ShapeDtypeStruct((t, LANES), F32),
                   jax.ShapeDtypeStruct((SUBLANES, t), I32),
                   jax.ShapeDtypeStruct((SUBLANES, LANES), I32)],
        scratch_shapes=[pltpu.VMEM((SUBLANES, LANES), F32)],
        compiler_params=pltpu.CompilerParams(dimension_semantics=("arbitrary",),
                                             vmem_limit_bytes=VMEM_LIMIT),
        name="post",
    )(xf, odn, osw, mod, w_out, n2w, wr_pad, br_pad)


INVERT_GROUP = 16


def _invert_kernel(fill_ref, dest_ref, rowtok_ref):
    def per_range(e, carry):
        def clear(rw, c2):
            rowtok_ref[rw] = 0
            return c2

        lax.fori_loop(fill_ref[0, e], fill_ref[1, e], clear, 0)
        return carry

    lax.fori_loop(0, fill_ref.shape[1], per_range, 0)

    n_tok = dest_ref.shape[0] // TOP_K
    for kk in range(TOP_K):
        def put(g, carry, kk=kk):
            tok0 = g * INVERT_GROUP
            rows = [dest_ref[kk * n_tok + tok0 + u] for u in range(INVERT_GROUP)]
            for u in range(INVERT_GROUP):
                rowtok_ref[rows[u]] = tok0 + u
            return carry

        lax.fori_loop(0, n_tok // INVERT_GROUP, put, 0)


def _invert(fill, dest_flat, n_pad):
    return pl.pallas_call(
        _invert_kernel,
        in_specs=[pl.BlockSpec(memory_space=pltpu.SMEM),
                  pl.BlockSpec(memory_space=pltpu.SMEM)],
        out_specs=pl.BlockSpec(memory_space=pltpu.SMEM),
        out_shape=jax.ShapeDtypeStruct((n_pad,), I32),
        name="invert",
    )(fill, dest_flat)


GATHER_SLOTS = 3


def _moe_kernel(be_ref, nv_ref, nxt_ref, rowtok_ref, h_ref, wu_hbm, bu_ref, wd_hbm, bd_ref, ys_ref,
                wu_f32, wd_f32, wu_bf, wd_bf, xbuf, sem, wsem):
    i = pl.program_id(0)
    slot = i % GATHER_SLOTS
    n_valid = nv_ref[0]

    def weight_copies(e):
        return (pltpu.make_async_copy(wu_hbm.at[e], wu_f32, wsem.at[0]),
                pltpu.make_async_copy(wd_hbm.at[e], wd_f32, wsem.at[1]))

    def gather(blk, s):
        for r in range(MOE_ROWS):
            _token_copy(h_ref, rowtok_ref[blk * MOE_ROWS + r], xbuf.at[s], r,
                        sem.at[s]).start(priority=r % 2)

    def gather_wait(s):
        pltpu.make_async_copy(h_ref.at[pl.ds(0, MOE_ROWS * SUBLANES), :], xbuf.at[s],
                              sem.at[s]).wait()

    last = n_valid - 1

    @pl.when(i == 0)
    def _():
        for cp in weight_copies(be_ref[0]):
            cp.start()
        gather(0, 0)
        gather(jnp.minimum(1, last), 1)

    @pl.when(i < n_valid)
    def _():
        expert = be_ref[i]
        prev = be_ref[jnp.maximum(i - 1, 0)]

        @pl.when((i == 0) | (expert != prev))
        def _():
            for cp in weight_copies(expert):
                cp.wait()
            wu_bf[...] = wu_f32[...].astype(BF16)
            wd_bf[...] = wd_f32[...].astype(BF16)
            nxt = nxt_ref[expert]

            @pl.when(nxt != expert)
            def _():
                for cp in weight_copies(nxt):
                    cp.start()

        gather_wait(slot)
        xb = _load_token_tiles(xbuf.at[slot], 0, MOE_ROWS).astype(BF16)
        gather(jnp.minimum(i + 2, last), (i + 2) % GATHER_SLOTS)
        hu = jnp.dot(xb, wu_bf[...], preferred_element_type=F32) + bu_ref[0]
        x_glu = jnp.minimum(hu[:, :D_FF], SWIGLU_LIMIT)
        x_lin = jnp.clip(hu[:, D_FF:], -SWIGLU_LIMIT, SWIGLU_LIMIT)
        act = x_glu * jax.nn.sigmoid(SWIGLU_ALPHA * x_glu) * (x_lin + 1.0)
        _store_token_tiles(
            ys_ref, jnp.dot(act.astype(BF16), wd_bf[...], preferred_element_type=F32) + bd_ref[0])

    @pl.when(i == last)
    def _():
        gather_wait((i + 1) % GATHER_SLOTS)
        gather_wait((i + 2) % GATHER_SLOTS)

    @pl.when(i >= n_valid)
    def _():
        ys_ref[...] = jnp.zeros(ys_ref.shape, F32)


def _moe(block_e, n_valid, next_e, row_tok, h2, w_up, b_up, w_down, b_down):
    n_pad = row_tok.shape[0]
    n_blocks = n_pad // MOE_ROWS
    out_rows = lambda i, be, nv, nx, rt: (i, 0)
    expert = lambda i, be, nv, nx, rt: (be[i], 0, 0)
    return pl.pallas_call(
        _moe_kernel,
        grid_spec=pltpu.PrefetchScalarGridSpec(
            num_scalar_prefetch=4,
            grid=(n_blocks,),
            in_specs=[pl.BlockSpec(memory_space=pl.ANY),
                      pl.BlockSpec(memory_space=pl.ANY),
                      pl.BlockSpec((1, 1, 2 * D_FF), expert),
                      pl.BlockSpec(memory_space=pl.ANY),
                      pl.BlockSpec((1, 1, D_MODEL), expert)],
            out_specs=pl.BlockSpec((MOE_ROWS * SUBLANES, LANES), out_rows),
            scratch_shapes=[pltpu.VMEM((D_MODEL, 2 * D_FF), F32),
                            pltpu.VMEM((D_FF, D_MODEL), F32),
                            pltpu.VMEM((D_MODEL, 2 * D_FF), BF16),
                            pltpu.VMEM((D_FF, D_MODEL), BF16),
                            pltpu.VMEM((GATHER_SLOTS, MOE_ROWS * SUBLANES, LANES), F32),
                            pltpu.SemaphoreType.DMA((GATHER_SLOTS,)),
                            pltpu.SemaphoreType.DMA((2,))]),
        out_shape=jax.ShapeDtypeStruct((n_pad * SUBLANES, LANES), F32),
        compiler_params=pltpu.CompilerParams(dimension_semantics=("arbitrary",),
                                             vmem_limit_bytes=VMEM_LIMIT),
        name="moe",
    )(block_e, n_valid, next_e, row_tok, h2, w_up, b_up, w_down, b_down)


def _combine_kernel(dcur_ref, dnext_ref, ys_ref, x1_ref, gate_ref, mod_ref, o_ref, buf, sem):
    i = pl.program_id(0)
    tm = x1_ref.shape[0]
    slot = i % 2

    def issue(dref, s):
        def body(tk, carry):
            for kk in range(TOP_K):
                _token_copy(ys_ref, dref[0, 0, kk * tm + tk], buf.at[s], kk * tm + tk,
                            sem.at[s]).start(priority=kk % 2)
            return carry

        lax.fori_loop(0, tm, body, 0, unroll=4)

    @pl.when(i == 0)
    def _():
        issue(dcur_ref, 0)

    @pl.when(i + 1 < pl.num_programs(0))
    def _():
        issue(dnext_ref, 1 - slot)

    pltpu.make_async_copy(ys_ref.at[pl.ds(0, TOP_K * tm * SUBLANES), :], buf.at[slot],
                          sem.at[slot]).wait()

    gates = gate_ref[...]
    acc = jnp.zeros((tm, D_MODEL), F32)
    for kk in range(TOP_K):
        acc = acc + gates[:, kk:kk + 1] * _load_token_tiles(buf.at[slot], kk * tm, tm)
    o_ref[...] = x1_ref[...] + mod_ref[0, 5:6, :] * acc


def _combine(dest3, ys, x1, gates, mod, seq):
    t = x1.shape[0]
    tiles_per_seq = seq // TM_COMB
    n_tiles = t // TM_COMB
    row = lambda i: (i, 0)
    return pl.pallas_call(
        _combine_kernel,
        grid=(n_tiles,),
        in_specs=[pl.BlockSpec((1, 1, TM_COMB * TOP_K), lambda i: (i, 0, 0),
                               memory_space=pltpu.SMEM),
                  pl.BlockSpec((1, 1, TM_COMB * TOP_K),
                               lambda i: (jnp.minimum(i + 1, n_tiles - 1), 0, 0),
                               memory_space=pltpu.SMEM),
                  pl.BlockSpec(memory_space=pl.ANY),
                  pl.BlockSpec((TM_COMB, D_MODEL), row),
                  pl.BlockSpec((TM_COMB, LANES), row),
                  pl.BlockSpec((1, 6, D_MODEL), lambda i: (i // tiles_per_seq, 0, 0))],
        out_specs=pl.BlockSpec((TM_COMB, D_MODEL), row),
        out_shape=jax.ShapeDtypeStruct((t, D_MODEL), F32),
        scratch_shapes=[pltpu.VMEM((2, TOP_K * TM_COMB * SUBLANES, LANES), F32),
                        pltpu.SemaphoreType.DMA((2,))],
        compiler_params=pltpu.CompilerParams(dimension_semantics=("arbitrary",),
                                             vmem_limit_bytes=VMEM_LIMIT),
        name="combine",
    )(dest3, dest3, ys, x1, gates, mod)


def _layer(x, mod, norm1_w, w_in, conv_w, a_log, dt_bias, dn_norm_w, q_norm_w, k_norm_w, sinks,
           w_out, norm2_w, w_router, b_router, w_up, b_up, w_down, b_down):
    batch, seq, _ = x.shape
    t = batch * seq
    xf = x.reshape(t, D_MODEL)

    o_qkv, o_z, o_a, o_b, o_q, o_k, o_v = 0, 1536, 2048, 2052, 2056, 2568, 2696
    w_ab = jnp.zeros((D_MODEL, LANES), F32).at[:, 0:2 * DN_HEADS].set(w_in[:, o_a:o_q])
    w_cat = jnp.concatenate([w_in[:, o_qkv:o_z], w_in[:, o_z:o_a], w_in[:, o_q:o_k],
                             w_in[:, o_k:o_v], w_in[:, o_v:], w_ab], axis=1).astype(BF16)
    gparams = jnp.zeros((2, LANES), F32)
    gparams = gparams.at[0, 0:DN_HEADS].set(-jnp.exp(a_log.astype(F32)))
    gparams = gparams.at[1, 0:DN_HEADS].set(dt_bias.astype(F32))
    qnw128 = jnp.tile(q_norm_w.astype(F32), 2)[None, :]
    knw128 = jnp.tile(k_norm_w.astype(F32), 2)[None, :]

    q, k, v, z, gb, swq, swk, swv = _inproj(xf, mod, norm1_w[None, :], w_cat, conv_w, gparams,
                                            qnw128, knw128, seq)
    o_dn = _deltanet(q, k, v, z, gb, dn_norm_w[None, :], batch, seq)
    o_sw = _swa(swq, swk, swv, sinks.astype(F32), batch, seq)

    wr_pad = jnp.zeros((D_MODEL, LANES), F32).at[:, 0:N_EXPERTS].set(w_router).astype(BF16)
    br_pad = jnp.zeros((1, LANES), F32).at[0, 0:N_EXPERTS].set(b_router)
    x1, h2, top_idx, gates, rank, cnt = _post(xf, o_dn, o_sw, mod, w_out.astype(BF16),
                                              norm2_w[None, :], wr_pad, br_pad, seq)

    counts = cnt[0, 0:N_EXPERTS]
    padded = (counts + MOE_ROWS - 1) // MOE_ROWS * MOE_ROWS
    pend = jnp.cumsum(padded)
    pstart = pend - padded
    n_blocks = (t * TOP_K) // MOE_ROWS + N_EXPERTS
    n_pad = n_blocks * MOE_ROWS
    n_valid = (pend[-1] // MOE_ROWS).astype(I32)
    blk = jnp.minimum(jnp.arange(n_blocks, dtype=I32), n_valid - 1)
    block_e = jnp.minimum(jnp.sum(pend[None, :] <= (blk * MOE_ROWS)[:, None], axis=1),
                          N_EXPERTS - 1).astype(I32)
    experts = jnp.arange(N_EXPERTS, dtype=I32)[:, None, None]
    dest = (jnp.sum(jnp.where(top_idx[None, 0:TOP_K] == experts, pstart[:, None, None], 0), axis=0)
            + rank[0:TOP_K]).astype(I32)
    fill = jnp.stack([jnp.append(pstart + counts, pend[-1]),
                      jnp.append(pend, n_pad)]).astype(I32)

    eids = jnp.arange(N_EXPERTS, dtype=I32)
    later = (eids[None, :] > eids[:, None]) & (counts[None, :] > 0)
    next_e = jnp.min(jnp.where(later, eids[None, :], N_EXPERTS), axis=1)
    next_e = jnp.where(next_e == N_EXPERTS, eids, next_e).astype(I32)

    row_tok = _invert(fill, dest.reshape(TOP_K * t), n_pad)
    ys = _moe(block_e, n_valid.reshape(1), next_e, row_tok, h2, w_up, b_up[:, None, :], w_down,
              b_down[:, None, :])
    n_tiles = t // TM_COMB
    dest_tiles = dest.reshape(TOP_K, n_tiles, TM_COMB).transpose(1, 0, 2).reshape(
        n_tiles, 1, TOP_K * TM_COMB)
    out = _combine(dest_tiles, ys, x1, gates, mod, seq)
    return out.reshape(batch, seq, D_MODEL)


def kernel(x, c, w_ada, b_ada, norm1_w, w_in, conv_w, a_log, dt_bias, dn_norm_w, q_norm_w,
           k_norm_w, sinks, w_out, norm2_w, w_router, b_router, w_up, b_up, w_down, b_down):
    out_dtype = x.dtype
    batch = x.shape[0]
    depth = w_ada.shape[0]
    c_pad = jnp.zeros((SUBLANES, D_MODEL), F32).at[0:batch].set(c.astype(F32))
    for l in range(depth):
        mod = _ada(c_pad, w_ada[l], b_ada[l][None, :])[0:batch].reshape(batch, 6, D_MODEL)
        x = _layer(x, mod, norm1_w[l], w_in[l], conv_w[l], a_log[l], dt_bias[l], dn_norm_w[l],
                   q_norm_w[l], k_norm_w[l], sinks[l], w_out[l], norm2_w[l], w_router[l],
                   b_router[l], w_up[l], b_up[l], w_down[l], b_down[l])
    return x.astype(out_dtype)
```

```python
import functools

import jax
import jax.numpy as jnp
from jax import lax
from jax.experimental import pallas as pl
from jax.experimental.pallas import tpu as pltpu

F32 = jnp.float32
BF16 = jnp.bfloat16
I32 = jnp.int32

D_MODEL = 1024
DN_HEADS = 4
DN_HEAD_DIM = 128
DN_WIDTH = DN_HEADS * DN_HEAD_DIM
CONV_WIDTH = 4
CHUNK = 64
SWA_HEADS = 8
SWA_KV_HEADS = 2
SWA_GROUP = SWA_HEADS // SWA_KV_HEADS
SWA_HEAD_DIM = 64
SWA_WIDTH = SWA_HEADS * SWA_HEAD_DIM
SWA_KV_WIDTH = SWA_KV_HEADS * SWA_HEAD_DIM
WINDOW = 128
N_EXPERTS = 32
TOP_K = 4
D_FF = D_MODEL
SWIGLU_ALPHA = 1.702
SWIGLU_LIMIT = 7.0
EPS = 1e-6
NEG = -1e30

LANES = 128
SUBLANES = 8
VMEM_LIMIT = 56 * 1024 * 1024

C_QKV = 0
C_Z = C_QKV + 3 * DN_WIDTH
C_SWQ = C_Z + DN_WIDTH
C_SWK = C_SWQ + SWA_WIDTH
C_SWV = C_SWK + SWA_KV_WIDTH
C_AB = C_SWV + SWA_KV_WIDTH
IN_PAD = C_AB + LANES

TM_IN = 256
DN_ROWS = 256
TM_POST = 512
MOE_ROWS = 256
TM_COMB = 128


def _silu(x):
    return x * jax.nn.sigmoid(x)


def _mm(a, b):
    return jnp.dot(a.astype(BF16), b.astype(BF16), preferred_element_type=F32)


def _mm_nt(a, b):
    return lax.dot_general(a.astype(BF16), b.astype(BF16), (((1,), (1,)), ((), ())),
                           preferred_element_type=F32)


def _mm_tn(a, b):
    return lax.dot_general(a.astype(BF16), b.astype(BF16), (((0,), (0,)), ((), ())),
                           preferred_element_type=F32)


def _load_token_tiles(ref, row0, rows):
    return jnp.concatenate(
        [ref[pl.ds(row0 * SUBLANES + s, rows, stride=SUBLANES), :] for s in range(D_MODEL // LANES)],
        axis=1)


def _store_token_tiles(ref, val):
    rows = val.shape[0]
    for s in range(D_MODEL // LANES):
        ref[pl.ds(s, rows, stride=SUBLANES), :] = val[:, s * LANES:(s + 1) * LANES]


def _token_copy(src_ref, s, dst_ref, d, sem):
    return pltpu.make_async_copy(
        src_ref.at[pl.ds(pl.multiple_of(s * SUBLANES, SUBLANES), SUBLANES), :],
        dst_ref.at[pl.ds(pl.multiple_of(d * SUBLANES, SUBLANES), SUBLANES), :], sem)


def _ada_kernel(c_ref, w_ref, b_ref, o_ref):
    c = c_ref[...]
    o_ref[...] = jnp.dot(_silu(c), w_ref[...], precision=lax.Precision.HIGHEST,
                         preferred_element_type=F32) + b_ref[...]


def _ada(c_pad, w_ada, b_ada):
    n = w_ada.shape[1]
    return pl.pallas_call(
        _ada_kernel,
        grid=(n // D_MODEL,),
        in_specs=[pl.BlockSpec((SUBLANES, D_MODEL), lambda j: (0, 0)),
                  pl.BlockSpec((D_MODEL, D_MODEL), lambda j: (0, j)),
                  pl.BlockSpec((1, D_MODEL), lambda j: (0, j))],
        out_specs=pl.BlockSpec((SUBLANES, D_MODEL), lambda j: (0, j)),
        out_shape=jax.ShapeDtypeStruct((SUBLANES, n), F32),
        name="ada",
    )(c_pad, w_ada, b_ada)


def _rms_heads64(a, w128):
    lane = lax.broadcasted_iota(I32, a.shape, 1)
    lo = lane < SWA_HEAD_DIM
    sq = a * a
    s_lo = jnp.sum(jnp.where(lo, sq, 0.0), axis=-1, keepdims=True)
    s_hi = jnp.sum(jnp.where(lo, 0.0, sq), axis=-1, keepdims=True)
    ms = jnp.where(lo, s_lo, s_hi) * (1.0 / SWA_HEAD_DIM)
    return a * lax.rsqrt(ms + EPS) * w128


def _inproj_kernel(tiles_per_seq, x_ref, mod_ref, n1w_ref, w_ref, convw_ref, gp_ref, qnw_ref,
                   knw_ref, q_ref, k_ref, v_ref, z_ref, gb_ref, swq_ref, swk_ref, swv_ref,
                   conv_scr):
    i = pl.program_id(0)
    tm = x_ref.shape[0]
    x = x_ref[...]
    y = x * lax.rsqrt(jnp.mean(x * x, axis=-1, keepdims=True) + EPS) * n1w_ref[...]
    h = (y * (1.0 + mod_ref[0, 1:2, :]) + mod_ref[0, 0:1, :]).astype(BF16)

    @pl.when(i % tiles_per_seq == 0)
    def _():
        conv_scr[0:SUBLANES, :] = jnp.zeros((SUBLANES, 3 * DN_WIDTH), F32)

    u = jnp.dot(h, w_ref[:, C_QKV:C_Z], preferred_element_type=F32)
    conv_scr[SUBLANES:SUBLANES + tm, :] = u
    cw = convw_ref[...]
    acc = u * cw[3:4, :]
    for j in range(CONV_WIDTH - 1):
        off = SUBLANES - (CONV_WIDTH - 1) + j
        acc = acc + conv_scr[off:off + tm, :] * cw[j:j + 1, :]
    conv_scr[0:SUBLANES, :] = conv_scr[tm:tm + SUBLANES, :]
    qkv = _silu(acc)

    for hh in range(DN_HEADS):
        sl = slice(hh * DN_HEAD_DIM, (hh + 1) * DN_HEAD_DIM)
        qh = qkv[:, hh * DN_HEAD_DIM:(hh + 1) * DN_HEAD_DIM]
        kh = qkv[:, DN_WIDTH + hh * DN_HEAD_DIM:DN_WIDTH + (hh + 1) * DN_HEAD_DIM]
        q_ref[:, sl] = qh * lax.rsqrt(jnp.sum(qh * qh, axis=-1, keepdims=True) + EPS) * (
            DN_HEAD_DIM ** -0.5)
        k_ref[:, sl] = kh * lax.rsqrt(jnp.sum(kh * kh, axis=-1, keepdims=True) + EPS)
    v_ref[...] = qkv[:, 2 * DN_WIDTH:3 * DN_WIDTH]

    z_ref[...] = jnp.dot(h, w_ref[:, C_Z:C_SWQ], preferred_element_type=F32)

    ab = jnp.dot(h, w_ref[:, C_AB:IN_PAD], preferred_element_type=F32)
    lane = lax.broadcasted_iota(I32, ab.shape, 1)
    sp_in = ab + gp_ref[1:2, :]
    softplus = jnp.maximum(sp_in, 0.0) + jnp.log1p(jnp.exp(-jnp.abs(sp_in)))
    gb_ref[...] = jnp.where(lane < DN_HEADS, gp_ref[0:1, :] * softplus, jax.nn.sigmoid(ab))

    sq = jnp.dot(h, w_ref[:, C_SWQ:C_SWK], preferred_element_type=F32)
    for t in range(SWA_WIDTH // LANES):
        sl = slice(t * LANES, (t + 1) * LANES)
        swq_ref[:, sl] = (_rms_heads64(sq[:, sl], qnw_ref[...]) * (SWA_HEAD_DIM ** -0.5)).astype(BF16)
    sk = jnp.dot(h, w_ref[:, C_SWK:C_SWV], preferred_element_type=F32)
    swk_ref[...] = _rms_heads64(sk, knw_ref[...]).astype(BF16)
    swv_ref[...] = jnp.dot(h, w_ref[:, C_SWV:C_AB], preferred_element_type=F32).astype(BF16)


def _inproj(xf, mod, n1w, w_cat, conv_w, gparams, qnw128, knw128, seq):
    t = xf.shape[0]
    tiles_per_seq = seq // TM_IN
    row = lambda i: (i, 0)
    const = lambda i: (0, 0)
    outs = [jax.ShapeDtypeStruct((t, DN_WIDTH), F32)] * 4 + [
        jax.ShapeDtypeStruct((t, LANES), F32),
        jax.ShapeDtypeStruct((t, SWA_WIDTH), BF16),
        jax.ShapeDtypeStruct((t, SWA_KV_WIDTH), BF16),
        jax.ShapeDtypeStruct((t, SWA_KV_WIDTH), BF16)]
    out_specs = [pl.BlockSpec((TM_IN, DN_WIDTH), row)] * 4 + [
        pl.BlockSpec((TM_IN, LANES), row),
        pl.BlockSpec((TM_IN, SWA_WIDTH), row),
        pl.BlockSpec((TM_IN, SWA_KV_WIDTH), row),
        pl.BlockSpec((TM_IN, SWA_KV_WIDTH), row)]
    return pl.pallas_call(
        functools.partial(_inproj_kernel, tiles_per_seq),
        grid=(t // TM_IN,),
        in_specs=[pl.BlockSpec((TM_IN, D_MODEL), row),
                  pl.BlockSpec((1, 6, D_MODEL), lambda i: (i // tiles_per_seq, 0, 0)),
                  pl.BlockSpec((1, D_MODEL), const),
                  pl.BlockSpec((D_MODEL, IN_PAD), const),
                  pl.BlockSpec((CONV_WIDTH, 3 * DN_WIDTH), const),
                  pl.BlockSpec((2, LANES), const),
                  pl.BlockSpec((1, LANES), const),
                  pl.BlockSpec((1, LANES), const)],
        out_specs=out_specs,
        out_shape=outs,
        scratch_shapes=[pltpu.VMEM((TM_IN + SUBLANES, 3 * DN_WIDTH), F32)],
        compiler_params=pltpu.CompilerParams(dimension_semantics=("arbitrary",),
                                             vmem_limit_bytes=VMEM_LIMIT),
        name="inproj",
    )(xf, mod, n1w, w_cat, conv_w, gparams, qnw128, knw128)


def _dn_block(j, q_ref, k_ref, v_ref, z_ref, gb_ref, nw_ref, o_ref,
              s_scr, u_scr, w_scr, qd_scr, kd_scr, a_scr):
    rows = q_ref.shape[0]
    n_chunks = rows // CHUNK
    heads = range(DN_HEADS)

    @pl.when(j == 0)
    def _():
        s_scr[...] = jnp.zeros(s_scr.shape, F32)

    r = lax.broadcasted_iota(I32, (rows, rows), 0)
    c = lax.broadcasted_iota(I32, (rows, rows), 1)
    same = (r // CHUNK) == (c // CHUNK)
    incl = same & (c <= r)
    strict = same & (c < r)
    gb = gb_ref[...]
    gc = jnp.dot(jnp.where(incl, 1.0, 0.0), gb, precision=lax.Precision.HIGHEST,
                 preferred_element_type=F32)
    gl = jnp.dot(jnp.where(same, 1.0, 0.0), gb, precision=lax.Precision.HIGHEST,
                 preferred_element_type=F32)
    gct = gc.T

    rmat = []
    rhs = []
    for h in heads:
        ls = slice(h * DN_HEAD_DIM, (h + 1) * DN_HEAD_DIM)
        g_col = gc[:, h:h + 1]
        beta = gb[:, DN_HEADS + h:DN_HEADS + h + 1]
        decay = jnp.where(incl, jnp.exp(jnp.where(incl, g_col - gct[h:h + 1, :], 0.0)), 0.0)
        qq = q_ref[:, ls]
        kk = k_ref[:, ls]
        kb = kk * beta
        eg = jnp.exp(g_col)
        rmat.append(-jnp.where(strict, _mm_nt(kb, kk) * decay, 0.0))
        rhs.append(jnp.concatenate([v_ref[:, ls] * beta, kb * eg], axis=1))
        a_scr[h] = jnp.where(incl, _mm_nt(qq, kk) * decay, 0.0)
        qd_scr[:, ls] = qq * eg
        kd_scr[:, ls] = kk * jnp.exp(gl[:, h:h + 1] - g_col)

    xm = list(rmat)
    pm = list(rmat)
    for it in range(6):
        xm = [xm[h] + _mm(pm[h], xm[h]) for h in heads]
        if it < 5:
            pm = [_mm(pm[h], pm[h]) for h in heads]
    for h in heads:
        ls = slice(h * DN_HEAD_DIM, (h + 1) * DN_HEAD_DIM)
        sol = rhs[h] + _mm(xm[h], rhs[h])
        u_scr[:, ls] = sol[:, 0:DN_HEAD_DIM]
        w_scr[:, ls] = sol[:, DN_HEAD_DIM:2 * DN_HEAD_DIM]

    nw = nw_ref[...]
    lsl = [slice(h * DN_HEAD_DIM, (h + 1) * DN_HEAD_DIM) for h in heads]
    for ci in range(n_chunks):
        rs = slice(ci * CHUNK, (ci + 1) * CHUNK)
        dvals = jnp.exp(gl[ci * CHUNK:ci * CHUNK + 1, :])
        st = [s_scr[h] for h in heads]
        ws = [_mm(w_scr[rs, lsl[h]], st[h]) for h in heads]
        qs = [_mm(qd_scr[rs, lsl[h]], st[h]) for h in heads]
        vn = [u_scr[rs, lsl[h]] - ws[h] for h in heads]
        av = [_mm(a_scr[h, rs, rs], vn[h]) for h in heads]
        kv = [_mm_tn(kd_scr[rs, lsl[h]], vn[h]) for h in heads]
        for h in heads:
            s_scr[h] = st[h] * dvals[:, h:h + 1] + kv[h]
            o = qs[h] + av[h]
            on = o * lax.rsqrt(jnp.mean(o * o, axis=-1, keepdims=True) + EPS) * nw
            o_ref[rs, lsl[h]] = (on * _silu(z_ref[rs, lsl[h]])).astype(o_ref.dtype)


def _swa_window(sink_ref, q, kcat, vcat, first):
    w = WINDOW
    rows = SWA_GROUP * w
    ri = lax.broadcasted_iota(I32, (rows, 2 * w), 0)
    kj = lax.broadcasted_iota(I32, (rows, 2 * w), 1)
    grp = ri // w
    rel = (ri - grp * w) + w - kj
    valid = (rel >= 0) & (rel < w)
    if first is not None:
        valid = valid & ((kj >= w) | jnp.logical_not(first))
    relf = rel.astype(F32)
    grp_col = lax.broadcasted_iota(I32, (rows, 1), 0) // w
    outs = []
    for kh in range(SWA_KV_HEADS):
        slope_col = jnp.zeros((rows, 1), F32)
        sink_col = jnp.zeros((rows, 1), F32)
        for g in range(SWA_GROUP):
            hd = kh * SWA_GROUP + g
            slope_col = jnp.where(grp_col == g, 2.0 ** (-8.0 * (hd + 1) / SWA_HEADS), slope_col)
            sink_col = jnp.where(grp_col == g, sink_ref[hd], sink_col)
        qs = jnp.concatenate(
            [q[:, (kh * SWA_GROUP + g) * SWA_HEAD_DIM:(kh * SWA_GROUP + g + 1) * SWA_HEAD_DIM]
             for g in range(SWA_GROUP)], axis=0)
        kk = kcat[:, kh * SWA_HEAD_DIM:(kh + 1) * SWA_HEAD_DIM]
        vv = vcat[:, kh * SWA_HEAD_DIM:(kh + 1) * SWA_HEAD_DIM]
        s = lax.dot_general(qs, kk, (((1,), (1,)), ((), ())), preferred_element_type=F32)
        logits = jnp.where(valid, s - slope_col * relf, NEG)
        m = jnp.maximum(jnp.max(logits, axis=-1, keepdims=True), sink_col)
        p = jnp.exp(logits - m)
        denom = jnp.sum(p, axis=-1, keepdims=True) + jnp.exp(sink_col - m)
        o = jnp.dot(p.astype(BF16), vv, preferred_element_type=F32) / denom
        for g in range(SWA_GROUP):
            outs.append(o[g * w:(g + 1) * w, :])
    return [jnp.concatenate([outs[2 * t], outs[2 * t + 1]], axis=-1)
            for t in range(SWA_WIDTH // LANES)]


def _mixer_kernel(sink_ref, q_ref, k_ref, v_ref, z_ref, gb_ref, nw_ref,
                  swq_ref, kp_ref, kc_ref, vp_ref, vc_ref, odn_ref, osw_ref,
                  s_scr, u_scr, w_scr, qd_scr, kd_scr, a_scr):
    j = pl.program_id(1)
    _dn_block(j, q_ref, k_ref, v_ref, z_ref, gb_ref, nw_ref, odn_ref,
              s_scr, u_scr, w_scr, qd_scr, kd_scr, a_scr)
    w = WINDOW
    kcur = kc_ref[...]
    vcur = vc_ref[...]
    for s in range(DN_ROWS // w):
        rows = slice(s * w, (s + 1) * w)
        if s == 0:
            kcat = jnp.concatenate([kp_ref[...], kcur[0:w, :]], axis=0)
            vcat = jnp.concatenate([vp_ref[...], vcur[0:w, :]], axis=0)
            first = j == 0
        else:
            kcat = kcur[(s - 1) * w:(s + 1) * w, :]
            vcat = vcur[(s - 1) * w:(s + 1) * w, :]
            first = None
        tiles = _swa_window(sink_ref, swq_ref[rows, :], kcat, vcat, first)
        for t, tile in enumerate(tiles):
            osw_ref[rows, t * LANES:(t + 1) * LANES] = tile.astype(osw_ref.dtype)


def _mixer(q, k, v, z, gb, nw, swq, swk, swv, sinks, batch, seq):
    t = q.shape[0]
    steps = seq // DN_ROWS
    per_step = DN_ROWS // WINDOW
    row = lambda b, j, s: (b * steps + j, 0)
    prev = lambda b, j, s: (b * steps * per_step + jnp.maximum(j * per_step - 1, 0), 0)
    blk = pl.BlockSpec((DN_ROWS, DN_WIDTH), row)
    kv_cur = pl.BlockSpec((DN_ROWS, SWA_KV_WIDTH), row)
    kv_prev = pl.BlockSpec((WINDOW, SWA_KV_WIDTH), prev)
    return pl.pallas_call(
        _mixer_kernel,
        grid_spec=pltpu.PrefetchScalarGridSpec(
            num_scalar_prefetch=1,
            grid=(batch, steps),
            in_specs=[blk, blk, blk, blk,
                      pl.BlockSpec((DN_ROWS, LANES), row),
                      pl.BlockSpec((1, DN_HEAD_DIM), lambda b, j, s: (0, 0)),
                      pl.BlockSpec((DN_ROWS, SWA_WIDTH), row),
                      kv_prev, kv_cur, kv_prev, kv_cur],
            out_specs=[blk, pl.BlockSpec((DN_ROWS, SWA_WIDTH), row)],
            scratch_shapes=[pltpu.VMEM((DN_HEADS, DN_HEAD_DIM, DN_HEAD_DIM), F32),
                            pltpu.VMEM((DN_ROWS, DN_WIDTH), F32),
                            pltpu.VMEM((DN_ROWS, DN_WIDTH), F32),
                            pltpu.VMEM((DN_ROWS, DN_WIDTH), F32),
                            pltpu.VMEM((DN_ROWS, DN_WIDTH), F32),
                            pltpu.VMEM((DN_HEADS, DN_ROWS, DN_ROWS), F32)]),
        out_shape=[jax.ShapeDtypeStruct((t, DN_WIDTH), BF16),
                   jax.ShapeDtypeStruct((t, SWA_WIDTH), BF16)],
        compiler_params=pltpu.CompilerParams(dimension_semantics=("arbitrary", "arbitrary"),
                                             vmem_limit_bytes=VMEM_LIMIT),
        name="mixer",
    )(sinks, q, k, v, z, gb, nw, swq, swk, swk, swv, swv)


def _post_kernel(x_ref, odn_ref, osw_ref, mod_ref, wo_ref, n2w_ref, wr_ref, br_ref,
                 x1_ref, h2_ref, idx_ref, gate_ref, rank_ref, cnt_ref, run_scr):
    i = pl.program_id(0)
    tm = x_ref.shape[0]

    @pl.when(i == 0)
    def _():
        run_scr[...] = jnp.zeros(run_scr.shape, F32)

    y = (jnp.dot(odn_ref[...], wo_ref[0:DN_WIDTH, :], preferred_element_type=F32)
         + jnp.dot(osw_ref[...], wo_ref[DN_WIDTH:, :], preferred_element_type=F32))
    x1 = x_ref[...] + mod_ref[0, 2:3, :] * y
    x1_ref[...] = x1
    hn = x1 * lax.rsqrt(jnp.mean(x1 * x1, axis=-1, keepdims=True) + EPS) * n2w_ref[...]
    h2 = hn * (1.0 + mod_ref[0, 4:5, :]) + mod_ref[0, 3:4, :]
    _store_token_tiles(h2_ref, h2)

    lane = lax.broadcasted_iota(I32, (tm, LANES), 1)
    lane_f = lane.astype(F32)
    logits = jnp.dot(h2.astype(BF16), wr_ref[...], preferred_element_type=F32) + br_ref[...]
    work = jnp.where(lane < N_EXPERTS, logits, NEG)
    sel = jnp.zeros((tm, LANES), F32)
    idx_out = jnp.zeros((tm, LANES), F32)
    vals = []
    idxs = []
    for kk in range(TOP_K):
        m = jnp.max(work, axis=-1, keepdims=True)
        ik = jnp.min(jnp.where(work == m, lane_f, float(LANES)), axis=-1, keepdims=True)
        hit = lane_f == ik
        work = jnp.where(hit, NEG, work)
        sel = jnp.where(hit, 1.0, sel)
        idx_out = jnp.where(lane == kk, ik, idx_out)
        vals.append(m)
        idxs.append(ik)
    es = [jnp.exp(v - vals[0]) for v in vals]
    tot = es[0] + es[1] + es[2] + es[3]
    gate_out = jnp.zeros((tm, LANES), F32)
    for kk in range(TOP_K):
        gate_out = jnp.where(lane == kk, es[kk] / tot, gate_out)
    gate_ref[...] = gate_out

    r = lax.broadcasted_iota(I32, (tm, tm), 0)
    c = lax.broadcasted_iota(I32, (tm, tm), 1)
    before = jnp.where(c < r, 1.0, 0.0).astype(BF16)
    base = jnp.dot(before, sel.astype(BF16), preferred_element_type=F32) + run_scr[0:1, :]
    rank_out = jnp.zeros((tm, LANES), F32)
    for kk in range(TOP_K):
        rk = jnp.sum(jnp.where(lane_f == idxs[kk], base, 0.0), axis=-1, keepdims=True)
        rank_out = jnp.where(lane == kk, rk, rank_out)
    idx_ref[...] = idx_out.T[0:SUBLANES, :].astype(I32)
    rank_ref[...] = rank_out.T[0:SUBLANES, :].astype(I32)
    run = run_scr[0:1, :] + jnp.sum(sel, axis=0, keepdims=True)
    run_scr[0:1, :] = run
    cnt_ref[...] = jnp.broadcast_to(run, cnt_ref.shape).astype(I32)


def _post(xf, odn, osw, mod, w_out, n2w, wr_pad, br_pad, seq):
    t = xf.shape[0]
    tiles_per_seq = seq // TM_POST
    row = lambda i: (i, 0)
    const = lambda i: (0, 0)
    return pl.pallas_call(
        _post_kernel,
        grid=(t // TM_POST,),
        in_specs=[pl.BlockSpec((TM_POST, D_MODEL), row),
                  pl.BlockSpec((TM_POST, DN_WIDTH), row),
                  pl.BlockSpec((TM_POST, SWA_WIDTH), row),
                  pl.BlockSpec((1, 6, D_MODEL), lambda i: (i // tiles_per_seq, 0, 0)),
                  pl.BlockSpec((D_MODEL, D_MODEL), const),
                  pl.BlockSpec((1, D_MODEL), const),
                  pl.BlockSpec((D_MODEL, LANES), const),
                  pl.BlockSpec((1, LANES), const)],
        out_specs=[pl.BlockSpec((TM_POST, D_MODEL), row),
                   pl.BlockSpec((TM_POST * SUBLANES, LANES), row),
                   pl.BlockSpec((SUBLANES, TM_POST), lambda i: (0, i)),
                   pl.BlockSpec((TM_POST, LANES), row),
                   pl.BlockSpec((SUBLANES, TM_POST), lambda i: (0, i)),
                   pl.BlockSpec((SUBLANES, LANES), const)],
        out_shape=[jax.ShapeDtypeStruct((t, D_MODEL), F32),
                   jax.ShapeDtypeStruct((t * SUBLANES, LANES), F32),
                   jax.ShapeDtypeStruct((SUBLANES, t), I32),
                   jax.ShapeDtypeStruct((t, LANES), F32),
                   jax.ShapeDtypeStruct((SUBLANES, t), I32),
                   jax.ShapeDtypeStruct((SUBLANES, LANES), I32)],
        scratch_shapes=[pltpu.VMEM((SUBLANES, LANES), F32)],
        compiler_params=pltpu.CompilerParams(dimension_semantics=("arbitrary",),
                                             vmem_limit_bytes=VMEM_LIMIT),
        name="post",
    )(xf, odn, osw, mod, w_out, n2w, wr_pad, br_pad)


INVERT_GROUP = 32


def _invert_kernel(fill_ref, dest_ref, rowtok_ref):
    def per_range(e, carry):
        def clear(rw, c2):
            rowtok_ref[rw] = 0
            return c2

        lax.fori_loop(fill_ref[0, e], fill_ref[1, e], clear, 0)
        return carry

    lax.fori_loop(0, fill_ref.shape[1], per_range, 0)

    n_tok = dest_ref.shape[0] // TOP_K
    for kk in range(TOP_K):
        def put(g, carry, kk=kk):
            tok0 = g * INVERT_GROUP
            rows = [dest_ref[kk * n_tok + tok0 + u] for u in range(INVERT_GROUP)]
            for u in range(INVERT_GROUP):
                rowtok_ref[rows[u]] = tok0 + u
            return carry

        lax.fori_loop(0, n_tok // INVERT_GROUP, put, 0)


def _invert(fill, dest_flat, n_pad):
    return pl.pallas_call(
        _invert_kernel,
        in_specs=[pl.BlockSpec(memory_space=pltpu.SMEM),
                  pl.BlockSpec(memory_space=pltpu.SMEM)],
        out_specs=pl.BlockSpec(memory_space=pltpu.SMEM),
        out_shape=jax.ShapeDtypeStruct((n_pad,), I32),
        name="invert",
    )(fill, dest_flat)


GATHER_SLOTS = 3


def _moe_kernel(be_ref, nv_ref, nxt_ref, rowtok_ref, h_ref, wu_hbm, bu_ref, wd_hbm, bd_ref, ys_ref,
                wu_f32, wd_f32, wu_bf, wd_bf, xbuf, sem, wsem):
    i = pl.program_id(0)
    slot = i % GATHER_SLOTS
    n_valid = nv_ref[0]

    def weight_copies(e):
        return (pltpu.make_async_copy(wu_hbm.at[e], wu_f32, wsem.at[0]),
                pltpu.make_async_copy(wd_hbm.at[e], wd_f32, wsem.at[1]))

    def gather(blk, s):
        for r in range(MOE_ROWS):
            _token_copy(h_ref, rowtok_ref[blk * MOE_ROWS + r], xbuf.at[s], r,
                        sem.at[s]).start(priority=r % 2)

    def gather_wait(s):
        pltpu.make_async_copy(h_ref.at[pl.ds(0, MOE_ROWS * SUBLANES), :], xbuf.at[s],
                              sem.at[s]).wait()

    last = n_valid - 1

    @pl.when(i == 0)
    def _():
        for cp in weight_copies(be_ref[0]):
            cp.start()
        gather(0, 0)
        gather(jnp.minimum(1, last), 1)

    @pl.when(i < n_valid)
    def _():
        expert = be_ref[i]
        prev = be_ref[jnp.maximum(i - 1, 0)]

        @pl.when((i == 0) | (expert != prev))
        def _():
            for cp in weight_copies(expert):
                cp.wait()
            wu_bf[...] = wu_f32[...].astype(BF16)
            wd_bf[...] = wd_f32[...].astype(BF16)
            nxt = nxt_ref[expert]

            @pl.when(nxt != expert)
            def _():
                for cp in weight_copies(nxt):
                    cp.start()

        gather_wait(slot)
        xb = _load_token_tiles(xbuf.at[slot], 0, MOE_ROWS).astype(BF16)
        gather(jnp.minimum(i + 2, last), (i + 2) % GATHER_SLOTS)
        hu = jnp.dot(xb, wu_bf[...], preferred_element_type=F32) + bu_ref[0]
        x_glu = jnp.minimum(hu[:, :D_FF], SWIGLU_LIMIT)
        x_lin = jnp.clip(hu[:, D_FF:], -SWIGLU_LIMIT, SWIGLU_LIMIT)
        act = x_glu * jax.nn.sigmoid(SWIGLU_ALPHA * x_glu) * (x_lin + 1.0)
        _store_token_tiles(
            ys_ref, jnp.dot(act.astype(BF16), wd_bf[...], preferred_element_type=F32) + bd_ref[0])

    @pl.when(i == last)
    def _():
        gather_wait((i + 1) % GATHER_SLOTS)
        gather_wait((i + 2) % GATHER_SLOTS)

    @pl.when(i >= n_valid)
    def _():
        ys_ref[...] = jnp.zeros(ys_ref.shape, F32)


def _moe(block_e, n_valid, next_e, row_tok, h2, w_up, b_up, w_down, b_down):
    n_pad = row_tok.shape[0]
    n_blocks = n_pad // MOE_ROWS
    out_rows = lambda i, be, nv, nx, rt: (i, 0)
    expert = lambda i, be, nv, nx, rt: (be[i], 0, 0)
    return pl.pallas_call(
        _moe_kernel,
        grid_spec=pltpu.PrefetchScalarGridSpec(
            num_scalar_prefetch=4,
            grid=(n_blocks,),
            in_specs=[pl.BlockSpec(memory_space=pl.ANY),
                      pl.BlockSpec(memory_space=pl.ANY),
                      pl.BlockSpec((1, 1, 2 * D_FF), expert),
                      pl.BlockSpec(memory_space=pl.ANY),
                      pl.BlockSpec((1, 1, D_MODEL), expert)],
            out_specs=pl.BlockSpec((MOE_ROWS * SUBLANES, LANES), out_rows),
            scratch_shapes=[pltpu.VMEM((D_MODEL, 2 * D_FF), F32),
                            pltpu.VMEM((D_FF, D_MODEL), F32),
                            pltpu.VMEM((D_MODEL, 2 * D_FF), BF16),
                            pltpu.VMEM((D_FF, D_MODEL), BF16),
                            pltpu.VMEM((GATHER_SLOTS, MOE_ROWS * SUBLANES, LANES), F32),
                            pltpu.SemaphoreType.DMA((GATHER_SLOTS,)),
                            pltpu.SemaphoreType.DMA((2,))]),
        out_shape=jax.ShapeDtypeStruct((n_pad * SUBLANES, LANES), F32),
        compiler_params=pltpu.CompilerParams(dimension_semantics=("arbitrary",),
                                             vmem_limit_bytes=VMEM_LIMIT),
        name="moe",
    )(block_e, n_valid, next_e, row_tok, h2, w_up, b_up, w_down, b_down)


def _combine_kernel(dcur_ref, dnext_ref, ys_ref, x1_ref, gate_ref, mod_ref, o_ref, buf, sem):
    i = pl.program_id(0)
    tm = x1_ref.shape[0]
    slot = i % 2

    def issue(dref, s):
        def body(tk, carry):
            for kk in range(TOP_K):
                _token_copy(ys_ref, dref[0, 0, kk * tm + tk], buf.at[s], kk * tm + tk,
                            sem.at[s]).start(priority=kk % 2)
            return carry

        lax.fori_loop(0, tm, body, 0, unroll=4)

    @pl.when(i == 0)
    def _():
        issue(dcur_ref, 0)

    @pl.when(i + 1 < pl.num_programs(0))
    def _():
        issue(dnext_ref, 1 - slot)

    pltpu.make_async_copy(ys_ref.at[pl.ds(0, TOP_K * tm * SUBLANES), :], buf.at[slot],
                          sem.at[slot]).wait()

    gates = gate_ref[...]
    acc = jnp.zeros((tm, D_MODEL), F32)
    for kk in range(TOP_K):
        acc = acc + gates[:, kk:kk + 1] * _load_token_tiles(buf.at[slot], kk * tm, tm)
    o_ref[...] = x1_ref[...] + mod_ref[0, 5:6, :] * acc


def _combine(dest3, ys, x1, gates, mod, seq):
    t = x1.shape[0]
    tiles_per_seq = seq // TM_COMB
    n_tiles = t // TM_COMB
    row = lambda i: (i, 0)
    return pl.pallas_call(
        _combine_kernel,
        grid=(n_tiles,),
        in_specs=[pl.BlockSpec((1, 1, TM_COMB * TOP_K), lambda i: (i, 0, 0),
                               memory_space=pltpu.SMEM),
                  pl.BlockSpec((1, 1, TM_COMB * TOP_K),
                               lambda i: (jnp.minimum(i + 1, n_tiles - 1), 0, 0),
                               memory_space=pltpu.SMEM),
                  pl.BlockSpec(memory_space=pl.ANY),
                  pl.BlockSpec((TM_COMB, D_MODEL), row),
                  pl.BlockSpec((TM_COMB, LANES), row),
                  pl.BlockSpec((1, 6, D_MODEL), lambda i: (i // tiles_per_seq, 0, 0))],
        out_specs=pl.BlockSpec((TM_COMB, D_MODEL), row),
        out_shape=jax.ShapeDtypeStruct((t, D_MODEL), F32),
        scratch_shapes=[pltpu.VMEM((2, TOP_K * TM_COMB * SUBLANES, LANES), F32),
                        pltpu.SemaphoreType.DMA((2,))],
        compiler_params=pltpu.CompilerParams(dimension_semantics=("arbitrary",),
                                             vmem_limit_bytes=VMEM_LIMIT),
        name="combine",
    )(dest3, dest3, ys, x1, gates, mod)


def _layer(x, mod, norm1_w, w_in, conv_w, a_log, dt_bias, dn_norm_w, q_norm_w, k_norm_w, sinks,
           w_out, norm2_w, w_router, b_router, w_up, b_up, w_down, b_down):
    batch, seq, _ = x.shape
    t = batch * seq
    xf = x.reshape(t, D_MODEL)

    o_qkv, o_z, o_a, o_b, o_q, o_k, o_v = 0, 1536, 2048, 2052, 2056, 2568, 2696
    w_ab = jnp.zeros((D_MODEL, LANES), F32).at[:, 0:2 * DN_HEADS].set(w_in[:, o_a:o_q])
    w_cat = jnp.concatenate([w_in[:, o_qkv:o_z], w_in[:, o_z:o_a], w_in[:, o_q:o_k],
                             w_in[:, o_k:o_v], w_in[:, o_v:], w_ab], axis=1).astype(BF16)
    gparams = jnp.zeros((2, LANES), F32)
    gparams = gparams.at[0, 0:DN_HEADS].set(-jnp.exp(a_log.astype(F32)))
    gparams = gparams.at[1, 0:DN_HEADS].set(dt_bias.astype(F32))
    qnw128 = jnp.tile(q_norm_w.astype(F32), 2)[None, :]
    knw128 = jnp.tile(k_norm_w.astype(F32), 2)[None, :]

    q, k, v, z, gb, swq, swk, swv = _inproj(xf, mod, norm1_w[None, :], w_cat, conv_w, gparams,
                                            qnw128, knw128, seq)
    o_dn, o_sw = _mixer(q, k, v, z, gb, dn_norm_w[None, :], swq, swk, swv, sinks.astype(F32),
                        batch, seq)

    wr_pad = jnp.zeros((D_MODEL, LANES), F32).at[:, 0:N_EXPERTS].set(w_router).astype(BF16)
    br_pad = jnp.zeros((1, LANES), F32).at[0, 0:N_EXPERTS].set(b_router)
    x1, h2, top_idx, gates, rank, cnt = _post(xf, o_dn, o_sw, mod, w_out.astype(BF16),
                                              norm2_w[None, :], wr_pad, br_pad, seq)

    counts = cnt[0, 0:N_EXPERTS]
    padded = (counts + MOE_ROWS - 1) // MOE_ROWS * MOE_ROWS
    pend = jnp.cumsum(padded)
    pstart = pend - padded
    n_blocks = (t * TOP_K) // MOE_ROWS + N_EXPERTS
    n_pad = n_blocks * MOE_ROWS
    n_valid = (pend[-1] // MOE_ROWS).astype(I32)
    blk = jnp.minimum(jnp.arange(n_blocks, dtype=I32), n_valid - 1)
    block_e = jnp.minimum(jnp.sum(pend[None, :] <= (blk * MOE_ROWS)[:, None], axis=1),
                          N_EXPERTS - 1).astype(I32)
    experts = jnp.arange(N_EXPERTS, dtype=I32)[:, None, None]
    dest = (jnp.sum(jnp.where(top_idx[None, 0:TOP_K] == experts, pstart[:, None, None], 0), axis=0)
            + rank[0:TOP_K]).astype(I32)
    fill = jnp.stack([jnp.append(pstart + counts, pend[-1]),
                      jnp.append(pend, n_pad)]).astype(I32)

    eids = jnp.arange(N_EXPERTS, dtype=I32)
    later = (eids[None, :] > eids[:, None]) & (counts[None, :] > 0)
    next_e = jnp.min(jnp.where(later, eids[None, :], N_EXPERTS), axis=1)
    next_e = jnp.where(next_e == N_EXPERTS, eids, next_e).astype(I32)

    row_tok = _invert(fill, dest.reshape(TOP_K * t), n_pad)
    ys = _moe(block_e, n_valid.reshape(1), next_e, row_tok, h2, w_up, b_up[:, None, :], w_down,
              b_down[:, None, :])
    n_tiles = t // TM_COMB
    dest_tiles = dest.reshape(TOP_K, n_tiles, TM_COMB).transpose(1, 0, 2).reshape(
        n_tiles, 1, TOP_K * TM_COMB)
    out = _combine(dest_tiles, ys, x1, gates, mod, seq)
    return out.reshape(batch, seq, D_MODEL)


def kernel(x, c, w_ada, b_ada, norm1_w, w_in, conv_w, a_log, dt_bias, dn_norm_w, q_norm_w,
           k_norm_w, sinks, w_out, norm2_w, w_router, b_router, w_up, b_up, w_down, b_down):
    out_dtype = x.dtype
    batch = x.shape[0]
    depth = w_ada.shape[0]
    c_pad = jnp.zeros((SUBLANES, D_MODEL), F32).at[0:batch].set(c.astype(F32))
    for l in range(depth):
        mod = _ada(c_pad, w_ada[l], b_ada[l][None, :])[0:batch].reshape(batch, 6, D_MODEL)
        x = _layer(x, mod, norm1_w[l], w_in[l], conv_w[l], a_log[l], dt_bias[l], dn_norm_w[l],
                   q_norm_w[l], k_norm_w[l], sinks[l], w_out[l], norm2_w[l], w_router[l],
                   b_router[l], w_up[l], b_up[l], w_down[l], b_down[l])
    return x.astype(out_dtype)
```

```python
import functools

import jax
import jax.numpy as jnp
from jax import lax
from jax.experimental import pallas as pl
from jax.experimental.pallas import tpu as pltpu

F32 = jnp.float32
BF16 = jnp.bfloat16
I32 = jnp.int32

D_MODEL = 1024
DN_HEADS = 4
DN_HEAD_DIM = 128
DN_WIDTH = DN_HEADS * DN_HEAD_DIM
CONV_WIDTH = 4
CHUNK = 64
SWA_HEADS = 8
SWA_KV_HEADS = 2
SWA_GROUP = SWA_HEADS // SWA_KV_HEADS
SWA_HEAD_DIM = 64
SWA_WIDTH = SWA_HEADS * SWA_HEAD_DIM
SWA_KV_WIDTH = SWA_KV_HEADS * SWA_HEAD_DIM
WINDOW = 128
N_EXPERTS = 32
TOP_K = 4
D_FF = D_MODEL
SWIGLU_ALPHA = 1.702
SWIGLU_LIMIT = 7.0
EPS = 1e-6
NEG = -1e30

LANES = 128
SUBLANES = 8
VMEM_LIMIT = 56 * 1024 * 1024

C_QKV = 0
C_Z = C_QKV + 3 * DN_WIDTH
C_SWQ = C_Z + DN_WIDTH
C_SWK = C_SWQ + SWA_WIDTH
C_SWV = C_SWK + SWA_KV_WIDTH
C_AB = C_SWV + SWA_KV_WIDTH
IN_PAD = C_AB + LANES

TM_IN = 512
DN_ROWS = 256
TM_POST = 512
MOE_ROWS = 256
TM_COMB = 128


def _silu(x):
    return x * jax.nn.sigmoid(x)


def _mm(a, b):
    return jnp.dot(a.astype(BF16), b.astype(BF16), preferred_element_type=F32)


def _mm_nt(a, b):
    return lax.dot_general(a.astype(BF16), b.astype(BF16), (((1,), (1,)), ((), ())),
                           preferred_element_type=F32)


def _mm_tn(a, b):
    return lax.dot_general(a.astype(BF16), b.astype(BF16), (((0,), (0,)), ((), ())),
                           preferred_element_type=F32)


def _load_token_tiles(ref, row0, rows):
    return jnp.concatenate(
        [ref[pl.ds(row0 * SUBLANES + s, rows, stride=SUBLANES), :] for s in range(D_MODEL // LANES)],
        axis=1)


def _store_token_tiles(ref, val):
    rows = val.shape[0]
    for s in range(D_MODEL // LANES):
        ref[pl.ds(s, rows, stride=SUBLANES), :] = val[:, s * LANES:(s + 1) * LANES]


def _token_copy(src_ref, s, dst_ref, d, sem):
    return pltpu.make_async_copy(
        src_ref.at[pl.ds(pl.multiple_of(s * SUBLANES, SUBLANES), SUBLANES), :],
        dst_ref.at[pl.ds(pl.multiple_of(d * SUBLANES, SUBLANES), SUBLANES), :], sem)


def _ada_kernel(c_ref, w_ref, b_ref, o_ref):
    c = c_ref[...]
    o_ref[...] = jnp.dot(_silu(c), w_ref[...], precision=lax.Precision.HIGHEST,
                         preferred_element_type=F32) + b_ref[...]


def _ada(c_pad, w_ada, b_ada):
    n = w_ada.shape[1]
    return pl.pallas_call(
        _ada_kernel,
        grid=(n // D_MODEL,),
        in_specs=[pl.BlockSpec((SUBLANES, D_MODEL), lambda j: (0, 0)),
                  pl.BlockSpec((D_MODEL, D_MODEL), lambda j: (0, j)),
                  pl.BlockSpec((1, D_MODEL), lambda j: (0, j))],
        out_specs=pl.BlockSpec((SUBLANES, D_MODEL), lambda j: (0, j)),
        out_shape=jax.ShapeDtypeStruct((SUBLANES, n), F32),
        name="ada",
    )(c_pad, w_ada, b_ada)


def _rms_heads64(a, w128):
    lane = lax.broadcasted_iota(I32, a.shape, 1)
    lo = lane < SWA_HEAD_DIM
    sq = a * a
    s_lo = jnp.sum(jnp.where(lo, sq, 0.0), axis=-1, keepdims=True)
    s_hi = jnp.sum(jnp.where(lo, 0.0, sq), axis=-1, keepdims=True)
    ms = jnp.where(lo, s_lo, s_hi) * (1.0 / SWA_HEAD_DIM)
    return a * lax.rsqrt(ms + EPS) * w128


def _inproj_kernel(tiles_per_seq, x_ref, mod_ref, n1w_ref, w_ref, convw_ref, gp_ref, qnw_ref,
                   knw_ref, q_ref, k_ref, v_ref, z_ref, gb_ref, swq_ref, swk_ref, swv_ref,
                   conv_scr):
    i = pl.program_id(0)
    tm = x_ref.shape[0]
    x = x_ref[...]
    y = x * lax.rsqrt(jnp.mean(x * x, axis=-1, keepdims=True) + EPS) * n1w_ref[...]
    h = (y * (1.0 + mod_ref[0, 1:2, :]) + mod_ref[0, 0:1, :]).astype(BF16)

    @pl.when(i % tiles_per_seq == 0)
    def _():
        conv_scr[0:SUBLANES, :] = jnp.zeros((SUBLANES, 3 * DN_WIDTH), F32)

    u = jnp.dot(h, w_ref[:, C_QKV:C_Z], preferred_element_type=F32)
    conv_scr[SUBLANES:SUBLANES + tm, :] = u
    cw = convw_ref[...]
    acc = u * cw[3:4, :]
    for j in range(CONV_WIDTH - 1):
        off = SUBLANES - (CONV_WIDTH - 1) + j
        acc = acc + conv_scr[off:off + tm, :] * cw[j:j + 1, :]
    conv_scr[0:SUBLANES, :] = conv_scr[tm:tm + SUBLANES, :]
    qkv = _silu(acc)

    for hh in range(DN_HEADS):
        sl = slice(hh * DN_HEAD_DIM, (hh + 1) * DN_HEAD_DIM)
        qh = qkv[:, hh * DN_HEAD_DIM:(hh + 1) * DN_HEAD_DIM]
        kh = qkv[:, DN_WIDTH + hh * DN_HEAD_DIM:DN_WIDTH + (hh + 1) * DN_HEAD_DIM]
        q_ref[:, sl] = qh * lax.rsqrt(jnp.sum(qh * qh, axis=-1, keepdims=True) + EPS) * (
            DN_HEAD_DIM ** -0.5)
        k_ref[:, sl] = kh * lax.rsqrt(jnp.sum(kh * kh, axis=-1, keepdims=True) + EPS)
    v_ref[...] = qkv[:, 2 * DN_WIDTH:3 * DN_WIDTH]

    z_ref[...] = jnp.dot(h, w_ref[:, C_Z:C_SWQ], preferred_element_type=F32)

    ab = jnp.dot(h, w_ref[:, C_AB:IN_PAD], preferred_element_type=F32)
    lane = lax.broadcasted_iota(I32, ab.shape, 1)
    sp_in = ab + gp_ref[1:2, :]
    softplus = jnp.maximum(sp_in, 0.0) + jnp.log1p(jnp.exp(-jnp.abs(sp_in)))
    gb_ref[...] = jnp.where(lane < DN_HEADS, gp_ref[0:1, :] * softplus, jax.nn.sigmoid(ab))

    sq = jnp.dot(h, w_ref[:, C_SWQ:C_SWK], preferred_element_type=F32)
    for t in range(SWA_WIDTH // LANES):
        sl = slice(t * LANES, (t + 1) * LANES)
        swq_ref[:, sl] = (_rms_heads64(sq[:, sl], qnw_ref[...]) * (SWA_HEAD_DIM ** -0.5)).astype(BF16)
    sk = jnp.dot(h, w_ref[:, C_SWK:C_SWV], preferred_element_type=F32)
    swk_ref[...] = _rms_heads64(sk, knw_ref[...]).astype(BF16)
    swv_ref[...] = jnp.dot(h, w_ref[:, C_SWV:C_AB], preferred_element_type=F32).astype(BF16)


def _inproj(xf, mod, n1w, w_cat, conv_w, gparams, qnw128, knw128, seq):
    t = xf.shape[0]
    tiles_per_seq = seq // TM_IN
    row = lambda i: (i, 0)
    const = lambda i: (0, 0)
    outs = [jax.ShapeDtypeStruct((t, DN_WIDTH), F32)] * 4 + [
        jax.ShapeDtypeStruct((t, LANES), F32),
        jax.ShapeDtypeStruct((t, SWA_WIDTH), BF16),
        jax.ShapeDtypeStruct((t, SWA_KV_WIDTH), BF16),
        jax.ShapeDtypeStruct((t, SWA_KV_WIDTH), BF16)]
    out_specs = [pl.BlockSpec((TM_IN, DN_WIDTH), row)] * 4 + [
        pl.BlockSpec((TM_IN, LANES), row),
        pl.BlockSpec((TM_IN, SWA_WIDTH), row),
        pl.BlockSpec((TM_IN, SWA_KV_WIDTH), row),
        pl.BlockSpec((TM_IN, SWA_KV_WIDTH), row)]
    return pl.pallas_call(
        functools.partial(_inproj_kernel, tiles_per_seq),
        grid=(t // TM_IN,),
        in_specs=[pl.BlockSpec((TM_IN, D_MODEL), row),
                  pl.BlockSpec((1, 6, D_MODEL), lambda i: (i // tiles_per_seq, 0, 0)),
                  pl.BlockSpec((1, D_MODEL), const),
                  pl.BlockSpec((D_MODEL, IN_PAD), const),
                  pl.BlockSpec((CONV_WIDTH, 3 * DN_WIDTH), const),
                  pl.BlockSpec((2, LANES), const),
                  pl.BlockSpec((1, LANES), const),
                  pl.BlockSpec((1, LANES), const)],
        out_specs=out_specs,
        out_shape=outs,
        scratch_shapes=[pltpu.VMEM((TM_IN + SUBLANES, 3 * DN_WIDTH), F32)],
        compiler_params=pltpu.CompilerParams(dimension_semantics=("arbitrary",),
                                             vmem_limit_bytes=VMEM_LIMIT),
        name="inproj",
    )(xf, mod, n1w, w_cat, conv_w, gparams, qnw128, knw128)


def _dn_block(j, q_ref, k_ref, v_ref, z_ref, gb_ref, nw_ref, o_ref,
              s_scr, u_scr, w_scr, qd_scr, kd_scr, a_scr):
    rows = q_ref.shape[0]
    n_chunks = rows // CHUNK
    heads = range(DN_HEADS)

    @pl.when(j == 0)
    def _():
        s_scr[...] = jnp.zeros(s_scr.shape, F32)

    r = lax.broadcasted_iota(I32, (rows, rows), 0)
    c = lax.broadcasted_iota(I32, (rows, rows), 1)
    same = (r // CHUNK) == (c // CHUNK)
    incl = same & (c <= r)
    strict = same & (c < r)
    gb = gb_ref[...]
    gc = jnp.dot(jnp.where(incl, 1.0, 0.0), gb, precision=lax.Precision.HIGHEST,
                 preferred_element_type=F32)
    gl = jnp.dot(jnp.where(same, 1.0, 0.0), gb, precision=lax.Precision.HIGHEST,
                 preferred_element_type=F32)
    gct = gc.T

    rmat = []
    rhs = []
    for h in heads:
        ls = slice(h * DN_HEAD_DIM, (h + 1) * DN_HEAD_DIM)
        g_col = gc[:, h:h + 1]
        beta = gb[:, DN_HEADS + h:DN_HEADS + h + 1]
        decay = jnp.where(incl, jnp.exp(jnp.where(incl, g_col - gct[h:h + 1, :], 0.0)), 0.0)
        qq = q_ref[:, ls]
        kk = k_ref[:, ls]
        kb = kk * beta
        eg = jnp.exp(g_col)
        rmat.append(-jnp.where(strict, _mm_nt(kb, kk) * decay, 0.0))
        rhs.append(jnp.concatenate([v_ref[:, ls] * beta, kb * eg], axis=1))
        a_scr[h] = jnp.where(incl, _mm_nt(qq, kk) * decay, 0.0)
        qd_scr[:, ls] = qq * eg
        kd_scr[:, ls] = kk * jnp.exp(gl[:, h:h + 1] - g_col)

    xm = list(rmat)
    pm = list(rmat)
    for it in range(6):
        xm = [xm[h] + _mm(pm[h], xm[h]) for h in heads]
        if it < 5:
            pm = [_mm(pm[h], pm[h]) for h in heads]
    for h in heads:
        ls = slice(h * DN_HEAD_DIM, (h + 1) * DN_HEAD_DIM)
        sol = rhs[h] + _mm(xm[h], rhs[h])
        u_scr[:, ls] = sol[:, 0:DN_HEAD_DIM]
        w_scr[:, ls] = sol[:, DN_HEAD_DIM:2 * DN_HEAD_DIM]

    nw = nw_ref[...]
    lsl = [slice(h * DN_HEAD_DIM, (h + 1) * DN_HEAD_DIM) for h in heads]
    for ci in range(n_chunks):
        rs = slice(ci * CHUNK, (ci + 1) * CHUNK)
        dvals = jnp.exp(gl[ci * CHUNK:ci * CHUNK + 1, :])
        st = [s_scr[h] for h in heads]
        ws = [_mm(w_scr[rs, lsl[h]], st[h]) for h in heads]
        qs = [_mm(qd_scr[rs, lsl[h]], st[h]) for h in heads]
        vn = [u_scr[rs, lsl[h]] - ws[h] for h in heads]
        av = [_mm(a_scr[h, rs, rs], vn[h]) for h in heads]
        kv = [_mm_tn(kd_scr[rs, lsl[h]], vn[h]) for h in heads]
        for h in heads:
            s_scr[h] = st[h] * dvals[:, h:h + 1] + kv[h]
            o = qs[h] + av[h]
            on = o * lax.rsqrt(jnp.mean(o * o, axis=-1, keepdims=True) + EPS) * nw
            o_ref[rs, lsl[h]] = (on * _silu(z_ref[rs, lsl[h]])).astype(o_ref.dtype)


def _swa_window(sink_ref, q, kcat, vcat, first):
    w = WINDOW
    rows = SWA_GROUP * w
    ri = lax.broadcasted_iota(I32, (rows, 2 * w), 0)
    kj = lax.broadcasted_iota(I32, (rows, 2 * w), 1)
    grp = ri // w
    rel = (ri - grp * w) + w - kj
    valid = (rel >= 0) & (rel < w)
    if first is not None:
        valid = valid & ((kj >= w) | jnp.logical_not(first))
    relf = rel.astype(F32)
    grp_col = lax.broadcasted_iota(I32, (rows, 1), 0) // w
    outs = []
    for kh in range(SWA_KV_HEADS):
        slope_col = jnp.zeros((rows, 1), F32)
        sink_col = jnp.zeros((rows, 1), F32)
        for g in range(SWA_GROUP):
            hd = kh * SWA_GROUP + g
            slope_col = jnp.where(grp_col == g, 2.0 ** (-8.0 * (hd + 1) / SWA_HEADS), slope_col)
            sink_col = jnp.where(grp_col == g, sink_ref[hd], sink_col)
        qs = jnp.concatenate(
            [q[:, (kh * SWA_GROUP + g) * SWA_HEAD_DIM:(kh * SWA_GROUP + g + 1) * SWA_HEAD_DIM]
             for g in range(SWA_GROUP)], axis=0)
        kk = kcat[:, kh * SWA_HEAD_DIM:(kh + 1) * SWA_HEAD_DIM]
        vv = vcat[:, kh * SWA_HEAD_DIM:(kh + 1) * SWA_HEAD_DIM]
        s = lax.dot_general(qs, kk, (((1,), (1,)), ((), ())), preferred_element_type=F32)
        logits = jnp.where(valid, s - slope_col * relf, NEG)
        m = jnp.maximum(jnp.max(logits, axis=-1, keepdims=True), sink_col)
        p = jnp.exp(logits - m)
        denom = jnp.sum(p, axis=-1, keepdims=True) + jnp.exp(sink_col - m)
        o = jnp.dot(p.astype(BF16), vv, preferred_element_type=F32) / denom
        for g in range(SWA_GROUP):
            outs.append(o[g * w:(g + 1) * w, :])
    return [jnp.concatenate([outs[2 * t], outs[2 * t + 1]], axis=-1)
            for t in range(SWA_WIDTH // LANES)]


def _mixer_kernel(sink_ref, q_ref, k_ref, v_ref, z_ref, gb_ref, nw_ref,
                  swq_ref, kp_ref, kc_ref, vp_ref, vc_ref, odn_ref, osw_ref,
                  s_scr, u_scr, w_scr, qd_scr, kd_scr, a_scr):
    j = pl.program_id(1)
    _dn_block(j, q_ref, k_ref, v_ref, z_ref, gb_ref, nw_ref, odn_ref,
              s_scr, u_scr, w_scr, qd_scr, kd_scr, a_scr)
    w = WINDOW
    kcur = kc_ref[...]
    vcur = vc_ref[...]
    for s in range(DN_ROWS // w):
        rows = slice(s * w, (s + 1) * w)
        if s == 0:
            kcat = jnp.concatenate([kp_ref[...], kcur[0:w, :]], axis=0)
            vcat = jnp.concatenate([vp_ref[...], vcur[0:w, :]], axis=0)
            first = j == 0
        else:
            kcat = kcur[(s - 1) * w:(s + 1) * w, :]
            vcat = vcur[(s - 1) * w:(s + 1) * w, :]
            first = None
        tiles = _swa_window(sink_ref, swq_ref[rows, :], kcat, vcat, first)
        for t, tile in enumerate(tiles):
            osw_ref[rows, t * LANES:(t + 1) * LANES] = tile.astype(osw_ref.dtype)


def _mixer(q, k, v, z, gb, nw, swq, swk, swv, sinks, batch, seq):
    t = q.shape[0]
    steps = seq // DN_ROWS
    per_step = DN_ROWS // WINDOW
    row = lambda b, j, s: (b * steps + j, 0)
    prev = lambda b, j, s: (b * steps * per_step + jnp.maximum(j * per_step - 1, 0), 0)
    blk = pl.BlockSpec((DN_ROWS, DN_WIDTH), row)
    kv_cur = pl.BlockSpec((DN_ROWS, SWA_KV_WIDTH), row)
    kv_prev = pl.BlockSpec((WINDOW, SWA_KV_WIDTH), prev)
    return pl.pallas_call(
        _mixer_kernel,
        grid_spec=pltpu.PrefetchScalarGridSpec(
            num_scalar_prefetch=1,
            grid=(batch, steps),
            in_specs=[blk, blk, blk, blk,
                      pl.BlockSpec((DN_ROWS, LANES), row),
                      pl.BlockSpec((1, DN_HEAD_DIM), lambda b, j, s: (0, 0)),
                      pl.BlockSpec((DN_ROWS, SWA_WIDTH), row),
                      kv_prev, kv_cur, kv_prev, kv_cur],
            out_specs=[blk, pl.BlockSpec((DN_ROWS, SWA_WIDTH), row)],
            scratch_shapes=[pltpu.VMEM((DN_HEADS, DN_HEAD_DIM, DN_HEAD_DIM), F32),
                            pltpu.VMEM((DN_ROWS, DN_WIDTH), F32),
                            pltpu.VMEM((DN_ROWS, DN_WIDTH), F32),
                            pltpu.VMEM((DN_ROWS, DN_WIDTH), F32),
                            pltpu.VMEM((DN_ROWS, DN_WIDTH), F32),
                            pltpu.VMEM((DN_HEADS, DN_ROWS, DN_ROWS), F32)]),
        out_shape=[jax.ShapeDtypeStruct((t, DN_WIDTH), BF16),
                   jax.ShapeDtypeStruct((t, SWA_WIDTH), BF16)],
        compiler_params=pltpu.CompilerParams(dimension_semantics=("arbitrary", "arbitrary"),
                                             vmem_limit_bytes=VMEM_LIMIT),
        name="mixer",
    )(sinks, q, k, v, z, gb, nw, swq, swk, swk, swv, swv)


def _post_kernel(x_ref, odn_ref, osw_ref, mod_ref, wo_ref, n2w_ref, wr_ref, br_ref,
                 x1_ref, h2_ref, idx_ref, gate_ref, rank_ref, cnt_ref, run_scr):
    i = pl.program_id(0)
    tm = x_ref.shape[0]

    @pl.when(i == 0)
    def _():
        run_scr[...] = jnp.zeros(run_scr.shape, F32)

    y = (jnp.dot(odn_ref[...], wo_ref[0:DN_WIDTH, :], preferred_element_type=F32)
         + jnp.dot(osw_ref[...], wo_ref[DN_WIDTH:, :], preferred_element_type=F32))
    x1 = x_ref[...] + mod_ref[0, 2:3, :] * y
    x1_ref[...] = x1
    hn = x1 * lax.rsqrt(jnp.mean(x1 * x1, axis=-1, keepdims=True) + EPS) * n2w_ref[...]
    h2 = hn * (1.0 + mod_ref[0, 4:5, :]) + mod_ref[0, 3:4, :]
    _store_token_tiles(h2_ref, h2)

    lane = lax.broadcasted_iota(I32, (tm, LANES), 1)
    lane_f = lane.astype(F32)
    logits = jnp.dot(h2.astype(BF16), wr_ref[...], preferred_element_type=F32) + br_ref[...]
    work = jnp.where(lane < N_EXPERTS, logits, NEG)
    sel = jnp.zeros((tm, LANES), F32)
    idx_out = jnp.zeros((tm, LANES), F32)
    vals = []
    idxs = []
    for kk in range(TOP_K):
        m = jnp.max(work, axis=-1, keepdims=True)
        ik = jnp.min(jnp.where(work == m, lane_f, float(LANES)), axis=-1, keepdims=True)
        hit = lane_f == ik
        work = jnp.where(hit, NEG, work)
        sel = jnp.where(hit, 1.0, sel)
        idx_out = jnp.where(lane == kk, ik, idx_out)
        vals.append(m)
        idxs.append(ik)
    es = [jnp.exp(v - vals[0]) for v in vals]
    tot = es[0] + es[1] + es[2] + es[3]
    gate_out = jnp.zeros((tm, LANES), F32)
    for kk in range(TOP_K):
        gate_out = jnp.where(lane == kk, es[kk] / tot, gate_out)
    gate_ref[...] = gate_out

    r = lax.broadcasted_iota(I32, (tm, tm), 0)
    c = lax.broadcasted_iota(I32, (tm, tm), 1)
    before = jnp.where(c < r, 1.0, 0.0).astype(BF16)
    base = jnp.dot(before, sel.astype(BF16), preferred_element_type=F32) + run_scr[0:1, :]
    rank_out = jnp.zeros((tm, LANES), F32)
    for kk in range(TOP_K):
        rk = jnp.sum(jnp.where(lane_f == idxs[kk], base, 0.0), axis=-1, keepdims=True)
        rank_out = jnp.where(lane == kk, rk, rank_out)
    idx_ref[...] = idx_out.T[0:SUBLANES, :].astype(I32)
    rank_ref[...] = rank_out.T[0:SUBLANES, :].astype(I32)
    run = run_scr[0:1, :] + jnp.sum(sel, axis=0, keepdims=True)
    run_scr[0:1, :] = run
    cnt_ref[...] = jnp.broadcast_to(run, cnt_ref.shape).astype(I32)


def _post(xf, odn, osw, mod, w_out, n2w, wr_pad, br_pad, seq):
    t = xf.shape[0]
    tiles_per_seq = seq // TM_POST
    row = lambda i: (i, 0)
    const = lambda i: (0, 0)
    return pl.pallas_call(
        _post_kernel,
        grid=(t // TM_POST,),
        in_specs=[pl.BlockSpec((TM_POST, D_MODEL), row),
                  pl.BlockSpec((TM_POST, DN_WIDTH), row),
                  pl.BlockSpec((TM_POST, SWA_WIDTH), row),
                  pl.BlockSpec((1, 6, D_MODEL), lambda i: (i // tiles_per_seq, 0, 0)),
                  pl.BlockSpec((D_MODEL, D_MODEL), const),
                  pl.BlockSpec((1, D_MODEL), const),
                  pl.BlockSpec((D_MODEL, LANES), const),
                  pl.BlockSpec((1, LANES), const)],
        out_specs=[pl.BlockSpec((TM_POST, D_MODEL), row),
                   pl.BlockSpec((TM_POST * SUBLANES, LANES), row),
                   pl.BlockSpec((SUBLANES, TM_POST), lambda i: (0, i)),
                   pl.BlockSpec((TM_POST, LANES), row),
                   pl.BlockSpec((SUBLANES, TM_POST), lambda i: (0, i)),
                   pl.BlockSpec((SUBLANES, LANES), const)],
        out_shape=[jax.ShapeDtypeStruct((t, D_MODEL), F32),
                   jax.ShapeDtypeStruct((t * SUBLANES, LANES), F32),
                   jax.ShapeDtypeStruct((SUBLANES, t), I32),
                   jax.ShapeDtypeStruct((t, LANES), F32),
                   jax.ShapeDtypeStruct((SUBLANES, t), I32),
                   jax.ShapeDtypeStruct((SUBLANES, LANES), I32)],
        scratch_shapes=[pltpu.VMEM((SUBLANES, LANES), F32)],
        compiler_params=pltpu.CompilerParams(dimension_semantics=("arbitrary",),
                                             vmem_limit_bytes=VMEM_LIMIT),
        name="post",
    )(xf, odn, osw, mod, w_out, n2w, wr_pad, br_pad)


INVERT_GROUP = 16


def _invert_kernel(fill_ref, dest_ref, rowtok_ref):
    def per_range(e, carry):
        def clear(rw, c2):
            rowtok_ref[rw] = 0
            return c2

        lax.fori_loop(fill_ref[0, e], fill_ref[1, e], clear, 0)
        return carry

    lax.fori_loop(0, fill_ref.shape[1], per_range, 0)

    n_tok = dest_ref.shape[0] // TOP_K
    for kk in range(TOP_K):
        def put(g, carry, kk=kk):
            tok0 = g * INVERT_GROUP
            rows = [dest_ref[kk * n_tok + tok0 + u] for u in range(INVERT_GROUP)]
            for u in range(INVERT_GROUP):
                rowtok_ref[rows[u]] = tok0 + u
            return carry

        lax.fori_loop(0, n_tok // INVERT_GROUP, put, 0)


def _invert(fill, dest_flat, n_pad):
    return pl.pallas_call(
        _invert_kernel,
        in_specs=[pl.BlockSpec(memory_space=pltpu.SMEM),
                  pl.BlockSpec(memory_space=pltpu.SMEM)],
        out_specs=pl.BlockSpec(memory_space=pltpu.SMEM),
        out_shape=jax.ShapeDtypeStruct((n_pad,), I32),
        name="invert",
    )(fill, dest_flat)


GATHER_SLOTS = 3
FF_CHUNKS = 4
FF_COLS = D_FF // FF_CHUNKS


def _moe_kernel(be_ref, nv_ref, nxt_ref, rowtok_ref, h_ref, wu_hbm, bu_ref, wd_hbm, bd_ref, ys_ref,
                wu_f32, wd_f32, wu_bf, wd_bf, xbuf, sem, wsem):
    i = pl.program_id(0)
    slot = i % GATHER_SLOTS
    n_valid = nv_ref[0]

    def weight_copies(e):
        return (pltpu.make_async_copy(wu_hbm.at[e], wu_f32, wsem.at[0]),
                pltpu.make_async_copy(wd_hbm.at[e], wd_f32, wsem.at[1]))

    def gather(blk, s, r0=0, r1=MOE_ROWS):
        for r in range(r0, r1):
            _token_copy(h_ref, rowtok_ref[blk * MOE_ROWS + r], xbuf.at[s], r,
                        sem.at[s]).start(priority=r % 2)

    def gather_wait(s):
        pltpu.make_async_copy(h_ref.at[pl.ds(0, MOE_ROWS * SUBLANES), :],
                              xbuf.at[s, pl.ds(0, MOE_ROWS * SUBLANES), :], sem.at[s]).wait()

    last = n_valid - 1

    @pl.when(i == 0)
    def _():
        for s in range(GATHER_SLOTS):
            xbuf[s, pl.ds(MOE_ROWS * SUBLANES, SUBLANES), :] = jnp.zeros((SUBLANES, LANES), F32)
        for cp in weight_copies(be_ref[0]):
            cp.start()
        gather(0, 0)
        gather(jnp.minimum(1, last), 1)

    @pl.when(i < n_valid)
    def _():
        expert = be_ref[i]
        prev = be_ref[jnp.maximum(i - 1, 0)]

        @pl.when((i == 0) | (expert != prev))
        def _():
            for cp in weight_copies(expert):
                cp.wait()
            wu_bf[...] = wu_f32[...].astype(BF16)
            wd_bf[...] = wd_f32[...].astype(BF16)
            nxt = nxt_ref[expert]

            @pl.when(nxt != expert)
            def _():
                for cp in weight_copies(nxt):
                    cp.start()

        gather_wait(slot)
        xb = _load_token_tiles(xbuf.at[slot], 0, MOE_ROWS).astype(BF16)
        nxt_blk = jnp.minimum(i + 2, last)
        nxt_slot = (i + 2) % GATHER_SLOTS
        group = MOE_ROWS // FF_CHUNKS
        acts = []
        tie = jnp.zeros((1, LANES), F32)
        for c in range(FF_CHUNKS):
            cg = slice(c * FF_COLS, (c + 1) * FF_COLS)
            cl = slice(D_FF + c * FF_COLS, D_FF + (c + 1) * FF_COLS)
            tie2 = jnp.concatenate([tie, tie], axis=1)
            g = jnp.dot(xb, wu_bf[:, cg], preferred_element_type=F32) + (bu_ref[0, :, cg] + tie2)
            l = jnp.dot(xb, wu_bf[:, cl], preferred_element_type=F32) + bu_ref[0, :, cl]
            x_glu = jnp.minimum(g, SWIGLU_LIMIT)
            x_lin = jnp.clip(l, -SWIGLU_LIMIT, SWIGLU_LIMIT)
            acts.append((x_glu * jax.nn.sigmoid(SWIGLU_ALPHA * x_glu) * (x_lin + 1.0)).astype(BF16))
            gather(nxt_blk, nxt_slot, c * group, (c + 1) * group)
            spare = xbuf[nxt_slot, pl.ds(MOE_ROWS * SUBLANES, 1), :]
            bits = pltpu.bitcast(spare, jnp.uint32)
            tie = pltpu.bitcast((bits >> 16) >> 16, F32)
        act = jnp.concatenate(acts, axis=1)
        y = jnp.dot(act, wd_bf[...], preferred_element_type=F32) + (
            bd_ref[0] + jnp.concatenate([tie] * (D_MODEL // LANES), axis=1))
        _store_token_tiles(ys_ref, y)

    @pl.when(i == last)
    def _():
        gather_wait((i + 1) % GATHER_SLOTS)
        gather_wait((i + 2) % GATHER_SLOTS)

    @pl.when(i >= n_valid)
    def _():
        ys_ref[...] = jnp.zeros(ys_ref.shape, F32)


def _moe(block_e, n_valid, next_e, row_tok, h2, w_up, b_up, w_down, b_down):
    n_pad = row_tok.shape[0]
    n_blocks = n_pad // MOE_ROWS
    out_rows = lambda i, be, nv, nx, rt: (i, 0)
    expert = lambda i, be, nv, nx, rt: (be[i], 0, 0)
    return pl.pallas_call(
        _moe_kernel,
        grid_spec=pltpu.PrefetchScalarGridSpec(
            num_scalar_prefetch=4,
            grid=(n_blocks,),
            in_specs=[pl.BlockSpec(memory_space=pl.ANY),
                      pl.BlockSpec(memory_space=pl.ANY),
                      pl.BlockSpec((1, 1, 2 * D_FF), expert),
                      pl.BlockSpec(memory_space=pl.ANY),
                      pl.BlockSpec((1, 1, D_MODEL), expert)],
            out_specs=pl.BlockSpec((MOE_ROWS * SUBLANES, LANES), out_rows),
            scratch_shapes=[pltpu.VMEM((D_MODEL, 2 * D_FF), F32),
                            pltpu.VMEM((D_FF, D_MODEL), F32),
                            pltpu.VMEM((D_MODEL, 2 * D_FF), BF16),
                            pltpu.VMEM((D_FF, D_MODEL), BF16),
                            pltpu.VMEM((GATHER_SLOTS, (MOE_ROWS + 1) * SUBLANES, LANES), F32),
                            pltpu.SemaphoreType.DMA((GATHER_SLOTS,)),
                            pltpu.SemaphoreType.DMA((2,))]),
        out_shape=jax.ShapeDtypeStruct((n_pad * SUBLANES, LANES), F32),
        compiler_params=pltpu.CompilerParams(dimension_semantics=("arbitrary",),
                                             vmem_limit_bytes=VMEM_LIMIT),
        name="moe",
    )(block_e, n_valid, next_e, row_tok, h2, w_up, b_up, w_down, b_down)


def _combine_kernel(dcur_ref, dnext_ref, ys_ref, x1_ref, gate_ref, mod_ref, o_ref, buf, sem):
    i = pl.program_id(0)
    tm = x1_ref.shape[0]
    slot = i % 2

    def issue(dref, s):
        def body(tk, carry):
            for kk in range(TOP_K):
                _token_copy(ys_ref, dref[0, 0, kk * tm + tk], buf.at[s], kk * tm + tk,
                            sem.at[s]).start(priority=kk % 2)
            return carry

        lax.fori_loop(0, tm, body, 0, unroll=4)

    @pl.when(i == 0)
    def _():
        issue(dcur_ref, 0)

    @pl.when(i + 1 < pl.num_programs(0))
    def _():
        issue(dnext_ref, 1 - slot)

    pltpu.make_async_copy(ys_ref.at[pl.ds(0, TOP_K * tm * SUBLANES), :], buf.at[slot],
                          sem.at[slot]).wait()

    gates = gate_ref[...]
    acc = jnp.zeros((tm, D_MODEL), F32)
    for kk in range(TOP_K):
        acc = acc + gates[:, kk:kk + 1] * _load_token_tiles(buf.at[slot], kk * tm, tm)
    o_ref[...] = x1_ref[...] + mod_ref[0, 5:6, :] * acc


def _combine(dest3, ys, x1, gates, mod, seq):
    t = x1.shape[0]
    tiles_per_seq = seq // TM_COMB
    n_tiles = t // TM_COMB
    row = lambda i: (i, 0)
    return pl.pallas_call(
        _combine_kernel,
        grid=(n_tiles,),
        in_specs=[pl.BlockSpec((1, 1, TM_COMB * TOP_K), lambda i: (i, 0, 0),
                               memory_space=pltpu.SMEM),
                  pl.BlockSpec((1, 1, TM_COMB * TOP_K),
                               lambda i: (jnp.minimum(i + 1, n_tiles - 1), 0, 0),
                               memory_space=pltpu.SMEM),
                  pl.BlockSpec(memory_space=pl.ANY),
                  pl.BlockSpec((TM_COMB, D_MODEL), row),
                  pl.BlockSpec((TM_COMB, LANES), row),
                  pl.BlockSpec((1, 6, D_MODEL), lambda i: (i // tiles_per_seq, 0, 0))],
        out_specs=pl.BlockSpec((TM_COMB, D_MODEL), row),
        out_shape=jax.ShapeDtypeStruct((t, D_MODEL), F32),
        scratch_shapes=[pltpu.VMEM((2, TOP_K * TM_COMB * SUBLANES, LANES), F32),
                        pltpu.SemaphoreType.DMA((2,))],
        compiler_params=pltpu.CompilerParams(dimension_semantics=("arbitrary",),
                                             vmem_limit_bytes=VMEM_LIMIT),
        name="combine",
    )(dest3, dest3, ys, x1, gates, mod)


def _layer(x, mod, norm1_w, w_in, conv_w, a_log, dt_bias, dn_norm_w, q_norm_w, k_norm_w, sinks,
           w_out, norm2_w, w_router, b_router, w_up, b_up, w_down, b_down):
    batch, seq, _ = x.shape
    t = batch * seq
    xf = x.reshape(t, D_MODEL)

    o_qkv, o_z, o_a, o_b, o_q, o_k, o_v = 0, 1536, 2048, 2052, 2056, 2568, 2696
    w_ab = jnp.zeros((D_MODEL, LANES), F32).at[:, 0:2 * DN_HEADS].set(w_in[:, o_a:o_q])
    w_cat = jnp.concatenate([w_in[:, o_qkv:o_z], w_in[:, o_z:o_a], w_in[:, o_q:o_k],
                             w_in[:, o_k:o_v], w_in[:, o_v:], w_ab], axis=1).astype(BF16)
    gparams = jnp.zeros((2, LANES), F32)
    gparams = gparams.at[0, 0:DN_HEADS].set(-jnp.exp(a_log.astype(F32)))
    gparams = gparams.at[1, 0:DN_HEADS].set(dt_bias.astype(F32))
    qnw128 = jnp.tile(q_norm_w.astype(F32), 2)[None, :]
    knw128 = jnp.tile(k_norm_w.astype(F32), 2)[None, :]

    q, k, v, z, gb, swq, swk, swv = _inproj(xf, mod, norm1_w[None, :], w_cat, conv_w, gparams,
                                            qnw128, knw128, seq)
    o_dn, o_sw = _mixer(q, k, v, z, gb, dn_norm_w[None, :], swq, swk, swv, sinks.astype(F32),
                        batch, seq)

    wr_pad = jnp.zeros((D_MODEL, LANES), F32).at[:, 0:N_EXPERTS].set(w_router).astype(BF16)
    br_pad = jnp.zeros((1, LANES), F32).at[0, 0:N_EXPERTS].set(b_router)
    x1, h2, top_idx, gates, rank, cnt = _post(xf, o_dn, o_sw, mod, w_out.astype(BF16),
                                              norm2_w[None, :], wr_pad, br_pad, seq)

    counts = cnt[0, 0:N_EXPERTS]
    padded = (counts + MOE_ROWS - 1) // MOE_ROWS * MOE_ROWS
    pend = jnp.cumsum(padded)
    pstart = pend - padded
    n_blocks = (t * TOP_K) // MOE_ROWS + N_EXPERTS
    n_pad = n_blocks * MOE_ROWS
    n_valid = (pend[-1] // MOE_ROWS).astype(I32)
    blk = jnp.minimum(jnp.arange(n_blocks, dtype=I32), n_valid - 1)
    block_e = jnp.minimum(jnp.sum(pend[None, :] <= (blk * MOE_ROWS)[:, None], axis=1),
                          N_EXPERTS - 1).astype(I32)
    experts = jnp.arange(N_EXPERTS, dtype=I32)[:, None, None]
    dest = (jnp.sum(jnp.where(top_idx[None, 0:TOP_K] == experts, pstart[:, None, None], 0), axis=0)
            + rank[0:TOP_K]).astype(I32)
    fill = jnp.stack([jnp.append(pstart + counts, pend[-1]),
                      jnp.append(pend, n_pad)]).astype(I32)

    eids = jnp.arange(N_EXPERTS, dtype=I32)
    later = (eids[None, :] > eids[:, None]) & (counts[None, :] > 0)
    next_e = jnp.min(jnp.where(later, eids[None, :], N_EXPERTS), axis=1)
    next_e = jnp.where(next_e == N_EXPERTS, eids, next_e).astype(I32)

    row_tok = _invert(fill, dest.reshape(TOP_K * t), n_pad)
    ys = _moe(block_e, n_valid.reshape(1), next_e, row_tok, h2, w_up, b_up[:, None, :], w_down,
              b_down[:, None, :])
    n_tiles = t // TM_COMB
    dest_tiles = dest.reshape(TOP_K, n_tiles, TM_COMB).transpose(1, 0, 2).reshape(
        n_tiles, 1, TOP_K * TM_COMB)
    out = _combine(dest_tiles, ys, x1, gates, mod, seq)
    return out.reshape(batch, seq, D_MODEL)


def kernel(x, c, w_ada, b_ada, norm1_w, w_in, conv_w, a_log, dt_bias, dn_norm_w, q_norm_w,
           k_norm_w, sinks, w_out, norm2_w, w_router, b_router, w_up, b_up, w_down, b_down):
    out_dtype = x.dtype
    batch = x.shape[0]
    depth = w_ada.shape[0]
    c_pad = jnp.zeros((SUBLANES, D_MODEL), F32).at[0:batch].set(c.astype(F32))
    for l in range(depth):
        mod = _ada(c_pad, w_ada[l], b_ada[l][None, :])[0:batch].reshape(batch, 6, D_MODEL)
        x = _layer(x, mod, norm1_w[l], w_in[l], conv_w[l], a_log[l], dt_bias[l], dn_norm_w[l],
                   q_norm_w[l], k_norm_w[l], sinks[l], w_out[l], norm2_w[l], w_router[l],
                   b_router[l], w_up[l], b_up[l], w_down[l], b_down[l])
    return x.astype(out_dtype)
```

```python
import functools

import jax
import jax.numpy as jnp
from jax import lax
from jax.experimental import pallas as pl
from jax.experimental.pallas import tpu as pltpu

F32 = jnp.float32
BF16 = jnp.bfloat16
I32 = jnp.int32

D_MODEL = 1024
DN_HEADS = 4
DN_HEAD_DIM = 128
DN_WIDTH = DN_HEADS * DN_HEAD_DIM
CONV_WIDTH = 4
CHUNK = 64
SWA_HEADS = 8
SWA_KV_HEADS = 2
SWA_GROUP = SWA_HEADS // SWA_KV_HEADS
SWA_HEAD_DIM = 64
SWA_WIDTH = SWA_HEADS * SWA_HEAD_DIM
SWA_KV_WIDTH = SWA_KV_HEADS * SWA_HEAD_DIM
WINDOW = 128
N_EXPERTS = 32
TOP_K = 4
D_FF = D_MODEL
SWIGLU_ALPHA = 1.702
SWIGLU_LIMIT = 7.0
EPS = 1e-6
NEG = -1e30

LANES = 128
SUBLANES = 8
VMEM_LIMIT = 56 * 1024 * 1024

C_QKV = 0
C_Z = C_QKV + 3 * DN_WIDTH
C_SWQ = C_Z + DN_WIDTH
C_SWK = C_SWQ + SWA_WIDTH
C_SWV = C_SWK + SWA_KV_WIDTH
C_AB = C_SWV + SWA_KV_WIDTH
IN_PAD = C_AB + LANES

TM_IN = 512
DN_ROWS = 256
TM_POST = 512
MOE_ROWS = 256
TM_COMB = 128


def _silu(x):
    return x * jax.nn.sigmoid(x)


def _mm(a, b):
    return jnp.dot(a.astype(BF16), b.astype(BF16), preferred_element_type=F32)


def _mm_nt(a, b):
    return lax.dot_general(a.astype(BF16), b.astype(BF16), (((1,), (1,)), ((), ())),
                           preferred_element_type=F32)


def _mm_tn(a, b):
    return lax.dot_general(a.astype(BF16), b.astype(BF16), (((0,), (0,)), ((), ())),
                           preferred_element_type=F32)


def _load_token_tiles(ref, row0, rows):
    return jnp.concatenate(
        [ref[pl.ds(row0 * SUBLANES + s, rows, stride=SUBLANES), :] for s in range(D_MODEL // LANES)],
        axis=1)


def _store_token_tiles(ref, val):
    rows = val.shape[0]
    for s in range(D_MODEL // LANES):
        ref[pl.ds(s, rows, stride=SUBLANES), :] = val[:, s * LANES:(s + 1) * LANES]


def _token_copy(src_ref, s, dst_ref, d, sem):
    return pltpu.make_async_copy(
        src_ref.at[pl.ds(pl.multiple_of(s * SUBLANES, SUBLANES), SUBLANES), :],
        dst_ref.at[pl.ds(pl.multiple_of(d * SUBLANES, SUBLANES), SUBLANES), :], sem)


def _ada_kernel(c_ref, w_ref, b_ref, o_ref):
    c = c_ref[...]
    o_ref[...] = jnp.dot(_silu(c), w_ref[...], precision=lax.Precision.HIGHEST,
                         preferred_element_type=F32) + b_ref[...]


def _ada(c_pad, w_ada, b_ada):
    n = w_ada.shape[1]
    return pl.pallas_call(
        _ada_kernel,
        grid=(n // D_MODEL,),
        in_specs=[pl.BlockSpec((SUBLANES, D_MODEL), lambda j: (0, 0)),
                  pl.BlockSpec((D_MODEL, D_MODEL), lambda j: (0, j)),
                  pl.BlockSpec((1, D_MODEL), lambda j: (0, j))],
        out_specs=pl.BlockSpec((SUBLANES, D_MODEL), lambda j: (0, j)),
        out_shape=jax.ShapeDtypeStruct((SUBLANES, n), F32),
        name="ada",
    )(c_pad, w_ada, b_ada)


def _rms_heads64(a, w128):
    lane = lax.broadcasted_iota(I32, a.shape, 1)
    lo = lane < SWA_HEAD_DIM
    sq = a * a
    s_lo = jnp.sum(jnp.where(lo, sq, 0.0), axis=-1, keepdims=True)
    s_hi = jnp.sum(jnp.where(lo, 0.0, sq), axis=-1, keepdims=True)
    ms = jnp.where(lo, s_lo, s_hi) * (1.0 / SWA_HEAD_DIM)
    return a * lax.rsqrt(ms + EPS) * w128


def _inproj_kernel(tiles_per_seq, x_ref, mod_ref, n1w_ref, w_ref, convw_ref, gp_ref, qnw_ref,
                   knw_ref, q_ref, k_ref, v_ref, z_ref, gb_ref, swq_ref, swk_ref, swv_ref,
                   conv_scr):
    i = pl.program_id(0)
    tm = x_ref.shape[0]
    x = x_ref[...]
    y = x * lax.rsqrt(jnp.mean(x * x, axis=-1, keepdims=True) + EPS) * n1w_ref[...]
    h = (y * (1.0 + mod_ref[0, 1:2, :]) + mod_ref[0, 0:1, :]).astype(BF16)

    @pl.when(i % tiles_per_seq == 0)
    def _():
        conv_scr[0:SUBLANES, :] = jnp.zeros((SUBLANES, 3 * DN_WIDTH), F32)

    u = jnp.dot(h, w_ref[:, C_QKV:C_Z], preferred_element_type=F32)
    conv_scr[SUBLANES:SUBLANES + tm, :] = u
    cw = convw_ref[...]
    acc = u * cw[3:4, :]
    for j in range(CONV_WIDTH - 1):
        off = SUBLANES - (CONV_WIDTH - 1) + j
        acc = acc + conv_scr[off:off + tm, :] * cw[j:j + 1, :]
    conv_scr[0:SUBLANES, :] = conv_scr[tm:tm + SUBLANES, :]
    qkv = _silu(acc)

    for hh in range(DN_HEADS):
        sl = slice(hh * DN_HEAD_DIM, (hh + 1) * DN_HEAD_DIM)
        qh = qkv[:, hh * DN_HEAD_DIM:(hh + 1) * DN_HEAD_DIM]
        kh = qkv[:, DN_WIDTH + hh * DN_HEAD_DIM:DN_WIDTH + (hh + 1) * DN_HEAD_DIM]
        q_ref[:, sl] = qh * lax.rsqrt(jnp.sum(qh * qh, axis=-1, keepdims=True) + EPS) * (
            DN_HEAD_DIM ** -0.5)
        k_ref[:, sl] = kh * lax.rsqrt(jnp.sum(kh * kh, axis=-1, keepdims=True) + EPS)
    v_ref[...] = qkv[:, 2 * DN_WIDTH:3 * DN_WIDTH]

    z_ref[...] = jnp.dot(h, w_ref[:, C_Z:C_SWQ], preferred_element_type=F32)

    ab = jnp.dot(h, w_ref[:, C_AB:IN_PAD], preferred_element_type=F32)
    lane = lax.broadcasted_iota(I32, ab.shape, 1)
    sp_in = ab + gp_ref[1:2, :]
    softplus = jnp.maximum(sp_in, 0.0) + jnp.log1p(jnp.exp(-jnp.abs(sp_in)))
    gb_ref[...] = jnp.where(lane < DN_HEADS, gp_ref[0:1, :] * softplus, jax.nn.sigmoid(ab))

    sq = jnp.dot(h, w_ref[:, C_SWQ:C_SWK], preferred_element_type=F32)
    for t in range(SWA_WIDTH // LANES):
        sl = slice(t * LANES, (t + 1) * LANES)
        swq_ref[:, sl] = (_rms_heads64(sq[:, sl], qnw_ref[...]) * (SWA_HEAD_DIM ** -0.5)).astype(BF16)
    sk = jnp.dot(h, w_ref[:, C_SWK:C_SWV], preferred_element_type=F32)
    swk_ref[...] = _rms_heads64(sk, knw_ref[...]).astype(BF16)
    swv_ref[...] = jnp.dot(h, w_ref[:, C_SWV:C_AB], preferred_element_type=F32).astype(BF16)


def _inproj(xf, mod, n1w, w_cat, conv_w, gparams, qnw128, knw128, seq):
    t = xf.shape[0]
    tiles_per_seq = seq // TM_IN
    row = lambda i: (i, 0)
    const = lambda i: (0, 0)
    outs = [jax.ShapeDtypeStruct((t, DN_WIDTH), F32)] * 4 + [
        jax.ShapeDtypeStruct((t, LANES), F32),
        jax.ShapeDtypeStruct((t, SWA_WIDTH), BF16),
        jax.ShapeDtypeStruct((t, SWA_KV_WIDTH), BF16),
        jax.ShapeDtypeStruct((t, SWA_KV_WIDTH), BF16)]
    out_specs = [pl.BlockSpec((TM_IN, DN_WIDTH), row)] * 4 + [
        pl.BlockSpec((TM_IN, LANES), row),
        pl.BlockSpec((TM_IN, SWA_WIDTH), row),
        pl.BlockSpec((TM_IN, SWA_KV_WIDTH), row),
        pl.BlockSpec((TM_IN, SWA_KV_WIDTH), row)]
    return pl.pallas_call(
        functools.partial(_inproj_kernel, tiles_per_seq),
        grid=(t // TM_IN,),
        in_specs=[pl.BlockSpec((TM_IN, D_MODEL), row),
                  pl.BlockSpec((1, 6, D_MODEL), lambda i: (i // tiles_per_seq, 0, 0)),
                  pl.BlockSpec((1, D_MODEL), const),
                  pl.BlockSpec((D_MODEL, IN_PAD), const),
                  pl.BlockSpec((CONV_WIDTH, 3 * DN_WIDTH), const),
                  pl.BlockSpec((2, LANES), const),
                  pl.BlockSpec((1, LANES), const),
                  pl.BlockSpec((1, LANES), const)],
        out_specs=out_specs,
        out_shape=outs,
        scratch_shapes=[pltpu.VMEM((TM_IN + SUBLANES, 3 * DN_WIDTH), F32)],
        compiler_params=pltpu.CompilerParams(dimension_semantics=("arbitrary",),
                                             vmem_limit_bytes=VMEM_LIMIT),
        name="inproj",
    )(xf, mod, n1w, w_cat, conv_w, gparams, qnw128, knw128)


def _dn_block(j, q_ref, k_ref, v_ref, z_ref, gb_ref, nw_ref, o_ref,
              s_scr, u_scr, w_scr, qd_scr, kd_scr, a_scr):
    rows = q_ref.shape[0]
    n_chunks = rows // CHUNK
    heads = range(DN_HEADS)

    @pl.when(j == 0)
    def _():
        s_scr[...] = jnp.zeros(s_scr.shape, F32)

    r = lax.broadcasted_iota(I32, (rows, rows), 0)
    c = lax.broadcasted_iota(I32, (rows, rows), 1)
    same = (r // CHUNK) == (c // CHUNK)
    incl = same & (c <= r)
    strict = same & (c < r)
    gb = gb_ref[...]
    gc = jnp.dot(jnp.where(incl, 1.0, 0.0), gb, precision=lax.Precision.HIGHEST,
                 preferred_element_type=F32)
    gl = jnp.dot(jnp.where(same, 1.0, 0.0), gb, precision=lax.Precision.HIGHEST,
                 preferred_element_type=F32)
    gct = gc.T

    rmat = []
    rhs = []
    for h in heads:
        ls = slice(h * DN_HEAD_DIM, (h + 1) * DN_HEAD_DIM)
        g_col = gc[:, h:h + 1]
        beta = gb[:, DN_HEADS + h:DN_HEADS + h + 1]
        decay = jnp.where(incl, jnp.exp(jnp.where(incl, g_col - gct[h:h + 1, :], 0.0)), 0.0)
        qq = q_ref[:, ls]
        kk = k_ref[:, ls]
        kb = kk * beta
        eg = jnp.exp(g_col)
        rmat.append(-jnp.where(strict, _mm_nt(kb, kk) * decay, 0.0))
        rhs.append(jnp.concatenate([v_ref[:, ls] * beta, kb * eg], axis=1))
        a_scr[h] = jnp.where(incl, _mm_nt(qq, kk) * decay, 0.0)
        qd_scr[:, ls] = qq * eg
        kd_scr[:, ls] = kk * jnp.exp(gl[:, h:h + 1] - g_col)

    xm = list(rmat)
    pm = list(rmat)
    for it in range(6):
        xm = [xm[h] + _mm(pm[h], xm[h]) for h in heads]
        if it < 5:
            pm = [_mm(pm[h], pm[h]) for h in heads]
    for h in heads:
        ls = slice(h * DN_HEAD_DIM, (h + 1) * DN_HEAD_DIM)
        sol = rhs[h] + _mm(xm[h], rhs[h])
        u_scr[:, ls] = sol[:, 0:DN_HEAD_DIM]
        w_scr[:, ls] = sol[:, DN_HEAD_DIM:2 * DN_HEAD_DIM]

    nw = nw_ref[...]
    lsl = [slice(h * DN_HEAD_DIM, (h + 1) * DN_HEAD_DIM) for h in heads]
    for ci in range(n_chunks):
        rs = slice(ci * CHUNK, (ci + 1) * CHUNK)
        dvals = jnp.exp(gl[ci * CHUNK:ci * CHUNK + 1, :])
        st = [s_scr[h] for h in heads]
        ws = [_mm(w_scr[rs, lsl[h]], st[h]) for h in heads]
        qs = [_mm(qd_scr[rs, lsl[h]], st[h]) for h in heads]
        vn = [u_scr[rs, lsl[h]] - ws[h] for h in heads]
        av = [_mm(a_scr[h, rs, rs], vn[h]) for h in heads]
        kv = [_mm_tn(kd_scr[rs, lsl[h]], vn[h]) for h in heads]
        for h in heads:
            s_scr[h] = st[h] * dvals[:, h:h + 1] + kv[h]
            o = qs[h] + av[h]
            on = o * lax.rsqrt(jnp.mean(o * o, axis=-1, keepdims=True) + EPS) * nw
            o_ref[rs, lsl[h]] = (on * _silu(z_ref[rs, lsl[h]])).astype(o_ref.dtype)


def _swa_window(sink_ref, q, kcat, vcat, first):
    w = WINDOW
    rows = SWA_GROUP * w
    ri = lax.broadcasted_iota(I32, (rows, 2 * w), 0)
    kj = lax.broadcasted_iota(I32, (rows, 2 * w), 1)
    grp = ri // w
    rel = (ri - grp * w) + w - kj
    valid = (rel >= 0) & (rel < w)
    if first is not None:
        valid = valid & ((kj >= w) | jnp.logical_not(first))
    relf = rel.astype(F32)
    grp_col = lax.broadcasted_iota(I32, (rows, 1), 0) // w
    outs = []
    for kh in range(SWA_KV_HEADS):
        slope_col = jnp.zeros((rows, 1), F32)
        sink_col = jnp.zeros((rows, 1), F32)
        for g in range(SWA_GROUP):
            hd = kh * SWA_GROUP + g
            slope_col = jnp.where(grp_col == g, 2.0 ** (-8.0 * (hd + 1) / SWA_HEADS), slope_col)
            sink_col = jnp.where(grp_col == g, sink_ref[hd], sink_col)
        qs = jnp.concatenate(
            [q[:, (kh * SWA_GROUP + g) * SWA_HEAD_DIM:(kh * SWA_GROUP + g + 1) * SWA_HEAD_DIM]
             for g in range(SWA_GROUP)], axis=0)
        kk = kcat[:, kh * SWA_HEAD_DIM:(kh + 1) * SWA_HEAD_DIM]
        vv = vcat[:, kh * SWA_HEAD_DIM:(kh + 1) * SWA_HEAD_DIM]
        s = lax.dot_general(qs, kk, (((1,), (1,)), ((), ())), preferred_element_type=F32)
        logits = jnp.where(valid, s - slope_col * relf, NEG)
        m = jnp.maximum(jnp.max(logits, axis=-1, keepdims=True), sink_col)
        p = jnp.exp(logits - m)
        denom = jnp.sum(p, axis=-1, keepdims=True) + jnp.exp(sink_col - m)
        o = jnp.dot(p.astype(BF16), vv, preferred_element_type=F32) / denom
        for g in range(SWA_GROUP):
            outs.append(o[g * w:(g + 1) * w, :])
    return [jnp.concatenate([outs[2 * t], outs[2 * t + 1]], axis=-1)
            for t in range(SWA_WIDTH // LANES)]


def _mixer_kernel(sink_ref, q_ref, k_ref, v_ref, z_ref, gb_ref, nw_ref,
                  swq_ref, kp_ref, kc_ref, vp_ref, vc_ref, odn_ref, osw_ref,
                  s_scr, u_scr, w_scr, qd_scr, kd_scr, a_scr):
    j = pl.program_id(1)
    _dn_block(j, q_ref, k_ref, v_ref, z_ref, gb_ref, nw_ref, odn_ref,
              s_scr, u_scr, w_scr, qd_scr, kd_scr, a_scr)
    w = WINDOW
    kcur = kc_ref[...]
    vcur = vc_ref[...]
    for s in range(DN_ROWS // w):
        rows = slice(s * w, (s + 1) * w)
        if s == 0:
            kcat = jnp.concatenate([kp_ref[...], kcur[0:w, :]], axis=0)
            vcat = jnp.concatenate([vp_ref[...], vcur[0:w, :]], axis=0)
            first = j == 0
        else:
            kcat = kcur[(s - 1) * w:(s + 1) * w, :]
            vcat = vcur[(s - 1) * w:(s + 1) * w, :]
            first = None
        tiles = _swa_window(sink_ref, swq_ref[rows, :], kcat, vcat, first)
        for t, tile in enumerate(tiles):
            osw_ref[rows, t * LANES:(t + 1) * LANES] = tile.astype(osw_ref.dtype)


def _mixer(q, k, v, z, gb, nw, swq, swk, swv, sinks, batch, seq):
    t = q.shape[0]
    steps = seq // DN_ROWS
    per_step = DN_ROWS // WINDOW
    row = lambda b, j, s: (b * steps + j, 0)
    prev = lambda b, j, s: (b * steps * per_step + jnp.maximum(j * per_step - 1, 0), 0)
    blk = pl.BlockSpec((DN_ROWS, DN_WIDTH), row)
    kv_cur = pl.BlockSpec((DN_ROWS, SWA_KV_WIDTH), row)
    kv_prev = pl.BlockSpec((WINDOW, SWA_KV_WIDTH), prev)
    return pl.pallas_call(
        _mixer_kernel,
        grid_spec=pltpu.PrefetchScalarGridSpec(
            num_scalar_prefetch=1,
            grid=(batch, steps),
            in_specs=[blk, blk, blk, blk,
                      pl.BlockSpec((DN_ROWS, LANES), row),
                      pl.BlockSpec((1, DN_HEAD_DIM), lambda b, j, s: (0, 0)),
                      pl.BlockSpec((DN_ROWS, SWA_WIDTH), row),
                      kv_prev, kv_cur, kv_prev, kv_cur],
            out_specs=[blk, pl.BlockSpec((DN_ROWS, SWA_WIDTH), row)],
            scratch_shapes=[pltpu.VMEM((DN_HEADS, DN_HEAD_DIM, DN_HEAD_DIM), F32),
                            pltpu.VMEM((DN_ROWS, DN_WIDTH), F32),
                            pltpu.VMEM((DN_ROWS, DN_WIDTH), F32),
                            pltpu.VMEM((DN_ROWS, DN_WIDTH), F32),
                            pltpu.VMEM((DN_ROWS, DN_WIDTH), F32),
                            pltpu.VMEM((DN_HEADS, DN_ROWS, DN_ROWS), F32)]),
        out_shape=[jax.ShapeDtypeStruct((t, DN_WIDTH), BF16),
                   jax.ShapeDtypeStruct((t, SWA_WIDTH), BF16)],
        compiler_params=pltpu.CompilerParams(dimension_semantics=("arbitrary", "arbitrary"),
                                             vmem_limit_bytes=VMEM_LIMIT),
        name="mixer",
    )(sinks, q, k, v, z, gb, nw, swq, swk, swk, swv, swv)


def _post_kernel(x_ref, odn_ref, osw_ref, mod_ref, wo_ref, n2w_ref, wr_ref, br_ref,
                 x1_ref, h2_ref, idx_ref, gate_ref, rank_ref, cnt_ref, run_scr):
    i = pl.program_id(0)
    tm = x_ref.shape[0]

    @pl.when(i == 0)
    def _():
        run_scr[...] = jnp.zeros(run_scr.shape, F32)

    y = (jnp.dot(odn_ref[...], wo_ref[0:DN_WIDTH, :], preferred_element_type=F32)
         + jnp.dot(osw_ref[...], wo_ref[DN_WIDTH:, :], preferred_element_type=F32))
    x1 = x_ref[...] + mod_ref[0, 2:3, :] * y
    x1_ref[...] = x1
    hn = x1 * lax.rsqrt(jnp.mean(x1 * x1, axis=-1, keepdims=True) + EPS) * n2w_ref[...]
    h2 = hn * (1.0 + mod_ref[0, 4:5, :]) + mod_ref[0, 3:4, :]
    _store_token_tiles(h2_ref, h2)

    lane = lax.broadcasted_iota(I32, (tm, LANES), 1)
    lane_f = lane.astype(F32)
    logits = jnp.dot(h2.astype(BF16), wr_ref[...], preferred_element_type=F32) + br_ref[...]
    work = jnp.where(lane < N_EXPERTS, logits, NEG)
    sel = jnp.zeros((tm, LANES), F32)
    idx_out = jnp.zeros((tm, LANES), F32)
    vals = []
    idxs = []
    for kk in range(TOP_K):
        m = jnp.max(work, axis=-1, keepdims=True)
        ik = jnp.min(jnp.where(work == m, lane_f, float(LANES)), axis=-1, keepdims=True)
        hit = lane_f == ik
        work = jnp.where(hit, NEG, work)
        sel = jnp.where(hit, 1.0, sel)
        idx_out = jnp.where(lane == kk, ik, idx_out)
        vals.append(m)
        idxs.append(ik)
    es = [jnp.exp(v - vals[0]) for v in vals]
    tot = es[0] + es[1] + es[2] + es[3]
    gate_out = jnp.zeros((tm, LANES), F32)
    for kk in range(TOP_K):
        gate_out = jnp.where(lane == kk, es[kk] / tot, gate_out)
    gate_ref[...] = gate_out

    r = lax.broadcasted_iota(I32, (tm, tm), 0)
    c = lax.broadcasted_iota(I32, (tm, tm), 1)
    before = jnp.where(c < r, 1.0, 0.0).astype(BF16)
    base = jnp.dot(before, sel.astype(BF16), preferred_element_type=F32) + run_scr[0:1, :]
    rank_out = jnp.zeros((tm, LANES), F32)
    for kk in range(TOP_K):
        rk = jnp.sum(jnp.where(lane_f == idxs[kk], base, 0.0), axis=-1, keepdims=True)
        rank_out = jnp.where(lane == kk, rk, rank_out)
    idx_ref[...] = idx_out.T[0:SUBLANES, :].astype(I32)
    rank_ref[...] = rank_out.T[0:SUBLANES, :].astype(I32)
    run = run_scr[0:1, :] + jnp.sum(sel, axis=0, keepdims=True)
    run_scr[0:1, :] = run
    cnt_ref[...] = jnp.broadcast_to(run, cnt_ref.shape).astype(I32)


def _post(xf, odn, osw, mod, w_out, n2w, wr_pad, br_pad, seq):
    t = xf.shape[0]
    tiles_per_seq = seq // TM_POST
    row = lambda i: (i, 0)
    const = lambda i: (0, 0)
    return pl.pallas_call(
        _post_kernel,
        grid=(t // TM_POST,),
        in_specs=[pl.BlockSpec((TM_POST, D_MODEL), row),
                  pl.BlockSpec((TM_POST, DN_WIDTH), row),
                  pl.BlockSpec((TM_POST, SWA_WIDTH), row),
                  pl.BlockSpec((1, 6, D_MODEL), lambda i: (i // tiles_per_seq, 0, 0)),
                  pl.BlockSpec((D_MODEL, D_MODEL), const),
                  pl.BlockSpec((1, D_MODEL), const),
                  pl.BlockSpec((D_MODEL, LANES), const),
                  pl.BlockSpec((1, LANES), const)],
        out_specs=[pl.BlockSpec((TM_POST, D_MODEL), row),
                   pl.BlockSpec((TM_POST * SUBLANES, LANES), row),
                   pl.BlockSpec((SUBLANES, TM_POST), lambda i: (0, i)),
                   pl.BlockSpec((TM_POST, LANES), row),
                   pl.BlockSpec((SUBLANES, TM_POST), lambda i: (0, i)),
                   pl.BlockSpec((SUBLANES, LANES), const)],
        out_shape=[jax.ShapeDtypeStruct((t, D_MODEL), F32),
                   jax.ShapeDtypeStruct((t * SUBLANES, LANES), F32),
                   jax.ShapeDtypeStruct((SUBLANES, t), I32),
                   jax.ShapeDtypeStruct((t, LANES), F32),
                   jax.ShapeDtypeStruct((SUBLANES, t), I32),
                   jax.ShapeDtypeStruct((SUBLANES, LANES), I32)],
        scratch_shapes=[pltpu.VMEM((SUBLANES, LANES), F32)],
        compiler_params=pltpu.CompilerParams(dimension_semantics=("arbitrary",),
                                             vmem_limit_bytes=VMEM_LIMIT),
        name="post",
    )(xf, odn, osw, mod, w_out, n2w, wr_pad, br_pad)


INVERT_GROUP = 16


def _invert_kernel(fill_ref, dest_ref, rowtok_ref):
    def per_range(e, carry):
        def clear(rw, c2):
            rowtok_ref[rw] = 0
            return c2

        lax.fori_loop(fill_ref[0, e], fill_ref[1, e], clear, 0)
        return carry

    lax.fori_loop(0, fill_ref.shape[1], per_range, 0)

    n_tok = dest_ref.shape[0] // TOP_K
    for kk in range(TOP_K):
        def put(g, carry, kk=kk):
            tok0 = g * INVERT_GROUP
            rows = [dest_ref[kk * n_tok + tok0 + u] for u in range(INVERT_GROUP)]
            for u in range(INVERT_GROUP):
                rowtok_ref[rows[u]] = tok0 + u
            return carry

        lax.fori_loop(0, n_tok // INVERT_GROUP, put, 0)


def _invert(fill, dest_flat, n_pad):
    return pl.pallas_call(
        _invert_kernel,
        in_specs=[pl.BlockSpec(memory_space=pltpu.SMEM),
                  pl.BlockSpec(memory_space=pltpu.SMEM)],
        out_specs=pl.BlockSpec(memory_space=pltpu.SMEM),
        out_shape=jax.ShapeDtypeStruct((n_pad,), I32),
        name="invert",
    )(fill, dest_flat)


GATHER_SLOTS = 8
GATHER_AHEAD = GATHER_SLOTS - 1
FF_CHUNKS = 4
FF_COLS = D_FF // FF_CHUNKS


def _moe_kernel(be_ref, nv_ref, nxt_ref, rowtok_ref, h_ref, wu_hbm, bu_ref, wd_hbm, bd_ref, ys_ref,
                wu_f32, wd_f32, wu_bf, wd_bf, xbuf, sem, wsem):
    i = pl.program_id(0)
    slot = i % GATHER_SLOTS
    n_valid = nv_ref[0]

    def weight_copies(e):
        return (pltpu.make_async_copy(wu_hbm.at[e], wu_f32, wsem.at[0]),
                pltpu.make_async_copy(wd_hbm.at[e], wd_f32, wsem.at[1]))

    def gather(blk, s, r0=0, r1=MOE_ROWS):
        for r in range(r0, r1):
            _token_copy(h_ref, rowtok_ref[blk * MOE_ROWS + r], xbuf.at[s], r,
                        sem.at[s]).start(priority=r % 2)

    def gather_wait(s):
        pltpu.make_async_copy(h_ref.at[pl.ds(0, MOE_ROWS * SUBLANES), :],
                              xbuf.at[s, pl.ds(0, MOE_ROWS * SUBLANES), :], sem.at[s]).wait()

    last = n_valid - 1

    @pl.when(i == 0)
    def _():
        for s in range(GATHER_SLOTS):
            xbuf[s, pl.ds(MOE_ROWS * SUBLANES, SUBLANES), :] = jnp.zeros((SUBLANES, LANES), F32)
        for cp in weight_copies(be_ref[0]):
            cp.start()
        for d in range(GATHER_AHEAD):
            gather(jnp.minimum(d, last), d)

    @pl.when(i < n_valid)
    def _():
        expert = be_ref[i]
        prev = be_ref[jnp.maximum(i - 1, 0)]

        @pl.when((i == 0) | (expert != prev))
        def _():
            for cp in weight_copies(expert):
                cp.wait()
            wu_bf[...] = wu_f32[...].astype(BF16)
            wd_bf[...] = wd_f32[...].astype(BF16)
            nxt = nxt_ref[expert]

            @pl.when(nxt != expert)
            def _():
                for cp in weight_copies(nxt):
                    cp.start()

        gather_wait(slot)
        xb = _load_token_tiles(xbuf.at[slot], 0, MOE_ROWS).astype(BF16)
        nxt_blk = jnp.minimum(i + GATHER_AHEAD, last)
        nxt_slot = (i + GATHER_AHEAD) % GATHER_SLOTS
        group = MOE_ROWS // FF_CHUNKS
        acts = []
        tie = jnp.zeros((1, LANES), F32)
        for c in range(FF_CHUNKS):
            cg = slice(c * FF_COLS, (c + 1) * FF_COLS)
            cl = slice(D_FF + c * FF_COLS, D_FF + (c + 1) * FF_COLS)
            tie2 = jnp.concatenate([tie, tie], axis=1)
            g = jnp.dot(xb, wu_bf[:, cg], preferred_element_type=F32) + (bu_ref[0, :, cg] + tie2)
            l = jnp.dot(xb, wu_bf[:, cl], preferred_element_type=F32) + bu_ref[0, :, cl]
            x_glu = jnp.minimum(g, SWIGLU_LIMIT)
            x_lin = jnp.clip(l, -SWIGLU_LIMIT, SWIGLU_LIMIT)
            acts.append((x_glu * jax.nn.sigmoid(SWIGLU_ALPHA * x_glu) * (x_lin + 1.0)).astype(BF16))
            gather(nxt_blk, nxt_slot, c * group, (c + 1) * group)
            spare = xbuf[nxt_slot, pl.ds(MOE_ROWS * SUBLANES, 1), :]
            bits = pltpu.bitcast(spare, jnp.uint32)
            tie = pltpu.bitcast((bits >> 16) >> 16, F32)
        act = jnp.concatenate(acts, axis=1)
        y = jnp.dot(act, wd_bf[...], preferred_element_type=F32) + (
            bd_ref[0] + jnp.concatenate([tie] * (D_MODEL // LANES), axis=1))
        _store_token_tiles(ys_ref, y)

    @pl.when(i == last)
    def _():
        for d in range(1, GATHER_SLOTS):
            gather_wait((i + d) % GATHER_SLOTS)

    @pl.when(i >= n_valid)
    def _():
        ys_ref[...] = jnp.zeros(ys_ref.shape, F32)


def _moe(block_e, n_valid, next_e, row_tok, h2, w_up, b_up, w_down, b_down):
    n_pad = row_tok.shape[0]
    n_blocks = n_pad // MOE_ROWS
    out_rows = lambda i, be, nv, nx, rt: (i, 0)
    expert = lambda i, be, nv, nx, rt: (be[i], 0, 0)
    return pl.pallas_call(
        _moe_kernel,
        grid_spec=pltpu.PrefetchScalarGridSpec(
            num_scalar_prefetch=4,
            grid=(n_blocks,),
            in_specs=[pl.BlockSpec(memory_space=pl.ANY),
                      pl.BlockSpec(memory_space=pl.ANY),
                      pl.BlockSpec((1, 1, 2 * D_FF), expert),
                      pl.BlockSpec(memory_space=pl.ANY),
                      pl.BlockSpec((1, 1, D_MODEL), expert)],
            out_specs=pl.BlockSpec((MOE_ROWS * SUBLANES, LANES), out_rows),
            scratch_shapes=[pltpu.VMEM((D_MODEL, 2 * D_FF), F32),
                            pltpu.VMEM((D_FF, D_MODEL), F32),
                            pltpu.VMEM((D_MODEL, 2 * D_FF), BF16),
                            pltpu.VMEM((D_FF, D_MODEL), BF16),
                            pltpu.VMEM((GATHER_SLOTS, (MOE_ROWS + 1) * SUBLANES, LANES), F32),
                            pltpu.SemaphoreType.DMA((GATHER_SLOTS,)),
                            pltpu.SemaphoreType.DMA((2,))]),
        out_shape=jax.ShapeDtypeStruct((n_pad * SUBLANES, LANES), F32),
        compiler_params=pltpu.CompilerParams(dimension_semantics=("arbitrary",),
                                             vmem_limit_bytes=VMEM_LIMIT),
        name="moe",
    )(block_e, n_valid, next_e, row_tok, h2, w_up, b_up, w_down, b_down)


def _combine_kernel(dcur_ref, dnext_ref, ys_ref, x1_ref, gate_ref, mod_ref, o_ref, buf, sem):
    i = pl.program_id(0)
    tm = x1_ref.shape[0]
    slot = i % 2

    def issue(dref, s):
        def body(tk, carry):
            for kk in range(TOP_K):
                _token_copy(ys_ref, dref[0, 0, kk * tm + tk], buf.at[s], kk * tm + tk,
                            sem.at[s]).start(priority=kk % 2)
            return carry

        lax.fori_loop(0, tm, body, 0, unroll=4)

    @pl.when(i == 0)
    def _():
        issue(dcur_ref, 0)

    @pl.when(i + 1 < pl.num_programs(0))
    def _():
        issue(dnext_ref, 1 - slot)

    pltpu.make_async_copy(ys_ref.at[pl.ds(0, TOP_K * tm * SUBLANES), :], buf.at[slot],
                          sem.at[slot]).wait()

    gates = gate_ref[...]
    acc = jnp.zeros((tm, D_MODEL), F32)
    for kk in range(TOP_K):
        acc = acc + gates[:, kk:kk + 1] * _load_token_tiles(buf.at[slot], kk * tm, tm)
    o_ref[...] = x1_ref[...] + mod_ref[0, 5:6, :] * acc


def _combine(dest3, ys, x1, gates, mod, seq):
    t = x1.shape[0]
    tiles_per_seq = seq // TM_COMB
    n_tiles = t // TM_COMB
    row = lambda i: (i, 0)
    return pl.pallas_call(
        _combine_kernel,
        grid=(n_tiles,),
        in_specs=[pl.BlockSpec((1, 1, TM_COMB * TOP_K), lambda i: (i, 0, 0),
                               memory_space=pltpu.SMEM),
                  pl.BlockSpec((1, 1, TM_COMB * TOP_K),
                               lambda i: (jnp.minimum(i + 1, n_tiles - 1), 0, 0),
                               memory_space=pltpu.SMEM),
                  pl.BlockSpec(memory_space=pl.ANY),
                  pl.BlockSpec((TM_COMB, D_MODEL), row),
                  pl.BlockSpec((TM_COMB, LANES), row),
                  pl.BlockSpec((1, 6, D_MODEL), lambda i: (i // tiles_per_seq, 0, 0))],
        out_specs=pl.BlockSpec((TM_COMB, D_MODEL), row),
        out_shape=jax.ShapeDtypeStruct((t, D_MODEL), F32),
        scratch_shapes=[pltpu.VMEM((2, TOP_K * TM_COMB * SUBLANES, LANES), F32),
                        pltpu.SemaphoreType.DMA((2,))],
        compiler_params=pltpu.CompilerParams(dimension_semantics=("arbitrary",),
                                             vmem_limit_bytes=VMEM_LIMIT),
        name="combine",
    )(dest3, dest3, ys, x1, gates, mod)


def _layer(x, mod, norm1_w, w_in, conv_w, a_log, dt_bias, dn_norm_w, q_norm_w, k_norm_w, sinks,
           w_out, norm2_w, w_router, b_router, w_up, b_up, w_down, b_down):
    batch, seq, _ = x.shape
    t = batch * seq
    xf = x.reshape(t, D_MODEL)

    o_qkv, o_z, o_a, o_b, o_q, o_k, o_v = 0, 1536, 2048, 2052, 2056, 2568, 2696
    w_ab = jnp.zeros((D_MODEL, LANES), F32).at[:, 0:2 * DN_HEADS].set(w_in[:, o_a:o_q])
    w_cat = jnp.concatenate([w_in[:, o_qkv:o_z], w_in[:, o_z:o_a], w_in[:, o_q:o_k],
                             w_in[:, o_k:o_v], w_in[:, o_v:], w_ab], axis=1).astype(BF16)
    gparams = jnp.zeros((2, LANES), F32)
    gparams = gparams.at[0, 0:DN_HEADS].set(-jnp.exp(a_log.astype(F32)))
    gparams = gparams.at[1, 0:DN_HEADS].set(dt_bias.astype(F32))
    qnw128 = jnp.tile(q_norm_w.astype(F32), 2)[None, :]
    knw128 = jnp.tile(k_norm_w.astype(F32), 2)[None, :]

    q, k, v, z, gb, swq, swk, swv = _inproj(xf, mod, norm1_w[None, :], w_cat, conv_w, gparams,
                                            qnw128, knw128, seq)
    o_dn, o_sw = _mixer(q, k, v, z, gb, dn_norm_w[None, :], swq, swk, swv, sinks.astype(F32),
                        batch, seq)

    wr_pad = jnp.zeros((D_MODEL, LANES), F32).at[:, 0:N_EXPERTS].set(w_router).astype(BF16)
    br_pad = jnp.zeros((1, LANES), F32).at[0, 0:N_EXPERTS].set(b_router)
    x1, h2, top_idx, gates, rank, cnt = _post(xf, o_dn, o_sw, mod, w_out.astype(BF16),
                                              norm2_w[None, :], wr_pad, br_pad, seq)

    counts = cnt[0, 0:N_EXPERTS]
    padded = (counts + MOE_ROWS - 1) // MOE_ROWS * MOE_ROWS
    pend = jnp.cumsum(padded)
    pstart = pend - padded
    n_blocks = (t * TOP_K) // MOE_ROWS + N_EXPERTS
    n_pad = n_blocks * MOE_ROWS
    n_valid = (pend[-1] // MOE_ROWS).astype(I32)
    blk = jnp.minimum(jnp.arange(n_blocks, dtype=I32), n_valid - 1)
    block_e = jnp.minimum(jnp.sum(pend[None, :] <= (blk * MOE_ROWS)[:, None], axis=1),
                          N_EXPERTS - 1).astype(I32)
    experts = jnp.arange(N_EXPERTS, dtype=I32)[:, None, None]
    dest = (jnp.sum(jnp.where(top_idx[None, 0:TOP_K] == experts, pstart[:, None, None], 0), axis=0)
            + rank[0:TOP_K]).astype(I32)
    fill = jnp.stack([jnp.append(pstart + counts, pend[-1]),
                      jnp.append(pend, n_pad)]).astype(I32)

    eids = jnp.arange(N_EXPERTS, dtype=I32)
    later = (eids[None, :] > eids[:, None]) & (counts[None, :] > 0)
    next_e = jnp.min(jnp.where(later, eids[None, :], N_EXPERTS), axis=1)
    next_e = jnp.where(next_e == N_EXPERTS, eids, next_e).astype(I32)

    row_tok = _invert(fill, dest.reshape(TOP_K * t), n_pad)
    ys = _moe(block_e, n_valid.reshape(1), next_e, row_tok, h2, w_up, b_up[:, None, :], w_down,
              b_down[:, None, :])
    n_tiles = t // TM_COMB
    dest_tiles = dest.reshape(TOP_K, n_tiles, TM_COMB).transpose(1, 0, 2).reshape(
        n_tiles, 1, TOP_K * TM_COMB)
    out = _combine(dest_tiles, ys, x1, gates, mod, seq)
    return out.reshape(batch, seq, D_MODEL)


def kernel(x, c, w_ada, b_ada, norm1_w, w_in, conv_w, a_log, dt_bias, dn_norm_w, q_norm_w,
           k_norm_w, sinks, w_out, norm2_w, w_router, b_router, w_up, b_up, w_down, b_down):
    out_dtype = x.dtype
    batch = x.shape[0]
    depth = w_ada.shape[0]
    c_pad = jnp.zeros((SUBLANES, D_MODEL), F32).at[0:batch].set(c.astype(F32))
    for l in range(depth):
        mod = _ada(c_pad, w_ada[l], b_ada[l][None, :])[0:batch].reshape(batch, 6, D_MODEL)
        x = _layer(x, mod, norm1_w[l], w_in[l], conv_w[l], a_log[l], dt_bias[l], dn_norm_w[l],
                   q_norm_w[l], k_norm_w[l], sinks[l], w_out[l], norm2_w[l], w_router[l],
                   b_router[l], w_up[l], b_up[l], w_down[l], b_down[l])
    return x.astype(out_dtype)
```

```python
import functools

import jax
import jax.numpy as jnp
from jax import lax
from jax.experimental import pallas as pl
from jax.experimental.pallas import tpu as pltpu

F32 = jnp.float32
BF16 = jnp.bfloat16
I32 = jnp.int32

D_MODEL = 1024
DN_HEADS = 4
DN_HEAD_DIM = 128
DN_WIDTH = DN_HEADS * DN_HEAD_DIM
CONV_WIDTH = 4
CHUNK = 64
SWA_HEADS = 8
SWA_KV_HEADS = 2
SWA_GROUP = SWA_HEADS // SWA_KV_HEADS
SWA_HEAD_DIM = 64
SWA_WIDTH = SWA_HEADS * SWA_HEAD_DIM
SWA_KV_WIDTH = SWA_KV_HEADS * SWA_HEAD_DIM
WINDOW = 128
N_EXPERTS = 32
TOP_K = 4
D_FF = D_MODEL
SWIGLU_ALPHA = 1.702
SWIGLU_LIMIT = 7.0
EPS = 1e-6
NEG = -1e30

LANES = 128
SUBLANES = 8
VMEM_LIMIT = 56 * 1024 * 1024

C_QKV = 0
C_Z = C_QKV + 3 * DN_WIDTH
C_SWQ = C_Z + DN_WIDTH
C_SWK = C_SWQ + SWA_WIDTH
C_SWV = C_SWK + SWA_KV_WIDTH
C_AB = C_SWV + SWA_KV_WIDTH
IN_PAD = C_AB + LANES

TM_IN = 512
DN_ROWS = 256
TM_POST = 512
MOE_ROWS = 256
TM_COMB = 128


def _silu(x):
    return x * jax.nn.sigmoid(x)


def _mm(a, b):
    return jnp.dot(a.astype(BF16), b.astype(BF16), preferred_element_type=F32)


def _mm_nt(a, b):
    return lax.dot_general(a.astype(BF16), b.astype(BF16), (((1,), (1,)), ((), ())),
                           preferred_element_type=F32)


def _mm_tn(a, b):
    return lax.dot_general(a.astype(BF16), b.astype(BF16), (((0,), (0,)), ((), ())),
                           preferred_element_type=F32)


def _load_token_tiles(ref, row0, rows):
    return jnp.concatenate(
        [ref[pl.ds(row0 * SUBLANES + s, rows, stride=SUBLANES), :] for s in range(D_MODEL // LANES)],
        axis=1)


def _store_token_tiles(ref, val):
    rows = val.shape[0]
    for s in range(D_MODEL // LANES):
        ref[pl.ds(s, rows, stride=SUBLANES), :] = val[:, s * LANES:(s + 1) * LANES]


def _token_copy(src_ref, s, dst_ref, d, sem):
    return pltpu.make_async_copy(
        src_ref.at[pl.ds(pl.multiple_of(s * SUBLANES, SUBLANES), SUBLANES), :],
        dst_ref.at[pl.ds(pl.multiple_of(d * SUBLANES, SUBLANES), SUBLANES), :], sem)


def _ada_kernel(c_ref, w_ref, b_ref, o_ref):
    c = c_ref[...]
    o_ref[...] = jnp.dot(_silu(c), w_ref[...], precision=lax.Precision.HIGHEST,
                         preferred_element_type=F32) + b_ref[...]


def _ada(c_pad, w_ada, b_ada):
    n = w_ada.shape[1]
    return pl.pallas_call(
        _ada_kernel,
        grid=(n // D_MODEL,),
        in_specs=[pl.BlockSpec((SUBLANES, D_MODEL), lambda j: (0, 0)),
                  pl.BlockSpec((D_MODEL, D_MODEL), lambda j: (0, j)),
                  pl.BlockSpec((1, D_MODEL), lambda j: (0, j))],
        out_specs=pl.BlockSpec((SUBLANES, D_MODEL), lambda j: (0, j)),
        out_shape=jax.ShapeDtypeStruct((SUBLANES, n), F32),
        name="ada",
    )(c_pad, w_ada, b_ada)


def _rms_heads64(a, w128):
    lane = lax.broadcasted_iota(I32, a.shape, 1)
    lo = lane < SWA_HEAD_DIM
    sq = a * a
    s_lo = jnp.sum(jnp.where(lo, sq, 0.0), axis=-1, keepdims=True)
    s_hi = jnp.sum(jnp.where(lo, 0.0, sq), axis=-1, keepdims=True)
    ms = jnp.where(lo, s_lo, s_hi) * (1.0 / SWA_HEAD_DIM)
    return a * lax.rsqrt(ms + EPS) * w128


def _inproj_kernel(tiles_per_seq, x_ref, mod_ref, n1w_ref, w_ref, convw_ref, gp_ref, qnw_ref,
                   knw_ref, q_ref, k_ref, v_ref, z_ref, gb_ref, swq_ref, swk_ref, swv_ref,
                   conv_scr):
    i = pl.program_id(0)
    tm = x_ref.shape[0]
    x = x_ref[...]
    y = x * lax.rsqrt(jnp.mean(x * x, axis=-1, keepdims=True) + EPS) * n1w_ref[...]
    h = (y * (1.0 + mod_ref[0, 1:2, :]) + mod_ref[0, 0:1, :]).astype(BF16)

    @pl.when(i % tiles_per_seq == 0)
    def _():
        conv_scr[0:SUBLANES, :] = jnp.zeros((SUBLANES, 3 * DN_WIDTH), F32)

    u = jnp.dot(h, w_ref[:, C_QKV:C_Z], preferred_element_type=F32)
    conv_scr[SUBLANES:SUBLANES + tm, :] = u
    cw = convw_ref[...]
    acc = u * cw[3:4, :]
    for j in range(CONV_WIDTH - 1):
        off = SUBLANES - (CONV_WIDTH - 1) + j
        acc = acc + conv_scr[off:off + tm, :] * cw[j:j + 1, :]
    conv_scr[0:SUBLANES, :] = conv_scr[tm:tm + SUBLANES, :]
    qkv = _silu(acc)

    for hh in range(DN_HEADS):
        sl = slice(hh * DN_HEAD_DIM, (hh + 1) * DN_HEAD_DIM)
        qh = qkv[:, hh * DN_HEAD_DIM:(hh + 1) * DN_HEAD_DIM]
        kh = qkv[:, DN_WIDTH + hh * DN_HEAD_DIM:DN_WIDTH + (hh + 1) * DN_HEAD_DIM]
        q_ref[:, sl] = qh * lax.rsqrt(jnp.sum(qh * qh, axis=-1, keepdims=True) + EPS) * (
            DN_HEAD_DIM ** -0.5)
        k_ref[:, sl] = kh * lax.rsqrt(jnp.sum(kh * kh, axis=-1, keepdims=True) + EPS)
    v_ref[...] = qkv[:, 2 * DN_WIDTH:3 * DN_WIDTH]

    z_ref[...] = jnp.dot(h, w_ref[:, C_Z:C_SWQ], preferred_element_type=F32)

    ab = jnp.dot(h, w_ref[:, C_AB:IN_PAD], preferred_element_type=F32)
    lane = lax.broadcasted_iota(I32, ab.shape, 1)
    sp_in = ab + gp_ref[1:2, :]
    softplus = jnp.maximum(sp_in, 0.0) + jnp.log1p(jnp.exp(-jnp.abs(sp_in)))
    gb_ref[...] = jnp.where(lane < DN_HEADS, gp_ref[0:1, :] * softplus, jax.nn.sigmoid(ab))

    sq = jnp.dot(h, w_ref[:, C_SWQ:C_SWK], preferred_element_type=F32)
    for t in range(SWA_WIDTH // LANES):
        sl = slice(t * LANES, (t + 1) * LANES)
        swq_ref[:, sl] = (_rms_heads64(sq[:, sl], qnw_ref[...]) * (SWA_HEAD_DIM ** -0.5)).astype(BF16)
    sk = jnp.dot(h, w_ref[:, C_SWK:C_SWV], preferred_element_type=F32)
    swk_ref[...] = _rms_heads64(sk, knw_ref[...]).astype(BF16)
    swv_ref[...] = jnp.dot(h, w_ref[:, C_SWV:C_AB], preferred_element_type=F32).astype(BF16)


def _inproj(xf, mod, n1w, w_cat, conv_w, gparams, qnw128, knw128, seq):
    t = xf.shape[0]
    tiles_per_seq = seq // TM_IN
    row = lambda i: (i, 0)
    const = lambda i: (0, 0)
    outs = [jax.ShapeDtypeStruct((t, DN_WIDTH), F32)] * 4 + [
        jax.ShapeDtypeStruct((t, LANES), F32),
        jax.ShapeDtypeStruct((t, SWA_WIDTH), BF16),
        jax.ShapeDtypeStruct((t, SWA_KV_WIDTH), BF16),
        jax.ShapeDtypeStruct((t, SWA_KV_WIDTH), BF16)]
    out_specs = [pl.BlockSpec((TM_IN, DN_WIDTH), row)] * 4 + [
        pl.BlockSpec((TM_IN, LANES), row),
        pl.BlockSpec((TM_IN, SWA_WIDTH), row),
        pl.BlockSpec((TM_IN, SWA_KV_WIDTH), row),
        pl.BlockSpec((TM_IN, SWA_KV_WIDTH), row)]
    return pl.pallas_call(
        functools.partial(_inproj_kernel, tiles_per_seq),
        grid=(t // TM_IN,),
        in_specs=[pl.BlockSpec((TM_IN, D_MODEL), row),
                  pl.BlockSpec((1, 6, D_MODEL), lambda i: (i // tiles_per_seq, 0, 0)),
                  pl.BlockSpec((1, D_MODEL), const),
                  pl.BlockSpec((D_MODEL, IN_PAD), const),
                  pl.BlockSpec((CONV_WIDTH, 3 * DN_WIDTH), const),
                  pl.BlockSpec((2, LANES), const),
                  pl.BlockSpec((1, LANES), const),
                  pl.BlockSpec((1, LANES), const)],
        out_specs=out_specs,
        out_shape=outs,
        scratch_shapes=[pltpu.VMEM((TM_IN + SUBLANES, 3 * DN_WIDTH), F32)],
        compiler_params=pltpu.CompilerParams(dimension_semantics=("arbitrary",),
                                             vmem_limit_bytes=VMEM_LIMIT),
        name="inproj",
    )(xf, mod, n1w, w_cat, conv_w, gparams, qnw128, knw128)


def _dn_block(j, q_ref, k_ref, v_ref, z_ref, gb_ref, nw_ref, o_ref,
              s_scr, u_scr, w_scr, qd_scr, kd_scr, a_scr):
    rows = q_ref.shape[0]
    n_chunks = rows // CHUNK
    heads = range(DN_HEADS)

    @pl.when(j == 0)
    def _():
        s_scr[...] = jnp.zeros(s_scr.shape, F32)

    r = lax.broadcasted_iota(I32, (rows, rows), 0)
    c = lax.broadcasted_iota(I32, (rows, rows), 1)
    same = (r // CHUNK) == (c // CHUNK)
    incl = same & (c <= r)
    strict = same & (c < r)
    gb = gb_ref[...]
    gc = jnp.dot(jnp.where(incl, 1.0, 0.0), gb, precision=lax.Precision.HIGHEST,
                 preferred_element_type=F32)
    gl = jnp.dot(jnp.where(same, 1.0, 0.0), gb, precision=lax.Precision.HIGHEST,
                 preferred_element_type=F32)
    gct = gc.T

    rmat = []
    rhs = []
    for h in heads:
        ls = slice(h * DN_HEAD_DIM, (h + 1) * DN_HEAD_DIM)
        g_col = gc[:, h:h + 1]
        beta = gb[:, DN_HEADS + h:DN_HEADS + h + 1]
        decay = jnp.where(incl, jnp.exp(jnp.where(incl, g_col - gct[h:h + 1, :], 0.0)), 0.0)
        qq = q_ref[:, ls]
        kk = k_ref[:, ls]
        kb = kk * beta
        eg = jnp.exp(g_col)
        rmat.append(-jnp.where(strict, _mm_nt(kb, kk) * decay, 0.0))
        rhs.append(jnp.concatenate([v_ref[:, ls] * beta, kb * eg], axis=1))
        a_scr[h] = jnp.where(incl, _mm_nt(qq, kk) * decay, 0.0)
        qd_scr[:, ls] = qq * eg
        kd_scr[:, ls] = kk * jnp.exp(gl[:, h:h + 1] - g_col)

    xm = list(rmat)
    pm = list(rmat)
    for it in range(6):
        xm = [xm[h] + _mm(pm[h], xm[h]) for h in heads]
        if it < 5:
            pm = [_mm(pm[h], pm[h]) for h in heads]
    for h in heads:
        ls = slice(h * DN_HEAD_DIM, (h + 1) * DN_HEAD_DIM)
        sol = rhs[h] + _mm(xm[h], rhs[h])
        u_scr[:, ls] = sol[:, 0:DN_HEAD_DIM]
        w_scr[:, ls] = sol[:, DN_HEAD_DIM:2 * DN_HEAD_DIM]

    nw = nw_ref[...]
    lsl = [slice(h * DN_HEAD_DIM, (h + 1) * DN_HEAD_DIM) for h in heads]
    for ci in range(n_chunks):
        rs = slice(ci * CHUNK, (ci + 1) * CHUNK)
        dvals = jnp.exp(gl[ci * CHUNK:ci * CHUNK + 1, :])
        st = [s_scr[h] for h in heads]
        ws = [_mm(w_scr[rs, lsl[h]], st[h]) for h in heads]
        qs = [_mm(qd_scr[rs, lsl[h]], st[h]) for h in heads]
        vn = [u_scr[rs, lsl[h]] - ws[h] for h in heads]
        av = [_mm(a_scr[h, rs, rs], vn[h]) for h in heads]
        kv = [_mm_tn(kd_scr[rs, lsl[h]], vn[h]) for h in heads]
        for h in heads:
            s_scr[h] = st[h] * dvals[:, h:h + 1] + kv[h]
            o = qs[h] + av[h]
            on = o * lax.rsqrt(jnp.mean(o * o, axis=-1, keepdims=True) + EPS) * nw
            o_ref[rs, lsl[h]] = (on * _silu(z_ref[rs, lsl[h]])).astype(o_ref.dtype)


def _swa_window(sink_ref, q, kcat, vcat, first):
    w = WINDOW
    rows = SWA_GROUP * w
    ri = lax.broadcasted_iota(I32, (rows, 2 * w), 0)
    kj = lax.broadcasted_iota(I32, (rows, 2 * w), 1)
    grp = ri // w
    rel = (ri - grp * w) + w - kj
    valid = (rel >= 0) & (rel < w)
    if first is not None:
        valid = valid & ((kj >= w) | jnp.logical_not(first))
    relf = rel.astype(F32)
    grp_col = lax.broadcasted_iota(I32, (rows, 1), 0) // w
    outs = []
    for kh in range(SWA_KV_HEADS):
        slope_col = jnp.zeros((rows, 1), F32)
        sink_col = jnp.zeros((rows, 1), F32)
        for g in range(SWA_GROUP):
            hd = kh * SWA_GROUP + g
            slope_col = jnp.where(grp_col == g, 2.0 ** (-8.0 * (hd + 1) / SWA_HEADS), slope_col)
            sink_col = jnp.where(grp_col == g, sink_ref[hd], sink_col)
        qs = jnp.concatenate(
            [q[:, (kh * SWA_GROUP + g) * SWA_HEAD_DIM:(kh * SWA_GROUP + g + 1) * SWA_HEAD_DIM]
             for g in range(SWA_GROUP)], axis=0)
        kk = kcat[:, kh * SWA_HEAD_DIM:(kh + 1) * SWA_HEAD_DIM]
        vv = vcat[:, kh * SWA_HEAD_DIM:(kh + 1) * SWA_HEAD_DIM]
        s = lax.dot_general(qs, kk, (((1,), (1,)), ((), ())), preferred_element_type=F32)
        logits = jnp.where(valid, s - slope_col * relf, NEG)
        m = jnp.maximum(jnp.max(logits, axis=-1, keepdims=True), sink_col)
        p = jnp.exp(logits - m)
        denom = jnp.sum(p, axis=-1, keepdims=True) + jnp.exp(sink_col - m)
        o = jnp.dot(p.astype(BF16), vv, preferred_element_type=F32) / denom
        for g in range(SWA_GROUP):
            outs.append(o[g * w:(g + 1) * w, :])
    return [jnp.concatenate([outs[2 * t], outs[2 * t + 1]], axis=-1)
            for t in range(SWA_WIDTH // LANES)]


def _mixer_kernel(sink_ref, q_ref, k_ref, v_ref, z_ref, gb_ref, nw_ref,
                  swq_ref, kp_ref, kc_ref, vp_ref, vc_ref, odn_ref, osw_ref,
                  s_scr, u_scr, w_scr, qd_scr, kd_scr, a_scr):
    j = pl.program_id(1)
    _dn_block(j, q_ref, k_ref, v_ref, z_ref, gb_ref, nw_ref, odn_ref,
              s_scr, u_scr, w_scr, qd_scr, kd_scr, a_scr)
    w = WINDOW
    kcur = kc_ref[...]
    vcur = vc_ref[...]
    for s in range(DN_ROWS // w):
        rows = slice(s * w, (s + 1) * w)
        if s == 0:
            kcat = jnp.concatenate([kp_ref[...], kcur[0:w, :]], axis=0)
            vcat = jnp.concatenate([vp_ref[...], vcur[0:w, :]], axis=0)
            first = j == 0
        else:
            kcat = kcur[(s - 1) * w:(s + 1) * w, :]
            vcat = vcur[(s - 1) * w:(s + 1) * w, :]
            first = None
        tiles = _swa_window(sink_ref, swq_ref[rows, :], kcat, vcat, first)
        for t, tile in enumerate(tiles):
            osw_ref[rows, t * LANES:(t + 1) * LANES] = tile.astype(osw_ref.dtype)


def _mixer(q, k, v, z, gb, nw, swq, swk, swv, sinks, batch, seq):
    t = q.shape[0]
    steps = seq // DN_ROWS
    per_step = DN_ROWS // WINDOW
    row = lambda b, j, s: (b * steps + j, 0)
    prev = lambda b, j, s: (b * steps * per_step + jnp.maximum(j * per_step - 1, 0), 0)
    blk = pl.BlockSpec((DN_ROWS, DN_WIDTH), row)
    kv_cur = pl.BlockSpec((DN_ROWS, SWA_KV_WIDTH), row)
    kv_prev = pl.BlockSpec((WINDOW, SWA_KV_WIDTH), prev)
    return pl.pallas_call(
        _mixer_kernel,
        grid_spec=pltpu.PrefetchScalarGridSpec(
            num_scalar_prefetch=1,
            grid=(batch, steps),
            in_specs=[blk, blk, blk, blk,
                      pl.BlockSpec((DN_ROWS, LANES), row),
                      pl.BlockSpec((1, DN_HEAD_DIM), lambda b, j, s: (0, 0)),
                      pl.BlockSpec((DN_ROWS, SWA_WIDTH), row),
                      kv_prev, kv_cur, kv_prev, kv_cur],
            out_specs=[blk, pl.BlockSpec((DN_ROWS, SWA_WIDTH), row)],
            scratch_shapes=[pltpu.VMEM((DN_HEADS, DN_HEAD_DIM, DN_HEAD_DIM), F32),
                            pltpu.VMEM((DN_ROWS, DN_WIDTH), F32),
                            pltpu.VMEM((DN_ROWS, DN_WIDTH), F32),
                            pltpu.VMEM((DN_ROWS, DN_WIDTH), F32),
                            pltpu.VMEM((DN_ROWS, DN_WIDTH), F32),
                            pltpu.VMEM((DN_HEADS, DN_ROWS, DN_ROWS), F32)]),
        out_shape=[jax.ShapeDtypeStruct((t, DN_WIDTH), BF16),
                   jax.ShapeDtypeStruct((t, SWA_WIDTH), BF16)],
        compiler_params=pltpu.CompilerParams(dimension_semantics=("arbitrary", "arbitrary"),
                                             vmem_limit_bytes=VMEM_LIMIT),
        name="mixer",
    )(sinks, q, k, v, z, gb, nw, swq, swk, swk, swv, swv)


def _post_kernel(x_ref, odn_ref, osw_ref, mod_ref, wo_ref, n2w_ref, wr_ref, br_ref,
                 x1_ref, h2_ref, idx_ref, gate_ref, rank_ref, cnt_ref, run_scr):
    i = pl.program_id(0)
    tm = x_ref.shape[0]

    @pl.when(i == 0)
    def _():
        run_scr[...] = jnp.zeros(run_scr.shape, F32)

    y = (jnp.dot(odn_ref[...], wo_ref[0:DN_WIDTH, :], preferred_element_type=F32)
         + jnp.dot(osw_ref[...], wo_ref[DN_WIDTH:, :], preferred_element_type=F32))
    x1 = x_ref[...] + mod_ref[0, 2:3, :] * y
    x1_ref[...] = x1
    hn = x1 * lax.rsqrt(jnp.mean(x1 * x1, axis=-1, keepdims=True) + EPS) * n2w_ref[...]
    h2 = hn * (1.0 + mod_ref[0, 4:5, :]) + mod_ref[0, 3:4, :]
    _store_token_tiles(h2_ref, h2)

    lane = lax.broadcasted_iota(I32, (tm, LANES), 1)
    lane_f = lane.astype(F32)
    logits = jnp.dot(h2.astype(BF16), wr_ref[...], preferred_element_type=F32) + br_ref[...]
    work = jnp.where(lane < N_EXPERTS, logits, NEG)
    sel = jnp.zeros((tm, LANES), F32)
    idx_out = jnp.zeros((tm, LANES), F32)
    vals = []
    idxs = []
    for kk in range(TOP_K):
        m = jnp.max(work, axis=-1, keepdims=True)
        ik = jnp.min(jnp.where(work == m, lane_f, float(LANES)), axis=-1, keepdims=True)
        hit = lane_f == ik
        work = jnp.where(hit, NEG, work)
        sel = jnp.where(hit, 1.0, sel)
        idx_out = jnp.where(lane == kk, ik, idx_out)
        vals.append(m)
        idxs.append(ik)
    es = [jnp.exp(v - vals[0]) for v in vals]
    tot = es[0] + es[1] + es[2] + es[3]
    gate_out = jnp.zeros((tm, LANES), F32)
    for kk in range(TOP_K):
        gate_out = jnp.where(lane == kk, es[kk] / tot, gate_out)
    gate_ref[...] = gate_out

    r = lax.broadcasted_iota(I32, (tm, tm), 0)
    c = lax.broadcasted_iota(I32, (tm, tm), 1)
    before = jnp.where(c < r, 1.0, 0.0).astype(BF16)
    base = jnp.dot(before, sel.astype(BF16), preferred_element_type=F32) + run_scr[0:1, :]
    rank_out = jnp.zeros((tm, LANES), F32)
    for kk in range(TOP_K):
        rk = jnp.sum(jnp.where(lane_f == idxs[kk], base, 0.0), axis=-1, keepdims=True)
        rank_out = jnp.where(lane == kk, rk, rank_out)
    idx_ref[...] = idx_out.T[0:SUBLANES, :].astype(I32)
    rank_ref[...] = rank_out.T[0:SUBLANES, :].astype(I32)
    run = run_scr[0:1, :] + jnp.sum(sel, axis=0, keepdims=True)
    run_scr[0:1, :] = run
    cnt_ref[...] = jnp.broadcast_to(run, cnt_ref.shape).astype(I32)


def _post(xf, odn, osw, mod, w_out, n2w, wr_pad, br_pad, seq):
    t = xf.shape[0]
    tiles_per_seq = seq // TM_POST
    row = lambda i: (i, 0)
    const = lambda i: (0, 0)
    return pl.pallas_call(
        _post_kernel,
        grid=(t // TM_POST,),
        in_specs=[pl.BlockSpec((TM_POST, D_MODEL), row),
                  pl.BlockSpec((TM_POST, DN_WIDTH), row),
                  pl.BlockSpec((TM_POST, SWA_WIDTH), row),
                  pl.BlockSpec((1, 6, D_MODEL), lambda i: (i // tiles_per_seq, 0, 0)),
                  pl.BlockSpec((D_MODEL, D_MODEL), const),
                  pl.BlockSpec((1, D_MODEL), const),
                  pl.BlockSpec((D_MODEL, LANES), const),
                  pl.BlockSpec((1, LANES), const)],
        out_specs=[pl.BlockSpec((TM_POST, D_MODEL), row),
                   pl.BlockSpec((TM_POST * SUBLANES, LANES), row),
                   pl.BlockSpec((SUBLANES, TM_POST), lambda i: (0, i)),
                   pl.BlockSpec((TM_POST, LANES), row),
                   pl.BlockSpec((SUBLANES, TM_POST), lambda i: (0, i)),
                   pl.BlockSpec((SUBLANES, LANES), const)],
        out_shape=[jax.ShapeDtypeStruct((t, D_MODEL), F32),
                   jax.ShapeDtypeStruct((t * SUBLANES, LANES), F32),
                   jax.ShapeDtypeStruct((SUBLANES, t), I32),
                   jax.ShapeDtypeStruct((t, LANES), F32),
                   jax.ShapeDtypeStruct((SUBLANES, t), I32),
                   jax.ShapeDtypeStruct((SUBLANES, LANES), I32)],
        scratch_shapes=[pltpu.VMEM((SUBLANES, LANES), F32)],
        compiler_params=pltpu.CompilerParams(dimension_semantics=("arbitrary",),
                                             vmem_limit_bytes=VMEM_LIMIT),
        name="post",
    )(xf, odn, osw, mod, w_out, n2w, wr_pad, br_pad)


INVERT_GROUP = 16


def _invert_kernel(fill_ref, dest_ref, rowtok_ref):
    def per_range(e, carry):
        def clear(rw, c2):
            rowtok_ref[rw] = 0
            return c2

        lax.fori_loop(fill_ref[0, e], fill_ref[1, e], clear, 0)
        return carry

    lax.fori_loop(0, fill_ref.shape[1], per_range, 0)

    n_tok = dest_ref.shape[0] // TOP_K
    for kk in range(TOP_K):
        def put(g, carry, kk=kk):
            tok0 = g * INVERT_GROUP
            rows = [dest_ref[kk * n_tok + tok0 + u] for u in range(INVERT_GROUP)]
            for u in range(INVERT_GROUP):
                rowtok_ref[rows[u]] = tok0 + u
            return carry

        lax.fori_loop(0, n_tok // INVERT_GROUP, put, 0)


def _invert(fill, dest_flat, n_pad):
    return pl.pallas_call(
        _invert_kernel,
        in_specs=[pl.BlockSpec(memory_space=pltpu.SMEM),
                  pl.BlockSpec(memory_space=pltpu.SMEM)],
        out_specs=pl.BlockSpec(memory_space=pltpu.SMEM),
        out_shape=jax.ShapeDtypeStruct((n_pad,), I32),
        name="invert",
    )(fill, dest_flat)


GATHER_SLOTS = 8
GATHER_AHEAD = GATHER_SLOTS - 1
FF_CHUNKS = 4
FF_COLS = D_FF // FF_CHUNKS


def _moe_kernel(be_ref, nv_ref, nxt_ref, rowtok_ref, h_ref, wu_hbm, bu_ref, wd_hbm, bd_ref, ys_ref,
                wu_f32, wd_f32, wu_bf, wd_bf, xbuf, sem, wsem):
    i = pl.program_id(0)
    slot = i % GATHER_SLOTS
    n_valid = nv_ref[0]

    def weight_copies(e):
        return (pltpu.make_async_copy(wu_hbm.at[e], wu_f32, wsem.at[0]),
                pltpu.make_async_copy(wd_hbm.at[e], wd_f32, wsem.at[1]))

    def gather(blk, s, r0=0, r1=MOE_ROWS):
        for r in range(r0, r1):
            _token_copy(h_ref, rowtok_ref[blk * MOE_ROWS + r], xbuf.at[s], r,
                        sem.at[s]).start(priority=r % 2)

    def gather_wait(s):
        pltpu.make_async_copy(h_ref.at[pl.ds(0, MOE_ROWS * SUBLANES), :],
                              xbuf.at[s, pl.ds(0, MOE_ROWS * SUBLANES), :], sem.at[s]).wait()

    last = n_valid - 1

    @pl.when(i == 0)
    def _():
        for s in range(GATHER_SLOTS):
            xbuf[s, pl.ds(MOE_ROWS * SUBLANES, SUBLANES), :] = jnp.zeros((SUBLANES, LANES), F32)
        for cp in weight_copies(be_ref[0]):
            cp.start()
        for d in range(GATHER_AHEAD):
            gather(jnp.minimum(d, last), d)

    @pl.when(i < n_valid)
    def _():
        expert = be_ref[i]
        prev = be_ref[jnp.maximum(i - 1, 0)]

        @pl.when((i == 0) | (expert != prev))
        def _():
            for cp in weight_copies(expert):
                cp.wait()
            wu_bf[...] = wu_f32[...].astype(BF16)
            wd_bf[...] = wd_f32[...].astype(BF16)
            nxt = nxt_ref[expert]

            @pl.when(nxt != expert)
            def _():
                for cp in weight_copies(nxt):
                    cp.start()

        gather_wait(slot)
        xb = _load_token_tiles(xbuf.at[slot], 0, MOE_ROWS).astype(BF16)
        nxt_blk = jnp.minimum(i + GATHER_AHEAD, last)
        nxt_slot = (i + GATHER_AHEAD) % GATHER_SLOTS
        group = MOE_ROWS // (2 * FF_CHUNKS)
        issued = [0]

        def issue_group():
            gather(nxt_blk, nxt_slot, issued[0] * group, (issued[0] + 1) * group)
            issued[0] += 1
            spare = xbuf[nxt_slot, pl.ds(MOE_ROWS * SUBLANES, 1), :]
            bits = pltpu.bitcast(spare, jnp.uint32)
            tied = pltpu.bitcast((bits >> 16) >> 16, F32)
            return jnp.concatenate([tied] * (FF_COLS // LANES), axis=1)

        acts = []
        tie = jnp.zeros((1, FF_COLS), F32)
        for c in range(FF_CHUNKS):
            cg = slice(c * FF_COLS, (c + 1) * FF_COLS)
            cl = slice(D_FF + c * FF_COLS, D_FF + (c + 1) * FF_COLS)
            g = jnp.dot(xb, wu_bf[:, cg], preferred_element_type=F32) + (bu_ref[0, :, cg] + tie)
            l = jnp.dot(xb, wu_bf[:, cl], preferred_element_type=F32) + bu_ref[0, :, cl]
            x_glu = jnp.minimum(g, SWIGLU_LIMIT)
            x_lin = jnp.clip(l, -SWIGLU_LIMIT, SWIGLU_LIMIT)
            acts.append((x_glu * jax.nn.sigmoid(SWIGLU_ALPHA * x_glu) * (x_lin + 1.0)).astype(BF16))
            tie = issue_group()
        act = jnp.concatenate(acts, axis=1)
        for c in range(FF_CHUNKS):
            cs = slice(c * FF_COLS, (c + 1) * FF_COLS)
            y = jnp.dot(act, wd_bf[:, cs], preferred_element_type=F32) + (bd_ref[0, :, cs] + tie)
            for u in range(FF_COLS // LANES):
                s = c * (FF_COLS // LANES) + u
                ys_ref[pl.ds(s, MOE_ROWS, stride=SUBLANES), :] = y[:, u * LANES:(u + 1) * LANES]
            tie = issue_group()

    @pl.when(i == last)
    def _():
        for d in range(1, GATHER_SLOTS):
            gather_wait((i + d) % GATHER_SLOTS)

    @pl.when(i >= n_valid)
    def _():
        ys_ref[...] = jnp.zeros(ys_ref.shape, F32)


def _moe(block_e, n_valid, next_e, row_tok, h2, w_up, b_up, w_down, b_down):
    n_pad = row_tok.shape[0]
    n_blocks = n_pad // MOE_ROWS
    out_rows = lambda i, be, nv, nx, rt: (i, 0)
    expert = lambda i, be, nv, nx, rt: (be[i], 0, 0)
    return pl.pallas_call(
        _moe_kernel,
        grid_spec=pltpu.PrefetchScalarGridSpec(
            num_scalar_prefetch=4,
            grid=(n_blocks,),
            in_specs=[pl.BlockSpec(memory_space=pl.ANY),
                      pl.BlockSpec(memory_space=pl.ANY),
                      pl.BlockSpec((1, 1, 2 * D_FF), expert),
                      pl.BlockSpec(memory_space=pl.ANY),
                      pl.BlockSpec((1, 1, D_MODEL), expert)],
            out_specs=pl.BlockSpec((MOE_ROWS * SUBLANES, LANES), out_rows),
            scratch_shapes=[pltpu.VMEM((D_MODEL, 2 * D_FF), F32),
                            pltpu.VMEM((D_FF, D_MODEL), F32),
                            pltpu.VMEM((D_MODEL, 2 * D_FF), BF16),
                            pltpu.VMEM((D_FF, D_MODEL), BF16),
                            pltpu.VMEM((GATHER_SLOTS, (MOE_ROWS + 1) * SUBLANES, LANES), F32),
                            pltpu.SemaphoreType.DMA((GATHER_SLOTS,)),
                            pltpu.SemaphoreType.DMA((2,))]),
        out_shape=jax.ShapeDtypeStruct((n_pad * SUBLANES, LANES), F32),
        compiler_params=pltpu.CompilerParams(dimension_semantics=("arbitrary",),
                                             vmem_limit_bytes=VMEM_LIMIT),
        name="moe",
    )(block_e, n_valid, next_e, row_tok, h2, w_up, b_up, w_down, b_down)


def _combine_kernel(dcur_ref, dnext_ref, ys_ref, x1_ref, gate_ref, mod_ref, o_ref, buf, sem):
    i = pl.program_id(0)
    tm = x1_ref.shape[0]
    slot = i % 2

    def issue(dref, s):
        def body(tk, carry):
            for kk in range(TOP_K):
                _token_copy(ys_ref, dref[0, 0, kk * tm + tk], buf.at[s], kk * tm + tk,
                            sem.at[s]).start(priority=kk % 2)
            return carry

        lax.fori_loop(0, tm, body, 0, unroll=4)

    @pl.when(i == 0)
    def _():
        issue(dcur_ref, 0)

    @pl.when(i + 1 < pl.num_programs(0))
    def _():
        issue(dnext_ref, 1 - slot)

    pltpu.make_async_copy(ys_ref.at[pl.ds(0, TOP_K * tm * SUBLANES), :], buf.at[slot],
                          sem.at[slot]).wait()

    gates = gate_ref[...]
    acc = jnp.zeros((tm, D_MODEL), F32)
    for kk in range(TOP_K):
        acc = acc + gates[:, kk:kk + 1] * _load_token_tiles(buf.at[slot], kk * tm, tm)
    o_ref[...] = x1_ref[...] + mod_ref[0, 5:6, :] * acc


def _combine(dest3, ys, x1, gates, mod, seq):
    t = x1.shape[0]
    tiles_per_seq = seq // TM_COMB
    n_tiles = t // TM_COMB
    row = lambda i: (i, 0)
    return pl.pallas_call(
        _combine_kernel,
        grid=(n_tiles,),
        in_specs=[pl.BlockSpec((1, 1, TM_COMB * TOP_K), lambda i: (i, 0, 0),
                               memory_space=pltpu.SMEM),
                  pl.BlockSpec((1, 1, TM_COMB * TOP_K),
                               lambda i: (jnp.minimum(i + 1, n_tiles - 1), 0, 0),
                               memory_space=pltpu.SMEM),
                  pl.BlockSpec(memory_space=pl.ANY),
                  pl.BlockSpec((TM_COMB, D_MODEL), row),
                  pl.BlockSpec((TM_COMB, LANES), row),
                  pl.BlockSpec((1, 6, D_MODEL), lambda i: (i // tiles_per_seq, 0, 0))],
        out_specs=pl.BlockSpec((TM_COMB, D_MODEL), row),
        out_shape=jax.ShapeDtypeStruct((t, D_MODEL), F32),
        scratch_shapes=[pltpu.VMEM((2, TOP_K * TM_COMB * SUBLANES, LANES), F32),
                        pltpu.SemaphoreType.DMA((2,))],
        compiler_params=pltpu.CompilerParams(dimension_semantics=("arbitrary",),
                                             vmem_limit_bytes=VMEM_LIMIT),
        name="combine",
    )(dest3, dest3, ys, x1, gates, mod)


def _layer(x, mod, norm1_w, w_in, conv_w, a_log, dt_bias, dn_norm_w, q_norm_w, k_norm_w, sinks,
           w_out, norm2_w, w_router, b_router, w_up, b_up, w_down, b_down):
    batch, seq, _ = x.shape
    t = batch * seq
    xf = x.reshape(t, D_MODEL)

    o_qkv, o_z, o_a, o_b, o_q, o_k, o_v = 0, 1536, 2048, 2052, 2056, 2568, 2696
    w_ab = jnp.zeros((D_MODEL, LANES), F32).at[:, 0:2 * DN_HEADS].set(w_in[:, o_a:o_q])
    w_cat = jnp.concatenate([w_in[:, o_qkv:o_z], w_in[:, o_z:o_a], w_in[:, o_q:o_k],
                             w_in[:, o_k:o_v], w_in[:, o_v:], w_ab], axis=1).astype(BF16)
    gparams = jnp.zeros((2, LANES), F32)
    gparams = gparams.at[0, 0:DN_HEADS].set(-jnp.exp(a_log.astype(F32)))
    gparams = gparams.at[1, 0:DN_HEADS].set(dt_bias.astype(F32))
    qnw128 = jnp.tile(q_norm_w.astype(F32), 2)[None, :]
    knw128 = jnp.tile(k_norm_w.astype(F32), 2)[None, :]

    q, k, v, z, gb, swq, swk, swv = _inproj(xf, mod, norm1_w[None, :], w_cat, conv_w, gparams,
                                            qnw128, knw128, seq)
    o_dn, o_sw = _mixer(q, k, v, z, gb, dn_norm_w[None, :], swq, swk, swv, sinks.astype(F32),
                        batch, seq)

    wr_pad = jnp.zeros((D_MODEL, LANES), F32).at[:, 0:N_EXPERTS].set(w_router).astype(BF16)
    br_pad = jnp.zeros((1, LANES), F32).at[0, 0:N_EXPERTS].set(b_router)
    x1, h2, top_idx, gates, rank, cnt = _post(xf, o_dn, o_sw, mod, w_out.astype(BF16),
                                              norm2_w[None, :], wr_pad, br_pad, seq)

    counts = cnt[0, 0:N_EXPERTS]
    padded = (counts + MOE_ROWS - 1) // MOE_ROWS * MOE_ROWS
    pend = jnp.cumsum(padded)
    pstart = pend - padded
    n_blocks = (t * TOP_K) // MOE_ROWS + N_EXPERTS
    n_pad = n_blocks * MOE_ROWS
    n_valid = (pend[-1] // MOE_ROWS).astype(I32)
    blk = jnp.minimum(jnp.arange(n_blocks, dtype=I32), n_valid - 1)
    block_e = jnp.minimum(jnp.sum(pend[None, :] <= (blk * MOE_ROWS)[:, None], axis=1),
                          N_EXPERTS - 1).astype(I32)
    experts = jnp.arange(N_EXPERTS, dtype=I32)[:, None, None]
    dest = (jnp.sum(jnp.where(top_idx[None, 0:TOP_K] == experts, pstart[:, None, None], 0), axis=0)
            + rank[0:TOP_K]).astype(I32)
    fill = jnp.stack([jnp.append(pstart + counts, pend[-1]),
                      jnp.append(pend, n_pad)]).astype(I32)

    eids = jnp.arange(N_EXPERTS, dtype=I32)
    later = (eids[None, :] > eids[:, None]) & (counts[None, :] > 0)
    next_e = jnp.min(jnp.where(later, eids[None, :], N_EXPERTS), axis=1)
    next_e = jnp.where(next_e == N_EXPERTS, eids, next_e).astype(I32)

    row_tok = _invert(fill, dest.reshape(TOP_K * t), n_pad)
    ys = _moe(block_e, n_valid.reshape(1), next_e, row_tok, h2, w_up, b_up[:, None, :], w_down,
              b_down[:, None, :])
    n_tiles = t // TM_COMB
    dest_tiles = dest.reshape(TOP_K, n_tiles, TM_COMB).transpose(1, 0, 2).reshape(
        n_tiles, 1, TOP_K * TM_COMB)
    out = _combine(dest_tiles, ys, x1, gates, mod, seq)
    return out.reshape(batch, seq, D_MODEL)


def kernel(x, c, w_ada, b_ada, norm1_w, w_in, conv_w, a_log, dt_bias, dn_norm_w, q_norm_w,
           k_norm_w, sinks, w_out, norm2_w, w_router, b_router, w_up, b_up, w_down, b_down):
    out_dtype = x.dtype
    batch = x.shape[0]
    depth = w_ada.shape[0]
    c_pad = jnp.zeros((SUBLANES, D_MODEL), F32).at[0:batch].set(c.astype(F32))
    for l in range(depth):
        mod = _ada(c_pad, w_ada[l], b_ada[l][None, :])[0:batch].reshape(batch, 6, D_MODEL)
        x = _layer(x, mod, norm1_w[l], w_in[l], conv_w[l], a_log[l], dt_bias[l], dn_norm_w[l],
                   q_norm_w[l], k_norm_w[l], sinks[l], w_out[l], norm2_w[l], w_router[l],
                   b_router[l], w_up[l], b_up[l], w_down[l], b_down[l])
    return x.astype(out_dtype)
```

```python
import functools

import jax
import jax.numpy as jnp
from jax import lax
from jax.experimental import pallas as pl
from jax.experimental.pallas import tpu as pltpu

F32 = jnp.float32
BF16 = jnp.bfloat16
I32 = jnp.int32

D_MODEL = 1024
DN_HEADS = 4
DN_HEAD_DIM = 128
DN_WIDTH = DN_HEADS * DN_HEAD_DIM
CONV_WIDTH = 4
CHUNK = 64
SWA_HEADS = 8
SWA_KV_HEADS = 2
SWA_GROUP = SWA_HEADS // SWA_KV_HEADS
SWA_HEAD_DIM = 64
SWA_WIDTH = SWA_HEADS * SWA_HEAD_DIM
SWA_KV_WIDTH = SWA_KV_HEADS * SWA_HEAD_DIM
WINDOW = 128
N_EXPERTS = 32
TOP_K = 4
D_FF = D_MODEL
SWIGLU_ALPHA = 1.702
SWIGLU_LIMIT = 7.0
EPS = 1e-6
NEG = -1e30

LANES = 128
SUBLANES = 8
VMEM_LIMIT = 56 * 1024 * 1024

C_QKV = 0
C_Z = C_QKV + 3 * DN_WIDTH
C_SWQ = C_Z + DN_WIDTH
C_SWK = C_SWQ + SWA_WIDTH
C_SWV = C_SWK + SWA_KV_WIDTH
C_AB = C_SWV + SWA_KV_WIDTH
IN_PAD = C_AB + LANES

TM_IN = 512
DN_ROWS = 256
TM_POST = 512
MOE_ROWS = 256
TM_COMB = 128


def _silu(x):
    return x * jax.nn.sigmoid(x)


def _mm(a, b):
    return jnp.dot(a.astype(BF16), b.astype(BF16), preferred_element_type=F32)


def _mm_nt(a, b):
    return lax.dot_general(a.astype(BF16), b.astype(BF16), (((1,), (1,)), ((), ())),
                           preferred_element_type=F32)


def _mm_tn(a, b):
    return lax.dot_general(a.astype(BF16), b.astype(BF16), (((0,), (0,)), ((), ())),
                           preferred_element_type=F32)


def _load_token_tiles(ref, row0, rows):
    return jnp.concatenate(
        [ref[pl.ds(row0 * SUBLANES + s, rows, stride=SUBLANES), :] for s in range(D_MODEL // LANES)],
        axis=1)


def _store_token_tiles(ref, val):
    rows = val.shape[0]
    for s in range(D_MODEL // LANES):
        ref[pl.ds(s, rows, stride=SUBLANES), :] = val[:, s * LANES:(s + 1) * LANES]


def _token_copy(src_ref, s, dst_ref, d, sem):
    return pltpu.make_async_copy(
        src_ref.at[pl.ds(pl.multiple_of(s * SUBLANES, SUBLANES), SUBLANES), :],
        dst_ref.at[pl.ds(pl.multiple_of(d * SUBLANES, SUBLANES), SUBLANES), :], sem)


def _ada_kernel(c_ref, w_ref, b_ref, o_ref):
    c = c_ref[...]
    o_ref[...] = jnp.dot(_silu(c), w_ref[...], precision=lax.Precision.HIGHEST,
                         preferred_element_type=F32) + b_ref[...]


def _ada(c_pad, w_ada, b_ada):
    n = w_ada.shape[1]
    return pl.pallas_call(
        _ada_kernel,
        grid=(n // D_MODEL,),
        in_specs=[pl.BlockSpec((SUBLANES, D_MODEL), lambda j: (0, 0)),
                  pl.BlockSpec((D_MODEL, D_MODEL), lambda j: (0, j)),
                  pl.BlockSpec((1, D_MODEL), lambda j: (0, j))],
        out_specs=pl.BlockSpec((SUBLANES, D_MODEL), lambda j: (0, j)),
        out_shape=jax.ShapeDtypeStruct((SUBLANES, n), F32),
        name="ada",
    )(c_pad, w_ada, b_ada)


def _rms_heads64(a, w128):
    lane = lax.broadcasted_iota(I32, a.shape, 1)
    lo = lane < SWA_HEAD_DIM
    sq = a * a
    s_lo = jnp.sum(jnp.where(lo, sq, 0.0), axis=-1, keepdims=True)
    s_hi = jnp.sum(jnp.where(lo, 0.0, sq), axis=-1, keepdims=True)
    ms = jnp.where(lo, s_lo, s_hi) * (1.0 / SWA_HEAD_DIM)
    return a * lax.rsqrt(ms + EPS) * w128


def _inproj_kernel(tiles_per_seq, x_ref, mod_ref, n1w_ref, w_ref, convw_ref, gp_ref, qnw_ref,
                   knw_ref, q_ref, k_ref, v_ref, z_ref, gb_ref, swq_ref, swk_ref, swv_ref,
                   conv_scr):
    i = pl.program_id(0)
    tm = x_ref.shape[0]
    x = x_ref[...]
    y = x * lax.rsqrt(jnp.mean(x * x, axis=-1, keepdims=True) + EPS) * n1w_ref[...]
    h = (y * (1.0 + mod_ref[0, 1:2, :]) + mod_ref[0, 0:1, :]).astype(BF16)

    @pl.when(i % tiles_per_seq == 0)
    def _():
        conv_scr[0:SUBLANES, :] = jnp.zeros((SUBLANES, 3 * DN_WIDTH), F32)

    u = jnp.dot(h, w_ref[:, C_QKV:C_Z], preferred_element_type=F32)
    conv_scr[SUBLANES:SUBLANES + tm, :] = u
    cw = convw_ref[...]
    acc = u * cw[3:4, :]
    for j in range(CONV_WIDTH - 1):
        off = SUBLANES - (CONV_WIDTH - 1) + j
        acc = acc + conv_scr[off:off + tm, :] * cw[j:j + 1, :]
    conv_scr[0:SUBLANES, :] = conv_scr[tm:tm + SUBLANES, :]
    qkv = _silu(acc)

    for hh in range(DN_HEADS):
        sl = slice(hh * DN_HEAD_DIM, (hh + 1) * DN_HEAD_DIM)
        qh = qkv[:, hh * DN_HEAD_DIM:(hh + 1) * DN_HEAD_DIM]
        kh = qkv[:, DN_WIDTH + hh * DN_HEAD_DIM:DN_WIDTH + (hh + 1) * DN_HEAD_DIM]
        q_ref[:, sl] = qh * lax.rsqrt(jnp.sum(qh * qh, axis=-1, keepdims=True) + EPS) * (
            DN_HEAD_DIM ** -0.5)
        k_ref[:, sl] = kh * lax.rsqrt(jnp.sum(kh * kh, axis=-1, keepdims=True) + EPS)
    v_ref[...] = qkv[:, 2 * DN_WIDTH:3 * DN_WIDTH]

    z_ref[...] = jnp.dot(h, w_ref[:, C_Z:C_SWQ], preferred_element_type=F32)

    ab = jnp.dot(h, w_ref[:, C_AB:IN_PAD], preferred_element_type=F32)
    lane = lax.broadcasted_iota(I32, ab.shape, 1)
    sp_in = ab + gp_ref[1:2, :]
    softplus = jnp.maximum(sp_in, 0.0) + jnp.log1p(jnp.exp(-jnp.abs(sp_in)))
    gb_ref[...] = jnp.where(lane < DN_HEADS, gp_ref[0:1, :] * softplus, jax.nn.sigmoid(ab))

    sq = jnp.dot(h, w_ref[:, C_SWQ:C_SWK], preferred_element_type=F32)
    for t in range(SWA_WIDTH // LANES):
        sl = slice(t * LANES, (t + 1) * LANES)
        swq_ref[:, sl] = (_rms_heads64(sq[:, sl], qnw_ref[...]) * (SWA_HEAD_DIM ** -0.5)).astype(BF16)
    sk = jnp.dot(h, w_ref[:, C_SWK:C_SWV], preferred_element_type=F32)
    swk_ref[...] = _rms_heads64(sk, knw_ref[...]).astype(BF16)
    swv_ref[...] = jnp.dot(h, w_ref[:, C_SWV:C_AB], preferred_element_type=F32).astype(BF16)


def _inproj(xf, mod, n1w, w_cat, conv_w, gparams, qnw128, knw128, seq):
    t = xf.shape[0]
    tiles_per_seq = seq // TM_IN
    row = lambda i: (i, 0)
    const = lambda i: (0, 0)
    outs = [jax.ShapeDtypeStruct((t, DN_WIDTH), F32)] * 4 + [
        jax.ShapeDtypeStruct((t, LANES), F32),
        jax.ShapeDtypeStruct((t, SWA_WIDTH), BF16),
        jax.ShapeDtypeStruct((t, SWA_KV_WIDTH), BF16),
        jax.ShapeDtypeStruct((t, SWA_KV_WIDTH), BF16)]
    out_specs = [pl.BlockSpec((TM_IN, DN_WIDTH), row)] * 4 + [
        pl.BlockSpec((TM_IN, LANES), row),
        pl.BlockSpec((TM_IN, SWA_WIDTH), row),
        pl.BlockSpec((TM_IN, SWA_KV_WIDTH), row),
        pl.BlockSpec((TM_IN, SWA_KV_WIDTH), row)]
    return pl.pallas_call(
        functools.partial(_inproj_kernel, tiles_per_seq),
        grid=(t // TM_IN,),
        in_specs=[pl.BlockSpec((TM_IN, D_MODEL), row),
                  pl.BlockSpec((1, 6, D_MODEL), lambda i: (i // tiles_per_seq, 0, 0)),
                  pl.BlockSpec((1, D_MODEL), const),
                  pl.BlockSpec((D_MODEL, IN_PAD), const),
                  pl.BlockSpec((CONV_WIDTH, 3 * DN_WIDTH), const),
                  pl.BlockSpec((2, LANES), const),
                  pl.BlockSpec((1, LANES), const),
                  pl.BlockSpec((1, LANES), const)],
        out_specs=out_specs,
        out_shape=outs,
        scratch_shapes=[pltpu.VMEM((TM_IN + SUBLANES, 3 * DN_WIDTH), F32)],
        compiler_params=pltpu.CompilerParams(dimension_semantics=("arbitrary",),
                                             vmem_limit_bytes=VMEM_LIMIT),
        name="inproj",
    )(xf, mod, n1w, w_cat, conv_w, gparams, qnw128, knw128)


def _dn_block(j, q_ref, k_ref, v_ref, z_ref, gb_ref, nw_ref, o_ref,
              s_scr, u_scr, w_scr, qd_scr, kd_scr, a_scr):
    rows = q_ref.shape[0]
    n_chunks = rows // CHUNK
    heads = range(DN_HEADS)

    @pl.when(j == 0)
    def _():
        s_scr[...] = jnp.zeros(s_scr.shape, F32)

    r = lax.broadcasted_iota(I32, (rows, rows), 0)
    c = lax.broadcasted_iota(I32, (rows, rows), 1)
    same = (r // CHUNK) == (c // CHUNK)
    incl = same & (c <= r)
    strict = same & (c < r)
    gb = gb_ref[...]
    gc = jnp.dot(jnp.where(incl, 1.0, 0.0), gb, precision=lax.Precision.HIGHEST,
                 preferred_element_type=F32)
    gl = jnp.dot(jnp.where(same, 1.0, 0.0), gb, precision=lax.Precision.HIGHEST,
                 preferred_element_type=F32)
    gct = gc.T

    rmat = []
    rhs = []
    for h in heads:
        ls = slice(h * DN_HEAD_DIM, (h + 1) * DN_HEAD_DIM)
        g_col = gc[:, h:h + 1]
        beta = gb[:, DN_HEADS + h:DN_HEADS + h + 1]
        decay = jnp.where(incl, jnp.exp(jnp.where(incl, g_col - gct[h:h + 1, :], 0.0)), 0.0)
        qq = q_ref[:, ls]
        kk = k_ref[:, ls]
        kb = kk * beta
        eg = jnp.exp(g_col)
        rmat.append(-jnp.where(strict, _mm_nt(kb, kk) * decay, 0.0))
        rhs.append(jnp.concatenate([v_ref[:, ls] * beta, kb * eg], axis=1))
        a_scr[h] = jnp.where(incl, _mm_nt(qq, kk) * decay, 0.0)
        qd_scr[:, ls] = qq * eg
        kd_scr[:, ls] = kk * jnp.exp(gl[:, h:h + 1] - g_col)

    xm = list(rmat)
    pm = list(rmat)
    for it in range(6):
        xm = [xm[h] + _mm(pm[h], xm[h]) for h in heads]
        if it < 5:
            pm = [_mm(pm[h], pm[h]) for h in heads]
    for h in heads:
        ls = slice(h * DN_HEAD_DIM, (h + 1) * DN_HEAD_DIM)
        sol = rhs[h] + _mm(xm[h], rhs[h])
        u_scr[:, ls] = sol[:, 0:DN_HEAD_DIM]
        w_scr[:, ls] = sol[:, DN_HEAD_DIM:2 * DN_HEAD_DIM]

    nw = nw_ref[...]
    lsl = [slice(h * DN_HEAD_DIM, (h + 1) * DN_HEAD_DIM) for h in heads]
    for ci in range(n_chunks):
        rs = slice(ci * CHUNK, (ci + 1) * CHUNK)
        dvals = jnp.exp(gl[ci * CHUNK:ci * CHUNK + 1, :])
        st = [s_scr[h] for h in heads]
        ws = [_mm(w_scr[rs, lsl[h]], st[h]) for h in heads]
        qs = [_mm(qd_scr[rs, lsl[h]], st[h]) for h in heads]
        vn = [u_scr[rs, lsl[h]] - ws[h] for h in heads]
        av = [_mm(a_scr[h, rs, rs], vn[h]) for h in heads]
        kv = [_mm_tn(kd_scr[rs, lsl[h]], vn[h]) for h in heads]
        for h in heads:
            s_scr[h] = st[h] * dvals[:, h:h + 1] + kv[h]
            o = qs[h] + av[h]
            on = o * lax.rsqrt(jnp.mean(o * o, axis=-1, keepdims=True) + EPS) * nw
            o_ref[rs, lsl[h]] = (on * _silu(z_ref[rs, lsl[h]])).astype(o_ref.dtype)


def _swa_window(sink_ref, q, kcat, vcat, first):
    w = WINDOW
    rows = SWA_GROUP * w
    ri = lax.broadcasted_iota(I32, (rows, 2 * w), 0)
    kj = lax.broadcasted_iota(I32, (rows, 2 * w), 1)
    grp = ri // w
    rel = (ri - grp * w) + w - kj
    valid = (rel >= 0) & (rel < w)
    if first is not None:
        valid = valid & ((kj >= w) | jnp.logical_not(first))
    relf = rel.astype(F32)
    grp_col = lax.broadcasted_iota(I32, (rows, 1), 0) // w
    outs = []
    for kh in range(SWA_KV_HEADS):
        slope_col = jnp.zeros((rows, 1), F32)
        sink_col = jnp.zeros((rows, 1), F32)
        for g in range(SWA_GROUP):
            hd = kh * SWA_GROUP + g
            slope_col = jnp.where(grp_col == g, 2.0 ** (-8.0 * (hd + 1) / SWA_HEADS), slope_col)
            sink_col = jnp.where(grp_col == g, sink_ref[hd], sink_col)
        qs = jnp.concatenate(
            [q[:, (kh * SWA_GROUP + g) * SWA_HEAD_DIM:(kh * SWA_GROUP + g + 1) * SWA_HEAD_DIM]
             for g in range(SWA_GROUP)], axis=0)
        kk = kcat[:, kh * SWA_HEAD_DIM:(kh + 1) * SWA_HEAD_DIM]
        vv = vcat[:, kh * SWA_HEAD_DIM:(kh + 1) * SWA_HEAD_DIM]
        s = lax.dot_general(qs, kk, (((1,), (1,)), ((), ())), preferred_element_type=F32)
        logits = jnp.where(valid, s - slope_col * relf, NEG)
        m = jnp.maximum(jnp.max(logits, axis=-1, keepdims=True), sink_col)
        p = jnp.exp(logits - m)
        denom = jnp.sum(p, axis=-1, keepdims=True) + jnp.exp(sink_col - m)
        o = jnp.dot(p.astype(BF16), vv, preferred_element_type=F32) / denom
        for g in range(SWA_GROUP):
            outs.append(o[g * w:(g + 1) * w, :])
    return [jnp.concatenate([outs[2 * t], outs[2 * t + 1]], axis=-1)
            for t in range(SWA_WIDTH // LANES)]


def _mixer_kernel(sink_ref, q_ref, k_ref, v_ref, z_ref, gb_ref, nw_ref,
                  swq_ref, kp_ref, kc_ref, vp_ref, vc_ref, odn_ref, osw_ref,
                  s_scr, u_scr, w_scr, qd_scr, kd_scr, a_scr):
    j = pl.program_id(1)
    _dn_block(j, q_ref, k_ref, v_ref, z_ref, gb_ref, nw_ref, odn_ref,
              s_scr, u_scr, w_scr, qd_scr, kd_scr, a_scr)
    w = WINDOW
    kcur = kc_ref[...]
    vcur = vc_ref[...]
    for s in range(DN_ROWS // w):
        rows = slice(s * w, (s + 1) * w)
        if s == 0:
            kcat = jnp.concatenate([kp_ref[...], kcur[0:w, :]], axis=0)
            vcat = jnp.concatenate([vp_ref[...], vcur[0:w, :]], axis=0)
            first = j == 0
        else:
            kcat = kcur[(s - 1) * w:(s + 1) * w, :]
            vcat = vcur[(s - 1) * w:(s + 1) * w, :]
            first = None
        tiles = _swa_window(sink_ref, swq_ref[rows, :], kcat, vcat, first)
        for t, tile in enumerate(tiles):
            osw_ref[rows, t * LANES:(t + 1) * LANES] = tile.astype(osw_ref.dtype)


def _mixer(q, k, v, z, gb, nw, swq, swk, swv, sinks, batch, seq):
    t = q.shape[0]
    steps = seq // DN_ROWS
    per_step = DN_ROWS // WINDOW
    row = lambda b, j, s: (b * steps + j, 0)
    prev = lambda b, j, s: (b * steps * per_step + jnp.maximum(j * per_step - 1, 0), 0)
    blk = pl.BlockSpec((DN_ROWS, DN_WIDTH), row)
    kv_cur = pl.BlockSpec((DN_ROWS, SWA_KV_WIDTH), row)
    kv_prev = pl.BlockSpec((WINDOW, SWA_KV_WIDTH), prev)
    return pl.pallas_call(
        _mixer_kernel,
        grid_spec=pltpu.PrefetchScalarGridSpec(
            num_scalar_prefetch=1,
            grid=(batch, steps),
            in_specs=[blk, blk, blk, blk,
                      pl.BlockSpec((DN_ROWS, LANES), row),
                      pl.BlockSpec((1, DN_HEAD_DIM), lambda b, j, s: (0, 0)),
                      pl.BlockSpec((DN_ROWS, SWA_WIDTH), row),
                      kv_prev, kv_cur, kv_prev, kv_cur],
            out_specs=[blk, pl.BlockSpec((DN_ROWS, SWA_WIDTH), row)],
            scratch_shapes=[pltpu.VMEM((DN_HEADS, DN_HEAD_DIM, DN_HEAD_DIM), F32),
                            pltpu.VMEM((DN_ROWS, DN_WIDTH), F32),
                            pltpu.VMEM((DN_ROWS, DN_WIDTH), F32),
                            pltpu.VMEM((DN_ROWS, DN_WIDTH), F32),
                            pltpu.VMEM((DN_ROWS, DN_WIDTH), F32),
                            pltpu.VMEM((DN_HEADS, DN_ROWS, DN_ROWS), F32)]),
        out_shape=[jax.ShapeDtypeStruct((t, DN_WIDTH), BF16),
                   jax.ShapeDtypeStruct((t, SWA_WIDTH), BF16)],
        compiler_params=pltpu.CompilerParams(dimension_semantics=("arbitrary", "arbitrary"),
                                             vmem_limit_bytes=VMEM_LIMIT),
        name="mixer",
    )(sinks, q, k, v, z, gb, nw, swq, swk, swk, swv, swv)


def _post_kernel(x_ref, odn_ref, osw_ref, mod_ref, wo_ref, n2w_ref, wr_ref, br_ref,
                 x1_ref, h2_ref, idx_ref, gate_ref, rank_ref, cnt_ref, run_scr):
    i = pl.program_id(0)
    tm = x_ref.shape[0]

    @pl.when(i == 0)
    def _():
        run_scr[...] = jnp.zeros(run_scr.shape, F32)

    y = (jnp.dot(odn_ref[...], wo_ref[0:DN_WIDTH, :], preferred_element_type=F32)
         + jnp.dot(osw_ref[...], wo_ref[DN_WIDTH:, :], preferred_element_type=F32))
    x1 = x_ref[...] + mod_ref[0, 2:3, :] * y
    x1_ref[...] = x1
    hn = x1 * lax.rsqrt(jnp.mean(x1 * x1, axis=-1, keepdims=True) + EPS) * n2w_ref[...]
    h2 = hn * (1.0 + mod_ref[0, 4:5, :]) + mod_ref[0, 3:4, :]
    _store_token_tiles(h2_ref, h2)

    lane = lax.broadcasted_iota(I32, (tm, LANES), 1)
    lane_f = lane.astype(F32)
    logits = jnp.dot(h2.astype(BF16), wr_ref[...], preferred_element_type=F32) + br_ref[...]
    work = jnp.where(lane < N_EXPERTS, logits, NEG)
    sel = jnp.zeros((tm, LANES), F32)
    idx_out = jnp.zeros((tm, LANES), F32)
    vals = []
    idxs = []
    for kk in range(TOP_K):
        m = jnp.max(work, axis=-1, keepdims=True)
        ik = jnp.min(jnp.where(work == m, lane_f, float(LANES)), axis=-1, keepdims=True)
        hit = lane_f == ik
        work = jnp.where(hit, NEG, work)
        sel = jnp.where(hit, 1.0, sel)
        idx_out = jnp.where(lane == kk, ik, idx_out)
        vals.append(m)
        idxs.append(ik)
    es = [jnp.exp(v - vals[0]) for v in vals]
    tot = es[0] + es[1] + es[2] + es[3]
    gate_out = jnp.zeros((tm, LANES), F32)
    for kk in range(TOP_K):
        gate_out = jnp.where(lane == kk, es[kk] / tot, gate_out)
    gate_ref[...] = gate_out

    r = lax.broadcasted_iota(I32, (tm, tm), 0)
    c = lax.broadcasted_iota(I32, (tm, tm), 1)
    before = jnp.where(c < r, 1.0, 0.0).astype(BF16)
    base = jnp.dot(before, sel.astype(BF16), preferred_element_type=F32) + run_scr[0:1, :]
    rank_out = jnp.zeros((tm, LANES), F32)
    for kk in range(TOP_K):
        rk = jnp.sum(jnp.where(lane_f == idxs[kk], base, 0.0), axis=-1, keepdims=True)
        rank_out = jnp.where(lane == kk, rk, rank_out)
    idx_ref[...] = idx_out.T[0:SUBLANES, :].astype(I32)
    rank_ref[...] = rank_out.T[0:SUBLANES, :].astype(I32)
    run = run_scr[0:1, :] + jnp.sum(sel, axis=0, keepdims=True)
    run_scr[0:1, :] = run
    cnt_ref[...] = jnp.broadcast_to(run, cnt_ref.shape).astype(I32)


def _post(xf, odn, osw, mod, w_out, n2w, wr_pad, br_pad, seq):
    t = xf.shape[0]
    tiles_per_seq = seq // TM_POST
    row = lambda i: (i, 0)
    const = lambda i: (0, 0)
    return pl.pallas_call(
        _post_kernel,
        grid=(t // TM_POST,),
        in_specs=[pl.BlockSpec((TM_POST, D_MODEL), row),
                  pl.BlockSpec((TM_POST, DN_WIDTH), row),
                  pl.BlockSpec((TM_POST, SWA_WIDTH), row),
                  pl.BlockSpec((1, 6, D_MODEL), lambda i: (i // tiles_per_seq, 0, 0)),
                  pl.BlockSpec((D_MODEL, D_MODEL), const),
                  pl.BlockSpec((1, D_MODEL), const),
                  pl.BlockSpec((D_MODEL, LANES), const),
                  pl.BlockSpec((1, LANES), const)],
        out_specs=[pl.BlockSpec((TM_POST, D_MODEL), row),
                   pl.BlockSpec((TM_POST * SUBLANES, LANES), row),
                   pl.BlockSpec((SUBLANES, TM_POST), lambda i: (0, i)),
                   pl.BlockSpec((TM_POST, LANES), row),
                   pl.BlockSpec((SUBLANES, TM_POST), lambda i: (0, i)),
                   pl.BlockSpec((SUBLANES, LANES), const)],
        out_shape=[jax.ShapeDtypeStruct((t, D_MODEL), F32),
                   jax.ShapeDtypeStruct((t * SUBLANES, LANES), F32),
                   jax.ShapeDtypeStruct((SUBLANES, t), I32),
                   jax.ShapeDtypeStruct((t, LANES), F32),
                   jax.ShapeDtypeStruct((SUBLANES, t), I32),
                   jax.ShapeDtypeStruct((SUBLANES, LANES), I32)],
        scratch_shapes=[pltpu.VMEM((SUBLANES, LANES), F32)],
        compiler_params=pltpu.CompilerParams(dimension_semantics=("arbitrary",),
                                             vmem_limit_bytes=VMEM_LIMIT),
        name="post",
    )(xf, odn, osw, mod, w_out, n2w, wr_pad, br_pad)


INVERT_GROUP = 16


def _invert_kernel(fill_ref, dest_ref, rowtok_ref):
    def per_range(e, carry):
        def clear(rw, c2):
            rowtok_ref[rw] = 0
            return c2

        lax.fori_loop(fill_ref[0, e], fill_ref[1, e], clear, 0)
        return carry

    lax.fori_loop(0, fill_ref.shape[1], per_range, 0)

    n_tok = dest_ref.shape[0] // TOP_K
    for kk in range(TOP_K):
        def put(g, carry, kk=kk):
            tok0 = g * INVERT_GROUP
            rows = [dest_ref[kk * n_tok + tok0 + u] for u in range(INVERT_GROUP)]
            for u in range(INVERT_GROUP):
                rowtok_ref[rows[u]] = tok0 + u
            return carry

        lax.fori_loop(0, n_tok // INVERT_GROUP, put, 0)


def _invert(fill, dest_flat, n_pad):
    return pl.pallas_call(
        _invert_kernel,
        in_specs=[pl.BlockSpec(memory_space=pltpu.SMEM),
                  pl.BlockSpec(memory_space=pltpu.SMEM)],
        out_specs=pl.BlockSpec(memory_space=pltpu.SMEM),
        out_shape=jax.ShapeDtypeStruct((n_pad,), I32),
        name="invert",
    )(fill, dest_flat)


GATHER_SLOTS = 8
GATHER_AHEAD = GATHER_SLOTS - 1
FF_CHUNKS = 4
FF_COLS = D_FF // FF_CHUNKS


def _moe_kernel(be_ref, nv_ref, nxt_ref, rowtok_ref, h_ref, wu_hbm, bu_hbm, wd_hbm, bd_hbm, ys_hbm,
                wu_f32, wd_f32, bu_in, bd_in, wu_bf, wd_bf, bu_ref, bd_ref, xbuf, ybuf,
                sem, wsem, ysem):
    i = pl.program_id(0)
    slot = i % GATHER_SLOTS
    yslot = i % 2
    n_valid = nv_ref[0]
    blk_rows = MOE_ROWS * SUBLANES

    def weight_copies(e):
        return (pltpu.make_async_copy(wu_hbm.at[e], wu_f32, wsem.at[0]),
                pltpu.make_async_copy(wd_hbm.at[e], wd_f32, wsem.at[1]),
                pltpu.make_async_copy(bu_hbm.at[e], bu_in, wsem.at[2]),
                pltpu.make_async_copy(bd_hbm.at[e], bd_in, wsem.at[3]))

    def result_copy(s, blk):
        return pltpu.make_async_copy(
            ybuf.at[s], ys_hbm.at[pl.ds(pl.multiple_of(blk * blk_rows, blk_rows), blk_rows), :],
            ysem.at[s])

    @pl.when(i >= 2)
    def _():
        result_copy(yslot, i - 2).wait()

    def gather(blk, s, r0=0, r1=MOE_ROWS):
        for r in range(r0, r1):
            _token_copy(h_ref, rowtok_ref[blk * MOE_ROWS + r], xbuf.at[s], r,
                        sem.at[s]).start(priority=r % 2)

    def gather_wait(s):
        pltpu.make_async_copy(h_ref.at[pl.ds(0, MOE_ROWS * SUBLANES), :],
                              xbuf.at[s, pl.ds(0, MOE_ROWS * SUBLANES), :], sem.at[s]).wait()

    last = n_valid - 1

    @pl.when(i == 0)
    def _():
        for s in range(GATHER_SLOTS):
            xbuf[s, pl.ds(MOE_ROWS * SUBLANES, SUBLANES), :] = jnp.zeros((SUBLANES, LANES), F32)
        for cp in weight_copies(be_ref[0]):
            cp.start()
        for d in range(GATHER_AHEAD):
            gather(jnp.minimum(d, last), d)

    @pl.when(i < n_valid)
    def _():
        expert = be_ref[i]
        prev = be_ref[jnp.maximum(i - 1, 0)]

        @pl.when((i == 0) | (expert != prev))
        def _():
            for cp in weight_copies(expert):
                cp.wait()
            wu_bf[...] = wu_f32[...].astype(BF16)
            wd_bf[...] = wd_f32[...].astype(BF16)
            bu_ref[...] = bu_in[...]
            bd_ref[...] = bd_in[...]
            nxt = nxt_ref[expert]

            @pl.when(nxt != expert)
            def _():
                for cp in weight_copies(nxt):
                    cp.start()

        gather_wait(slot)
        xb = _load_token_tiles(xbuf.at[slot], 0, MOE_ROWS).astype(BF16)
        nxt_blk = jnp.minimum(i + GATHER_AHEAD, last)
        nxt_slot = (i + GATHER_AHEAD) % GATHER_SLOTS
        group = MOE_ROWS // (2 * FF_CHUNKS)
        issued = [0]

        def issue_group():
            gather(nxt_blk, nxt_slot, issued[0] * group, (issued[0] + 1) * group)
            issued[0] += 1
            spare = xbuf[nxt_slot, pl.ds(MOE_ROWS * SUBLANES, 1), :]
            bits = pltpu.bitcast(spare, jnp.uint32)
            tied = pltpu.bitcast((bits >> 16) >> 16, F32)
            return jnp.concatenate([tied] * (FF_COLS // LANES), axis=1)

        acts = []
        tie = jnp.zeros((1, FF_COLS), F32)
        for c in range(FF_CHUNKS):
            cg = slice(c * FF_COLS, (c + 1) * FF_COLS)
            cl = slice(D_FF + c * FF_COLS, D_FF + (c + 1) * FF_COLS)
            g = jnp.dot(xb, wu_bf[:, cg], preferred_element_type=F32) + (bu_ref[:, cg] + tie)
            l = jnp.dot(xb, wu_bf[:, cl], preferred_element_type=F32) + bu_ref[:, cl]
            x_glu = jnp.minimum(g, SWIGLU_LIMIT)
            x_lin = jnp.clip(l, -SWIGLU_LIMIT, SWIGLU_LIMIT)
            acts.append((x_glu * jax.nn.sigmoid(SWIGLU_ALPHA * x_glu) * (x_lin + 1.0)).astype(BF16))
            tie = issue_group()
        act = jnp.concatenate(acts, axis=1)
        yout = ybuf.at[yslot]
        for c in range(FF_CHUNKS):
            cs = slice(c * FF_COLS, (c + 1) * FF_COLS)
            y = jnp.dot(act, wd_bf[:, cs], preferred_element_type=F32) + (bd_ref[:, cs] + tie)
            for u in range(FF_COLS // LANES):
                s = c * (FF_COLS // LANES) + u
                yout[pl.ds(s, MOE_ROWS, stride=SUBLANES), :] = y[:, u * LANES:(u + 1) * LANES]
            tie = issue_group()

    @pl.when(i == last)
    def _():
        for d in range(1, GATHER_SLOTS):
            gather_wait((i + d) % GATHER_SLOTS)

    @pl.when(i >= n_valid)
    def _():
        ybuf[yslot] = jnp.zeros((blk_rows, LANES), F32)

    result_copy(yslot, i).start()

    @pl.when(i == pl.num_programs(0) - 1)
    def _():
        result_copy(1 - yslot, i - 1).wait()
        result_copy(yslot, i).wait()


def _moe(block_e, n_valid, next_e, row_tok, h2, w_up, b_up, w_down, b_down):
    n_pad = row_tok.shape[0]
    n_blocks = n_pad // MOE_ROWS
    hbm = pl.BlockSpec(memory_space=pl.ANY)
    return pl.pallas_call(
        _moe_kernel,
        grid_spec=pltpu.PrefetchScalarGridSpec(
            num_scalar_prefetch=4,
            grid=(n_blocks,),
            in_specs=[hbm, hbm, hbm, hbm, hbm],
            out_specs=hbm,
            scratch_shapes=[pltpu.VMEM((D_MODEL, 2 * D_FF), F32),
                            pltpu.VMEM((D_FF, D_MODEL), F32),
                            pltpu.VMEM((1, 2 * D_FF), F32),
                            pltpu.VMEM((1, D_MODEL), F32),
                            pltpu.VMEM((D_MODEL, 2 * D_FF), BF16),
                            pltpu.VMEM((D_FF, D_MODEL), BF16),
                            pltpu.VMEM((1, 2 * D_FF), F32),
                            pltpu.VMEM((1, D_MODEL), F32),
                            pltpu.VMEM((GATHER_SLOTS, (MOE_ROWS + 1) * SUBLANES, LANES), F32),
                            pltpu.VMEM((2, MOE_ROWS * SUBLANES, LANES), F32),
                            pltpu.SemaphoreType.DMA((GATHER_SLOTS,)),
                            pltpu.SemaphoreType.DMA((4,)),
                            pltpu.SemaphoreType.DMA((2,))]),
        out_shape=jax.ShapeDtypeStruct((n_pad * SUBLANES, LANES), F32),
        compiler_params=pltpu.CompilerParams(dimension_semantics=("arbitrary",),
                                             vmem_limit_bytes=VMEM_LIMIT),
        name="moe",
    )(block_e, n_valid, next_e, row_tok, h2, w_up, b_up, w_down, b_down)


def _combine_kernel(dcur_ref, dnext_ref, ys_ref, x1_ref, gate_ref, mod_ref, o_ref, buf, sem):
    i = pl.program_id(0)
    tm = x1_ref.shape[0]
    slot = i % 2

    def issue(dref, s):
        def body(tk, carry):
            for kk in range(TOP_K):
                _token_copy(ys_ref, dref[0, 0, kk * tm + tk], buf.at[s], kk * tm + tk,
                            sem.at[s]).start(priority=kk % 2)
            return carry

        lax.fori_loop(0, tm, body, 0, unroll=4)

    @pl.when(i == 0)
    def _():
        issue(dcur_ref, 0)

    @pl.when(i + 1 < pl.num_programs(0))
    def _():
        issue(dnext_ref, 1 - slot)

    pltpu.make_async_copy(ys_ref.at[pl.ds(0, TOP_K * tm * SUBLANES), :], buf.at[slot],
                          sem.at[slot]).wait()

    gates = gate_ref[...]
    acc = jnp.zeros((tm, D_MODEL), F32)
    for kk in range(TOP_K):
        acc = acc + gates[:, kk:kk + 1] * _load_token_tiles(buf.at[slot], kk * tm, tm)
    o_ref[...] = x1_ref[...] + mod_ref[0, 5:6, :] * acc


def _combine(dest3, ys, x1, gates, mod, seq):
    t = x1.shape[0]
    tiles_per_seq = seq // TM_COMB
    n_tiles = t // TM_COMB
    row = lambda i: (i, 0)
    return pl.pallas_call(
        _combine_kernel,
        grid=(n_tiles,),
        in_specs=[pl.BlockSpec((1, 1, TM_COMB * TOP_K), lambda i: (i, 0, 0),
                               memory_space=pltpu.SMEM),
                  pl.BlockSpec((1, 1, TM_COMB * TOP_K),
                               lambda i: (jnp.minimum(i + 1, n_tiles - 1), 0, 0),
                               memory_space=pltpu.SMEM),
                  pl.BlockSpec(memory_space=pl.ANY),
                  pl.BlockSpec((TM_COMB, D_MODEL), row),
                  pl.BlockSpec((TM_COMB, LANES), row),
                  pl.BlockSpec((1, 6, D_MODEL), lambda i: (i // tiles_per_seq, 0, 0))],
        out_specs=pl.BlockSpec((TM_COMB, D_MODEL), row),
        out_shape=jax.ShapeDtypeStruct((t, D_MODEL), F32),
        scratch_shapes=[pltpu.VMEM((2, TOP_K * TM_COMB * SUBLANES, LANES), F32),
                        pltpu.SemaphoreType.DMA((2,))],
        compiler_params=pltpu.CompilerParams(dimension_semantics=("arbitrary",),
                                             vmem_limit_bytes=VMEM_LIMIT),
        name="combine",
    )(dest3, dest3, ys, x1, gates, mod)


def _layer(x, mod, norm1_w, w_in, conv_w, a_log, dt_bias, dn_norm_w, q_norm_w, k_norm_w, sinks,
           w_out, norm2_w, w_router, b_router, w_up, b_up, w_down, b_down):
    batch, seq, _ = x.shape
    t = batch * seq
    xf = x.reshape(t, D_MODEL)

    o_qkv, o_z, o_a, o_b, o_q, o_k, o_v = 0, 1536, 2048, 2052, 2056, 2568, 2696
    w_ab = jnp.zeros((D_MODEL, LANES), F32).at[:, 0:2 * DN_HEADS].set(w_in[:, o_a:o_q])
    w_cat = jnp.concatenate([w_in[:, o_qkv:o_z], w_in[:, o_z:o_a], w_in[:, o_q:o_k],
                             w_in[:, o_k:o_v], w_in[:, o_v:], w_ab], axis=1).astype(BF16)
    gparams = jnp.zeros((2, LANES), F32)
    gparams = gparams.at[0, 0:DN_HEADS].set(-jnp.exp(a_log.astype(F32)))
    gparams = gparams.at[1, 0:DN_HEADS].set(dt_bias.astype(F32))
    qnw128 = jnp.tile(q_norm_w.astype(F32), 2)[None, :]
    knw128 = jnp.tile(k_norm_w.astype(F32), 2)[None, :]

    q, k, v, z, gb, swq, swk, swv = _inproj(xf, mod, norm1_w[None, :], w_cat, conv_w, gparams,
                                            qnw128, knw128, seq)
    o_dn, o_sw = _mixer(q, k, v, z, gb, dn_norm_w[None, :], swq, swk, swv, sinks.astype(F32),
                        batch, seq)

    wr_pad = jnp.zeros((D_MODEL, LANES), F32).at[:, 0:N_EXPERTS].set(w_router).astype(BF16)
    br_pad = jnp.zeros((1, LANES), F32).at[0, 0:N_EXPERTS].set(b_router)
    x1, h2, top_idx, gates, rank, cnt = _post(xf, o_dn, o_sw, mod, w_out.astype(BF16),
                                              norm2_w[None, :], wr_pad, br_pad, seq)

    counts = cnt[0, 0:N_EXPERTS]
    padded = (counts + MOE_ROWS - 1) // MOE_ROWS * MOE_ROWS
    pend = jnp.cumsum(padded)
    pstart = pend - padded
    n_blocks = (t * TOP_K) // MOE_ROWS + N_EXPERTS
    n_pad = n_blocks * MOE_ROWS
    n_valid = (pend[-1] // MOE_ROWS).astype(I32)
    blk = jnp.minimum(jnp.arange(n_blocks, dtype=I32), n_valid - 1)
    block_e = jnp.minimum(jnp.sum(pend[None, :] <= (blk * MOE_ROWS)[:, None], axis=1),
                          N_EXPERTS - 1).astype(I32)
    experts = jnp.arange(N_EXPERTS, dtype=I32)[:, None, None]
    dest = (jnp.sum(jnp.where(top_idx[None, 0:TOP_K] == experts, pstart[:, None, None], 0), axis=0)
            + rank[0:TOP_K]).astype(I32)
    fill = jnp.stack([jnp.append(pstart + counts, pend[-1]),
                      jnp.append(pend, n_pad)]).astype(I32)

    eids = jnp.arange(N_EXPERTS, dtype=I32)
    later = (eids[None, :] > eids[:, None]) & (counts[None, :] > 0)
    next_e = jnp.min(jnp.where(later, eids[None, :], N_EXPERTS), axis=1)
    next_e = jnp.where(next_e == N_EXPERTS, eids, next_e).astype(I32)

    row_tok = _invert(fill, dest.reshape(TOP_K * t), n_pad)
    ys = _moe(block_e, n_valid.reshape(1), next_e, row_tok, h2, w_up, b_up[:, None, :], w_down,
              b_down[:, None, :])
    n_tiles = t // TM_COMB
    dest_tiles = dest.reshape(TOP_K, n_tiles, TM_COMB).transpose(1, 0, 2).reshape(
        n_tiles, 1, TOP_K * TM_COMB)
    out = _combine(dest_tiles, ys, x1, gates, mod, seq)
    return out.reshape(batch, seq, D_MODEL)


def kernel(x, c, w_ada, b_ada, norm1_w, w_in, conv_w, a_log, dt_bias, dn_norm_w, q_norm_w,
           k_norm_w, sinks, w_out, norm2_w, w_router, b_router, w_up, b_up, w_down, b_down):
    out_dtype = x.dtype
    batch = x.shape[0]
    depth = w_ada.shape[0]
    c_pad = jnp.zeros((SUBLANES, D_MODEL), F32).at[0:batch].set(c.astype(F32))
    for l in range(depth):
        mod = _ada(c_pad, w_ada[l], b_ada[l][None, :])[0:batch].reshape(batch, 6, D_MODEL)
        x = _layer(x, mod, norm1_w[l], w_in[l], conv_w[l], a_log[l], dt_bias[l], dn_norm_w[l],
                   q_norm_w[l], k_norm_w[l], sinks[l], w_out[l], norm2_w[l], w_router[l],
                   b_router[l], w_up[l], b_up[l], w_down[l], b_down[l])
    return x.astype(out_dtype)
```

```python
import functools

import jax
import jax.numpy as jnp
from jax import lax
from jax.experimental import pallas as pl
from jax.experimental.pallas import tpu as pltpu

F32 = jnp.float32
BF16 = jnp.bfloat16
I32 = jnp.int32

D_MODEL = 1024
DN_HEADS = 4
DN_HEAD_DIM = 128
DN_WIDTH = DN_HEADS * DN_HEAD_DIM
CONV_WIDTH = 4
CHUNK = 64
SWA_HEADS = 8
SWA_KV_HEADS = 2
SWA_GROUP = SWA_HEADS // SWA_KV_HEADS
SWA_HEAD_DIM = 64
SWA_WIDTH = SWA_HEADS * SWA_HEAD_DIM
SWA_KV_WIDTH = SWA_KV_HEADS * SWA_HEAD_DIM
WINDOW = 128
N_EXPERTS = 32
TOP_K = 4
D_FF = D_MODEL
SWIGLU_ALPHA = 1.702
SWIGLU_LIMIT = 7.0
EPS = 1e-6
NEG = -1e30

LANES = 128
SUBLANES = 8
VMEM_LIMIT = 56 * 1024 * 1024

C_QKV = 0
C_Z = C_QKV + 3 * DN_WIDTH
C_SWQ = C_Z + DN_WIDTH
C_SWK = C_SWQ + SWA_WIDTH
C_SWV = C_SWK + SWA_KV_WIDTH
C_AB = C_SWV + SWA_KV_WIDTH
IN_PAD = C_AB + LANES

TM_IN = 512
DN_ROWS = 256
MIX_SEQS = 2
TM_POST = 1024
MOE_ROWS = 256
TM_COMB = 128


def _silu(x):
    return x * jax.nn.sigmoid(x)


def _mm(a, b):
    return jnp.dot(a.astype(BF16), b.astype(BF16), preferred_element_type=F32)


def _mm_nt(a, b):
    return lax.dot_general(a.astype(BF16), b.astype(BF16), (((1,), (1,)), ((), ())),
                           preferred_element_type=F32)


def _mm_tn(a, b):
    return lax.dot_general(a.astype(BF16), b.astype(BF16), (((0,), (0,)), ((), ())),
                           preferred_element_type=F32)


def _load_token_tiles(ref, row0, rows):
    return jnp.concatenate(
        [ref[pl.ds(row0 * SUBLANES + s, rows, stride=SUBLANES), :] for s in range(D_MODEL // LANES)],
        axis=1)


def _store_token_tiles(ref, val):
    rows = val.shape[0]
    for s in range(D_MODEL // LANES):
        ref[pl.ds(s, rows, stride=SUBLANES), :] = val[:, s * LANES:(s + 1) * LANES]


def _token_copy(src_ref, s, dst_ref, d, sem):
    return pltpu.make_async_copy(
        src_ref.at[pl.ds(pl.multiple_of(s * SUBLANES, SUBLANES), SUBLANES), :],
        dst_ref.at[pl.ds(pl.multiple_of(d * SUBLANES, SUBLANES), SUBLANES), :], sem)


def _ada_kernel(c_ref, w_ref, b_ref, o_ref):
    c = c_ref[...]
    o_ref[...] = jnp.dot(_silu(c), w_ref[...], precision=lax.Precision.HIGHEST,
                         preferred_element_type=F32) + b_ref[...]


def _ada(c_pad, w_ada, b_ada):
    n = w_ada.shape[1]
    return pl.pallas_call(
        _ada_kernel,
        grid=(n // D_MODEL,),
        in_specs=[pl.BlockSpec((SUBLANES, D_MODEL), lambda j: (0, 0)),
                  pl.BlockSpec((D_MODEL, D_MODEL), lambda j: (0, j)),
                  pl.BlockSpec((1, D_MODEL), lambda j: (0, j))],
        out_specs=pl.BlockSpec((SUBLANES, D_MODEL), lambda j: (0, j)),
        out_shape=jax.ShapeDtypeStruct((SUBLANES, n), F32),
        name="ada",
    )(c_pad, w_ada, b_ada)


def _rms_heads64(a, w128):
    lane = lax.broadcasted_iota(I32, a.shape, 1)
    lo = lane < SWA_HEAD_DIM
    sq = a * a
    s_lo = jnp.sum(jnp.where(lo, sq, 0.0), axis=-1, keepdims=True)
    s_hi = jnp.sum(jnp.where(lo, 0.0, sq), axis=-1, keepdims=True)
    ms = jnp.where(lo, s_lo, s_hi) * (1.0 / SWA_HEAD_DIM)
    return a * lax.rsqrt(ms + EPS) * w128


def _inproj_kernel(tiles_per_seq, x_ref, mod_ref, n1w_ref, w_ref, convw_ref, gp_ref, qnw_ref,
                   knw_ref, q_ref, k_ref, v_ref, z_ref, gb_ref, swq_ref, swk_ref, swv_ref,
                   conv_scr):
    i = pl.program_id(0)
    tm = x_ref.shape[0]
    x = x_ref[...]
    y = x * lax.rsqrt(jnp.mean(x * x, axis=-1, keepdims=True) + EPS) * n1w_ref[...]
    h = (y * (1.0 + mod_ref[0, 1:2, :]) + mod_ref[0, 0:1, :]).astype(BF16)

    @pl.when(i % tiles_per_seq == 0)
    def _():
        conv_scr[0:SUBLANES, :] = jnp.zeros((SUBLANES, 3 * DN_WIDTH), F32)

    u = jnp.dot(h, w_ref[:, C_QKV:C_Z], preferred_element_type=F32)
    conv_scr[SUBLANES:SUBLANES + tm, :] = u
    cw = convw_ref[...]
    acc = u * cw[3:4, :]
    for j in range(CONV_WIDTH - 1):
        off = SUBLANES - (CONV_WIDTH - 1) + j
        acc = acc + conv_scr[off:off + tm, :] * cw[j:j + 1, :]
    conv_scr[0:SUBLANES, :] = conv_scr[tm:tm + SUBLANES, :]
    qkv = _silu(acc)

    for hh in range(DN_HEADS):
        sl = slice(hh * DN_HEAD_DIM, (hh + 1) * DN_HEAD_DIM)
        qh = qkv[:, hh * DN_HEAD_DIM:(hh + 1) * DN_HEAD_DIM]
        kh = qkv[:, DN_WIDTH + hh * DN_HEAD_DIM:DN_WIDTH + (hh + 1) * DN_HEAD_DIM]
        q_ref[:, sl] = qh * lax.rsqrt(jnp.sum(qh * qh, axis=-1, keepdims=True) + EPS) * (
            DN_HEAD_DIM ** -0.5)
        k_ref[:, sl] = kh * lax.rsqrt(jnp.sum(kh * kh, axis=-1, keepdims=True) + EPS)
    v_ref[...] = qkv[:, 2 * DN_WIDTH:3 * DN_WIDTH]

    z_ref[...] = jnp.dot(h, w_ref[:, C_Z:C_SWQ], preferred_element_type=F32)

    ab = jnp.dot(h, w_ref[:, C_AB:IN_PAD], preferred_element_type=F32)
    lane = lax.broadcasted_iota(I32, ab.shape, 1)
    sp_in = ab + gp_ref[1:2, :]
    softplus = jnp.maximum(sp_in, 0.0) + jnp.log1p(jnp.exp(-jnp.abs(sp_in)))
    gb_ref[...] = jnp.where(lane < DN_HEADS, gp_ref[0:1, :] * softplus, jax.nn.sigmoid(ab))

    sq = jnp.dot(h, w_ref[:, C_SWQ:C_SWK], preferred_element_type=F32)
    for t in range(SWA_WIDTH // LANES):
        sl = slice(t * LANES, (t + 1) * LANES)
        swq_ref[:, sl] = (_rms_heads64(sq[:, sl], qnw_ref[...]) * (SWA_HEAD_DIM ** -0.5)).astype(BF16)
    sk = jnp.dot(h, w_ref[:, C_SWK:C_SWV], preferred_element_type=F32)
    swk_ref[...] = _rms_heads64(sk, knw_ref[...]).astype(BF16)
    swv_ref[...] = jnp.dot(h, w_ref[:, C_SWV:C_AB], preferred_element_type=F32).astype(BF16)


def _inproj(xf, mod, n1w, w_cat, conv_w, gparams, qnw128, knw128, seq):
    t = xf.shape[0]
    tiles_per_seq = seq // TM_IN
    row = lambda i: (i, 0)
    const = lambda i: (0, 0)
    outs = [jax.ShapeDtypeStruct((t, DN_WIDTH), F32)] * 4 + [
        jax.ShapeDtypeStruct((t, LANES), F32),
        jax.ShapeDtypeStruct((t, SWA_WIDTH), BF16),
        jax.ShapeDtypeStruct((t, SWA_KV_WIDTH), BF16),
        jax.ShapeDtypeStruct((t, SWA_KV_WIDTH), BF16)]
    out_specs = [pl.BlockSpec((TM_IN, DN_WIDTH), row)] * 4 + [
        pl.BlockSpec((TM_IN, LANES), row),
        pl.BlockSpec((TM_IN, SWA_WIDTH), row),
        pl.BlockSpec((TM_IN, SWA_KV_WIDTH), row),
        pl.BlockSpec((TM_IN, SWA_KV_WIDTH), row)]
    return pl.pallas_call(
        functools.partial(_inproj_kernel, tiles_per_seq),
        grid=(t // TM_IN,),
        in_specs=[pl.BlockSpec((TM_IN, D_MODEL), row),
                  pl.BlockSpec((1, 6, D_MODEL), lambda i: (i // tiles_per_seq, 0, 0)),
                  pl.BlockSpec((1, D_MODEL), const),
                  pl.BlockSpec((D_MODEL, IN_PAD), const),
                  pl.BlockSpec((CONV_WIDTH, 3 * DN_WIDTH), const),
                  pl.BlockSpec((2, LANES), const),
                  pl.BlockSpec((1, LANES), const),
                  pl.BlockSpec((1, LANES), const)],
        out_specs=out_specs,
        out_shape=outs,
        scratch_shapes=[pltpu.VMEM((TM_IN + SUBLANES, 3 * DN_WIDTH), F32)],
        compiler_params=pltpu.CompilerParams(dimension_semantics=("arbitrary",),
                                             vmem_limit_bytes=VMEM_LIMIT),
        name="inproj",
    )(xf, mod, n1w, w_cat, conv_w, gparams, qnw128, knw128)


def _dn_block(j, q_ref, k_ref, v_ref, z_ref, gb_ref, nw_ref, o_ref,
              s_scr, u_scr, w_scr, qd_scr, kd_scr, a_scr):
    n_seq, rows = q_ref.shape[0], q_ref.shape[1]
    n_chunks = rows // CHUNK
    chains = [(g, h) for g in range(n_seq) for h in range(DN_HEADS)]
    lsl = [slice(h * DN_HEAD_DIM, (h + 1) * DN_HEAD_DIM) for h in range(DN_HEADS)]

    @pl.when(j == 0)
    def _():
        s_scr[...] = jnp.zeros(s_scr.shape, F32)

    r = lax.broadcasted_iota(I32, (rows, rows), 0)
    c = lax.broadcasted_iota(I32, (rows, rows), 1)
    same = (r // CHUNK) == (c // CHUNK)
    incl = same & (c <= r)
    strict = same & (c < r)
    masks = jnp.concatenate([jnp.where(incl, 1.0, 0.0), jnp.where(same, 1.0, 0.0)],
                            axis=0).astype(BF16)
    gbs, gcs, gls, gcts = [], [], [], []
    for g in range(n_seq):
        gb = gb_ref[g]
        g_hi = gb.astype(BF16)
        rem = gb - g_hi.astype(F32)
        g_mid = rem.astype(BF16)
        g_lo = (rem - g_mid.astype(F32)).astype(BF16)
        sums = jnp.dot(masks, jnp.concatenate([g_hi, g_mid, g_lo], axis=1),
                       preferred_element_type=F32)
        sums = sums[:, 0:LANES] + sums[:, LANES:2 * LANES] + sums[:, 2 * LANES:3 * LANES]
        gbs.append(gb)
        gcs.append(sums[0:rows, :])
        gls.append(sums[rows:2 * rows, :])
        gcts.append(sums[0:rows, :].T)

    rmat = []
    rhs = []
    for n, (g, h) in enumerate(chains):
        ls = lsl[h]
        g_col = gcs[g][:, h:h + 1]
        beta = gbs[g][:, DN_HEADS + h:DN_HEADS + h + 1]
        decay = jnp.where(incl, jnp.exp(jnp.where(incl, g_col - gcts[g][h:h + 1, :], 0.0)), 0.0)
        qq = q_ref[g, :, ls]
        kk = k_ref[g, :, ls]
        kb = kk * beta
        eg = jnp.exp(g_col)
        rmat.append(-jnp.where(strict, _mm_nt(kb, kk) * decay, 0.0))
        rhs.append(jnp.concatenate([v_ref[g, :, ls] * beta, kb * eg], axis=1))
        a_scr[n] = jnp.where(incl, _mm_nt(qq, kk) * decay, 0.0)
        qd_scr[g, :, ls] = qq * eg
        kd_scr[g, :, ls] = kk * jnp.exp(gls[g][:, h:h + 1] - g_col)

    ids = range(len(chains))
    xm = list(rmat)
    pm = list(rmat)
    for it in range(6):
        xm = [xm[n] + _mm(pm[n], xm[n]) for n in ids]
        if it < 5:
            pm = [_mm(pm[n], pm[n]) for n in ids]
    for n, (g, h) in enumerate(chains):
        sol = rhs[n] + _mm(xm[n], rhs[n])
        u_scr[g, :, lsl[h]] = sol[:, 0:DN_HEAD_DIM]
        w_scr[g, :, lsl[h]] = sol[:, DN_HEAD_DIM:2 * DN_HEAD_DIM]

    nw = nw_ref[...]
    for ci in range(n_chunks):
        rs = slice(ci * CHUNK, (ci + 1) * CHUNK)
        dvals = [jnp.exp(gls[g][ci * CHUNK:ci * CHUNK + 1, :]) for g in range(n_seq)]
        st = [s_scr[n] for n in ids]
        ws = [_mm(w_scr[g, rs, lsl[h]], st[n]) for n, (g, h) in enumerate(chains)]
        qs = [_mm(qd_scr[g, rs, lsl[h]], st[n]) for n, (g, h) in enumerate(chains)]
        vn = [u_scr[g, rs, lsl[h]] - ws[n] for n, (g, h) in enumerate(chains)]
        av = [_mm(a_scr[n, rs, rs], vn[n]) for n in ids]
        kv = [_mm_tn(kd_scr[g, rs, lsl[h]], vn[n]) for n, (g, h) in enumerate(chains)]
        for n, (g, h) in enumerate(chains):
            s_scr[n] = st[n] * dvals[g][:, h:h + 1] + kv[n]
            o = qs[n] + av[n]
            on = o * lax.rsqrt(jnp.mean(o * o, axis=-1, keepdims=True) + EPS) * nw
            o_ref[g, rs, lsl[h]] = (on * _silu(z_ref[g, rs, lsl[h]])).astype(o_ref.dtype)


def _swa_window(sink_ref, q, kcat, vcat, first):
    w = WINDOW
    rows = SWA_GROUP * w
    ri = lax.broadcasted_iota(I32, (rows, 2 * w), 0)
    kj = lax.broadcasted_iota(I32, (rows, 2 * w), 1)
    grp = ri // w
    rel = (ri - grp * w) + w - kj
    valid = (rel >= 0) & (rel < w)
    if first is not None:
        valid = valid & ((kj >= w) | jnp.logical_not(first))
    relf = rel.astype(F32)
    grp_col = lax.broadcasted_iota(I32, (rows, 1), 0) // w
    outs = []
    for kh in range(SWA_KV_HEADS):
        slope_col = jnp.zeros((rows, 1), F32)
        sink_col = jnp.zeros((rows, 1), F32)
        for g in range(SWA_GROUP):
            hd = kh * SWA_GROUP + g
            slope_col = jnp.where(grp_col == g, 2.0 ** (-8.0 * (hd + 1) / SWA_HEADS), slope_col)
            sink_col = jnp.where(grp_col == g, sink_ref[hd], sink_col)
        qs = jnp.concatenate(
            [q[:, (kh * SWA_GROUP + g) * SWA_HEAD_DIM:(kh * SWA_GROUP + g + 1) * SWA_HEAD_DIM]
             for g in range(SWA_GROUP)], axis=0)
        kk = kcat[:, kh * SWA_HEAD_DIM:(kh + 1) * SWA_HEAD_DIM]
        vv = vcat[:, kh * SWA_HEAD_DIM:(kh + 1) * SWA_HEAD_DIM]
        s = lax.dot_general(qs, kk, (((1,), (1,)), ((), ())), preferred_element_type=F32)
        logits = jnp.where(valid, s - slope_col * relf, NEG)
        m = jnp.maximum(jnp.max(logits, axis=-1, keepdims=True), sink_col)
        p = jnp.exp(logits - m)
        denom = jnp.sum(p, axis=-1, keepdims=True) + jnp.exp(sink_col - m)
        o = jnp.dot(p.astype(BF16), vv, preferred_element_type=F32) / denom
        for g in range(SWA_GROUP):
            outs.append(o[g * w:(g + 1) * w, :])
    return [jnp.concatenate([outs[2 * t], outs[2 * t + 1]], axis=-1)
            for t in range(SWA_WIDTH // LANES)]


def _mixer_kernel(sink_ref, q_ref, k_ref, v_ref, z_ref, gb_ref, nw_ref,
                  swq_ref, kp_ref, kc_ref, vp_ref, vc_ref, odn_ref, osw_ref,
                  s_scr, u_scr, w_scr, qd_scr, kd_scr, a_scr):
    j = pl.program_id(1)
    _dn_block(j, q_ref, k_ref, v_ref, z_ref, gb_ref, nw_ref, odn_ref,
              s_scr, u_scr, w_scr, qd_scr, kd_scr, a_scr)
    w = WINDOW
    for g in range(swq_ref.shape[0]):
        kcur = kc_ref[g]
        vcur = vc_ref[g]
        for s in range(DN_ROWS // w):
            rows = slice(s * w, (s + 1) * w)
            if s == 0:
                kcat = jnp.concatenate([kp_ref[g], kcur[0:w, :]], axis=0)
                vcat = jnp.concatenate([vp_ref[g], vcur[0:w, :]], axis=0)
                first = j == 0
            else:
                kcat = kcur[(s - 1) * w:(s + 1) * w, :]
                vcat = vcur[(s - 1) * w:(s + 1) * w, :]
                first = None
            tiles = _swa_window(sink_ref, swq_ref[g, rows, :], kcat, vcat, first)
            for t, tile in enumerate(tiles):
                osw_ref[g, rows, t * LANES:(t + 1) * LANES] = tile.astype(osw_ref.dtype)


def _mixer(q, k, v, z, gb, nw, swq, swk, swv, sinks, batch, seq):
    steps = seq // DN_ROWS
    per_step = DN_ROWS // WINDOW
    row = lambda b, j, s: (b, j, 0)
    prev = lambda b, j, s: (b, jnp.maximum(j * per_step - 1, 0), 0)
    blk = pl.BlockSpec((MIX_SEQS, DN_ROWS, DN_WIDTH), row)
    kv_cur = pl.BlockSpec((MIX_SEQS, DN_ROWS, SWA_KV_WIDTH), row)
    kv_prev = pl.BlockSpec((MIX_SEQS, WINDOW, SWA_KV_WIDTH), prev)
    n_chain = MIX_SEQS * DN_HEADS
    return pl.pallas_call(
        _mixer_kernel,
        grid_spec=pltpu.PrefetchScalarGridSpec(
            num_scalar_prefetch=1,
            grid=(batch // MIX_SEQS, steps),
            in_specs=[blk, blk, blk, blk,
                      pl.BlockSpec((MIX_SEQS, DN_ROWS, LANES), row),
                      pl.BlockSpec((1, DN_HEAD_DIM), lambda b, j, s: (0, 0)),
                      pl.BlockSpec((MIX_SEQS, DN_ROWS, SWA_WIDTH), row),
                      kv_prev, kv_cur, kv_prev, kv_cur],
            out_specs=[blk, pl.BlockSpec((MIX_SEQS, DN_ROWS, SWA_WIDTH), row)],
            scratch_shapes=[pltpu.VMEM((n_chain, DN_HEAD_DIM, DN_HEAD_DIM), F32),
                            pltpu.VMEM((MIX_SEQS, DN_ROWS, DN_WIDTH), F32),
                            pltpu.VMEM((MIX_SEQS, DN_ROWS, DN_WIDTH), F32),
                            pltpu.VMEM((MIX_SEQS, DN_ROWS, DN_WIDTH), F32),
                            pltpu.VMEM((MIX_SEQS, DN_ROWS, DN_WIDTH), F32),
                            pltpu.VMEM((n_chain, DN_ROWS, DN_ROWS), F32)]),
        out_shape=[jax.ShapeDtypeStruct((batch, seq, DN_WIDTH), BF16),
                   jax.ShapeDtypeStruct((batch, seq, SWA_WIDTH), BF16)],
        compiler_params=pltpu.CompilerParams(dimension_semantics=("arbitrary", "arbitrary"),
                                             vmem_limit_bytes=VMEM_LIMIT),
        name="mixer",
    )(sinks, q, k, v, z, gb, nw, swq, swk, swk, swv, swv)


def _post_kernel(x_ref, odn_ref, osw_ref, mod_ref, wo_ref, n2w_ref, wr_ref, br_ref,
                 x1_ref, h2_ref, idx_ref, gate_ref, rank_ref, cnt_ref, run_scr):
    i = pl.program_id(0)
    tm = x_ref.shape[0]

    @pl.when(i == 0)
    def _():
        run_scr[...] = jnp.zeros(run_scr.shape, F32)

    y = (jnp.dot(odn_ref[...], wo_ref[0:DN_WIDTH, :], preferred_element_type=F32)
         + jnp.dot(osw_ref[...], wo_ref[DN_WIDTH:, :], preferred_element_type=F32))
    x1 = x_ref[...] + mod_ref[0, 2:3, :] * y
    x1_ref[...] = x1
    hn = x1 * lax.rsqrt(jnp.mean(x1 * x1, axis=-1, keepdims=True) + EPS) * n2w_ref[...]
    h2 = hn * (1.0 + mod_ref[0, 4:5, :]) + mod_ref[0, 3:4, :]
    _store_token_tiles(h2_ref, h2)

    lane = lax.broadcasted_iota(I32, (tm, LANES), 1)
    lane_f = lane.astype(F32)
    logits = jnp.dot(h2.astype(BF16), wr_ref[...], preferred_element_type=F32) + br_ref[...]
    work = jnp.where(lane < N_EXPERTS, logits, NEG)
    sel = jnp.zeros((tm, LANES), F32)
    idx_out = jnp.zeros((tm, LANES), F32)
    vals = []
    idxs = []
    for kk in range(TOP_K):
        m = jnp.max(work, axis=-1, keepdims=True)
        ik = jnp.min(jnp.where(work == m, lane_f, float(LANES)), axis=-1, keepdims=True)
        hit = lane_f == ik
        work = jnp.where(hit, NEG, work)
        sel = jnp.where(hit, 1.0, sel)
        idx_out = jnp.where(lane == kk, ik, idx_out)
        vals.append(m)
        idxs.append(ik)
    es = [jnp.exp(v - vals[0]) for v in vals]
    tot = es[0] + es[1] + es[2] + es[3]
    gate_out = jnp.zeros((tm, LANES), F32)
    for kk in range(TOP_K):
        gate_out = jnp.where(lane == kk, es[kk] / tot, gate_out)
    gate_ref[...] = gate_out

    r = lax.broadcasted_iota(I32, (tm, tm), 0)
    c = lax.broadcasted_iota(I32, (tm, tm), 1)
    before = jnp.where(c < r, 1.0, 0.0).astype(BF16)
    base = jnp.dot(before, sel.astype(BF16), preferred_element_type=F32) + run_scr[0:1, :]
    rank_out = jnp.zeros((tm, LANES), F32)
    for kk in range(TOP_K):
        rk = jnp.sum(jnp.where(lane_f == idxs[kk], base, 0.0), axis=-1, keepdims=True)
        rank_out = jnp.where(lane == kk, rk, rank_out)
    idx_ref[...] = idx_out.T[0:SUBLANES, :].astype(I32)
    rank_ref[...] = rank_out.T[0:SUBLANES, :].astype(I32)
    run = run_scr[0:1, :] + jnp.sum(sel, axis=0, keepdims=True)
    run_scr[0:1, :] = run
    cnt_ref[...] = jnp.broadcast_to(run, cnt_ref.shape).astype(I32)


def _post(xf, odn, osw, mod, w_out, n2w, wr_pad, br_pad, seq):
    t = xf.shape[0]
    tiles_per_seq = seq // TM_POST
    row = lambda i: (i, 0)
    const = lambda i: (0, 0)
    return pl.pallas_call(
        _post_kernel,
        grid=(t // TM_POST,),
        in_specs=[pl.BlockSpec((TM_POST, D_MODEL), row),
                  pl.BlockSpec((TM_POST, DN_WIDTH), row),
                  pl.BlockSpec((TM_POST, SWA_WIDTH), row),
                  pl.BlockSpec((1, 6, D_MODEL), lambda i: (i // tiles_per_seq, 0, 0)),
                  pl.BlockSpec((D_MODEL, D_MODEL), const),
                  pl.BlockSpec((1, D_MODEL), const),
                  pl.BlockSpec((D_MODEL, LANES), const),
                  pl.BlockSpec((1, LANES), const)],
        out_specs=[pl.BlockSpec((TM_POST, D_MODEL), row),
                   pl.BlockSpec((TM_POST * SUBLANES, LANES), row),
                   pl.BlockSpec((SUBLANES, TM_POST), lambda i: (0, i)),
                   pl.BlockSpec((TM_POST, LANES), row),
                   pl.BlockSpec((SUBLANES, TM_POST), lambda i: (0, i)),
                   pl.BlockSpec((SUBLANES, LANES), const)],
        out_shape=[jax.ShapeDtypeStruct((t, D_MODEL), F32),
                   jax.ShapeDtypeStruct((t * SUBLANES, LANES), F32),
                   jax.ShapeDtypeStruct((SUBLANES, t), I32),
                   jax.ShapeDtypeStruct((t, LANES), F32),
                   jax.ShapeDtypeStruct((SUBLANES, t), I32),
                   jax.ShapeDtypeStruct((SUBLANES, LANES), I32)],
        scratch_shapes=[pltpu.VMEM((SUBLANES, LANES), F32)],
        compiler_params=pltpu.CompilerParams(dimension_semantics=("arbitrary",),
                                             vmem_limit_bytes=VMEM_LIMIT),
        name="post",
    )(xf, odn, osw, mod, w_out, n2w, wr_pad, br_pad)


INVERT_GROUP = 16


def _invert_kernel(fill_ref, dest_ref, rowtok_ref):
    def per_range(e, carry):
        def clear(rw, c2):
            rowtok_ref[rw] = 0
            return c2

        lax.fori_loop(fill_ref[0, e], fill_ref[1, e], clear, 0)
        return carry

    lax.fori_loop(0, fill_ref.shape[1], per_range, 0)

    n_tok = dest_ref.shape[0] // TOP_K
    for kk in range(TOP_K):
        def put(g, carry, kk=kk):
            tok0 = g * INVERT_GROUP
            rows = [dest_ref[kk * n_tok + tok0 + u] for u in range(INVERT_GROUP)]
            for u in range(INVERT_GROUP):
                rowtok_ref[rows[u]] = tok0 + u
            return carry

        lax.fori_loop(0, n_tok // INVERT_GROUP, put, 0)


def _invert(fill, dest_flat, n_pad):
    return pl.pallas_call(
        _invert_kernel,
        in_specs=[pl.BlockSpec(memory_space=pltpu.SMEM),
                  pl.BlockSpec(memory_space=pltpu.SMEM)],
        out_specs=pl.BlockSpec(memory_space=pltpu.SMEM),
        out_shape=jax.ShapeDtypeStruct((n_pad,), I32),
        name="invert",
    )(fill, dest_flat)


GATHER_SLOTS = 8
GATHER_AHEAD = GATHER_SLOTS - 1
FF_CHUNKS = 4
FF_COLS = D_FF // FF_CHUNKS


def _moe_kernel(be_ref, nv_ref, nxt_ref, rowtok_ref, h_ref, wu_hbm, bu_ref, wd_hbm, bd_ref, ys_ref,
                wu_f32, wd_f32, wu_bf, wd_bf, xbuf, sem, wsem):
    i = pl.program_id(0)
    slot = i % GATHER_SLOTS
    n_valid = nv_ref[0]

    def weight_copies(e):
        return (pltpu.make_async_copy(wu_hbm.at[e], wu_f32, wsem.at[0]),
                pltpu.make_async_copy(wd_hbm.at[e], wd_f32, wsem.at[1]))

    def gather(blk, s, r0=0, r1=MOE_ROWS):
        for r in range(r0, r1):
            _token_copy(h_ref, rowtok_ref[blk * MOE_ROWS + r], xbuf.at[s], r,
                        sem.at[s]).start(priority=r % 2)

    def gather_wait(s):
        pltpu.make_async_copy(h_ref.at[pl.ds(0, MOE_ROWS * SUBLANES), :],
                              xbuf.at[s, pl.ds(0, MOE_ROWS * SUBLANES), :], sem.at[s]).wait()

    last = n_valid - 1

    @pl.when(i == 0)
    def _():
        for s in range(GATHER_SLOTS):
            xbuf[s, pl.ds(MOE_ROWS * SUBLANES, SUBLANES), :] = jnp.zeros((SUBLANES, LANES), F32)
        for cp in weight_copies(be_ref[0]):
            cp.start()
        for d in range(GATHER_AHEAD):
            gather(jnp.minimum(d, last), d)

    @pl.when(i < n_valid)
    def _():
        expert = be_ref[i]
        prev = be_ref[jnp.maximum(i - 1, 0)]

        @pl.when((i == 0) | (expert != prev))
        def _():
            for cp in weight_copies(expert):
                cp.wait()
            wu_bf[...] = wu_f32[...].astype(BF16)
            wd_bf[...] = wd_f32[...].astype(BF16)
            nxt = nxt_ref[expert]

            @pl.when(nxt != expert)
            def _():
                for cp in weight_copies(nxt):
                    cp.start()

        gather_wait(slot)
        xb = _load_token_tiles(xbuf.at[slot], 0, MOE_ROWS).astype(BF16)
        nxt_blk = jnp.minimum(i + GATHER_AHEAD, last)
        nxt_slot = (i + GATHER_AHEAD) % GATHER_SLOTS
        group = MOE_ROWS // (2 * FF_CHUNKS)
        issued = [0]

        def issue_group():
            gather(nxt_blk, nxt_slot, issued[0] * group, (issued[0] + 1) * group)
            issued[0] += 1
            spare = xbuf[nxt_slot, pl.ds(MOE_ROWS * SUBLANES, 1), :]
            bits = pltpu.bitcast(spare, jnp.uint32)
            tied = pltpu.bitcast((bits >> 16) >> 16, F32)
            return jnp.concatenate([tied] * (FF_COLS // LANES), axis=1)

        acts = []
        tie = jnp.zeros((1, FF_COLS), F32)
        for c in range(FF_CHUNKS):
            cg = slice(c * FF_COLS, (c + 1) * FF_COLS)
            cl = slice(D_FF + c * FF_COLS, D_FF + (c + 1) * FF_COLS)
            g = jnp.dot(xb, wu_bf[:, cg], preferred_element_type=F32) + (bu_ref[0, :, cg] + tie)
            l = jnp.dot(xb, wu_bf[:, cl], preferred_element_type=F32) + bu_ref[0, :, cl]
            x_glu = jnp.minimum(g, SWIGLU_LIMIT)
            x_lin = jnp.clip(l, -SWIGLU_LIMIT, SWIGLU_LIMIT)
            acts.append((x_glu * jax.nn.sigmoid(SWIGLU_ALPHA * x_glu) * (x_lin + 1.0)).astype(BF16))
            tie = issue_group()
        act = jnp.concatenate(acts, axis=1)
        for c in range(FF_CHUNKS):
            cs = slice(c * FF_COLS, (c + 1) * FF_COLS)
            y = jnp.dot(act, wd_bf[:, cs], preferred_element_type=F32) + (bd_ref[0, :, cs] + tie)
            for u in range(FF_COLS // LANES):
                s = c * (FF_COLS // LANES) + u
                ys_ref[pl.ds(s, MOE_ROWS, stride=SUBLANES), :] = y[:, u * LANES:(u + 1) * LANES]
            tie = issue_group()

    @pl.when(i == last)
    def _():
        for d in range(1, GATHER_SLOTS):
            gather_wait((i + d) % GATHER_SLOTS)

    @pl.when(i >= n_valid)
    def _():
        ys_ref[...] = jnp.zeros(ys_ref.shape, F32)


def _moe(block_e, n_valid, next_e, row_tok, h2, w_up, b_up, w_down, b_down):
    n_pad = row_tok.shape[0]
    n_blocks = n_pad // MOE_ROWS
    out_rows = lambda i, be, nv, nx, rt: (i, 0)
    expert = lambda i, be, nv, nx, rt: (be[i], 0, 0)
    return pl.pallas_call(
        _moe_kernel,
        grid_spec=pltpu.PrefetchScalarGridSpec(
            num_scalar_prefetch=4,
            grid=(n_blocks,),
            in_specs=[pl.BlockSpec(memory_space=pl.ANY),
                      pl.BlockSpec(memory_space=pl.ANY),
                      pl.BlockSpec((1, 1, 2 * D_FF), expert),
                      pl.BlockSpec(memory_space=pl.ANY),
                      pl.BlockSpec((1, 1, D_MODEL), expert)],
            out_specs=pl.BlockSpec((MOE_ROWS * SUBLANES, LANES), out_rows),
            scratch_shapes=[pltpu.VMEM((D_MODEL, 2 * D_FF), F32),
                            pltpu.VMEM((D_FF, D_MODEL), F32),
                            pltpu.VMEM((D_MODEL, 2 * D_FF), BF16),
                            pltpu.VMEM((D_FF, D_MODEL), BF16),
                            pltpu.VMEM((GATHER_SLOTS, (MOE_ROWS + 1) * SUBLANES, LANES), F32),
                            pltpu.SemaphoreType.DMA((GATHER_SLOTS,)),
                            pltpu.SemaphoreType.DMA((2,))]),
        out_shape=jax.ShapeDtypeStruct((n_pad * SUBLANES, LANES), F32),
        compiler_params=pltpu.CompilerParams(dimension_semantics=("arbitrary",),
                                             vmem_limit_bytes=VMEM_LIMIT),
        name="moe",
    )(block_e, n_valid, next_e, row_tok, h2, w_up, b_up, w_down, b_down)


def _combine_kernel(dcur_ref, dnext_ref, ys_ref, x1_ref, gate_ref, mod_ref, o_ref, buf, sem):
    i = pl.program_id(0)
    tm = x1_ref.shape[0]
    slot = i % 2

    def issue(dref, s):
        def body(tk, carry):
            for kk in range(TOP_K):
                _token_copy(ys_ref, dref[0, 0, kk * tm + tk], buf.at[s], kk * tm + tk,
                            sem.at[s]).start(priority=kk % 2)
            return carry

        lax.fori_loop(0, tm, body, 0, unroll=4)

    @pl.when(i == 0)
    def _():
        issue(dcur_ref, 0)

    @pl.when(i + 1 < pl.num_programs(0))
    def _():
        issue(dnext_ref, 1 - slot)

    pltpu.make_async_copy(ys_ref.at[pl.ds(0, TOP_K * tm * SUBLANES), :], buf.at[slot],
                          sem.at[slot]).wait()

    gates = gate_ref[...]
    acc = jnp.zeros((tm, D_MODEL), F32)
    for kk in range(TOP_K):
        acc = acc + gates[:, kk:kk + 1] * _load_token_tiles(buf.at[slot], kk * tm, tm)
    o_ref[...] = x1_ref[...] + mod_ref[0, 5:6, :] * acc


def _combine(dest3, ys, x1, gates, mod, seq):
    t = x1.shape[0]
    tiles_per_seq = seq // TM_COMB
    n_tiles = t // TM_COMB
    row = lambda i: (i, 0)
    return pl.pallas_call(
        _combine_kernel,
        grid=(n_tiles,),
        in_specs=[pl.BlockSpec((1, 1, TM_COMB * TOP_K), lambda i: (i, 0, 0),
                               memory_space=pltpu.SMEM),
                  pl.BlockSpec((1, 1, TM_COMB * TOP_K),
                               lambda i: (jnp.minimum(i + 1, n_tiles - 1), 0, 0),
                               memory_space=pltpu.SMEM),
                  pl.BlockSpec(memory_space=pl.ANY),
                  pl.BlockSpec((TM_COMB, D_MODEL), row),
                  pl.BlockSpec((TM_COMB, LANES), row),
                  pl.BlockSpec((1, 6, D_MODEL), lambda i: (i // tiles_per_seq, 0, 0))],
        out_specs=pl.BlockSpec((TM_COMB, D_MODEL), row),
        out_shape=jax.ShapeDtypeStruct((t, D_MODEL), F32),
        scratch_shapes=[pltpu.VMEM((2, TOP_K * TM_COMB * SUBLANES, LANES), F32),
                        pltpu.SemaphoreType.DMA((2,))],
        compiler_params=pltpu.CompilerParams(dimension_semantics=("arbitrary",),
                                             vmem_limit_bytes=VMEM_LIMIT),
        name="combine",
    )(dest3, dest3, ys, x1, gates, mod)


def _layer(x, mod, norm1_w, w_in, conv_w, a_log, dt_bias, dn_norm_w, q_norm_w, k_norm_w, sinks,
           w_out, norm2_w, w_router, b_router, w_up, b_up, w_down, b_down):
    batch, seq, _ = x.shape
    t = batch * seq
    xf = x.reshape(t, D_MODEL)

    o_qkv, o_z, o_a, o_b, o_q, o_k, o_v = 0, 1536, 2048, 2052, 2056, 2568, 2696
    w_ab = jnp.zeros((D_MODEL, LANES), F32).at[:, 0:2 * DN_HEADS].set(w_in[:, o_a:o_q])
    w_cat = jnp.concatenate([w_in[:, o_qkv:o_z], w_in[:, o_z:o_a], w_in[:, o_q:o_k],
                             w_in[:, o_k:o_v], w_in[:, o_v:], w_ab], axis=1).astype(BF16)
    gparams = jnp.zeros((2, LANES), F32)
    gparams = gparams.at[0, 0:DN_HEADS].set(-jnp.exp(a_log.astype(F32)))
    gparams = gparams.at[1, 0:DN_HEADS].set(dt_bias.astype(F32))
    qnw128 = jnp.tile(q_norm_w.astype(F32), 2)[None, :]
    knw128 = jnp.tile(k_norm_w.astype(F32), 2)[None, :]

    q, k, v, z, gb, swq, swk, swv = _inproj(xf, mod, norm1_w[None, :], w_cat, conv_w, gparams,
                                            qnw128, knw128, seq)
    by_seq = lambda a: a.reshape(batch, seq, a.shape[-1])
    o_dn, o_sw = _mixer(by_seq(q), by_seq(k), by_seq(v), by_seq(z), by_seq(gb),
                        dn_norm_w[None, :], by_seq(swq), by_seq(swk), by_seq(swv),
                        sinks.astype(F32), batch, seq)
    o_dn = o_dn.reshape(t, DN_WIDTH)
    o_sw = o_sw.reshape(t, SWA_WIDTH)

    wr_pad = jnp.zeros((D_MODEL, LANES), F32).at[:, 0:N_EXPERTS].set(w_router).astype(BF16)
    br_pad = jnp.zeros((1, LANES), F32).at[0, 0:N_EXPERTS].set(b_router)
    x1, h2, top_idx, gates, rank, cnt = _post(xf, o_dn, o_sw, mod, w_out.astype(BF16),
                                              norm2_w[None, :], wr_pad, br_pad, seq)

    counts = cnt[0, 0:N_EXPERTS]
    padded = (counts + MOE_ROWS - 1) // MOE_ROWS * MOE_ROWS
    pend = jnp.cumsum(padded)
    pstart = pend - padded
    n_blocks = (t * TOP_K) // MOE_ROWS + N_EXPERTS
    n_pad = n_blocks * MOE_ROWS
    n_valid = (pend[-1] // MOE_ROWS).astype(I32)
    blk = jnp.minimum(jnp.arange(n_blocks, dtype=I32), n_valid - 1)
    block_e = jnp.minimum(jnp.sum(pend[None, :] <= (blk * MOE_ROWS)[:, None], axis=1),
                          N_EXPERTS - 1).astype(I32)
    experts = jnp.arange(N_EXPERTS, dtype=I32)[:, None, None]
    dest = (jnp.sum(jnp.where(top_idx[None, 0:TOP_K] == experts, pstart[:, None, None], 0), axis=0)
            + rank[0:TOP_K]).astype(I32)
    fill = jnp.stack([jnp.append(pstart + counts, pend[-1]),
                      jnp.append(pend, n_pad)]).astype(I32)

    eids = jnp.arange(N_EXPERTS, dtype=I32)
    later = (eids[None, :] > eids[:, None]) & (counts[None, :] > 0)
    next_e = jnp.min(jnp.where(later, eids[None, :], N_EXPERTS), axis=1)
    next_e = jnp.where(next_e == N_EXPERTS, eids, next_e).astype(I32)

    row_tok = _invert(fill, dest.reshape(TOP_K * t), n_pad)
    ys = _moe(block_e, n_valid.reshape(1), next_e, row_tok, h2, w_up, b_up[:, None, :], w_down,
              b_down[:, None, :])
    n_tiles = t // TM_COMB
    dest_tiles = dest.reshape(TOP_K, n_tiles, TM_COMB).transpose(1, 0, 2).reshape(
        n_tiles, 1, TOP_K * TM_COMB)
    out = _combine(dest_tiles, ys, x1, gates, mod, seq)
    return out.reshape(batch, seq, D_MODEL)


def kernel(x, c, w_ada, b_ada, norm1_w, w_in, conv_w, a_log, dt_bias, dn_norm_w, q_norm_w,
           k_norm_w, sinks, w_out, norm2_w, w_router, b_router, w_up, b_up, w_down, b_down):
    out_dtype = x.dtype
    batch = x.shape[0]
    depth = w_ada.shape[0]
    c_pad = jnp.zeros((SUBLANES, D_MODEL), F32).at[0:batch].set(c.astype(F32))
    for l in range(depth):
        mod = _ada(c_pad, w_ada[l], b_ada[l][None, :])[0:batch].reshape(batch, 6, D_MODEL)
        x = _layer(x, mod, norm1_w[l], w_in[l], conv_w[l], a_log[l], dt_bias[l], dn_norm_w[l],
                   q_norm_w[l], k_norm_w[l], sinks[l], w_out[l], norm2_w[l], w_router[l],
                   b_router[l], w_up[l], b_up[l], w_down[l], b_down[l])
    return x.astype(out_dtype)
```

```python
import functools

import jax
import jax.numpy as jnp
from jax import lax
from jax.experimental import pallas as pl
from jax.experimental.pallas import tpu as pltpu

F32 = jnp.float32
BF16 = jnp.bfloat16
I32 = jnp.int32

D_MODEL = 1024
DN_HEADS = 4
DN_HEAD_DIM = 128
DN_WIDTH = DN_HEADS * DN_HEAD_DIM
CONV_WIDTH = 4
CHUNK = 64
SWA_HEADS = 8
SWA_KV_HEADS = 2
SWA_GROUP = SWA_HEADS // SWA_KV_HEADS
SWA_HEAD_DIM = 64
SWA_WIDTH = SWA_HEADS * SWA_HEAD_DIM
SWA_KV_WIDTH = SWA_KV_HEADS * SWA_HEAD_DIM
WINDOW = 128
N_EXPERTS = 32
TOP_K = 4
D_FF = D_MODEL
SWIGLU_ALPHA = 1.702
SWIGLU_LIMIT = 7.0
EPS = 1e-6
NEG = -1e30

LANES = 128
SUBLANES = 8
VMEM_LIMIT = 56 * 1024 * 1024

C_QKV = 0
C_Z = C_QKV + 3 * DN_WIDTH
C_SWQ = C_Z + DN_WIDTH
C_SWK = C_SWQ + SWA_WIDTH
C_SWV = C_SWK + SWA_KV_WIDTH
C_AB = C_SWV + SWA_KV_WIDTH
IN_PAD = C_AB + LANES

TM_IN = 512
DN_ROWS = 256
MIX_SEQS = 2
TM_POST = 1024
MOE_ROWS = 256
TM_COMB = 128


def _silu(x):
    return x * jax.nn.sigmoid(x)


def _mm(a, b):
    return jnp.dot(a.astype(BF16), b.astype(BF16), preferred_element_type=F32)


def _mm_nt(a, b):
    return lax.dot_general(a.astype(BF16), b.astype(BF16), (((1,), (1,)), ((), ())),
                           preferred_element_type=F32)


def _mm_tn(a, b):
    return lax.dot_general(a.astype(BF16), b.astype(BF16), (((0,), (0,)), ((), ())),
                           preferred_element_type=F32)


def _load_token_tiles(ref, row0, rows):
    return jnp.concatenate(
        [ref[pl.ds(row0 * SUBLANES + s, rows, stride=SUBLANES), :] for s in range(D_MODEL // LANES)],
        axis=1)


def _store_token_tiles(ref, val):
    rows = val.shape[0]
    for s in range(D_MODEL // LANES):
        ref[pl.ds(s, rows, stride=SUBLANES), :] = val[:, s * LANES:(s + 1) * LANES]


def _token_copy(src_ref, s, dst_ref, d, sem):
    return pltpu.make_async_copy(
        src_ref.at[pl.ds(pl.multiple_of(s * SUBLANES, SUBLANES), SUBLANES), :],
        dst_ref.at[pl.ds(pl.multiple_of(d * SUBLANES, SUBLANES), SUBLANES), :], sem)


def _ada_kernel(c_ref, w_ref, b_ref, o_ref):
    c = c_ref[...]
    o_ref[...] = jnp.dot(_silu(c), w_ref[...], precision=lax.Precision.HIGHEST,
                         preferred_element_type=F32) + b_ref[...]


def _ada(c_pad, w_ada, b_ada):
    n = w_ada.shape[1]
    return pl.pallas_call(
        _ada_kernel,
        grid=(n // D_MODEL,),
        in_specs=[pl.BlockSpec((SUBLANES, D_MODEL), lambda j: (0, 0)),
                  pl.BlockSpec((D_MODEL, D_MODEL), lambda j: (0, j)),
                  pl.BlockSpec((1, D_MODEL), lambda j: (0, j))],
        out_specs=pl.BlockSpec((SUBLANES, D_MODEL), lambda j: (0, j)),
        out_shape=jax.ShapeDtypeStruct((SUBLANES, n), F32),
        name="ada",
    )(c_pad, w_ada, b_ada)


def _rms_heads64(a, w128):
    lane = lax.broadcasted_iota(I32, a.shape, 1)
    lo = lane < SWA_HEAD_DIM
    sq = a * a
    s_lo = jnp.sum(jnp.where(lo, sq, 0.0), axis=-1, keepdims=True)
    s_hi = jnp.sum(jnp.where(lo, 0.0, sq), axis=-1, keepdims=True)
    ms = jnp.where(lo, s_lo, s_hi) * (1.0 / SWA_HEAD_DIM)
    return a * lax.rsqrt(ms + EPS) * w128


def _inproj_kernel(tiles_per_seq, x_ref, mod_ref, n1w_ref, w_ref, convw_ref, gp_ref, qnw_ref,
                   knw_ref, q_ref, k_ref, v_ref, z_ref, gb_ref, swq_ref, swk_ref, swv_ref,
                   conv_scr):
    i = pl.program_id(0)
    tm = x_ref.shape[0]
    x = x_ref[...]
    y = x * lax.rsqrt(jnp.mean(x * x, axis=-1, keepdims=True) + EPS) * n1w_ref[...]
    h = (y * (1.0 + mod_ref[0, 1:2, :]) + mod_ref[0, 0:1, :]).astype(BF16)

    @pl.when(i % tiles_per_seq == 0)
    def _():
        conv_scr[0:SUBLANES, :] = jnp.zeros((SUBLANES, 3 * DN_WIDTH), F32)

    u = jnp.dot(h, w_ref[:, C_QKV:C_Z], preferred_element_type=F32)
    conv_scr[SUBLANES:SUBLANES + tm, :] = u
    cw = convw_ref[...]
    acc = u * cw[3:4, :]
    for j in range(CONV_WIDTH - 1):
        off = SUBLANES - (CONV_WIDTH - 1) + j
        acc = acc + conv_scr[off:off + tm, :] * cw[j:j + 1, :]
    conv_scr[0:SUBLANES, :] = conv_scr[tm:tm + SUBLANES, :]
    qkv = _silu(acc)

    for hh in range(DN_HEADS):
        sl = slice(hh * DN_HEAD_DIM, (hh + 1) * DN_HEAD_DIM)
        qh = qkv[:, hh * DN_HEAD_DIM:(hh + 1) * DN_HEAD_DIM]
        kh = qkv[:, DN_WIDTH + hh * DN_HEAD_DIM:DN_WIDTH + (hh + 1) * DN_HEAD_DIM]
        q_ref[:, sl] = qh * lax.rsqrt(jnp.sum(qh * qh, axis=-1, keepdims=True) + EPS) * (
            DN_HEAD_DIM ** -0.5)
        k_ref[:, sl] = kh * lax.rsqrt(jnp.sum(kh * kh, axis=-1, keepdims=True) + EPS)
    v_ref[...] = qkv[:, 2 * DN_WIDTH:3 * DN_WIDTH]

    z_ref[...] = jnp.dot(h, w_ref[:, C_Z:C_SWQ], preferred_element_type=F32)

    ab = jnp.dot(h, w_ref[:, C_AB:IN_PAD], preferred_element_type=F32)
    lane = lax.broadcasted_iota(I32, ab.shape, 1)
    sp_in = ab + gp_ref[1:2, :]
    softplus = jnp.maximum(sp_in, 0.0) + jnp.log1p(jnp.exp(-jnp.abs(sp_in)))
    gb_ref[...] = jnp.where(lane < DN_HEADS, gp_ref[0:1, :] * softplus, jax.nn.sigmoid(ab))

    sq = jnp.dot(h, w_ref[:, C_SWQ:C_SWK], preferred_element_type=F32)
    for t in range(SWA_WIDTH // LANES):
        sl = slice(t * LANES, (t + 1) * LANES)
        swq_ref[:, sl] = (_rms_heads64(sq[:, sl], qnw_ref[...]) * (SWA_HEAD_DIM ** -0.5)).astype(BF16)
    sk = jnp.dot(h, w_ref[:, C_SWK:C_SWV], preferred_element_type=F32)
    swk_ref[...] = _rms_heads64(sk, knw_ref[...]).astype(BF16)
    swv_ref[...] = jnp.dot(h, w_ref[:, C_SWV:C_AB], preferred_element_type=F32).astype(BF16)


def _inproj(xf, mod, n1w, w_cat, conv_w, gparams, qnw128, knw128, seq):
    t = xf.shape[0]
    tiles_per_seq = seq // TM_IN
    row = lambda i: (i, 0)
    const = lambda i: (0, 0)
    outs = [jax.ShapeDtypeStruct((t, DN_WIDTH), F32)] * 4 + [
        jax.ShapeDtypeStruct((t, LANES), F32),
        jax.ShapeDtypeStruct((t, SWA_WIDTH), BF16),
        jax.ShapeDtypeStruct((t, SWA_KV_WIDTH), BF16),
        jax.ShapeDtypeStruct((t, SWA_KV_WIDTH), BF16)]
    out_specs = [pl.BlockSpec((TM_IN, DN_WIDTH), row)] * 4 + [
        pl.BlockSpec((TM_IN, LANES), row),
        pl.BlockSpec((TM_IN, SWA_WIDTH), row),
        pl.BlockSpec((TM_IN, SWA_KV_WIDTH), row),
        pl.BlockSpec((TM_IN, SWA_KV_WIDTH), row)]
    return pl.pallas_call(
        functools.partial(_inproj_kernel, tiles_per_seq),
        grid=(t // TM_IN,),
        in_specs=[pl.BlockSpec((TM_IN, D_MODEL), row),
                  pl.BlockSpec((1, 6, D_MODEL), lambda i: (i // tiles_per_seq, 0, 0)),
                  pl.BlockSpec((1, D_MODEL), const),
                  pl.BlockSpec((D_MODEL, IN_PAD), const),
                  pl.BlockSpec((CONV_WIDTH, 3 * DN_WIDTH), const),
                  pl.BlockSpec((2, LANES), const),
                  pl.BlockSpec((1, LANES), const),
                  pl.BlockSpec((1, LANES), const)],
        out_specs=out_specs,
        out_shape=outs,
        scratch_shapes=[pltpu.VMEM((TM_IN + SUBLANES, 3 * DN_WIDTH), F32)],
        compiler_params=pltpu.CompilerParams(dimension_semantics=("arbitrary",),
                                             vmem_limit_bytes=VMEM_LIMIT),
        name="inproj",
    )(xf, mod, n1w, w_cat, conv_w, gparams, qnw128, knw128)


def _dn_block(j, q_ref, k_ref, v_ref, z_ref, gb_ref, nw_ref, o_ref,
              s_scr, u_scr, w_scr, qd_scr, kd_scr, a_scr):
    n_seq, rows = q_ref.shape[0], q_ref.shape[1]
    n_chunks = rows // CHUNK
    chains = [(g, h) for g in range(n_seq) for h in range(DN_HEADS)]
    lsl = [slice(h * DN_HEAD_DIM, (h + 1) * DN_HEAD_DIM) for h in range(DN_HEADS)]

    @pl.when(j == 0)
    def _():
        s_scr[...] = jnp.zeros(s_scr.shape, F32)

    r = lax.broadcasted_iota(I32, (rows, rows), 0)
    c = lax.broadcasted_iota(I32, (rows, rows), 1)
    same = (r // CHUNK) == (c // CHUNK)
    incl = same & (c <= r)
    strict = same & (c < r)
    masks = jnp.concatenate([jnp.where(incl, 1.0, 0.0), jnp.where(same, 1.0, 0.0)],
                            axis=0).astype(BF16)
    gbs, gcs, gls, gcts = [], [], [], []
    for g in range(n_seq):
        gb = gb_ref[g]
        g_hi = gb.astype(BF16)
        rem = gb - g_hi.astype(F32)
        g_mid = rem.astype(BF16)
        g_lo = (rem - g_mid.astype(F32)).astype(BF16)
        sums = jnp.dot(masks, jnp.concatenate([g_hi, g_mid, g_lo], axis=1),
                       preferred_element_type=F32)
        sums = sums[:, 0:LANES] + sums[:, LANES:2 * LANES] + sums[:, 2 * LANES:3 * LANES]
        gbs.append(gb)
        gcs.append(sums[0:rows, :])
        gls.append(sums[rows:2 * rows, :])
        gcts.append(sums[0:rows, :].T)

    rmat = []
    rhs = []
    for n, (g, h) in enumerate(chains):
        ls = lsl[h]
        g_col = gcs[g][:, h:h + 1]
        beta = gbs[g][:, DN_HEADS + h:DN_HEADS + h + 1]
        decay = jnp.where(incl, jnp.exp(jnp.where(incl, g_col - gcts[g][h:h + 1, :], 0.0)), 0.0)
        qq = q_ref[g, :, ls]
        kk = k_ref[g, :, ls]
        kb = kk * beta
        eg = jnp.exp(g_col)
        rmat.append(-jnp.where(strict, _mm_nt(kb, kk) * decay, 0.0))
        rhs.append(jnp.concatenate([v_ref[g, :, ls] * beta, kb * eg], axis=1))
        a_scr[n] = jnp.where(incl, _mm_nt(qq, kk) * decay, 0.0)
        qd_scr[g, :, ls] = qq * eg
        kd_scr[g, :, ls] = kk * jnp.exp(gls[g][:, h:h + 1] - g_col)

    ids = range(len(chains))
    xm = list(rmat)
    pm = list(rmat)
    for it in range(6):
        xm = [xm[n] + _mm(pm[n], xm[n]) for n in ids]
        if it < 5:
            pm = [_mm(pm[n], pm[n]) for n in ids]
    for n, (g, h) in enumerate(chains):
        sol = rhs[n] + _mm(xm[n], rhs[n])
        u_scr[g, :, lsl[h]] = sol[:, 0:DN_HEAD_DIM]
        w_scr[g, :, lsl[h]] = sol[:, DN_HEAD_DIM:2 * DN_HEAD_DIM]

    nw = nw_ref[...]
    for ci in range(n_chunks):
        rs = slice(ci * CHUNK, (ci + 1) * CHUNK)
        dvals = [jnp.exp(gls[g][ci * CHUNK:ci * CHUNK + 1, :]) for g in range(n_seq)]
        st = [s_scr[n] for n in ids]
        ws = [_mm(w_scr[g, rs, lsl[h]], st[n]) for n, (g, h) in enumerate(chains)]
        qs = [_mm(qd_scr[g, rs, lsl[h]], st[n]) for n, (g, h) in enumerate(chains)]
        vn = [u_scr[g, rs, lsl[h]] - ws[n] for n, (g, h) in enumerate(chains)]
        av = [_mm(a_scr[n, rs, rs], vn[n]) for n in ids]
        kv = [_mm_tn(kd_scr[g, rs, lsl[h]], vn[n]) for n, (g, h) in enumerate(chains)]
        for n, (g, h) in enumerate(chains):
            s_scr[n] = st[n] * dvals[g][:, h:h + 1] + kv[n]
            o = qs[n] + av[n]
            on = o * lax.rsqrt(jnp.mean(o * o, axis=-1, keepdims=True) + EPS) * nw
            o_ref[g, rs, lsl[h]] = (on * _silu(z_ref[g, rs, lsl[h]])).astype(o_ref.dtype)


def _swa_window(sink_ref, q, kcat, vcat, first):
    w = WINDOW
    rows = SWA_GROUP * w
    ri = lax.broadcasted_iota(I32, (rows, 2 * w), 0)
    kj = lax.broadcasted_iota(I32, (rows, 2 * w), 1)
    grp = ri // w
    rel = (ri - grp * w) + w - kj
    valid = (rel >= 0) & (rel < w)
    if first is not None:
        valid = valid & ((kj >= w) | jnp.logical_not(first))
    relf = rel.astype(F32)
    grp_col = lax.broadcasted_iota(I32, (rows, 1), 0) // w
    outs = []
    for kh in range(SWA_KV_HEADS):
        slope_col = jnp.zeros((rows, 1), F32)
        sink_col = jnp.zeros((rows, 1), F32)
        for g in range(SWA_GROUP):
            hd = kh * SWA_GROUP + g
            slope_col = jnp.where(grp_col == g, 2.0 ** (-8.0 * (hd + 1) / SWA_HEADS), slope_col)
            sink_col = jnp.where(grp_col == g, sink_ref[hd], sink_col)
        qs = jnp.concatenate(
            [q[:, (kh * SWA_GROUP + g) * SWA_HEAD_DIM:(kh * SWA_GROUP + g + 1) * SWA_HEAD_DIM]
             for g in range(SWA_GROUP)], axis=0)
        kk = kcat[:, kh * SWA_HEAD_DIM:(kh + 1) * SWA_HEAD_DIM]
        vv = vcat[:, kh * SWA_HEAD_DIM:(kh + 1) * SWA_HEAD_DIM]
        s = lax.dot_general(qs, kk, (((1,), (1,)), ((), ())), preferred_element_type=F32)
        logits = jnp.where(valid, s - slope_col * relf, NEG)
        m = jnp.maximum(jnp.max(logits, axis=-1, keepdims=True), sink_col)
        p = jnp.exp(logits - m)
        denom = jnp.sum(p, axis=-1, keepdims=True) + jnp.exp(sink_col - m)
        o = jnp.dot(p.astype(BF16), vv, preferred_element_type=F32) / denom
        for g in range(SWA_GROUP):
            outs.append(o[g * w:(g + 1) * w, :])
    return [jnp.concatenate([outs[2 * t], outs[2 * t + 1]], axis=-1)
            for t in range(SWA_WIDTH // LANES)]


def _mixer_kernel(sink_ref, q_ref, k_ref, v_ref, z_ref, gb_ref, nw_ref,
                  swq_ref, kp_ref, kc_ref, vp_ref, vc_ref, odn_ref, osw_ref,
                  s_scr, u_scr, w_scr, qd_scr, kd_scr, a_scr):
    j = pl.program_id(1)
    _dn_block(j, q_ref, k_ref, v_ref, z_ref, gb_ref, nw_ref, odn_ref,
              s_scr, u_scr, w_scr, qd_scr, kd_scr, a_scr)
    w = WINDOW
    for g in range(swq_ref.shape[0]):
        kcur = kc_ref[g]
        vcur = vc_ref[g]
        for s in range(DN_ROWS // w):
            rows = slice(s * w, (s + 1) * w)
            if s == 0:
                kcat = jnp.concatenate([kp_ref[g], kcur[0:w, :]], axis=0)
                vcat = jnp.concatenate([vp_ref[g], vcur[0:w, :]], axis=0)
                first = j == 0
            else:
                kcat = kcur[(s - 1) * w:(s + 1) * w, :]
                vcat = vcur[(s - 1) * w:(s + 1) * w, :]
                first = None
            tiles = _swa_window(sink_ref, swq_ref[g, rows, :], kcat, vcat, first)
            for t, tile in enumerate(tiles):
                osw_ref[g, rows, t * LANES:(t + 1) * LANES] = tile.astype(osw_ref.dtype)


def _mixer(q, k, v, z, gb, nw, swq, swk, swv, sinks, batch, seq):
    steps = seq // DN_ROWS
    per_step = DN_ROWS // WINDOW
    row = lambda b, j, s: (b, j, 0)
    prev = lambda b, j, s: (b, jnp.maximum(j * per_step - 1, 0), 0)
    blk = pl.BlockSpec((MIX_SEQS, DN_ROWS, DN_WIDTH), row)
    kv_cur = pl.BlockSpec((MIX_SEQS, DN_ROWS, SWA_KV_WIDTH), row)
    kv_prev = pl.BlockSpec((MIX_SEQS, WINDOW, SWA_KV_WIDTH), prev)
    n_chain = MIX_SEQS * DN_HEADS
    return pl.pallas_call(
        _mixer_kernel,
        grid_spec=pltpu.PrefetchScalarGridSpec(
            num_scalar_prefetch=1,
            grid=(batch // MIX_SEQS, steps),
            in_specs=[blk, blk, blk, blk,
                      pl.BlockSpec((MIX_SEQS, DN_ROWS, LANES), row),
                      pl.BlockSpec((1, DN_HEAD_DIM), lambda b, j, s: (0, 0)),
                      pl.BlockSpec((MIX_SEQS, DN_ROWS, SWA_WIDTH), row),
                      kv_prev, kv_cur, kv_prev, kv_cur],
            out_specs=[blk, pl.BlockSpec((MIX_SEQS, DN_ROWS, SWA_WIDTH), row)],
            scratch_shapes=[pltpu.VMEM((n_chain, DN_HEAD_DIM, DN_HEAD_DIM), F32),
                            pltpu.VMEM((MIX_SEQS, DN_ROWS, DN_WIDTH), F32),
                            pltpu.VMEM((MIX_SEQS, DN_ROWS, DN_WIDTH), F32),
                            pltpu.VMEM((MIX_SEQS, DN_ROWS, DN_WIDTH), F32),
                            pltpu.VMEM((MIX_SEQS, DN_ROWS, DN_WIDTH), F32),
                            pltpu.VMEM((n_chain, DN_ROWS, DN_ROWS), F32)]),
        out_shape=[jax.ShapeDtypeStruct((batch, seq, DN_WIDTH), BF16),
                   jax.ShapeDtypeStruct((batch, seq, SWA_WIDTH), BF16)],
        compiler_params=pltpu.CompilerParams(dimension_semantics=("arbitrary", "arbitrary"),
                                             vmem_limit_bytes=VMEM_LIMIT),
        name="mixer",
    )(sinks, q, k, v, z, gb, nw, swq, swk, swk, swv, swv)


def _post_kernel(x_ref, odn_ref, osw_ref, mod_ref, wo_ref, n2w_ref, wr_ref, br_ref,
                 x1_ref, h2_ref, idx_ref, gate_ref, rank_ref, cnt_ref, run_scr):
    i = pl.program_id(0)
    tm = x_ref.shape[0]

    @pl.when(i == 0)
    def _():
        run_scr[...] = jnp.zeros(run_scr.shape, F32)

    y = (jnp.dot(odn_ref[...], wo_ref[0:DN_WIDTH, :], preferred_element_type=F32)
         + jnp.dot(osw_ref[...], wo_ref[DN_WIDTH:, :], preferred_element_type=F32))
    x1 = x_ref[...] + mod_ref[0, 2:3, :] * y
    x1_ref[...] = x1
    hn = x1 * lax.rsqrt(jnp.mean(x1 * x1, axis=-1, keepdims=True) + EPS) * n2w_ref[...]
    h2 = hn * (1.0 + mod_ref[0, 4:5, :]) + mod_ref[0, 3:4, :]
    _store_token_tiles(h2_ref, h2)

    lane = lax.broadcasted_iota(I32, (tm, LANES), 1)
    lane_f = lane.astype(F32)
    logits = jnp.dot(h2.astype(BF16), wr_ref[...], preferred_element_type=F32) + br_ref[...]
    work = jnp.where(lane < N_EXPERTS, logits, NEG)
    sel = jnp.zeros((tm, LANES), F32)
    idx_out = jnp.zeros((tm, LANES), F32)
    vals = []
    idxs = []
    for kk in range(TOP_K):
        m = jnp.max(work, axis=-1, keepdims=True)
        ik = jnp.min(jnp.where(work == m, lane_f, float(LANES)), axis=-1, keepdims=True)
        hit = lane_f == ik
        work = jnp.where(hit, NEG, work)
        sel = jnp.where(hit, 1.0, sel)
        idx_out = jnp.where(lane == kk, ik, idx_out)
        vals.append(m)
        idxs.append(ik)
    es = [jnp.exp(v - vals[0]) for v in vals]
    tot = es[0] + es[1] + es[2] + es[3]
    gate_out = jnp.zeros((tm, LANES), F32)
    for kk in range(TOP_K):
        gate_out = jnp.where(lane == kk, es[kk] / tot, gate_out)
    gate_ref[...] = gate_out

    r = lax.broadcasted_iota(I32, (tm, tm), 0)
    c = lax.broadcasted_iota(I32, (tm, tm), 1)
    before = jnp.where(c < r, 1.0, 0.0).astype(BF16)
    base = jnp.dot(before, sel.astype(BF16), preferred_element_type=F32) + run_scr[0:1, :]
    rank_out = jnp.zeros((tm, LANES), F32)
    for kk in range(TOP_K):
        rk = jnp.sum(jnp.where(lane_f == idxs[kk], base, 0.0), axis=-1, keepdims=True)
        rank_out = jnp.where(lane == kk, rk, rank_out)
    idx_ref[...] = idx_out.T[0:SUBLANES, :].astype(I32)
    rank_ref[...] = rank_out.T[0:SUBLANES, :].astype(I32)
    run = run_scr[0:1, :] + jnp.sum(sel, axis=0, keepdims=True)
    run_scr[0:1, :] = run
    cnt_ref[...] = jnp.broadcast_to(run, cnt_ref.shape).astype(I32)


def _post(xf, odn, osw, mod, w_out, n2w, wr_pad, br_pad, seq):
    t = xf.shape[0]
    tiles_per_seq = seq // TM_POST
    row = lambda i: (i, 0)
    const = lambda i: (0, 0)
    return pl.pallas_call(
        _post_kernel,
        grid=(t // TM_POST,),
        in_specs=[pl.BlockSpec((TM_POST, D_MODEL), row),
                  pl.BlockSpec((TM_POST, DN_WIDTH), row),
                  pl.BlockSpec((TM_POST, SWA_WIDTH), row),
                  pl.BlockSpec((1, 6, D_MODEL), lambda i: (i // tiles_per_seq, 0, 0)),
                  pl.BlockSpec((D_MODEL, D_MODEL), const),
                  pl.BlockSpec((1, D_MODEL), const),
                  pl.BlockSpec((D_MODEL, LANES), const),
                  pl.BlockSpec((1, LANES), const)],
        out_specs=[pl.BlockSpec((TM_POST, D_MODEL), row),
                   pl.BlockSpec((TM_POST * SUBLANES, LANES), row),
                   pl.BlockSpec((SUBLANES, TM_POST), lambda i: (0, i)),
                   pl.BlockSpec((TM_POST, LANES), row),
                   pl.BlockSpec((SUBLANES, TM_POST), lambda i: (0, i)),
                   pl.BlockSpec((SUBLANES, LANES), const)],
        out_shape=[jax.ShapeDtypeStruct((t, D_MODEL), F32),
                   jax.ShapeDtypeStruct((t * SUBLANES, LANES), F32),
                   jax.ShapeDtypeStruct((SUBLANES, t), I32),
                   jax.ShapeDtypeStruct((t, LANES), F32),
                   jax.ShapeDtypeStruct((SUBLANES, t), I32),
                   jax.ShapeDtypeStruct((SUBLANES, LANES), I32)],
        scratch_shapes=[pltpu.VMEM((SUBLANES, LANES), F32)],
        compiler_params=pltpu.CompilerParams(dimension_semantics=("arbitrary",),
                                             vmem_limit_bytes=VMEM_LIMIT),
        name="post",
    )(xf, odn, osw, mod, w_out, n2w, wr_pad, br_pad)


INVERT_GROUP = 16


def _invert_kernel(fill_ref, dest_ref, rowtok_ref):
    def per_range(e, carry):
        def clear(rw, c2):
            rowtok_ref[rw] = 0
            return c2

        lax.fori_loop(fill_ref[0, e], fill_ref[1, e], clear, 0)
        return carry

    lax.fori_loop(0, fill_ref.shape[1], per_range, 0)

    n_tok = dest_ref.shape[0] // TOP_K
    for kk in range(TOP_K):
        def put(g, carry, kk=kk):
            tok0 = g * INVERT_GROUP
            rows = [dest_ref[kk * n_tok + tok0 + u] for u in range(INVERT_GROUP)]
            for u in range(INVERT_GROUP):
                rowtok_ref[rows[u]] = tok0 + u
            return carry

        lax.fori_loop(0, n_tok // INVERT_GROUP, put, 0)


def _invert(fill, dest_flat, n_pad):
    return pl.pallas_call(
        _invert_kernel,
        in_specs=[pl.BlockSpec(memory_space=pltpu.SMEM),
                  pl.BlockSpec(memory_space=pltpu.SMEM)],
        out_specs=pl.BlockSpec(memory_space=pltpu.SMEM),
        out_shape=jax.ShapeDtypeStruct((n_pad,), I32),
        name="invert",
    )(fill, dest_flat)


GATHER_SLOTS = 8
GATHER_AHEAD = GATHER_SLOTS - 1
FF_CHUNKS = 4
FF_COLS = D_FF // FF_CHUNKS


def _moe_kernel(be_ref, nv_ref, nxt_ref, rowtok_ref, h_ref, wu_hbm, bu_ref, wd_hbm, bd_ref, ys_ref,
                wu_f32, wd_f32, wu_bf, wd_bf, xbuf, sem, wsem):
    i = pl.program_id(0)
    slot = i % GATHER_SLOTS
    n_valid = nv_ref[0]

    def weight_copies(e):
        return (pltpu.make_async_copy(wu_hbm.at[e], wu_f32, wsem.at[0]),
                pltpu.make_async_copy(wd_hbm.at[e], wd_f32, wsem.at[1]))

    def gather(blk, s, r0=0, r1=MOE_ROWS):
        for r in range(r0, r1):
            _token_copy(h_ref, rowtok_ref[blk * MOE_ROWS + r], xbuf.at[s], r,
                        sem.at[s]).start(priority=r % 2)

    def gather_wait(s):
        pltpu.make_async_copy(h_ref.at[pl.ds(0, MOE_ROWS * SUBLANES), :],
                              xbuf.at[s, pl.ds(0, MOE_ROWS * SUBLANES), :], sem.at[s]).wait()

    last = n_valid - 1

    @pl.when(i == 0)
    def _():
        for s in range(GATHER_SLOTS):
            xbuf[s, pl.ds(MOE_ROWS * SUBLANES, SUBLANES), :] = jnp.zeros((SUBLANES, LANES), F32)
        for cp in weight_copies(be_ref[0]):
            cp.start()
        for d in range(GATHER_AHEAD):
            gather(jnp.minimum(d, last), d)

    @pl.when(i < n_valid)
    def _():
        expert = be_ref[i]
        prev = be_ref[jnp.maximum(i - 1, 0)]

        @pl.when((i == 0) | (expert != prev))
        def _():
            for cp in weight_copies(expert):
                cp.wait()
            wu_bf[...] = wu_f32[...].astype(BF16)
            wd_bf[...] = wd_f32[...].astype(BF16)
            nxt = nxt_ref[expert]

            @pl.when(nxt != expert)
            def _():
                for cp in weight_copies(nxt):
                    cp.start()

        gather_wait(slot)
        xb = _load_token_tiles(xbuf.at[slot], 0, MOE_ROWS).astype(BF16)
        nxt_blk = jnp.minimum(i + GATHER_AHEAD, last)
        nxt_slot = (i + GATHER_AHEAD) % GATHER_SLOTS
        up_group = MOE_ROWS // (4 * FF_CHUNKS)
        down_group = MOE_ROWS // (2 * FF_CHUNKS)
        issued = [0]

        def issue_group(n_rows):
            gather(nxt_blk, nxt_slot, issued[0], issued[0] + n_rows)
            issued[0] += n_rows
            spare = xbuf[nxt_slot, pl.ds(MOE_ROWS * SUBLANES, 1), :]
            bits = pltpu.bitcast(spare, jnp.uint32)
            tied = pltpu.bitcast((bits >> 16) >> 16, F32)
            return jnp.concatenate([tied] * (FF_COLS // LANES), axis=1)

        acts = []
        tie = jnp.zeros((1, FF_COLS), F32)
        for c in range(FF_CHUNKS):
            cg = slice(c * FF_COLS, (c + 1) * FF_COLS)
            cl = slice(D_FF + c * FF_COLS, D_FF + (c + 1) * FF_COLS)
            g = jnp.dot(xb, wu_bf[:, cg], preferred_element_type=F32) + (bu_ref[0, :, cg] + tie)
            tie = issue_group(up_group)
            l = jnp.dot(xb, wu_bf[:, cl], preferred_element_type=F32) + (bu_ref[0, :, cl] + tie)
            x_glu = jnp.minimum(g, SWIGLU_LIMIT)
            x_lin = jnp.clip(l, -SWIGLU_LIMIT, SWIGLU_LIMIT)
            acts.append((x_glu * jax.nn.sigmoid(SWIGLU_ALPHA * x_glu) * (x_lin + 1.0)).astype(BF16))
            tie = issue_group(up_group)
        act = jnp.concatenate(acts, axis=1)
        for c in range(FF_CHUNKS):
            cs = slice(c * FF_COLS, (c + 1) * FF_COLS)
            y = jnp.dot(act, wd_bf[:, cs], preferred_element_type=F32) + (bd_ref[0, :, cs] + tie)
            for u in range(FF_COLS // LANES):
                s = c * (FF_COLS // LANES) + u
                ys_ref[pl.ds(s, MOE_ROWS, stride=SUBLANES), :] = y[:, u * LANES:(u + 1) * LANES]
            tie = issue_group(down_group)

    @pl.when(i == last)
    def _():
        for d in range(1, GATHER_SLOTS):
            gather_wait((i + d) % GATHER_SLOTS)

    @pl.when(i >= n_valid)
    def _():
        ys_ref[...] = jnp.zeros(ys_ref.shape, F32)


def _moe(block_e, n_valid, next_e, row_tok, h2, w_up, b_up, w_down, b_down):
    n_pad = row_tok.shape[0]
    n_blocks = n_pad // MOE_ROWS
    out_rows = lambda i, be, nv, nx, rt: (i, 0)
    expert = lambda i, be, nv, nx, rt: (be[i], 0, 0)
    return pl.pallas_call(
        _moe_kernel,
        grid_spec=pltpu.PrefetchScalarGridSpec(
            num_scalar_prefetch=4,
            grid=(n_blocks,),
            in_specs=[pl.BlockSpec(memory_space=pl.ANY),
                      pl.BlockSpec(memory_space=pl.ANY),
                      pl.BlockSpec((1, 1, 2 * D_FF), expert),
                      pl.BlockSpec(memory_space=pl.ANY),
                      pl.BlockSpec((1, 1, D_MODEL), expert)],
            out_specs=pl.BlockSpec((MOE_ROWS * SUBLANES, LANES), out_rows),
            scratch_shapes=[pltpu.VMEM((D_MODEL, 2 * D_FF), F32),
                            pltpu.VMEM((D_FF, D_MODEL), F32),
                            pltpu.VMEM((D_MODEL, 2 * D_FF), BF16),
                            pltpu.VMEM((D_FF, D_MODEL), BF16),
                            pltpu.VMEM((GATHER_SLOTS, (MOE_ROWS + 1) * SUBLANES, LANES), F32),
                            pltpu.SemaphoreType.DMA((GATHER_SLOTS,)),
                            pltpu.SemaphoreType.DMA((2,))]),
        out_shape=jax.ShapeDtypeStruct((n_pad * SUBLANES, LANES), F32),
        compiler_params=pltpu.CompilerParams(dimension_semantics=("arbitrary",),
                                             vmem_limit_bytes=VMEM_LIMIT),
        name="moe",
    )(block_e, n_valid, next_e, row_tok, h2, w_up, b_up, w_down, b_down)


def _combine_kernel(dcur_ref, dnext_ref, ys_ref, x1_ref, gate_ref, mod_ref, o_ref, buf, sem):
    i = pl.program_id(0)
    tm = x1_ref.shape[0]
    slot = i % 2

    def issue(dref, s):
        def body(tk, carry):
            for kk in range(TOP_K):
                _token_copy(ys_ref, dref[0, 0, kk * tm + tk], buf.at[s], kk * tm + tk,
                            sem.at[s]).start(priority=kk % 2)
            return carry

        lax.fori_loop(0, tm, body, 0, unroll=4)

    @pl.when(i == 0)
    def _():
        issue(dcur_ref, 0)

    @pl.when(i + 1 < pl.num_programs(0))
    def _():
        issue(dnext_ref, 1 - slot)

    pltpu.make_async_copy(ys_ref.at[pl.ds(0, TOP_K * tm * SUBLANES), :], buf.at[slot],
                          sem.at[slot]).wait()

    gates = gate_ref[...]
    acc = jnp.zeros((tm, D_MODEL), F32)
    for kk in range(TOP_K):
        acc = acc + gates[:, kk:kk + 1] * _load_token_tiles(buf.at[slot], kk * tm, tm)
    o_ref[...] = x1_ref[...] + mod_ref[0, 5:6, :] * acc


def _combine(dest3, ys, x1, gates, mod, seq):
    t = x1.shape[0]
    tiles_per_seq = seq // TM_COMB
    n_tiles = t // TM_COMB
    row = lambda i: (i, 0)
    return pl.pallas_call(
        _combine_kernel,
        grid=(n_tiles,),
        in_specs=[pl.BlockSpec((1, 1, TM_COMB * TOP_K), lambda i: (i, 0, 0),
                               memory_space=pltpu.SMEM),
                  pl.BlockSpec((1, 1, TM_COMB * TOP_K),
                               lambda i: (jnp.minimum(i + 1, n_tiles - 1), 0, 0),
                               memory_space=pltpu.SMEM),
                  pl.BlockSpec(memory_space=pl.ANY),
                  pl.BlockSpec((TM_COMB, D_MODEL), row),
                  pl.BlockSpec((TM_COMB, LANES), row),
                  pl.BlockSpec((1, 6, D_MODEL), lambda i: (i // tiles_per_seq, 0, 0))],
        out_specs=pl.BlockSpec((TM_COMB, D_MODEL), row),
        out_shape=jax.ShapeDtypeStruct((t, D_MODEL), F32),
        scratch_shapes=[pltpu.VMEM((2, TOP_K * TM_COMB * SUBLANES, LANES), F32),
                        pltpu.SemaphoreType.DMA((2,))],
        compiler_params=pltpu.CompilerParams(dimension_semantics=("arbitrary",),
                                             vmem_limit_bytes=VMEM_LIMIT),
        name="combine",
    )(dest3, dest3, ys, x1, gates, mod)


def _layer(x, mod, norm1_w, w_in, conv_w, a_log, dt_bias, dn_norm_w, q_norm_w, k_norm_w, sinks,
           w_out, norm2_w, w_router, b_router, w_up, b_up, w_down, b_down):
    batch, seq, _ = x.shape
    t = batch * seq
    xf = x.reshape(t, D_MODEL)

    o_qkv, o_z, o_a, o_b, o_q, o_k, o_v = 0, 1536, 2048, 2052, 2056, 2568, 2696
    w_ab = jnp.zeros((D_MODEL, LANES), F32).at[:, 0:2 * DN_HEADS].set(w_in[:, o_a:o_q])
    w_cat = jnp.concatenate([w_in[:, o_qkv:o_z], w_in[:, o_z:o_a], w_in[:, o_q:o_k],
                             w_in[:, o_k:o_v], w_in[:, o_v:], w_ab], axis=1).astype(BF16)
    gparams = jnp.zeros((2, LANES), F32)
    gparams = gparams.at[0, 0:DN_HEADS].set(-jnp.exp(a_log.astype(F32)))
    gparams = gparams.at[1, 0:DN_HEADS].set(dt_bias.astype(F32))
    qnw128 = jnp.tile(q_norm_w.astype(F32), 2)[None, :]
    knw128 = jnp.tile(k_norm_w.astype(F32), 2)[None, :]

    q, k, v, z, gb, swq, swk, swv = _inproj(xf, mod, norm1_w[None, :], w_cat, conv_w, gparams,
                                            qnw128, knw128, seq)
    by_seq = lambda a: a.reshape(batch, seq, a.shape[-1])
    o_dn, o_sw = _mixer(by_seq(q), by_seq(k), by_seq(v), by_seq(z), by_seq(gb),
                        dn_norm_w[None, :], by_seq(swq), by_seq(swk), by_seq(swv),
                        sinks.astype(F32), batch, seq)
    o_dn = o_dn.reshape(t, DN_WIDTH)
    o_sw = o_sw.reshape(t, SWA_WIDTH)

    wr_pad = jnp.zeros((D_MODEL, LANES), F32).at[:, 0:N_EXPERTS].set(w_router).astype(BF16)
    br_pad = jnp.zeros((1, LANES), F32).at[0, 0:N_EXPERTS].set(b_router)
    x1, h2, top_idx, gates, rank, cnt = _post(xf, o_dn, o_sw, mod, w_out.astype(BF16),
                                              norm2_w[None, :], wr_pad, br_pad, seq)

    counts = cnt[0, 0:N_EXPERTS]
    padded = (counts + MOE_ROWS - 1) // MOE_ROWS * MOE_ROWS
    pend = jnp.cumsum(padded)
    pstart = pend - padded
    n_blocks = (t * TOP_K) // MOE_ROWS + N_EXPERTS
    n_pad = n_blocks * MOE_ROWS
    n_valid = (pend[-1] // MOE_ROWS).astype(I32)
    blk = jnp.minimum(jnp.arange(n_blocks, dtype=I32), n_valid - 1)
    block_e = jnp.minimum(jnp.sum(pend[None, :] <= (blk * MOE_ROWS)[:, None], axis=1),
                          N_EXPERTS - 1).astype(I32)
    experts = jnp.arange(N_EXPERTS, dtype=I32)[:, None, None]
    dest = (jnp.sum(jnp.where(top_idx[None, 0:TOP_K] == experts, pstart[:, None, None], 0), axis=0)
            + rank[0:TOP_K]).astype(I32)
    fill = jnp.stack([jnp.append(pstart + counts, pend[-1]),
                      jnp.append(pend, n_pad)]).astype(I32)

    eids = jnp.arange(N_EXPERTS, dtype=I32)
    later = (eids[None, :] > eids[:, None]) & (counts[None, :] > 0)
    next_e = jnp.min(jnp.where(later, eids[None, :], N_EXPERTS), axis=1)
    next_e = jnp.where(next_e == N_EXPERTS, eids, next_e).astype(I32)

    row_tok = _invert(fill, dest.reshape(TOP_K * t), n_pad)
    ys = _moe(block_e, n_valid.reshape(1), next_e, row_tok, h2, w_up, b_up[:, None, :], w_down,
              b_down[:, None, :])
    n_tiles = t // TM_COMB
    dest_tiles = dest.reshape(TOP_K, n_tiles, TM_COMB).transpose(1, 0, 2).reshape(
        n_tiles, 1, TOP_K * TM_COMB)
    out = _combine(dest_tiles, ys, x1, gates, mod, seq)
    return out.reshape(batch, seq, D_MODEL)


def kernel(x, c, w_ada, b_ada, norm1_w, w_in, conv_w, a_log, dt_bias, dn_norm_w, q_norm_w,
           k_norm_w, sinks, w_out, norm2_w, w_router, b_router, w_up, b_up, w_down, b_down):
    out_dtype = x.dtype
    batch = x.shape[0]
    depth = w_ada.shape[0]
    c_pad = jnp.zeros((SUBLANES, D_MODEL), F32).at[0:batch].set(c.astype(F32))
    for l in range(depth):
        mod = _ada(c_pad, w_ada[l], b_ada[l][None, :])[0:batch].reshape(batch, 6, D_MODEL)
        x = _layer(x, mod, norm1_w[l], w_in[l], conv_w[l], a_log[l], dt_bias[l], dn_norm_w[l],
                   q_norm_w[l], k_norm_w[l], sinks[l], w_out[l], norm2_w[l], w_router[l],
                   b_router[l], w_up[l], b_up[l], w_down[l], b_down[l])
    return x.astype(out_dtype)
```

```python
import functools

import jax
import jax.numpy as jnp
from jax import lax
from jax.experimental import pallas as pl
from jax.experimental.pallas import tpu as pltpu

F32 = jnp.float32
BF16 = jnp.bfloat16
I32 = jnp.int32

D_MODEL = 1024
DN_HEADS = 4
DN_HEAD_DIM = 128
DN_WIDTH = DN_HEADS * DN_HEAD_DIM
CONV_WIDTH = 4
CHUNK = 64
SWA_HEADS = 8
SWA_KV_HEADS = 2
SWA_GROUP = SWA_HEADS // SWA_KV_HEADS
SWA_HEAD_DIM = 64
SWA_WIDTH = SWA_HEADS * SWA_HEAD_DIM
SWA_KV_WIDTH = SWA_KV_HEADS * SWA_HEAD_DIM
WINDOW = 128
N_EXPERTS = 32
TOP_K = 4
D_FF = D_MODEL
SWIGLU_ALPHA = 1.702
SWIGLU_LIMIT = 7.0
EPS = 1e-6
NEG = -1e30

LANES = 128
SUBLANES = 8
VMEM_LIMIT = 56 * 1024 * 1024

C_QKV = 0
C_Z = C_QKV + 3 * DN_WIDTH
C_SWQ = C_Z + DN_WIDTH
C_SWK = C_SWQ + SWA_WIDTH
C_SWV = C_SWK + SWA_KV_WIDTH
C_AB = C_SWV + SWA_KV_WIDTH
IN_PAD = C_AB + LANES

TM_IN = 512
DN_ROWS = 256
MIX_SEQS = 2
TM_POST = 1024
MOE_ROWS = 256
TM_COMB = 256


def _silu(x):
    return x * jax.nn.sigmoid(x)


def _mm(a, b):
    return jnp.dot(a.astype(BF16), b.astype(BF16), preferred_element_type=F32)


def _mm_nt(a, b):
    return lax.dot_general(a.astype(BF16), b.astype(BF16), (((1,), (1,)), ((), ())),
                           preferred_element_type=F32)


def _mm_tn(a, b):
    return lax.dot_general(a.astype(BF16), b.astype(BF16), (((0,), (0,)), ((), ())),
                           preferred_element_type=F32)


def _load_token_tiles(ref, row0, rows):
    return jnp.concatenate(
        [ref[pl.ds(row0 * SUBLANES + s, rows, stride=SUBLANES), :] for s in range(D_MODEL // LANES)],
        axis=1)


def _store_token_tiles(ref, val):
    rows = val.shape[0]
    for s in range(D_MODEL // LANES):
        ref[pl.ds(s, rows, stride=SUBLANES), :] = val[:, s * LANES:(s + 1) * LANES]


def _token_copy(src_ref, s, dst_ref, d, sem):
    return pltpu.make_async_copy(
        src_ref.at[pl.ds(pl.multiple_of(s * SUBLANES, SUBLANES), SUBLANES), :],
        dst_ref.at[pl.ds(pl.multiple_of(d * SUBLANES, SUBLANES), SUBLANES), :], sem)


def _ada_kernel(c_ref, w_ref, b_ref, o_ref):
    c = c_ref[...]
    o_ref[...] = jnp.dot(_silu(c), w_ref[...], precision=lax.Precision.HIGHEST,
                         preferred_element_type=F32) + b_ref[...]


def _ada(c_pad, w_ada, b_ada):
    n = w_ada.shape[1]
    return pl.pallas_call(
        _ada_kernel,
        grid=(n // D_MODEL,),
        in_specs=[pl.BlockSpec((SUBLANES, D_MODEL), lambda j: (0, 0)),
                  pl.BlockSpec((D_MODEL, D_MODEL), lambda j: (0, j)),
                  pl.BlockSpec((1, D_MODEL), lambda j: (0, j))],
        out_specs=pl.BlockSpec((SUBLANES, D_MODEL), lambda j: (0, j)),
        out_shape=jax.ShapeDtypeStruct((SUBLANES, n), F32),
        name="ada",
    )(c_pad, w_ada, b_ada)


def _rms_heads64(a, w128):
    lane = lax.broadcasted_iota(I32, a.shape, 1)
    lo = lane < SWA_HEAD_DIM
    sq = a * a
    s_lo = jnp.sum(jnp.where(lo, sq, 0.0), axis=-1, keepdims=True)
    s_hi = jnp.sum(jnp.where(lo, 0.0, sq), axis=-1, keepdims=True)
    ms = jnp.where(lo, s_lo, s_hi) * (1.0 / SWA_HEAD_DIM)
    return a * lax.rsqrt(ms + EPS) * w128


def _inproj_kernel(tiles_per_seq, x_ref, mod_ref, n1w_ref, w_ref, convw_ref, gp_ref, qnw_ref,
                   knw_ref, q_ref, k_ref, v_ref, z_ref, gb_ref, swq_ref, swk_ref, swv_ref,
                   conv_scr):
    i = pl.program_id(0)
    tm = x_ref.shape[0]
    x = x_ref[...]
    y = x * lax.rsqrt(jnp.mean(x * x, axis=-1, keepdims=True) + EPS) * n1w_ref[...]
    h = (y * (1.0 + mod_ref[0, 1:2, :]) + mod_ref[0, 0:1, :]).astype(BF16)

    @pl.when(i % tiles_per_seq == 0)
    def _():
        conv_scr[0:SUBLANES, :] = jnp.zeros((SUBLANES, 3 * DN_WIDTH), F32)

    u = jnp.dot(h, w_ref[:, C_QKV:C_Z], preferred_element_type=F32)
    conv_scr[SUBLANES:SUBLANES + tm, :] = u
    cw = convw_ref[...]
    acc = u * cw[3:4, :]
    for j in range(CONV_WIDTH - 1):
        off = SUBLANES - (CONV_WIDTH - 1) + j
        acc = acc + conv_scr[off:off + tm, :] * cw[j:j + 1, :]
    conv_scr[0:SUBLANES, :] = conv_scr[tm:tm + SUBLANES, :]
    qkv = _silu(acc)

    for hh in range(DN_HEADS):
        sl = slice(hh * DN_HEAD_DIM, (hh + 1) * DN_HEAD_DIM)
        qh = qkv[:, hh * DN_HEAD_DIM:(hh + 1) * DN_HEAD_DIM]
        kh = qkv[:, DN_WIDTH + hh * DN_HEAD_DIM:DN_WIDTH + (hh + 1) * DN_HEAD_DIM]
        q_ref[:, sl] = qh * lax.rsqrt(jnp.sum(qh * qh, axis=-1, keepdims=True) + EPS) * (
            DN_HEAD_DIM ** -0.5)
        k_ref[:, sl] = kh * lax.rsqrt(jnp.sum(kh * kh, axis=-1, keepdims=True) + EPS)
    v_ref[...] = qkv[:, 2 * DN_WIDTH:3 * DN_WIDTH]

    z_ref[...] = jnp.dot(h, w_ref[:, C_Z:C_SWQ], preferred_element_type=F32)

    ab = jnp.dot(h, w_ref[:, C_AB:IN_PAD], preferred_element_type=F32)
    lane = lax.broadcasted_iota(I32, ab.shape, 1)
    sp_in = ab + gp_ref[1:2, :]
    softplus = jnp.maximum(sp_in, 0.0) + jnp.log1p(jnp.exp(-jnp.abs(sp_in)))
    gb_ref[...] = jnp.where(lane < DN_HEADS, gp_ref[0:1, :] * softplus, jax.nn.sigmoid(ab))

    sq = jnp.dot(h, w_ref[:, C_SWQ:C_SWK], preferred_element_type=F32)
    for t in range(SWA_WIDTH // LANES):
        sl = slice(t * LANES, (t + 1) * LANES)
        swq_ref[:, sl] = (_rms_heads64(sq[:, sl], qnw_ref[...]) * (SWA_HEAD_DIM ** -0.5)).astype(BF16)
    sk = jnp.dot(h, w_ref[:, C_SWK:C_SWV], preferred_element_type=F32)
    swk_ref[...] = _rms_heads64(sk, knw_ref[...]).astype(BF16)
    swv_ref[...] = jnp.dot(h, w_ref[:, C_SWV:C_AB], preferred_element_type=F32).astype(BF16)


def _inproj(xf, mod, n1w, w_cat, conv_w, gparams, qnw128, knw128, seq):
    t = xf.shape[0]
    tiles_per_seq = seq // TM_IN
    row = lambda i: (i, 0)
    const = lambda i: (0, 0)
    outs = [jax.ShapeDtypeStruct((t, DN_WIDTH), F32)] * 4 + [
        jax.ShapeDtypeStruct((t, LANES), F32),
        jax.ShapeDtypeStruct((t, SWA_WIDTH), BF16),
        jax.ShapeDtypeStruct((t, SWA_KV_WIDTH), BF16),
        jax.ShapeDtypeStruct((t, SWA_KV_WIDTH), BF16)]
    out_specs = [pl.BlockSpec((TM_IN, DN_WIDTH), row)] * 4 + [
        pl.BlockSpec((TM_IN, LANES), row),
        pl.BlockSpec((TM_IN, SWA_WIDTH), row),
        pl.BlockSpec((TM_IN, SWA_KV_WIDTH), row),
        pl.BlockSpec((TM_IN, SWA_KV_WIDTH), row)]
    return pl.pallas_call(
        functools.partial(_inproj_kernel, tiles_per_seq),
        grid=(t // TM_IN,),
        in_specs=[pl.BlockSpec((TM_IN, D_MODEL), row),
                  pl.BlockSpec((1, 6, D_MODEL), lambda i: (i // tiles_per_seq, 0, 0)),
                  pl.BlockSpec((1, D_MODEL), const),
                  pl.BlockSpec((D_MODEL, IN_PAD), const),
                  pl.BlockSpec((CONV_WIDTH, 3 * DN_WIDTH), const),
                  pl.BlockSpec((2, LANES), const),
                  pl.BlockSpec((1, LANES), const),
                  pl.BlockSpec((1, LANES), const)],
        out_specs=out_specs,
        out_shape=outs,
        scratch_shapes=[pltpu.VMEM((TM_IN + SUBLANES, 3 * DN_WIDTH), F32)],
        compiler_params=pltpu.CompilerParams(dimension_semantics=("arbitrary",),
                                             vmem_limit_bytes=VMEM_LIMIT),
        name="inproj",
    )(xf, mod, n1w, w_cat, conv_w, gparams, qnw128, knw128)


def _dn_block(j, q_ref, k_ref, v_ref, z_ref, gb_ref, nw_ref, o_ref,
              s_scr, u_scr, w_scr, qd_scr, kd_scr, a_scr):
    n_seq, rows = q_ref.shape[0], q_ref.shape[1]
    n_chunks = rows // CHUNK
    chains = [(g, h) for g in range(n_seq) for h in range(DN_HEADS)]
    lsl = [slice(h * DN_HEAD_DIM, (h + 1) * DN_HEAD_DIM) for h in range(DN_HEADS)]

    @pl.when(j == 0)
    def _():
        s_scr[...] = jnp.zeros(s_scr.shape, F32)

    r = lax.broadcasted_iota(I32, (rows, rows), 0)
    c = lax.broadcasted_iota(I32, (rows, rows), 1)
    same = (r // CHUNK) == (c // CHUNK)
    incl = same & (c <= r)
    strict = same & (c < r)
    masks = jnp.concatenate([jnp.where(incl, 1.0, 0.0), jnp.where(same, 1.0, 0.0)],
                            axis=0).astype(BF16)
    gbs, gcs, gls, gcts = [], [], [], []
    for g in range(n_seq):
        gb = gb_ref[g]
        g_hi = gb.astype(BF16)
        rem = gb - g_hi.astype(F32)
        g_mid = rem.astype(BF16)
        g_lo = (rem - g_mid.astype(F32)).astype(BF16)
        sums = jnp.dot(masks, jnp.concatenate([g_hi, g_mid, g_lo], axis=1),
                       preferred_element_type=F32)
        sums = sums[:, 0:LANES] + sums[:, LANES:2 * LANES] + sums[:, 2 * LANES:3 * LANES]
        gbs.append(gb)
        gcs.append(sums[0:rows, :])
        gls.append(sums[rows:2 * rows, :])
        gcts.append(sums[0:rows, :].T)

    rmat = []
    rhs = []
    for n, (g, h) in enumerate(chains):
        ls = lsl[h]
        g_col = gcs[g][:, h:h + 1]
        beta = gbs[g][:, DN_HEADS + h:DN_HEADS + h + 1]
        decay = jnp.where(incl, jnp.exp(jnp.where(incl, g_col - gcts[g][h:h + 1, :], 0.0)), 0.0)
        qq = q_ref[g, :, ls]
        kk = k_ref[g, :, ls]
        kb = kk * beta
        eg = jnp.exp(g_col)
        rmat.append(-jnp.where(strict, _mm_nt(kb, kk) * decay, 0.0))
        rhs.append(jnp.concatenate([v_ref[g, :, ls] * beta, kb * eg], axis=1))
        a_scr[n] = jnp.where(incl, _mm_nt(qq, kk) * decay, 0.0)
        qd_scr[g, :, ls] = qq * eg
        kd_scr[g, :, ls] = kk * jnp.exp(gls[g][:, h:h + 1] - g_col)

    ids = range(len(chains))
    xm = list(rmat)
    pm = list(rmat)
    for it in range(6):
        xm = [xm[n] + _mm(pm[n], xm[n]) for n in ids]
        if it < 5:
            pm = [_mm(pm[n], pm[n]) for n in ids]
    for n, (g, h) in enumerate(chains):
        sol = rhs[n] + _mm(xm[n], rhs[n])
        u_scr[g, :, lsl[h]] = sol[:, 0:DN_HEAD_DIM]
        w_scr[g, :, lsl[h]] = sol[:, DN_HEAD_DIM:2 * DN_HEAD_DIM]

    nw = nw_ref[...]
    for ci in range(n_chunks):
        rs = slice(ci * CHUNK, (ci + 1) * CHUNK)
        dvals = [jnp.exp(gls[g][ci * CHUNK:ci * CHUNK + 1, :]) for g in range(n_seq)]
        st = [s_scr[n] for n in ids]
        ws = [_mm(w_scr[g, rs, lsl[h]], st[n]) for n, (g, h) in enumerate(chains)]
        qs = [_mm(qd_scr[g, rs, lsl[h]], st[n]) for n, (g, h) in enumerate(chains)]
        vn = [u_scr[g, rs, lsl[h]] - ws[n] for n, (g, h) in enumerate(chains)]
        av = [_mm(a_scr[n, rs, rs], vn[n]) for n in ids]
        kv = [_mm_tn(kd_scr[g, rs, lsl[h]], vn[n]) for n, (g, h) in enumerate(chains)]
        for n, (g, h) in enumerate(chains):
            s_scr[n] = st[n] * dvals[g][:, h:h + 1] + kv[n]
            o = qs[n] + av[n]
            on = o * lax.rsqrt(jnp.mean(o * o, axis=-1, keepdims=True) + EPS) * nw
            o_ref[g, rs, lsl[h]] = (on * _silu(z_ref[g, rs, lsl[h]])).astype(o_ref.dtype)


def _swa_window(sink_ref, q, kcat, vcat, first):
    w = WINDOW
    rows = SWA_GROUP * w
    ri = lax.broadcasted_iota(I32, (rows, 2 * w), 0)
    kj = lax.broadcasted_iota(I32, (rows, 2 * w), 1)
    grp = ri // w
    rel = (ri - grp * w) + w - kj
    valid = (rel >= 0) & (rel < w)
    if first is not None:
        valid = valid & ((kj >= w) | jnp.logical_not(first))
    relf = rel.astype(F32)
    grp_col = lax.broadcasted_iota(I32, (rows, 1), 0) // w
    outs = []
    for kh in range(SWA_KV_HEADS):
        slope_col = jnp.zeros((rows, 1), F32)
        sink_col = jnp.zeros((rows, 1), F32)
        for g in range(SWA_GROUP):
            hd = kh * SWA_GROUP + g
            slope_col = jnp.where(grp_col == g, 2.0 ** (-8.0 * (hd + 1) / SWA_HEADS), slope_col)
            sink_col = jnp.where(grp_col == g, sink_ref[hd], sink_col)
        qs = jnp.concatenate(
            [q[:, (kh * SWA_GROUP + g) * SWA_HEAD_DIM:(kh * SWA_GROUP + g + 1) * SWA_HEAD_DIM]
             for g in range(SWA_GROUP)], axis=0)
        kk = kcat[:, kh * SWA_HEAD_DIM:(kh + 1) * SWA_HEAD_DIM]
        vv = vcat[:, kh * SWA_HEAD_DIM:(kh + 1) * SWA_HEAD_DIM]
        s = lax.dot_general(qs, kk, (((1,), (1,)), ((), ())), preferred_element_type=F32)
        logits = jnp.where(valid, s - slope_col * relf, NEG)
        m = jnp.maximum(jnp.max(logits, axis=-1, keepdims=True), sink_col)
        p = jnp.exp(logits - m)
        denom = jnp.sum(p, axis=-1, keepdims=True) + jnp.exp(sink_col - m)
        o = jnp.dot(p.astype(BF16), vv, preferred_element_type=F32) / denom
        for g in range(SWA_GROUP):
            outs.append(o[g * w:(g + 1) * w, :])
    return [jnp.concatenate([outs[2 * t], outs[2 * t + 1]], axis=-1)
            for t in range(SWA_WIDTH // LANES)]


def _mixer_kernel(sink_ref, q_ref, k_ref, v_ref, z_ref, gb_ref, nw_ref,
                  swq_ref, kp_ref, kc_ref, vp_ref, vc_ref, odn_ref, osw_ref,
                  s_scr, u_scr, w_scr, qd_scr, kd_scr, a_scr):
    j = pl.program_id(1)
    _dn_block(j, q_ref, k_ref, v_ref, z_ref, gb_ref, nw_ref, odn_ref,
              s_scr, u_scr, w_scr, qd_scr, kd_scr, a_scr)
    w = WINDOW
    for g in range(swq_ref.shape[0]):
        kcur = kc_ref[g]
        vcur = vc_ref[g]
        for s in range(DN_ROWS // w):
            rows = slice(s * w, (s + 1) * w)
            if s == 0:
                kcat = jnp.concatenate([kp_ref[g], kcur[0:w, :]], axis=0)
                vcat = jnp.concatenate([vp_ref[g], vcur[0:w, :]], axis=0)
                first = j == 0
            else:
                kcat = kcur[(s - 1) * w:(s + 1) * w, :]
                vcat = vcur[(s - 1) * w:(s + 1) * w, :]
                first = None
            tiles = _swa_window(sink_ref, swq_ref[g, rows, :], kcat, vcat, first)
            for t, tile in enumerate(tiles):
                osw_ref[g, rows, t * LANES:(t + 1) * LANES] = tile.astype(osw_ref.dtype)


def _mixer(q, k, v, z, gb, nw, swq, swk, swv, sinks, batch, seq):
    steps = seq // DN_ROWS
    per_step = DN_ROWS // WINDOW
    row = lambda b, j, s: (b, j, 0)
    prev = lambda b, j, s: (b, jnp.maximum(j * per_step - 1, 0), 0)
    blk = pl.BlockSpec((MIX_SEQS, DN_ROWS, DN_WIDTH), row)
    kv_cur = pl.BlockSpec((MIX_SEQS, DN_ROWS, SWA_KV_WIDTH), row)
    kv_prev = pl.BlockSpec((MIX_SEQS, WINDOW, SWA_KV_WIDTH), prev)
    n_chain = MIX_SEQS * DN_HEADS
    return pl.pallas_call(
        _mixer_kernel,
        grid_spec=pltpu.PrefetchScalarGridSpec(
            num_scalar_prefetch=1,
            grid=(batch // MIX_SEQS, steps),
            in_specs=[blk, blk, blk, blk,
                      pl.BlockSpec((MIX_SEQS, DN_ROWS, LANES), row),
                      pl.BlockSpec((1, DN_HEAD_DIM), lambda b, j, s: (0, 0)),
                      pl.BlockSpec((MIX_SEQS, DN_ROWS, SWA_WIDTH), row),
                      kv_prev, kv_cur, kv_prev, kv_cur],
            out_specs=[blk, pl.BlockSpec((MIX_SEQS, DN_ROWS, SWA_WIDTH), row)],
            scratch_shapes=[pltpu.VMEM((n_chain, DN_HEAD_DIM, DN_HEAD_DIM), F32),
                            pltpu.VMEM((MIX_SEQS, DN_ROWS, DN_WIDTH), F32),
                            pltpu.VMEM((MIX_SEQS, DN_ROWS, DN_WIDTH), F32),
                            pltpu.VMEM((MIX_SEQS, DN_ROWS, DN_WIDTH), F32),
                            pltpu.VMEM((MIX_SEQS, DN_ROWS, DN_WIDTH), F32),
                            pltpu.VMEM((n_chain, DN_ROWS, DN_ROWS), F32)]),
        out_shape=[jax.ShapeDtypeStruct((batch, seq, DN_WIDTH), BF16),
                   jax.ShapeDtypeStruct((batch, seq, SWA_WIDTH), BF16)],
        compiler_params=pltpu.CompilerParams(dimension_semantics=("arbitrary", "arbitrary"),
                                             vmem_limit_bytes=VMEM_LIMIT),
        name="mixer",
    )(sinks, q, k, v, z, gb, nw, swq, swk, swk, swv, swv)


def _post_kernel(x_ref, odn_ref, osw_ref, mod_ref, wo_ref, n2w_ref, wr_ref, br_ref,
                 x1_ref, h2_ref, idx_ref, gate_ref, rank_ref, cnt_ref, run_scr):
    i = pl.program_id(0)
    tm = x_ref.shape[0]

    @pl.when(i == 0)
    def _():
        run_scr[...] = jnp.zeros(run_scr.shape, F32)

    y = (jnp.dot(odn_ref[...], wo_ref[0:DN_WIDTH, :], preferred_element_type=F32)
         + jnp.dot(osw_ref[...], wo_ref[DN_WIDTH:, :], preferred_element_type=F32))
    x1 = x_ref[...] + mod_ref[0, 2:3, :] * y
    x1_ref[...] = x1
    hn = x1 * lax.rsqrt(jnp.mean(x1 * x1, axis=-1, keepdims=True) + EPS) * n2w_ref[...]
    h2 = hn * (1.0 + mod_ref[0, 4:5, :]) + mod_ref[0, 3:4, :]
    _store_token_tiles(h2_ref, h2)

    lane = lax.broadcasted_iota(I32, (tm, LANES), 1)
    lane_f = lane.astype(F32)
    logits = jnp.dot(h2.astype(BF16), wr_ref[...], preferred_element_type=F32) + br_ref[...]
    work = jnp.where(lane < N_EXPERTS, logits, NEG)
    sel = jnp.zeros((tm, LANES), F32)
    idx_out = jnp.zeros((tm, LANES), F32)
    vals = []
    idxs = []
    for kk in range(TOP_K):
        m = jnp.max(work, axis=-1, keepdims=True)
        ik = jnp.min(jnp.where(work == m, lane_f, float(LANES)), axis=-1, keepdims=True)
        hit = lane_f == ik
        work = jnp.where(hit, NEG, work)
        sel = jnp.where(hit, 1.0, sel)
        idx_out = jnp.where(lane == kk, ik, idx_out)
        vals.append(m)
        idxs.append(ik)
    es = [jnp.exp(v - vals[0]) for v in vals]
    tot = es[0] + es[1] + es[2] + es[3]
    gate_out = jnp.zeros((tm, LANES), F32)
    for kk in range(TOP_K):
        gate_out = jnp.where(lane == kk, es[kk] / tot, gate_out)
    gate_ref[...] = gate_out

    r = lax.broadcasted_iota(I32, (tm, tm), 0)
    c = lax.broadcasted_iota(I32, (tm, tm), 1)
    before = jnp.where(c < r, 1.0, 0.0).astype(BF16)
    base = jnp.dot(before, sel.astype(BF16), preferred_element_type=F32) + run_scr[0:1, :]
    rank_out = jnp.zeros((tm, LANES), F32)
    for kk in range(TOP_K):
        rk = jnp.sum(jnp.where(lane_f == idxs[kk], base, 0.0), axis=-1, keepdims=True)
        rank_out = jnp.where(lane == kk, rk, rank_out)
    idx_ref[...] = idx_out.T[0:SUBLANES, :].astype(I32)
    rank_ref[...] = rank_out.T[0:SUBLANES, :].astype(I32)
    run = run_scr[0:1, :] + jnp.sum(sel, axis=0, keepdims=True)
    run_scr[0:1, :] = run
    cnt_ref[...] = jnp.broadcast_to(run, cnt_ref.shape).astype(I32)


def _post(xf, odn, osw, mod, w_out, n2w, wr_pad, br_pad, seq):
    t = xf.shape[0]
    tiles_per_seq = seq // TM_POST
    row = lambda i: (i, 0)
    const = lambda i: (0, 0)
    return pl.pallas_call(
        _post_kernel,
        grid=(t // TM_POST,),
        in_specs=[pl.BlockSpec((TM_POST, D_MODEL), row),
                  pl.BlockSpec((TM_POST, DN_WIDTH), row),
                  pl.BlockSpec((TM_POST, SWA_WIDTH), row),
                  pl.BlockSpec((1, 6, D_MODEL), lambda i: (i // tiles_per_seq, 0, 0)),
                  pl.BlockSpec((D_MODEL, D_MODEL), const),
                  pl.BlockSpec((1, D_MODEL), const),
                  pl.BlockSpec((D_MODEL, LANES), const),
                  pl.BlockSpec((1, LANES), const)],
        out_specs=[pl.BlockSpec((TM_POST, D_MODEL), row),
                   pl.BlockSpec((TM_POST * SUBLANES, LANES), row),
                   pl.BlockSpec((SUBLANES, TM_POST), lambda i: (0, i)),
                   pl.BlockSpec((TM_POST, LANES), row),
                   pl.BlockSpec((SUBLANES, TM_POST), lambda i: (0, i)),
                   pl.BlockSpec((SUBLANES, LANES), const)],
        out_shape=[jax.ShapeDtypeStruct((t, D_MODEL), F32),
                   jax.ShapeDtypeStruct((t * SUBLANES, LANES), F32),
                   jax.ShapeDtypeStruct((SUBLANES, t), I32),
                   jax.ShapeDtypeStruct((t, LANES), F32),
                   jax.ShapeDtypeStruct((SUBLANES, t), I32),
                   jax.ShapeDtypeStruct((SUBLANES, LANES), I32)],
        scratch_shapes=[pltpu.VMEM((SUBLANES, LANES), F32)],
        compiler_params=pltpu.CompilerParams(dimension_semantics=("arbitrary",),
                                             vmem_limit_bytes=VMEM_LIMIT),
        name="post",
    )(xf, odn, osw, mod, w_out, n2w, wr_pad, br_pad)


INVERT_GROUP = 16


def _invert_kernel(fill_ref, dest_ref, rowtok_ref):
    def per_range(e, carry):
        def clear(rw, c2):
            rowtok_ref[rw] = 0
            return c2

        lax.fori_loop(fill_ref[0, e], fill_ref[1, e], clear, 0)
        return carry

    lax.fori_loop(0, fill_ref.shape[1], per_range, 0)

    n_tok = dest_ref.shape[0] // TOP_K
    for kk in range(TOP_K):
        def put(g, carry, kk=kk):
            tok0 = g * INVERT_GROUP
            rows = [dest_ref[kk * n_tok + tok0 + u] for u in range(INVERT_GROUP)]
            for u in range(INVERT_GROUP):
                rowtok_ref[rows[u]] = tok0 + u
            return carry

        lax.fori_loop(0, n_tok // INVERT_GROUP, put, 0)


def _invert(fill, dest_flat, n_pad):
    return pl.pallas_call(
        _invert_kernel,
        in_specs=[pl.BlockSpec(memory_space=pltpu.SMEM),
                  pl.BlockSpec(memory_space=pltpu.SMEM)],
        out_specs=pl.BlockSpec(memory_space=pltpu.SMEM),
        out_shape=jax.ShapeDtypeStruct((n_pad,), I32),
        name="invert",
    )(fill, dest_flat)


GATHER_SLOTS = 8
GATHER_AHEAD = GATHER_SLOTS - 1
FF_CHUNKS = 4
FF_COLS = D_FF // FF_CHUNKS


def _moe_kernel(be_ref, nv_ref, nxt_ref, rowtok_ref, h_ref, wu_hbm, bu_ref, wd_hbm, bd_ref, ys_ref,
                wu_f32, wd_f32, wu_bf, wd_bf, xbuf, sem, wsem):
    i = pl.program_id(0)
    slot = i % GATHER_SLOTS
    n_valid = nv_ref[0]

    def weight_copies(e):
        return (pltpu.make_async_copy(wu_hbm.at[e], wu_f32, wsem.at[0]),
                pltpu.make_async_copy(wd_hbm.at[e], wd_f32, wsem.at[1]))

    def gather(blk, s, r0=0, r1=MOE_ROWS):
        for r in range(r0, r1):
            _token_copy(h_ref, rowtok_ref[blk * MOE_ROWS + r], xbuf.at[s], r,
                        sem.at[s]).start(priority=r % 2)

    def gather_wait(s):
        pltpu.make_async_copy(h_ref.at[pl.ds(0, MOE_ROWS * SUBLANES), :],
                              xbuf.at[s, pl.ds(0, MOE_ROWS * SUBLANES), :], sem.at[s]).wait()

    last = n_valid - 1

    @pl.when(i == 0)
    def _():
        for s in range(GATHER_SLOTS):
            xbuf[s, pl.ds(MOE_ROWS * SUBLANES, SUBLANES), :] = jnp.zeros((SUBLANES, LANES), F32)
        for cp in weight_copies(be_ref[0]):
            cp.start()
        for d in range(GATHER_AHEAD):
            gather(jnp.minimum(d, last), d)

    @pl.when(i < n_valid)
    def _():
        expert = be_ref[i]
        prev = be_ref[jnp.maximum(i - 1, 0)]

        @pl.when((i == 0) | (expert != prev))
        def _():
            for cp in weight_copies(expert):
                cp.wait()
            wu_bf[...] = wu_f32[...].astype(BF16)
            wd_bf[...] = wd_f32[...].astype(BF16)
            nxt = nxt_ref[expert]

            @pl.when(nxt != expert)
            def _():
                for cp in weight_copies(nxt):
                    cp.start()

        gather_wait(slot)
        xb = _load_token_tiles(xbuf.at[slot], 0, MOE_ROWS).astype(BF16)
        nxt_blk = jnp.minimum(i + GATHER_AHEAD, last)
        nxt_slot = (i + GATHER_AHEAD) % GATHER_SLOTS
        group = MOE_ROWS // (2 * FF_CHUNKS)
        issued = [0]

        def issue_group():
            gather(nxt_blk, nxt_slot, issued[0] * group, (issued[0] + 1) * group)
            issued[0] += 1
            spare = xbuf[nxt_slot, pl.ds(MOE_ROWS * SUBLANES, 1), :]
            bits = pltpu.bitcast(spare, jnp.uint32)
            tied = pltpu.bitcast((bits >> 16) >> 16, F32)
            return jnp.concatenate([tied] * (FF_COLS // LANES), axis=1)

        acts = []
        tie = jnp.zeros((1, FF_COLS), F32)
        for c in range(FF_CHUNKS):
            cg = slice(c * FF_COLS, (c + 1) * FF_COLS)
            cl = slice(D_FF + c * FF_COLS, D_FF + (c + 1) * FF_COLS)
            g = jnp.dot(xb, wu_bf[:, cg], preferred_element_type=F32) + (bu_ref[0, :, cg] + tie)
            l = jnp.dot(xb, wu_bf[:, cl], preferred_element_type=F32) + bu_ref[0, :, cl]
            x_glu = jnp.minimum(g, SWIGLU_LIMIT)
            x_lin = jnp.clip(l, -SWIGLU_LIMIT, SWIGLU_LIMIT)
            acts.append((x_glu * jax.nn.sigmoid(SWIGLU_ALPHA * x_glu) * (x_lin + 1.0)).astype(BF16))
            tie = issue_group()
        act = jnp.concatenate(acts, axis=1)
        for c in range(FF_CHUNKS):
            cs = slice(c * FF_COLS, (c + 1) * FF_COLS)
            y = jnp.dot(act, wd_bf[:, cs], preferred_element_type=F32) + (bd_ref[0, :, cs] + tie)
            for u in range(FF_COLS // LANES):
                s = c * (FF_COLS // LANES) + u
                ys_ref[pl.ds(s, MOE_ROWS, stride=SUBLANES), :] = y[:, u * LANES:(u + 1) * LANES]
            tie = issue_group()

    @pl.when(i == last)
    def _():
        for d in range(1, GATHER_SLOTS):
            gather_wait((i + d) % GATHER_SLOTS)

    @pl.when(i >= n_valid)
    def _():
        ys_ref[...] = jnp.zeros(ys_ref.shape, F32)


def _moe(block_e, n_valid, next_e, row_tok, h2, w_up, b_up, w_down, b_down):
    n_pad = row_tok.shape[0]
    n_blocks = n_pad // MOE_ROWS
    out_rows = lambda i, be, nv, nx, rt: (i, 0)
    expert = lambda i, be, nv, nx, rt: (be[i], 0, 0)
    return pl.pallas_call(
        _moe_kernel,
        grid_spec=pltpu.PrefetchScalarGridSpec(
            num_scalar_prefetch=4,
            grid=(n_blocks,),
            in_specs=[pl.BlockSpec(memory_space=pl.ANY),
                      pl.BlockSpec(memory_space=pl.ANY),
                      pl.BlockSpec((1, 1, 2 * D_FF), expert),
                      pl.BlockSpec(memory_space=pl.ANY),
                      pl.BlockSpec((1, 1, D_MODEL), expert)],
            out_specs=pl.BlockSpec((MOE_ROWS * SUBLANES, LANES), out_rows),
            scratch_shapes=[pltpu.VMEM((D_MODEL, 2 * D_FF), F32),
                            pltpu.VMEM((D_FF, D_MODEL), F32),
                            pltpu.VMEM((D_MODEL, 2 * D_FF), BF16),
                            pltpu.VMEM((D_FF, D_MODEL), BF16),
                            pltpu.VMEM((GATHER_SLOTS, (MOE_ROWS + 1) * SUBLANES, LANES), F32),
                            pltpu.SemaphoreType.DMA((GATHER_SLOTS,)),
                            pltpu.SemaphoreType.DMA((2,))]),
        out_shape=jax.ShapeDtypeStruct((n_pad * SUBLANES, LANES), F32),
        compiler_params=pltpu.CompilerParams(dimension_semantics=("arbitrary",),
                                             vmem_limit_bytes=VMEM_LIMIT),
        name="moe",
    )(block_e, n_valid, next_e, row_tok, h2, w_up, b_up, w_down, b_down)


def _combine_kernel(dcur_ref, dnext_ref, ys_ref, x1_ref, gate_ref, mod_ref, o_ref, buf, sem):
    i = pl.program_id(0)
    tm = x1_ref.shape[0]
    slot = i % 2

    def issue(dref, s):
        def body(tk, carry):
            for kk in range(TOP_K):
                _token_copy(ys_ref, dref[0, 0, kk * tm + tk], buf.at[s], kk * tm + tk,
                            sem.at[s]).start(priority=kk % 2)
            return carry

        lax.fori_loop(0, tm, body, 0, unroll=4)

    @pl.when(i == 0)
    def _():
        issue(dcur_ref, 0)

    @pl.when(i + 1 < pl.num_programs(0))
    def _():
        issue(dnext_ref, 1 - slot)

    pltpu.make_async_copy(ys_ref.at[pl.ds(0, TOP_K * tm * SUBLANES), :], buf.at[slot],
                          sem.at[slot]).wait()

    gates = gate_ref[...]
    acc = jnp.zeros((tm, D_MODEL), F32)
    for kk in range(TOP_K):
        acc = acc + gates[:, kk:kk + 1] * _load_token_tiles(buf.at[slot], kk * tm, tm)
    o_ref[...] = x1_ref[...] + mod_ref[0, 5:6, :] * acc


def _combine(dest3, ys, x1, gates, mod, seq):
    t = x1.shape[0]
    tiles_per_seq = seq // TM_COMB
    n_tiles = t // TM_COMB
    row = lambda i: (i, 0)
    return pl.pallas_call(
        _combine_kernel,
        grid=(n_tiles,),
        in_specs=[pl.BlockSpec((1, 1, TM_COMB * TOP_K), lambda i: (i, 0, 0),
                               memory_space=pltpu.SMEM),
                  pl.BlockSpec((1, 1, TM_COMB * TOP_K),
                               lambda i: (jnp.minimum(i + 1, n_tiles - 1), 0, 0),
                               memory_space=pltpu.SMEM),
                  pl.BlockSpec(memory_space=pl.ANY),
                  pl.BlockSpec((TM_COMB, D_MODEL), row),
                  pl.BlockSpec((TM_COMB, LANES), row),
                  pl.BlockSpec((1, 6, D_MODEL), lambda i: (i // tiles_per_seq, 0, 0))],
        out_specs=pl.BlockSpec((TM_COMB, D_MODEL), row),
        out_shape=jax.ShapeDtypeStruct((t, D_MODEL), F32),
        scratch_shapes=[pltpu.VMEM((2, TOP_K * TM_COMB * SUBLANES, LANES), F32),
                        pltpu.SemaphoreType.DMA((2,))],
        compiler_params=pltpu.CompilerParams(dimension_semantics=("arbitrary",),
                                             vmem_limit_bytes=VMEM_LIMIT),
        name="combine",
    )(dest3, dest3, ys, x1, gates, mod)


def _layer(x, mod, norm1_w, w_in, conv_w, a_log, dt_bias, dn_norm_w, q_norm_w, k_norm_w, sinks,
           w_out, norm2_w, w_router, b_router, w_up, b_up, w_down, b_down):
    batch, seq, _ = x.shape
    t = batch * seq
    xf = x.reshape(t, D_MODEL)

    o_qkv, o_z, o_a, o_b, o_q, o_k, o_v = 0, 1536, 2048, 2052, 2056, 2568, 2696
    w_ab = jnp.zeros((D_MODEL, LANES), F32).at[:, 0:2 * DN_HEADS].set(w_in[:, o_a:o_q])
    w_cat = jnp.concatenate([w_in[:, o_qkv:o_z], w_in[:, o_z:o_a], w_in[:, o_q:o_k],
                             w_in[:, o_k:o_v], w_in[:, o_v:], w_ab], axis=1).astype(BF16)
    gparams = jnp.zeros((2, LANES), F32)
    gparams = gparams.at[0, 0:DN_HEADS].set(-jnp.exp(a_log.astype(F32)))
    gparams = gparams.at[1, 0:DN_HEADS].set(dt_bias.astype(F32))
    qnw128 = jnp.tile(q_norm_w.astype(F32), 2)[None, :]
    knw128 = jnp.tile(k_norm_w.astype(F32), 2)[None, :]

    q, k, v, z, gb, swq, swk, swv = _inproj(xf, mod, norm1_w[None, :], w_cat, conv_w, gparams,
                                            qnw128, knw128, seq)
    by_seq = lambda a: a.reshape(batch, seq, a.shape[-1])
    o_dn, o_sw = _mixer(by_seq(q), by_seq(k), by_seq(v), by_seq(z), by_seq(gb),
                        dn_norm_w[None, :], by_seq(swq), by_seq(swk), by_seq(swv),
                        sinks.astype(F32), batch, seq)
    o_dn = o_dn.reshape(t, DN_WIDTH)
    o_sw = o_sw.reshape(t, SWA_WIDTH)

    wr_pad = jnp.zeros((D_MODEL, LANES), F32).at[:, 0:N_EXPERTS].set(w_router).astype(BF16)
    br_pad = jnp.zeros((1, LANES), F32).at[0, 0:N_EXPERTS].set(b_router)
    x1, h2, top_idx, gates, rank, cnt = _post(xf, o_dn, o_sw, mod, w_out.astype(BF16),
                                              norm2_w[None, :], wr_pad, br_pad, seq)

    counts = cnt[0, 0:N_EXPERTS]
    padded = (counts + MOE_ROWS - 1) // MOE_ROWS * MOE_ROWS
    pend = jnp.cumsum(padded)
    pstart = pend - padded
    n_blocks = (t * TOP_K) // MOE_ROWS + N_EXPERTS
    n_pad = n_blocks * MOE_ROWS
    n_valid = (pend[-1] // MOE_ROWS).astype(I32)
    blk = jnp.minimum(jnp.arange(n_blocks, dtype=I32), n_valid - 1)
    block_e = jnp.minimum(jnp.sum(pend[None, :] <= (blk * MOE_ROWS)[:, None], axis=1),
                          N_EXPERTS - 1).astype(I32)
    experts = jnp.arange(N_EXPERTS, dtype=I32)[:, None, None]
    dest = (jnp.sum(jnp.where(top_idx[None, 0:TOP_K] == experts, pstart[:, None, None], 0), axis=0)
            + rank[0:TOP_K]).astype(I32)
    fill = jnp.stack([jnp.append(pstart + counts, pend[-1]),
                      jnp.append(pend, n_pad)]).astype(I32)

    eids = jnp.arange(N_EXPERTS, dtype=I32)
    later = (eids[None, :] > eids[:, None]) & (counts[None, :] > 0)
    next_e = jnp.min(jnp.where(later, eids[None, :], N_EXPERTS), axis=1)
    next_e = jnp.where(next_e == N_EXPERTS, eids, next_e).astype(I32)

    row_tok = _invert(fill, dest.reshape(TOP_K * t), n_pad)
    ys = _moe(block_e, n_valid.reshape(1), next_e, row_tok, h2, w_up, b_up[:, None, :], w_down,
              b_down[:, None, :])
    n_tiles = t // TM_COMB
    dest_tiles = dest.reshape(TOP_K, n_tiles, TM_COMB).transpose(1, 0, 2).reshape(
        n_tiles, 1, TOP_K * TM_COMB)
    out = _combine(dest_tiles, ys, x1, gates, mod, seq)
    return out.reshape(batch, seq, D_MODEL)


def kernel(x, c, w_ada, b_ada, norm1_w, w_in, conv_w, a_log, dt_bias, dn_norm_w, q_norm_w,
           k_norm_w, sinks, w_out, norm2_w, w_router, b_router, w_up, b_up, w_down, b_down):
    out_dtype = x.dtype
    batch = x.shape[0]
    depth = w_ada.shape[0]
    c_pad = jnp.zeros((SUBLANES, D_MODEL), F32).at[0:batch].set(c.astype(F32))
    for l in range(depth):
        mod = _ada(c_pad, w_ada[l], b_ada[l][None, :])[0:batch].reshape(batch, 6, D_MODEL)
        x = _layer(x, mod, norm1_w[l], w_in[l], conv_w[l], a_log[l], dt_bias[l], dn_norm_w[l],
                   q_norm_w[l], k_norm_w[l], sinks[l], w_out[l], norm2_w[l], w_router[l],
                   b_router[l], w_up[l], b_up[l], w_down[l], b_down[l])
    return x.astype(out_dtype)
```
